```python
import jax, jax.numpy as jnp
from jax import lax
import numpy as np

D_MODEL = 1024
BATCH = 2
SEQ = 8192
DEPTH = 1
DEC_BATCH = 32
DEC_SEQ = 1
PAST_LEN = 8192
PAGE_SIZE = 128

HEAD_DIM = 64
NSA_HEADS = 8
NSA_KV_HEADS = 2
NSA_REP = NSA_HEADS // NSA_KV_HEADS
CMP_LEN = 32
CMP_STRIDE = 16
SEL_LEN = 64
SEL_TOPK = 16
WINDOW = 512
Q_BLOCK = 128
ROPE_THETA = 10000.0
GDN_HEADS = 4
GDN_DK = 128
GDN_DV = 128
CONV_W = 4
GDN_CHUNK = 64
NSA_Q_W = NSA_HEADS * HEAD_DIM
NSA_KV_W = NSA_KV_HEADS * HEAD_DIM
GDN_QK_W = GDN_HEADS * GDN_DK
GDN_V_W = GDN_HEADS * GDN_DV
GDN_CONV_CH = 2 * GDN_QK_W + GDN_V_W
MIX_W = NSA_Q_W + GDN_V_W
IN_W = NSA_Q_W + 6 * NSA_KV_W + 3 * NSA_HEADS + GDN_CONV_CH + GDN_V_W + 2 * GDN_HEADS
_FF_RAW = -(-8 * D_MODEL // 3)
D_FF = -(-_FF_RAW // 256) * 256
NEG = -1e30
SEL_BONUS = 1e4
EPS = 1e-6

kernel_name = 'hybrid_nsa_gdn_decode_step'

F32 = jnp.float32


def rms_norm(x, g):
    xf = x.astype(F32)
    y = xf * lax.rsqrt(jnp.mean(xf * xf, axis=-1, keepdims=True) + EPS)
    return (y * g.astype(F32)).astype(x.dtype)


def rope(x, pos):
    half = HEAD_DIM // 2
    inv = jnp.power(ROPE_THETA, -jnp.arange(half, dtype=F32) * 2.0 / HEAD_DIM)
    ang = pos.astype(F32)[:, None] * inv[None, :]
    cos = jnp.cos(ang)[None, :, None, :]
    sin = jnp.sin(ang)[None, :, None, :]
    xf = x.astype(F32)
    x1, x2 = xf[..., :half], xf[..., half:]
    return jnp.concatenate([x1 * cos - x2 * sin, x2 * cos + x1 * sin], axis=-1).astype(x.dtype)


def masked_softmax(s, mask):
    p = jax.nn.softmax(jnp.where(mask, s, NEG), axis=-1)
    return jnp.where(mask, p, 0.0)


def l2norm(x):
    return x * lax.rsqrt(jnp.sum(x * x, axis=-1, keepdims=True) + EPS)


def split_proj(proj, pos):
    B, T, _ = proj.shape
    G, R, dh = NSA_KV_HEADS, NSA_REP, HEAD_DIM
    cuts = np.cumsum([NSA_Q_W, 6 * NSA_KV_W, 3 * NSA_HEADS, GDN_CONV_CH, GDN_V_W, GDN_HEADS]).tolist()
    q, kv, gt, conv_in, z, a, b = jnp.split(proj, cuts, axis=-1)
    q = rope(q.reshape(B, T, NSA_HEADS, dh), pos).reshape(B, T, G, R, dh)
    kv = kv.reshape(B, T, 3, 2, G, dh)
    k = rope(kv[:, :, :, 0].reshape(B, T, 3 * G, dh), pos).reshape(B, T, 3, G, dh)
    v = kv[:, :, :, 1]
    kv4 = jnp.stack([k[:, :, 0], v[:, :, 0], k[:, :, 1], v[:, :, 1]], axis=2)
    kvw = jnp.stack([k[:, :, 2], v[:, :, 2]], axis=2)
    gates = jax.nn.sigmoid(gt.astype(F32)).reshape(B, T, G, R, 3)
    return q, gates, kv4, kvw, conv_in, z, a, b


def compress(k, pe, w):
    L = k.shape[1]
    n_cmp = (L - CMP_LEN) // CMP_STRIDE + 1
    idx = jnp.arange(n_cmp)[:, None] * CMP_STRIDE + jnp.arange(CMP_LEN)[None, :]
    blk = k[:, idx] + pe[None, None, :, None, :]
    return jnp.einsum('bnlgd,lde->bnge', blk, w)


def sel_blocks(k):
    B, L, G, dh = k.shape
    n_sel = -(-L // SEL_LEN)
    kp = jnp.pad(k, ((0, 0), (0, n_sel * SEL_LEN - L), (0, 0), (0, 0)))
    return kp.reshape(B, n_sel, SEL_LEN, G, dh).transpose(0, 3, 1, 2, 4)


def cmp_to_sel(n_cmp, n_sel):
    cs = jnp.arange(n_cmp)[:, None] * CMP_STRIDE
    ss = jnp.arange(n_sel)[None, :] * SEL_LEN
    return ((cs < ss + SEL_LEN) & (cs + CMP_LEN > ss)).astype(F32)


def nsa_core(q, q_pos, gates, ck, cv, ksb, vsb, kw, vw, kw_pos):
    B, Tq, G, R, dh = q.shape
    n_cmp = ck.shape[1]
    n_sel = ksb.shape[2]
    qf = q.astype(F32) * (HEAD_DIM ** -0.5)
    cmp_end = jnp.arange(n_cmp) * CMP_STRIDE + CMP_LEN - 1
    m_cmp = (cmp_end[None, :] <= q_pos[:, None])[None, :, None, None, :]
    p_cmp = masked_softmax(jnp.einsum('bqgrd,bngd->bqgrn', qf, ck.astype(F32)), m_cmp)
    o_cmp = jnp.einsum('bqgrn,bngd->bqgrd', p_cmp, cv.astype(F32))
    imp = jnp.einsum('bqgrn,ns->bqgs', p_cmp, cmp_to_sel(n_cmp, n_sel))
    blk = jnp.arange(n_sel)[None, :]
    cur = (q_pos // SEL_LEN)[:, None]
    valid = blk * SEL_LEN <= q_pos[:, None]
    forced = (blk == 0) | (blk == cur) | (blk == cur - 1)
    score = jnp.where(valid[None, :, None, :], imp + jnp.where(forced, SEL_BONUS, 0.0)[None, :, None, :], NEG)
    top_val, top_idx = lax.top_k(score, min(SEL_TOPK, n_sel))
    kk = top_idx.shape[-1]
    idx = top_idx.transpose(0, 2, 1, 3)
    sel_ok = (top_val > NEG / 2).transpose(0, 2, 1, 3)
    bi = jnp.arange(B)[:, None, None, None]
    gi = jnp.arange(G)[None, :, None, None]
    kg = ksb[bi, gi, idx]
    vg = vsb[bi, gi, idx]
    tok = idx[..., None] * SEL_LEN + jnp.arange(SEL_LEN)
    m_sel = ((tok <= q_pos[None, None, :, None, None]) & sel_ok[..., None]).reshape(B, G, Tq, 1, kk * SEL_LEN)
    s_sel = jnp.einsum('bqgrd,bgqkld->bgqrkl', qf, kg.astype(F32)).reshape(B, G, Tq, R, kk * SEL_LEN)
    p_sel = masked_softmax(s_sel, m_sel).reshape(B, G, Tq, R, kk, SEL_LEN)
    o_sel = jnp.einsum('bgqrkl,bgqkld->bqgrd', p_sel, vg.astype(F32))
    m_win = ((kw_pos[None, :] <= q_pos[:, None]) & (kw_pos[None, :] >= q_pos[:, None] - WINDOW)
             & (kw_pos[None, :] >= 0))[None, :, None, None, :]
    p_win = masked_softmax(jnp.einsum('bqgrd,bsgd->bqgrs', qf, kw.astype(F32)), m_win)
    o_win = jnp.einsum('bqgrs,bsgd->bqgrd', p_win, vw.astype(F32))
    g = gates.astype(F32)
    return g[..., 0:1] * o_cmp + g[..., 1:2] * o_sel + g[..., 2:3] * o_win


def nsa_prompt(q, gates, kv4, kvw, pe, wc):
    B, T, G, R, dh = q.shape
    ck = compress(kv4[:, :, 0], pe[0], wc[0])
    cv = compress(kv4[:, :, 1], pe[1], wc[1])
    ksb = sel_blocks(kv4[:, :, 2])
    vsb = sel_blocks(kv4[:, :, 3])
    padw = ((0, 0), (WINDOW, 0), (0, 0), (0, 0))
    kwp = jnp.pad(kvw[:, :, 0], padw)
    vwp = jnp.pad(kvw[:, :, 1], padw)
    nb = T // Q_BLOCK
    qb = q.reshape(B, nb, Q_BLOCK, G, R, dh).swapaxes(0, 1)
    gb = gates.reshape(B, nb, Q_BLOCK, G, R, 3).swapaxes(0, 1)

    def block(args):
        qi, gi, i = args
        start = i * Q_BLOCK
        q_pos = start + jnp.arange(Q_BLOCK)
        kw = lax.dynamic_slice_in_dim(kwp, start, WINDOW + Q_BLOCK, axis=1)
        vw = lax.dynamic_slice_in_dim(vwp, start, WINDOW + Q_BLOCK, axis=1)
        kw_pos = start - WINDOW + jnp.arange(WINDOW + Q_BLOCK)
        return nsa_core(qi, q_pos, gi, ck, cv, ksb, vsb, kw, vw, kw_pos)

    ob = lax.map(block, (qb, gb, jnp.arange(nb, dtype=jnp.int32)))
    return ob.swapaxes(0, 1).reshape(B, T, NSA_Q_W)


def nsa_sample(q, gates, kv4, kvw, kv_past, win_buf, pos, pe, wc):
    B, T = q.shape[0], q.shape[1]
    P = kv_past.shape[1]
    W = win_buf.shape[1]
    kv_all = jnp.concatenate([kv_past.astype(kv4.dtype), kv4], axis=1)
    ck = compress(kv_all[:, :, 0], pe[0], wc[0])
    cv = compress(kv_all[:, :, 1], pe[1], wc[1])
    ksb = sel_blocks(kv_all[:, :, 2])
    vsb = sel_blocks(kv_all[:, :, 3])
    kw_all = jnp.concatenate([win_buf.astype(kvw.dtype), kvw], axis=1)
    kw_pos = P - W + jnp.arange(W + T)
    o = nsa_core(q, pos, gates, ck, cv, ksb, vsb, kw_all[:, :, 0], kw_all[:, :, 1], kw_pos)
    new_win = kw_all[:, -min(WINDOW, P + T):]
    return o.reshape(B, T, NSA_Q_W), new_win


def gated_delta(q, k, v, g, beta, S0, C):
    B, T, H, dk = q.shape
    dv = v.shape[-1]
    N = T // C
    qc = q.reshape(B, N, C, H, dk).transpose(1, 0, 3, 2, 4)
    kc = k.reshape(B, N, C, H, dk).transpose(1, 0, 3, 2, 4)
    vc = v.reshape(B, N, C, H, dv).transpose(1, 0, 3, 2, 4)
    gc = g.reshape(B, N, C, H).transpose(1, 0, 3, 2)
    bc = beta.reshape(B, N, C, H).transpose(1, 0, 3, 2)
    gcum = jnp.cumsum(gc, axis=-1)
    ii = jnp.arange(C)[:, None]
    jj = jnp.arange(C)[None, :]
    incl = ii >= jj
    strict = ii > jj
    diff = gcum[..., :, None] - gcum[..., None, :]
    dec = jnp.where(incl, jnp.exp(jnp.where(incl, diff, 0.0)), 0.0)
    kb = kc * bc[..., None]
    Lm = jnp.where(strict, jnp.einsum('nbhid,nbhjd->nbhij', kb, kc) * dec, 0.0)
    A = Lm + jnp.eye(C, dtype=F32)
    rhs = jnp.concatenate([vc * bc[..., None], kb * jnp.exp(gcum)[..., None]], axis=-1)
    sol = lax.linalg.triangular_solve(A, rhs, left_side=True, lower=True)
    u, w = sol[..., :dv], sol[..., dv:]
    aqk = jnp.einsum('nbhid,nbhjd->nbhij', qc, kc) * dec
    qg = qc * jnp.exp(gcum)[..., None]
    kg = kc * jnp.exp(gcum[..., -1:] - gcum)[..., None]
    glast = jnp.exp(gcum[..., -1])

    def step(S, xs):
        u_i, w_i, aqk_i, qg_i, kg_i, gl_i = xs
        vn = u_i - jnp.einsum('bhck,bhkv->bhcv', w_i, S)
        o = jnp.einsum('bhck,bhkv->bhcv', qg_i, S) + jnp.einsum('bhij,bhjv->bhiv', aqk_i, vn)
        S = S * gl_i[..., None, None] + jnp.einsum('bhck,bhcv->bhkv', kg_i, vn)
        return S, o

    S, o = lax.scan(step, S0, (u, w, aqk, qg, kg, glast))
    return o.transpose(1, 0, 3, 2, 4).reshape(B, T, H, dv), S


def gdn_mixer(conv_in, z, a, b, conv_buf, S0, conv_w, a_log, dt_bias, norm_w, chunk):
    B, T, _ = conv_in.shape
    xp = jnp.concatenate([conv_buf.astype(conv_in.dtype), conv_in], axis=1)
    conv = xp[:, 0:T] * conv_w[0]
    for j in range(1, CONV_W):
        conv = conv + xp[:, j:j + T] * conv_w[j]
    c = jax.nn.silu(conv.astype(F32))
    qg, kg, vg = jnp.split(c, [GDN_QK_W, 2 * GDN_QK_W], axis=-1)
    qg = l2norm(qg.reshape(B, T, GDN_HEADS, GDN_DK)) * (GDN_DK ** -0.5)
    kg = l2norm(kg.reshape(B, T, GDN_HEADS, GDN_DK))
    vg = vg.reshape(B, T, GDN_HEADS, GDN_DV)
    beta = jax.nn.sigmoid(b.astype(F32))
    g = -jnp.exp(a_log.astype(F32)) * jax.nn.softplus(a.astype(F32) + dt_bias.astype(F32))
    o, S = gated_delta(qg, kg, vg, g, beta, S0.astype(F32), chunk)
    o = o * lax.rsqrt(jnp.mean(o * o, axis=-1, keepdims=True) + EPS) * norm_w.astype(F32)
    o = o * jax.nn.silu(z.astype(F32).reshape(B, T, GDN_HEADS, GDN_DV))
    return o.reshape(B, T, GDN_V_W), S, xp[:, -(CONV_W - 1):]


def channel_mix(h, g, w_gate_up, w_down):
    gate, up = jnp.split(rms_norm(h, g) @ w_gate_up, 2, axis=-1)
    return h + (jax.nn.silu(gate) * up) @ w_down


def setup_inputs(seed: int = 0) -> dict:
    key = jax.random.key(seed)
    ks = jax.random.split(key, 20)
    n_pages = PAST_LEN // PAGE_SIZE
    n_pool = DEC_BATCH * n_pages * 5 // 4
    w_buf = min(WINDOW, PAST_LEN)
    G, dh = NSA_KV_HEADS, HEAD_DIM
    nrm = jax.random.normal
    page_table = jax.random.permutation(ks[0], n_pool)[:DEC_BATCH * n_pages].reshape(DEC_BATCH, n_pages).astype(jnp.int32)
    return {
        'x_prompt': nrm(ks[1], (BATCH, SEQ, D_MODEL), F32),
        'x_sample': nrm(ks[2], (DEC_BATCH, DEC_SEQ, D_MODEL), F32),
        'cache_nsa_kv': nrm(ks[3], (DEPTH, n_pool, PAGE_SIZE, 4, G, dh), F32),
        'cache_nsa_win': nrm(ks[4], (DEPTH, DEC_BATCH, w_buf, 2, G, dh), F32),
        'state_gdn_S': 0.1 * nrm(ks[5], (DEPTH, DEC_BATCH, GDN_HEADS, GDN_DK, GDN_DV), F32),
        'state_gdn_conv': nrm(ks[6], (DEPTH, DEC_BATCH, CONV_W - 1, GDN_CONV_CH), F32),
        'page_table': page_table,
        'norm_mix': 1.0 + 0.02 * nrm(ks[7], (DEPTH, D_MODEL), F32),
        'w_in': nrm(ks[8], (DEPTH, D_MODEL, IN_W), F32) * D_MODEL ** -0.5,
        'nsa_cmp_pe': 0.1 * nrm(ks[9], (DEPTH, 2, CMP_LEN, dh), F32),
        'nsa_cmp_w': nrm(ks[10], (DEPTH, 2, CMP_LEN, dh, dh), F32) * (CMP_LEN * dh) ** -0.5,
        'gdn_conv_w': 0.5 * nrm(ks[11], (DEPTH, CONV_W, GDN_CONV_CH), F32),
        'gdn_a_log': jnp.log(jax.random.uniform(ks[12], (DEPTH, GDN_HEADS), F32, 1.0, 16.0)),
        'gdn_dt_bias': jnp.log(jnp.expm1(jax.random.uniform(ks[13], (DEPTH, GDN_HEADS), F32, 0.001, 0.1))),
        'gdn_norm': 1.0 + 0.02 * nrm(ks[14], (DEPTH, GDN_DV), F32),
        'w_out': nrm(ks[15], (DEPTH, MIX_W, D_MODEL), F32) * MIX_W ** -0.5,
        'norm_ffn': 1.0 + 0.02 * nrm(ks[16], (DEPTH, D_MODEL), F32),
        'w_gate_up': nrm(ks[17], (DEPTH, D_MODEL, 2 * D_FF), F32) * D_MODEL ** -0.5,
        'w_down': nrm(ks[18], (DEPTH, D_FF, D_MODEL), F32) * D_FF ** -0.5,
        'norm_final': 1.0 + 0.02 * nrm(ks[19], (D_MODEL,), F32),
    }


def reference(x_prompt, x_sample, cache_nsa_kv, cache_nsa_win, state_gdn_S, state_gdn_conv, page_table,
              norm_mix, w_in, nsa_cmp_pe, nsa_cmp_w, gdn_conv_w, gdn_a_log, gdn_dt_bias, gdn_norm,
              w_out, norm_ffn, w_gate_up, w_down, norm_final):
    Bp, Tp, _ = x_prompt.shape
    Bs, Ts, _ = x_sample.shape
    n_pages = page_table.shape[1]
    past_len = n_pages * PAGE_SIZE
    pos_p = jnp.arange(Tp, dtype=jnp.int32)
    pos_s = past_len + jnp.arange(Ts, dtype=jnp.int32)
    hp, hs = x_prompt, x_sample
    kv_p, win_p, S_p, conv_p = [], [], [], []
    kv_s, win_s, S_s, conv_s = [], [], [], []
    for l in range(DEPTH):
        proj = rms_norm(hp, norm_mix[l]) @ w_in[l]
        q, gates, kv4, kvw, conv_in, z, a, b = split_proj(proj, pos_p)
        o_nsa = nsa_prompt(q, gates, kv4, kvw, nsa_cmp_pe[l], nsa_cmp_w[l])
        buf0 = jnp.zeros((Bp, CONV_W - 1, GDN_CONV_CH), proj.dtype)
        S0 = jnp.zeros((Bp, GDN_HEADS, GDN_DK, GDN_DV), F32)
        o_gdn, S_new, conv_new = gdn_mixer(conv_in, z, a, b, buf0, S0, gdn_conv_w[l], gdn_a_log[l],
                                           gdn_dt_bias[l], gdn_norm[l], min(GDN_CHUNK, Tp))
        hp = hp + jnp.concatenate([o_nsa, o_gdn], axis=-1).astype(hp.dtype) @ w_out[l]
        hp = channel_mix(hp, norm_ffn[l], w_gate_up[l], w_down[l])
        kv_p.append(kv4)
        win_p.append(kvw[:, -min(WINDOW, Tp):])
        S_p.append(S_new)
        conv_p.append(conv_new)
        proj = rms_norm(hs, norm_mix[l]) @ w_in[l]
        q, gates, kv4, kvw, conv_in, z, a, b = split_proj(proj, pos_s)
        kv_past = cache_nsa_kv[l][page_table].reshape(Bs, past_len, 4, NSA_KV_HEADS, HEAD_DIM)
        o_nsa, win_new = nsa_sample(q, gates, kv4, kvw, kv_past, cache_nsa_win[l], pos_s,
                                    nsa_cmp_pe[l], nsa_cmp_w[l])
        o_gdn, S_new, conv_new = gdn_mixer(conv_in, z, a, b, state_gdn_conv[l], state_gdn_S[l],
                                           gdn_conv_w[l], gdn_a_log[l], gdn_dt_bias[l], gdn_norm[l], Ts)
        hs = hs + jnp.concatenate([o_nsa, o_gdn], axis=-1).astype(hs.dtype) @ w_out[l]
        hs = channel_mix(hs, norm_ffn[l], w_gate_up[l], w_down[l])
        kv_s.append(kv4)
        win_s.append(win_new)
        S_s.append(S_new)
        conv_s.append(conv_new)
    y_prompt = rms_norm(hp, norm_final)
    y_sample = rms_norm(hs, norm_final)
    return (y_prompt, y_sample,
            jnp.stack(kv_p), jnp.stack(win_p), jnp.stack(S_p), jnp.stack(conv_p),
            jnp.stack(kv_s), jnp.stack(win_s), jnp.stack(S_s), jnp.stack(conv_s))
```

```python
import functools

import numpy as np
import jax
import jax.numpy as jnp
from jax import lax
from jax.experimental import pallas as pl
from jax.experimental.pallas import tpu as pltpu

F32 = jnp.float32
BF16 = jnp.bfloat16

D_MODEL = 1024
PAGE_SIZE = 128
HEAD_DIM = 64
NSA_HEADS = 8
NSA_KV_HEADS = 2
NSA_REP = NSA_HEADS // NSA_KV_HEADS
CMP_LEN = 32
CMP_STRIDE = 16
SEL_LEN = 64
SEL_TOPK = 16
WINDOW = 512
Q_BLOCK = 128
ROPE_THETA = 10000.0
GDN_HEADS = 4
GDN_DK = 128
GDN_DV = 128
CONV_W = 4
NSA_Q_W = NSA_HEADS * HEAD_DIM
NSA_KV_W = NSA_KV_HEADS * HEAD_DIM
GDN_QK_W = GDN_HEADS * GDN_DK
GDN_V_W = GDN_HEADS * GDN_DV
GDN_CONV_CH = 2 * GDN_QK_W + GDN_V_W
NEG = -1e30
SEL_BONUS = 1e4
EPS = 1e-6

LANES = 128
SUBLANES = 8
VMEM_LIMIT = 56 * 1024 * 1024

GATE_COLS = 3 * NSA_HEADS
A_COL = GATE_COLS
B_COL = GATE_COLS + GDN_HEADS

NT_DIMS = (((1,), (1,)), ((), ()))
TN_DIMS = (((0,), (0,)), ((), ()))


def _dot(a, b):
    return jnp.dot(a, b, preferred_element_type=F32)


def _dot_nt(a, b):
    return lax.dot_general(a, b, NT_DIMS, preferred_element_type=F32)


def _sigmoid(x):
    return 1.0 / (1.0 + jnp.exp(-x))


def _silu(x):
    return x * _sigmoid(x)


def _split3(x):
    p1 = x.astype(BF16)
    r1 = x - p1.astype(F32)
    p2 = r1.astype(BF16)
    p3 = (r1 - p2.astype(F32)).astype(BF16)
    return p1, p2, p3


def _const_spec(shape):
    nd = len(shape)
    return pl.BlockSpec(shape, lambda *_: (0,) * nd, pipeline_mode=pl.Buffered(1))


def _proj_kernel(x_ref, g_ref, wm_ref, ws_ref, cos_ref, sin_ref,
                 q_ref, kv_ref, conv_ref, z_ref, small_ref, kaug_ref, vsel_ref, kwin_ref, vwin_ref,
                 *, tm, pos_rows):
    x = x_ref[...]
    ms = jnp.mean(x * x, axis=-1, keepdims=True)
    xn = (x * lax.rsqrt(ms + EPS) * g_ref[...]).astype(BF16)
    cos = cos_ref[...]
    sin = sin_ref[...]
    lane = lax.broadcasted_iota(jnp.int32, (tm, LANES), 1)
    low_half = (lane % HEAD_DIM) < (HEAD_DIM // 2)

    def rope(v):
        rot = jnp.where(low_half, pltpu.roll(v, LANES - HEAD_DIM // 2, 1), pltpu.roll(v, HEAD_DIM // 2, 1))
        return v * cos + rot * sin

    q = _dot(xn, wm_ref[:, 0:NSA_Q_W])
    for r in range(NSA_REP):
        sl = slice(r * LANES, (r + 1) * LANES)
        q_ref[:, sl] = (rope(q[:, sl]) * (HEAD_DIM ** -0.5)).astype(BF16)

    kv = _dot(xn, wm_ref[:, NSA_Q_W:NSA_Q_W + 6 * NSA_KV_W])
    bf_outs = {2: None, 3: vsel_ref, 4: kwin_ref, 5: vwin_ref}
    for c in range(6):
        sl = slice(c * LANES, (c + 1) * LANES)
        blk = kv[:, sl]
        if c % 2 == 0:
            blk = rope(blk)
        kv_ref[:, sl] = blk
        if c == 2:
            kaug_ref[:, LANES:2 * LANES] = blk.astype(BF16)
        elif c in bf_outs:
            bf_outs[c][...] = blk.astype(BF16)

    row0 = (pl.program_id(0) * tm) % pos_rows
    rows = row0 + lax.broadcasted_iota(jnp.int32, (tm, LANES), 0)
    kaug_ref[:, 0:LANES] = jnp.where(rows // SEL_LEN == lane, 1.0, 0.0).astype(BF16)

    c0 = NSA_Q_W + 6 * NSA_KV_W
    conv_ref[...] = _dot(xn, wm_ref[:, c0:c0 + GDN_CONV_CH])
    z_ref[...] = _dot(xn, wm_ref[:, c0 + GDN_CONV_CH:c0 + GDN_CONV_CH + GDN_V_W])
    small_ref[...] = _dot(xn, ws_ref[...])


def _proj(x, g, wm, ws, cos, sin, *, tm):
    rows = x.shape[0]
    pos_rows = cos.shape[0]
    n_pos_blk = pos_rows // tm
    grid = (rows // tm,)
    row_spec = lambda w: pl.BlockSpec((tm, w), lambda i: (i, 0))
    pos_spec = pl.BlockSpec((tm, LANES), lambda i: (i % n_pos_blk, 0))
    out_shape = (
        jax.ShapeDtypeStruct((rows, NSA_Q_W), BF16),
        jax.ShapeDtypeStruct((rows, 6 * NSA_KV_W), F32),
        jax.ShapeDtypeStruct((rows, GDN_CONV_CH), F32),
        jax.ShapeDtypeStruct((rows, GDN_V_W), F32),
        jax.ShapeDtypeStruct((rows, LANES), F32),
        jax.ShapeDtypeStruct((rows, 2 * LANES), BF16),
        jax.ShapeDtypeStruct((rows, LANES), BF16),
        jax.ShapeDtypeStruct((rows, LANES), BF16),
        jax.ShapeDtypeStruct((rows, LANES), BF16),
    )
    return pl.pallas_call(
        functools.partial(_proj_kernel, tm=tm, pos_rows=pos_rows),
        grid=grid,
        in_specs=[row_spec(D_MODEL), _const_spec((1, D_MODEL)), _const_spec(wm.shape), _const_spec(ws.shape),
                  pos_spec, pos_spec],
        out_specs=tuple(row_spec(s.shape[1]) for s in out_shape),
        out_shape=out_shape,
        compiler_params=pltpu.CompilerParams(dimension_semantics=("arbitrary",), vmem_limit_bytes=VMEM_LIMIT),
        name="proj",
    )(x, g, wm, ws, cos, sin)


def _mix_ffn_kernel(x_ref, on_ref, og_ref, wo_ref, gf_ref, wgu_ref, wd_ref, gl_ref, y_ref, *, d_ff):
    h = x_ref[...] + _dot(on_ref[...], wo_ref[0:NSA_Q_W, :]) + _dot(og_ref[...], wo_ref[NSA_Q_W:, :])
    ms = jnp.mean(h * h, axis=-1, keepdims=True)
    hn = (h * lax.rsqrt(ms + EPS) * gf_ref[...]).astype(BF16)
    gate = _dot(hn, wgu_ref[:, 0:d_ff])
    up = _dot(hn, wgu_ref[:, d_ff:])
    act = (_silu(gate) * up).astype(BF16)
    h = h + _dot(act, wd_ref[...])
    ms = jnp.mean(h * h, axis=-1, keepdims=True)
    y_ref[...] = h * lax.rsqrt(ms + EPS) * gl_ref[...]


def _mix_ffn(x, o_nsa, o_gdn, wo, gf, wgu, wd, gl, *, tm):
    rows = x.shape[0]
    d_ff = wd.shape[0]
    row_spec = lambda w: pl.BlockSpec((tm, w), lambda i: (i, 0))
    return pl.pallas_call(
        functools.partial(_mix_ffn_kernel, d_ff=d_ff),
        grid=(rows // tm,),
        in_specs=[row_spec(D_MODEL), row_spec(NSA_Q_W), row_spec(GDN_V_W), _const_spec(wo.shape),
                  _const_spec((1, D_MODEL)), _const_spec(wgu.shape), _const_spec(wd.shape),
                  _const_spec((1, D_MODEL))],
        out_specs=row_spec(D_MODEL),
        out_shape=jax.ShapeDtypeStruct((rows, D_MODEL), F32),
        compiler_params=pltpu.CompilerParams(dimension_semantics=("arbitrary",), vmem_limit_bytes=VMEM_LIMIT),
        name="mix_ffn",
    )(x, o_nsa, o_gdn, wo, gf, wgu, wd, gl)


def _compress_rows(load_rows, wlo_ref, whi_ref, pelo_ref, pehi_ref, nc):
    acc_lo = jnp.zeros((nc, 2 * LANES), F32)
    acc_hi = jnp.zeros((nc, 2 * LANES), F32)
    for l in range(CMP_STRIDE):
        x = load_rows(l)
        acc_lo = acc_lo + _dot((x + pelo_ref[l:l + 1, :]).astype(BF16), wlo_ref[l])
        acc_hi = acc_hi + _dot((x + pehi_ref[l:l + 1, :]).astype(BF16), whi_ref[l])
    nxt = pltpu.roll(acc_hi, nc - 1, 0)
    row = lax.broadcasted_iota(jnp.int32, (nc, 2 * LANES), 0)
    return jnp.where(row < nc - 1, acc_lo + nxt, 0.0)


def _compress_kernel(k_ref, v_ref, wlo_ref, whi_ref, pelo_ref, pehi_ref, ckv_ref, *, nc):
    load = lambda l: jnp.concatenate([k_ref[pl.ds(l, nc, stride=CMP_STRIDE), :],
                                      v_ref[pl.ds(l, nc, stride=CMP_STRIDE), :]], axis=1)
    ckv_ref[...] = _compress_rows(load, wlo_ref, whi_ref, pelo_ref, pehi_ref, nc).astype(BF16)


def _compress(kv, wlo, whi, pelo, pehi, *, batch, seq):
    nc = seq // CMP_STRIDE
    return pl.pallas_call(
        functools.partial(_compress_kernel, nc=nc),
        grid=(batch,),
        in_specs=[pl.BlockSpec((seq, LANES), lambda b: (b, 0)), pl.BlockSpec((seq, LANES), lambda b: (b, 1)),
                  _const_spec(wlo.shape), _const_spec(whi.shape), _const_spec(pelo.shape), _const_spec(pehi.shape)],
        out_specs=pl.BlockSpec((nc, 2 * LANES), lambda b: (b, 0)),
        out_shape=jax.ShapeDtypeStruct((batch * nc, 2 * LANES), BF16),
        compiler_params=pltpu.CompilerParams(dimension_semantics=("arbitrary",), vmem_limit_bytes=VMEM_LIMIT),
        name="compress",
    )(kv, kv, wlo, whi, pelo, pehi)


def _cmp_softmax(s, valid):
    s = jnp.where(valid, s, NEG)
    m = jnp.max(s, axis=-1, keepdims=True)
    e = jnp.exp(s - m)
    return jnp.where(valid, e / jnp.sum(e, axis=-1, keepdims=True), 0.0)


def _select_blocks(score, blk, axis):
    sel = jnp.zeros(score.shape, F32)
    n = score.shape[axis]
    for _ in range(SEL_TOPK):
        mx = jnp.max(score, axis=axis, keepdims=True)
        idx = jnp.min(jnp.where(score == mx, blk, n), axis=axis, keepdims=True)
        hit = blk == idx
        sel = jnp.where(hit & (mx > NEG / 2), 1.0, sel)
        score = jnp.where(hit, -3e38, score)
    return sel


def _nsa_prompt_kernel(q_ref, small_ref, kaug_ref, vsel_ref, kwin_ref, vwin_ref, ckv_ref, mt_ref, o_ref,
                       *, seq, kc):
    nc = seq // CMP_STRIDE
    n_cmp = nc - 1
    qb = Q_BLOCK
    rows = NSA_REP * qb
    start = pl.program_id(1) * qb
    wlen = WINDOW + qb

    lane = lax.broadcasted_iota(jnp.int32, (qb, LANES), 1)
    qpos_r = start + lax.broadcasted_iota(jnp.int32, (rows, 1), 0) % qb
    gates = _sigmoid(small_ref[...])

    ncol = lax.broadcasted_iota(jnp.int32, (1, nc), 1)
    cmp_valid = (ncol * CMP_STRIDE + CMP_LEN - 1 <= qpos_r) & (ncol < n_cmp)

    blk_t = lax.broadcasted_iota(jnp.int32, (LANES, qb), 0)
    qpos_t = start + lax.broadcasted_iota(jnp.int32, (LANES, qb), 1)
    cur_t = qpos_t // SEL_LEN
    blk_valid = blk_t * SEL_LEN <= qpos_t
    forced = (blk_t == 0) | (blk_t == cur_t) | (blk_t == cur_t - 1)

    wbase = pl.multiple_of(jnp.maximum(start - WINDOW, 0), qb)
    wpos = wbase + lax.broadcasted_iota(jnp.int32, (1, wlen), 1)
    win_valid = (wpos <= qpos_r) & (wpos >= qpos_r - WINDOW)

    n_chunks = (start + qb + kc - 1) // kc
    kcol = lax.broadcasted_iota(jnp.int32, (1, kc), 1)

    outs = [None] * NSA_REP
    for g in range(NSA_KV_HEADS):
        mine = (lane >= HEAD_DIM) == (g == 1)
        q = jnp.concatenate(
            [jnp.where(mine, q_ref[:, r * LANES:(r + 1) * LANES], 0.0).astype(BF16) for r in range(NSA_REP)], axis=0)

        p = _cmp_softmax(_dot_nt(q, ckv_ref[:, 0:LANES]), cmp_valid)
        o_cmp = _dot(p.astype(BF16), ckv_ref[:, LANES:2 * LANES])

        psum = p[0:qb] + p[qb:2 * qb] + p[2 * qb:3 * qb] + p[3 * qb:4 * qb]
        imp_t = sum(_dot_nt(mt_ref[...], piece) for piece in _split3(psum))
        score = jnp.where(blk_valid, imp_t + jnp.where(forced, SEL_BONUS, 0.0), NEG)
        sel_t = _select_blocks(score, blk_t, 0)
        bias = jnp.where(sel_t.T > 0.5, 0.0, NEG).astype(BF16)
        q_aug = jnp.concatenate([jnp.concatenate([bias] * NSA_REP, axis=0), q], axis=1)

        def sel_step(c, carry):
            m, l, acc = carry
            off = pl.multiple_of(c * kc, kc)
            s = _dot_nt(q_aug, kaug_ref[pl.ds(off, kc), :])
            s = jnp.where(off + kcol <= qpos_r, s, NEG)
            m_new = jnp.maximum(m, jnp.max(s, axis=-1, keepdims=True))
            alpha = jnp.exp(m - m_new)
            e = jnp.exp(s - m_new)
            l = alpha * l + jnp.sum(e, axis=-1, keepdims=True)
            acc = alpha * acc + _dot(e.astype(BF16), vsel_ref[pl.ds(off, kc), :])
            return m_new, l, acc

        init = (jnp.full((rows, 1), NEG, F32), jnp.zeros((rows, 1), F32), jnp.zeros((rows, LANES), F32))
        _, l_sel, acc_sel = lax.fori_loop(0, n_chunks, sel_step, init)
        o_sel = acc_sel / l_sel

        s = jnp.where(win_valid, _dot_nt(q, kwin_ref[pl.ds(wbase, wlen), :]), NEG)
        e = jnp.exp(s - jnp.max(s, axis=-1, keepdims=True))
        pw = e / jnp.sum(e, axis=-1, keepdims=True)
        o_win = _dot(pw.astype(BF16), vwin_ref[pl.ds(wbase, wlen), :])

        for r in range(NSA_REP):
            c0 = (g * NSA_REP + r) * 3
            rs = slice(r * qb, (r + 1) * qb)
            o = (gates[:, c0:c0 + 1] * o_cmp[rs] + gates[:, c0 + 1:c0 + 2] * o_sel[rs]
                 + gates[:, c0 + 2:c0 + 3] * o_win[rs])
            outs[r] = o if g == 0 else jnp.where(lane < HEAD_DIM, outs[r], o)

    for r in range(NSA_REP):
        o_ref[:, r * LANES:(r + 1) * LANES] = outs[r].astype(BF16)


def _nsa_prompt(q, small, kaug, vsel, kwin, vwin, ckv, mt, *, batch, seq, kc):
    nb = seq // Q_BLOCK
    nc = seq // CMP_STRIDE
    blk_spec = lambda w: pl.BlockSpec((Q_BLOCK, w), lambda b, i: (b * nb + i, 0))
    seq_spec = lambda w: pl.BlockSpec((seq, w), lambda b, i: (b, 0))
    return pl.pallas_call(
        functools.partial(_nsa_prompt_kernel, seq=seq, kc=kc),
        grid=(batch, nb),
        in_specs=[blk_spec(NSA_Q_W), blk_spec(LANES), seq_spec(2 * LANES), seq_spec(LANES), seq_spec(LANES),
                  seq_spec(LANES), pl.BlockSpec((nc, 2 * LANES), lambda b, i: (b, 0)), _const_spec(mt.shape)],
        out_specs=blk_spec(NSA_Q_W),
        out_shape=jax.ShapeDtypeStruct((batch * seq, NSA_Q_W), BF16),
        compiler_params=pltpu.CompilerParams(dimension_semantics=("arbitrary", "arbitrary"),
                                             vmem_limit_bytes=VMEM_LIMIT),
        name="nsa_prompt",
    )(q, small, kaug, vsel, kwin, vwin, ckv, mt)


def _nsa_sample_kernel(pt_ref, *refs, past, pps):
    del pt_ref
    pages = refs[:pps]
    (q_ref, kvn_ref, small_ref, win_ref, wlo_ref, whi_ref, pelo_ref, pehi_ref, mt_ref, exp_ref,
     o_ref, past_ref) = refs[pps:]
    step = pl.program_id(1)
    for k in range(pps):
        for c in range(4):
            past_ref[c, pl.ds((step * pps + k) * PAGE_SIZE, PAGE_SIZE), :] = pages[k][:, c * LANES:(c + 1) * LANES]

    @pl.when(step == pl.num_programs(1) - 1)
    def _():
        nc = past // CMP_STRIDE
        n_cmp = nc - 1
        nsp = mt_ref.shape[0]
        cur = past // SEL_LEN
        heads = NSA_HEADS
        row = lax.broadcasted_iota(jnp.int32, (heads, LANES), 0)
        lane = lax.broadcasted_iota(jnp.int32, (heads, LANES), 1)
        mine = (lane >= HEAD_DIM) == (row >= NSA_REP)
        rr = row % NSA_REP

        load = lambda l: jnp.concatenate([past_ref[0, pl.ds(l, nc, stride=CMP_STRIDE), :],
                                          past_ref[1, pl.ds(l, nc, stride=CMP_STRIDE), :]], axis=1)
        ckv = _compress_rows(load, wlo_ref, whi_ref, pelo_ref, pehi_ref, nc).astype(BF16)

        qsel = jnp.zeros((heads, LANES), F32)
        for r in range(NSA_REP):
            qr = q_ref[:, r * LANES:(r + 1) * LANES].astype(F32)
            qsel = jnp.where(rr == r, qr, qsel)
        qf = jnp.where(mine, qsel, 0.0)
        q = qf.astype(BF16)

        ncol = lax.broadcasted_iota(jnp.int32, (1, nc), 1)
        cmp_valid = (ncol * CMP_STRIDE + CMP_LEN - 1 <= past) & (ncol < n_cmp)
        p = _cmp_softmax(_dot_nt(q, ckv[:, 0:LANES]), cmp_valid)
        o_cmp = _dot(p.astype(BF16), ckv[:, LANES:2 * LANES])

        prow = lax.broadcasted_iota(jnp.int32, (heads, nc), 0)
        ps = [jnp.sum(jnp.where(prow // NSA_REP == g, p, 0.0), axis=0, keepdims=True) for g in range(NSA_KV_HEADS)]
        psum = jnp.where(prow == 0, ps[0], jnp.where(prow == 1, ps[1], 0.0))
        imp = sum(_dot_nt(piece, mt_ref[...]) for piece in _split3(psum))
        blk = lax.broadcasted_iota(jnp.int32, (heads, nsp), 1)
        forced = (blk == 0) | (blk == cur) | (blk == cur - 1)
        score = jnp.where(blk * SEL_LEN <= past, imp + jnp.where(forced, SEL_BONUS, 0.0), NEG)
        sel = _select_blocks(score, blk, 1)
        srow = lax.broadcasted_iota(jnp.int32, (heads, nsp), 0)
        sel8 = jnp.where(srow < NSA_REP, sel[0:1], sel[1:2])

        kvn = kvn_ref[...]
        keymask = _dot(sel8.astype(BF16), exp_ref[...]) > 0.5
        s = jnp.where(keymask, _dot_nt(q, past_ref[2].astype(BF16)), NEG)
        new_ok = sel8[:, cur:cur + 1] > 0.5
        s_new = jnp.where(new_ok, jnp.sum(qf * kvn[:, 2 * LANES:3 * LANES], axis=-1, keepdims=True), NEG)
        m = jnp.maximum(jnp.max(s, axis=-1, keepdims=True), s_new)
        e = jnp.where(keymask, jnp.exp(s - m), 0.0)
        e_new = jnp.where(new_ok, jnp.exp(s_new - m), 0.0)
        l = jnp.sum(e, axis=-1, keepdims=True) + e_new
        o_sel = (_dot(e.astype(BF16), past_ref[3].astype(BF16))
                 + e_new * kvn[:, 3 * LANES:4 * LANES]) / l

        s = _dot_nt(q, win_ref[:, 0:LANES].astype(BF16))
        s_new = jnp.sum(qf * kvn[:, 4 * LANES:5 * LANES], axis=-1, keepdims=True)
        m = jnp.maximum(jnp.max(s, axis=-1, keepdims=True), s_new)
        e = jnp.exp(s - m)
        e_new = jnp.exp(s_new - m)
        l = jnp.sum(e, axis=-1, keepdims=True) + e_new
        o_win = (_dot(e.astype(BF16), win_ref[:, LANES:2 * LANES].astype(BF16))
                 + e_new * kvn[:, 5 * LANES:6 * LANES]) / l

        gates = _sigmoid(small_ref[...])
        gate = lambda br: jnp.sum(jnp.where(lane == row * 3 + br, gates, 0.0), axis=-1, keepdims=True)
        o = gate(0) * o_cmp + gate(1) * o_sel + gate(2) * o_win
        for r in range(NSA_REP):
            o_ref[:, r * LANES:(r + 1) * LANES] = jnp.where(
                lane[0:1] < HEAD_DIM, o[r:r + 1], o[NSA_REP + r:NSA_REP + r + 1]).astype(BF16)


def _nsa_sample(page_table, cache, q, kvn, small, win, wlo, whi, pelo, pehi, mt, expand, *, pps):
    batch, n_pages = page_table.shape
    past = n_pages * PAGE_SIZE
    row_w = cache.shape[-1]

    def page_spec(k):
        return pl.BlockSpec((None, PAGE_SIZE, row_w), lambda b, s, pt: (pt[b * n_pages + s * pps + k], 0, 0))

    per_b = lambda shape: pl.BlockSpec((None,) + shape, lambda b, s, pt: (b, 0, 0))
    const = lambda a: pl.BlockSpec(a.shape, lambda b, s, pt: (0,) * a.ndim, pipeline_mode=pl.Buffered(1))
    grid_spec = pltpu.PrefetchScalarGridSpec(
        num_scalar_prefetch=1,
        grid=(batch, n_pages // pps),
        in_specs=[page_spec(k) for k in range(pps)] + [
            per_b((1, NSA_Q_W)), per_b((1, 6 * NSA_KV_W)), per_b((1, LANES)), per_b(win.shape[1:]),
            const(wlo), const(whi), const(pelo), const(pehi), const(mt), const(expand)],
        out_specs=per_b((1, NSA_Q_W)),
        scratch_shapes=[pltpu.VMEM((row_w // LANES, past, LANES), F32)],
    )
    return pl.pallas_call(
        functools.partial(_nsa_sample_kernel, past=past, pps=pps),
        grid_spec=grid_spec,
        out_shape=jax.ShapeDtypeStruct((batch, 1, NSA_Q_W), BF16),
        compiler_params=pltpu.CompilerParams(dimension_semantics=("arbitrary", "arbitrary"),
                                             vmem_limit_bytes=VMEM_LIMIT),
        name="nsa_sample",
    )(page_table.reshape(-1), *([cache] * pps), q, kvn, small, win, wlo, whi, pelo, pehi, mt, expand)


def _l2norm(x):
    return x * lax.rsqrt(jnp.sum(x * x, axis=-1, keepdims=True) + EPS)


def _softplus(x):
    return jnp.maximum(x, 0.0) + jnp.log(1.0 + jnp.exp(-jnp.abs(x)))


def _gdn_kernel(conv_ref, z_ref, small_ref, smallt_ref, buf_ref, s0_ref, cw_ref, pcol_ref, prow_ref, nw_ref,
                tril_ref, triu_ref, o_ref, sout_ref, xs_ref, s_ref, *, tb, chunk, t_real):
    j = pl.program_id(1)
    pad = SUBLANES
    hist = CONV_W - 1

    @pl.when(j == 0)
    def _():
        xs_ref[0:pad, :] = jnp.zeros((pad, GDN_CONV_CH), F32)
        xs_ref[pad - hist:pad, :] = buf_ref[...]
        s_ref[...] = s0_ref[...]

    xs_ref[pad:pad + tb, :] = conv_ref[...]
    conv = xs_ref[pad - hist:pad - hist + tb, :] * cw_ref[0:1, :]
    for t in range(1, CONV_W):
        conv = conv + xs_ref[pad - hist + t:pad - hist + t + tb, :] * cw_ref[t:t + 1, :]
    xs_ref[pad - hist:pad, :] = xs_ref[pad + tb - hist:pad + tb, :]
    act = _silu(conv)

    small = small_ref[...]
    tok_c = j * tb + lax.broadcasted_iota(jnp.int32, (tb, LANES), 0)
    g_col = jnp.where(tok_c < t_real, -jnp.exp(pcol_ref[0:1, :]) * _softplus(small + pcol_ref[1:2, :]), 0.0)
    beta = jnp.where(tok_c < t_real, _sigmoid(small), 0.0)
    tok_r = j * tb + lax.broadcasted_iota(jnp.int32, (SUBLANES, tb), 1)
    g_row = jnp.where(tok_r < t_real,
                      -jnp.exp(prow_ref[:, 0:1]) * _softplus(smallt_ref[...] + prow_ref[:, 1:2]), 0.0)
    gcum_col = sum(_dot(tril_ref[...], piece) for piece in _split3(g_col))
    gcum_row = sum(_dot(piece, triu_ref[...]) for piece in _split3(g_row))

    ii = lax.broadcasted_iota(jnp.int32, (chunk, chunk), 0)
    jj = lax.broadcasted_iota(jnp.int32, (chunk, chunk), 1)
    incl = ii >= jj
    eye = jnp.where(ii == jj, 1.0, 0.0)

    for ci in range(tb // chunk):
        r0 = ci * chunk
        rs = slice(r0, r0 + chunk)
        for h in range(GDN_HEADS):
            hs = slice(h * GDN_DK, (h + 1) * GDN_DK)
            gc = gcum_col[rs, A_COL + h:A_COL + h + 1]
            gr = gcum_row[h:h + 1, r0:r0 + chunk]
            glast = gcum_col[r0 + chunk - 1:r0 + chunk, A_COL + h:A_COL + h + 1]
            bt = beta[rs, B_COL + h:B_COL + h + 1]
            dec = jnp.where(incl, jnp.exp(jnp.where(incl, gc - gr, 0.0)), 0.0)
            qh = _l2norm(act[rs, hs]) * (GDN_DK ** -0.5)
            kh = _l2norm(act[rs, GDN_QK_W + h * GDN_DK:GDN_QK_W + (h + 1) * GDN_DK])
            vh = act[rs, 2 * GDN_QK_W + h * GDN_DV:2 * GDN_QK_W + (h + 1) * GDN_DV]
            kb = kh * bt
            k16 = kh.astype(BF16)
            lm = jnp.where(ii > jj, _dot_nt(kb.astype(BF16), k16) * dec, 0.0)
            ainv = eye - lm
            pw = lm
            n = 2
            while n < chunk:
                pw16 = pw.astype(BF16)
                pw = _dot(pw16, pw16)
                ainv = ainv + _dot(ainv.astype(BF16), pw.astype(BF16))
                n *= 2
            eg = jnp.exp(gc)
            rhs = jnp.concatenate([vh * bt, kb * eg], axis=1).astype(BF16)
            sol = _dot(ainv.astype(BF16), rhs)
            u = sol[:, 0:GDN_DV]
            w = sol[:, GDN_DV:]
            aqk = _dot_nt(qh.astype(BF16), k16) * dec
            s16 = s_ref[h].astype(BF16)
            vn = u - _dot(w.astype(BF16), s16)
            vn16 = vn.astype(BF16)
            o = _dot((qh * eg).astype(BF16), s16) + _dot(aqk.astype(BF16), vn16)
            kg = kh * jnp.exp(glast - gc)
            s_ref[h] = s_ref[h] * jnp.exp(glast) + lax.dot_general(kg.astype(BF16), vn16, TN_DIMS,
                                                                    preferred_element_type=F32)
            on = o * lax.rsqrt(jnp.mean(o * o, axis=-1, keepdims=True) + EPS) * nw_ref[...]
            o_ref[rs, hs] = (on * _silu(z_ref[rs, hs])).astype(BF16)

    sout_ref[...] = s_ref[...]


def _gdn(conv_in, z, small, smallt, buf, s0, cw, pcol, prow, nw, *, batch, seq, tb, chunk, t_real):
    nblk = seq // tb
    blk = np.arange(tb)
    same = (blk[:, None] // chunk) == (blk[None, :] // chunk)
    tril = jnp.asarray(same & (blk[:, None] >= blk[None, :]), BF16)
    triu = jnp.asarray(same & (blk[:, None] <= blk[None, :]), BF16)
    row_spec = lambda w: pl.BlockSpec((tb, w), lambda b, j: (b * nblk + j, 0))
    per_b = lambda shape: pl.BlockSpec((None,) + shape, lambda b, j: (b,) + (0,) * len(shape))
    return pl.pallas_call(
        functools.partial(_gdn_kernel, tb=tb, chunk=chunk, t_real=t_real),
        grid=(batch, nblk),
        in_specs=[row_spec(GDN_CONV_CH), row_spec(GDN_V_W), row_spec(LANES),
                  pl.BlockSpec((None, SUBLANES, tb), lambda b, j: (b, 0, j)),
                  per_b((CONV_W - 1, GDN_CONV_CH)), per_b((GDN_HEADS, GDN_DK, GDN_DV)),
                  _const_spec(cw.shape), _const_spec(pcol.shape), _const_spec(prow.shape), _const_spec(nw.shape),
                  _const_spec(tril.shape), _const_spec(triu.shape)],
        out_specs=(row_spec(GDN_V_W), per_b((GDN_HEADS, GDN_DK, GDN_DV))),
        out_shape=(jax.ShapeDtypeStruct((batch * seq, GDN_V_W), BF16),
                   jax.ShapeDtypeStruct((batch, GDN_HEADS, GDN_DK, GDN_DV), F32)),
        scratch_shapes=[pltpu.VMEM((SUBLANES + tb, GDN_CONV_CH), F32),
                        pltpu.VMEM((GDN_HEADS, GDN_DK, GDN_DV), F32)],
        compiler_params=pltpu.CompilerParams(dimension_semantics=("arbitrary", "arbitrary"),
                                             vmem_limit_bytes=VMEM_LIMIT),
        name="gdn",
    )(conv_in, z, small, smallt, buf, s0, cw, pcol, prow, nw, tril, triu)


def _rope_tables(pos):
    half = HEAD_DIM // 2
    inv = jnp.power(ROPE_THETA, -jnp.arange(half, dtype=F32) * 2.0 / HEAD_DIM)
    ang = pos.astype(F32)[:, None] * inv[None, :]
    cos, sin = jnp.cos(ang), jnp.sin(ang)
    return jnp.tile(cos, (1, 4)), jnp.concatenate([-sin, sin, -sin, sin], axis=1)


def _cmp_to_sel_t(n_cmp, n_sel, rows, cols):
    cs = np.arange(cols)[None, :] * CMP_STRIDE
    ss = np.arange(rows)[:, None] * SEL_LEN
    hit = (cs < ss + SEL_LEN) & (cs + CMP_LEN > ss)
    hit &= (np.arange(cols)[None, :] < n_cmp) & (np.arange(rows)[:, None] < n_sel)
    return jnp.asarray(hit, BF16)


def _layer_weights(w_in, cmp_pe, cmp_w, conv_w, a_log, dt_bias, gdn_norm, w_out):
    cuts = np.cumsum([NSA_Q_W, 6 * NSA_KV_W, GATE_COLS, GDN_CONV_CH, GDN_V_W, GDN_HEADS]).tolist()
    wq, wkv, wgt, wconv, wz, wa, wb = jnp.split(w_in, cuts, axis=1)
    order = np.array([g * NSA_REP + r for r in range(NSA_REP) for g in range(NSA_KV_HEADS)])
    cols = (order[:, None] * HEAD_DIM + np.arange(HEAD_DIM)[None, :]).reshape(-1)
    wm = jnp.concatenate([wq[:, cols], wkv, wconv, wz], axis=1).astype(BF16)
    ws = jnp.concatenate([wgt, wa, wb], axis=1)
    ws = jnp.pad(ws, ((0, 0), (0, LANES - ws.shape[1]))).astype(BF16)
    wo = jnp.concatenate([w_out[:NSA_Q_W][cols], w_out[NSA_Q_W:]], axis=0).astype(BF16)

    def blockdiag(l0):
        wk, wv = cmp_w[0, l0:l0 + CMP_STRIDE], cmp_w[1, l0:l0 + CMP_STRIDE]
        z = jnp.zeros_like(wk)
        rows = [jnp.concatenate(r, axis=2) for r in ([wk, z, z, z], [z, wk, z, z], [z, z, wv, z], [z, z, z, wv])]
        return jnp.concatenate(rows, axis=1).astype(BF16)

    def pe_rows(l0):
        pk, pv = cmp_pe[0, l0:l0 + CMP_STRIDE], cmp_pe[1, l0:l0 + CMP_STRIDE]
        return jnp.concatenate([pk, pk, pv, pv], axis=1)

    pcol = jnp.zeros((2, LANES), F32).at[0, A_COL:A_COL + GDN_HEADS].set(a_log)
    pcol = pcol.at[1, A_COL:A_COL + GDN_HEADS].set(dt_bias)
    prow = jnp.zeros((SUBLANES, 2), F32).at[0:GDN_HEADS, 0].set(a_log).at[0:GDN_HEADS, 1].set(dt_bias)
    return dict(wm=wm, ws=ws, wo=wo, wlo=blockdiag(0), whi=blockdiag(CMP_STRIDE), pelo=pe_rows(0),
                pehi=pe_rows(CMP_STRIDE), cw=conv_w, pcol=pcol, prow=prow, nw=gdn_norm[None, :])


def kernel(x_prompt, x_sample, cache_nsa_kv, cache_nsa_win, state_gdn_S, state_gdn_conv, page_table, norm_mix, w_in,
           nsa_cmp_pe, nsa_cmp_w, gdn_conv_w, gdn_a_log, gdn_dt_bias, gdn_norm, w_out, norm_ffn, w_gate_up, w_down,
           norm_final):
    bp, tp, _ = x_prompt.shape
    bs, ts, _ = x_sample.shape
    depth = w_in.shape[0]
    n_pages = page_table.shape[1]
    past = n_pages * PAGE_SIZE
    assert depth == 1 and ts == 1, "one layer and one new token per sample row"
    assert tp % 512 == 0 and tp // SEL_LEN <= LANES and past % 512 == 0
    l = 0
    wts = _layer_weights(w_in[l], nsa_cmp_pe[l], nsa_cmp_w[l], gdn_conv_w[l], gdn_a_log[l], gdn_dt_bias[l],
                         gdn_norm[l], w_out[l])
    g_mix, g_ffn, g_fin = norm_mix[l][None, :], norm_ffn[l][None, :], norm_final[None, :]
    wgu, wd = w_gate_up[l].astype(BF16), w_down[l].astype(BF16)
    hist = CONV_W - 1

    xp = x_prompt.reshape(bp * tp, D_MODEL)
    cos_p, sin_p = _rope_tables(jnp.arange(tp, dtype=jnp.int32))
    q, kv, conv_in, z, small, kaug, vsel, kwin, vwin = _proj(xp, g_mix, wts["wm"], wts["ws"], cos_p, sin_p, tm=256)
    ckv = _compress(kv, wts["wlo"], wts["whi"], wts["pelo"], wts["pehi"], batch=bp, seq=tp)
    nc = tp // CMP_STRIDE
    mt_p = _cmp_to_sel_t(nc - 1, tp // SEL_LEN, LANES, nc)
    o_nsa = _nsa_prompt(q, small, kaug, vsel, kwin, vwin, ckv, mt_p, batch=bp, seq=tp, kc=512)
    smallt = small[:, A_COL:A_COL + SUBLANES].reshape(bp, tp, SUBLANES).transpose(0, 2, 1)
    o_gdn, s_p = _gdn(conv_in, z, small, smallt, jnp.zeros((bp, hist, GDN_CONV_CH), F32),
                      jnp.zeros((bp, GDN_HEADS, GDN_DK, GDN_DV), F32), wts["cw"], wts["pcol"], wts["prow"],
                      wts["nw"], batch=bp, seq=tp, tb=256, chunk=64, t_real=tp)
    y_p = _mix_ffn(xp, o_nsa, o_gdn, wts["wo"], g_ffn, wgu, wd, g_fin, tm=256)
    kv_p = kv.reshape(bp, tp, 3, 2, NSA_KV_HEADS, HEAD_DIM)
    conv_p = conv_in.reshape(bp, tp, GDN_CONV_CH)[:, tp - hist:]

    xs = x_sample.reshape(bs, D_MODEL)
    cos_s, sin_s = _rope_tables(jnp.full((bs,), past, jnp.int32))
    q, kv, conv_in, z, small, _, _, _, _ = _proj(xs, g_mix, wts["wm"], wts["ws"], cos_s, sin_s, tm=bs)
    n_sel = past // SEL_LEN + 1
    nsp = -(-n_sel // LANES) * LANES
    mt_s = _cmp_to_sel_t(past // CMP_STRIDE - 1, n_sel, nsp, past // CMP_STRIDE)
    expand = jnp.asarray(np.arange(nsp)[:, None] == np.arange(past)[None, :] // SEL_LEN, BF16)
    cache = cache_nsa_kv[l].reshape(-1, PAGE_SIZE, 4 * NSA_KV_W)
    win = cache_nsa_win[l].reshape(bs, -1, 2 * NSA_KV_W)
    o_nsa = _nsa_sample(page_table, cache, q[:, None], kv[:, None], small[:, None], win, wts["wlo"], wts["whi"],
                        wts["pelo"], wts["pehi"], mt_s, expand, pps=8).reshape(bs, NSA_Q_W)
    pad_rows = lambda a: jnp.pad(a[:, None], ((0, 0), (0, SUBLANES - 1), (0, 0))).reshape(bs * SUBLANES, -1)
    smallt = jnp.pad(small[:, A_COL:A_COL + SUBLANES][:, :, None], ((0, 0), (0, 0), (0, SUBLANES - 1)))
    o_gdn, s_s = _gdn(pad_rows(conv_in), pad_rows(z), pad_rows(small), smallt, state_gdn_conv[l], state_gdn_S[l],
                      wts["cw"], wts["pcol"], wts["prow"], wts["nw"], batch=bs, seq=SUBLANES, tb=SUBLANES,
                      chunk=SUBLANES, t_real=1)
    o_gdn = o_gdn.reshape(bs, SUBLANES, GDN_V_W)[:, 0]
    y_s = _mix_ffn(xs, o_nsa, o_gdn, wts["wo"], g_ffn, wgu, wd, g_fin, tm=bs)
    kv_s = kv.reshape(bs, 1, 3, 2, NSA_KV_HEADS, HEAD_DIM)
    win_s = jnp.concatenate([cache_nsa_win[l], kv_s[:, :, 2]], axis=1)[:, -min(WINDOW, past + 1):]
    conv_s = jnp.concatenate([state_gdn_conv[l], conv_in[:, None]], axis=1)[:, -hist:]

    return (y_p.reshape(bp, tp, D_MODEL), y_s.reshape(bs, 1, D_MODEL),
            kv_p[:, :, 0:2].reshape(bp, tp, 4, NSA_KV_HEADS, HEAD_DIM)[None],
            kv_p[:, tp - min(WINDOW, tp):, 2][None], s_p[None], conv_p[None],
            kv_s[:, :, 0:2].reshape(bs, 1, 4, NSA_KV_HEADS, HEAD_DIM)[None], win_s[None], s_s[None], conv_s[None])
```

```python
import functools

import numpy as np
import jax
import jax.numpy as jnp
from jax import lax
from jax.experimental import pallas as pl
from jax.experimental.pallas import tpu as pltpu

F32 = jnp.float32
BF16 = jnp.bfloat16

D_MODEL = 1024
PAGE_SIZE = 128
HEAD_DIM = 64
NSA_HEADS = 8
NSA_KV_HEADS = 2
NSA_REP = NSA_HEADS // NSA_KV_HEADS
CMP_LEN = 32
CMP_STRIDE = 16
SEL_LEN = 64
SEL_TOPK = 16
WINDOW = 512
Q_BLOCK = 128
ROPE_THETA = 10000.0
GDN_HEADS = 4
GDN_DK = 128
GDN_DV = 128
CONV_W = 4
NSA_Q_W = NSA_HEADS * HEAD_DIM
NSA_KV_W = NSA_KV_HEADS * HEAD_DIM
GDN_QK_W = GDN_HEADS * GDN_DK
GDN_V_W = GDN_HEADS * GDN_DV
GDN_CONV_CH = 2 * GDN_QK_W + GDN_V_W
NEG = -1e30
SEL_BONUS = 1e4
EPS = 1e-6

LANES = 128
SUBLANES = 8
VMEM_LIMIT = 56 * 1024 * 1024

GATE_COLS = 3 * NSA_HEADS
A_COL = GATE_COLS
B_COL = GATE_COLS + GDN_HEADS

NT_DIMS = (((1,), (1,)), ((), ()))
TN_DIMS = (((0,), (0,)), ((), ()))


def _dot(a, b):
    return jnp.dot(a, b, preferred_element_type=F32)


def _dot_nt(a, b):
    return lax.dot_general(a, b, NT_DIMS, preferred_element_type=F32)


def _sigmoid(x):
    return 1.0 / (1.0 + jnp.exp(-x))


def _silu(x):
    return x * _sigmoid(x)


def _split3(x):
    p1 = x.astype(BF16)
    r1 = x - p1.astype(F32)
    p2 = r1.astype(BF16)
    p3 = (r1 - p2.astype(F32)).astype(BF16)
    return p1, p2, p3


def _const_spec(shape):
    nd = len(shape)
    return pl.BlockSpec(shape, lambda *_: (0,) * nd, pipeline_mode=pl.Buffered(1))


def _proj_kernel(x_ref, g_ref, wm_ref, ws_ref, cos_ref, sin_ref,
                 q_ref, kv_ref, conv_ref, z_ref, small_ref, kaug_ref, vaug_ref, kwin_ref, vwin_ref,
                 *, tm, pos_rows):
    x = x_ref[...]
    ms = jnp.mean(x * x, axis=-1, keepdims=True)
    xn = (x * lax.rsqrt(ms + EPS) * g_ref[...]).astype(BF16)
    cos = cos_ref[...]
    sin = sin_ref[...]
    lane = lax.broadcasted_iota(jnp.int32, (tm, LANES), 1)
    low_half = (lane % HEAD_DIM) < (HEAD_DIM // 2)

    def rope(v):
        rot = jnp.where(low_half, pltpu.roll(v, LANES - HEAD_DIM // 2, 1), pltpu.roll(v, HEAD_DIM // 2, 1))
        return v * cos + rot * sin

    q = _dot(xn, wm_ref[:, 0:NSA_Q_W])
    for r in range(NSA_REP):
        sl = slice(r * LANES, (r + 1) * LANES)
        q_ref[:, sl] = (rope(q[:, sl]) * (HEAD_DIM ** -0.5)).astype(BF16)

    kv = _dot(xn, wm_ref[:, NSA_Q_W:NSA_Q_W + 6 * NSA_KV_W])
    bf_outs = {2: kaug_ref.at[:, LANES:2 * LANES], 3: vaug_ref.at[:, 0:LANES], 4: kwin_ref, 5: vwin_ref}
    for c in range(6):
        sl = slice(c * LANES, (c + 1) * LANES)
        blk = kv[:, sl]
        if c % 2 == 0:
            blk = rope(blk)
        kv_ref[:, sl] = blk
        if c in bf_outs:
            bf_outs[c][...] = blk.astype(BF16)
    vaug_ref[:, LANES:2 * LANES] = jnp.ones((tm, LANES), BF16)

    row0 = (pl.program_id(0) * tm) % pos_rows
    rows = row0 + lax.broadcasted_iota(jnp.int32, (tm, LANES), 0)
    kaug_ref[:, 0:LANES] = jnp.where(rows // SEL_LEN == lane, 1.0, 0.0).astype(BF16)

    c0 = NSA_Q_W + 6 * NSA_KV_W
    conv_ref[...] = _dot(xn, wm_ref[:, c0:c0 + GDN_CONV_CH])
    z_ref[...] = _dot(xn, wm_ref[:, c0 + GDN_CONV_CH:c0 + GDN_CONV_CH + GDN_V_W])
    small_ref[...] = _dot(xn, ws_ref[...])


def _proj(x, g, wm, ws, cos, sin, *, tm):
    rows = x.shape[0]
    pos_rows = cos.shape[0]
    n_pos_blk = pos_rows // tm
    grid = (rows // tm,)
    row_spec = lambda w: pl.BlockSpec((tm, w), lambda i: (i, 0))
    pos_spec = pl.BlockSpec((tm, LANES), lambda i: (i % n_pos_blk, 0))
    out_shape = (
        jax.ShapeDtypeStruct((rows, NSA_Q_W), BF16),
        jax.ShapeDtypeStruct((rows, 6 * NSA_KV_W), F32),
        jax.ShapeDtypeStruct((rows, GDN_CONV_CH), F32),
        jax.ShapeDtypeStruct((rows, GDN_V_W), F32),
        jax.ShapeDtypeStruct((rows, LANES), F32),
        jax.ShapeDtypeStruct((rows, 2 * LANES), BF16),
        jax.ShapeDtypeStruct((rows, 2 * LANES), BF16),
        jax.ShapeDtypeStruct((rows, LANES), BF16),
        jax.ShapeDtypeStruct((rows, LANES), BF16),
    )
    return pl.pallas_call(
        functools.partial(_proj_kernel, tm=tm, pos_rows=pos_rows),
        grid=grid,
        in_specs=[row_spec(D_MODEL), _const_spec((1, D_MODEL)), _const_spec(wm.shape), _const_spec(ws.shape),
                  pos_spec, pos_spec],
        out_specs=tuple(row_spec(s.shape[1]) for s in out_shape),
        out_shape=out_shape,
        compiler_params=pltpu.CompilerParams(dimension_semantics=("arbitrary",), vmem_limit_bytes=VMEM_LIMIT),
        name="proj",
    )(x, g, wm, ws, cos, sin)


def _mix_ffn_kernel(x_ref, on_ref, og_ref, wo_ref, gf_ref, wgu_ref, wd_ref, gl_ref, y_ref, *, d_ff):
    h = x_ref[...] + _dot(on_ref[...], wo_ref[0:NSA_Q_W, :]) + _dot(og_ref[...], wo_ref[NSA_Q_W:, :])
    ms = jnp.mean(h * h, axis=-1, keepdims=True)
    hn = (h * lax.rsqrt(ms + EPS) * gf_ref[...]).astype(BF16)
    gate = _dot(hn, wgu_ref[:, 0:d_ff])
    up = _dot(hn, wgu_ref[:, d_ff:])
    act = (_silu(gate) * up).astype(BF16)
    h = h + _dot(act, wd_ref[...])
    ms = jnp.mean(h * h, axis=-1, keepdims=True)
    y_ref[...] = h * lax.rsqrt(ms + EPS) * gl_ref[...]


def _mix_ffn(x, o_nsa, o_gdn, wo, gf, wgu, wd, gl, *, tm):
    rows = x.shape[0]
    d_ff = wd.shape[0]
    row_spec = lambda w: pl.BlockSpec((tm, w), lambda i: (i, 0))
    return pl.pallas_call(
        functools.partial(_mix_ffn_kernel, d_ff=d_ff),
        grid=(rows // tm,),
        in_specs=[row_spec(D_MODEL), row_spec(NSA_Q_W), row_spec(GDN_V_W), _const_spec(wo.shape),
                  _const_spec((1, D_MODEL)), _const_spec(wgu.shape), _const_spec(wd.shape),
                  _const_spec((1, D_MODEL))],
        out_specs=row_spec(D_MODEL),
        out_shape=jax.ShapeDtypeStruct((rows, D_MODEL), F32),
        compiler_params=pltpu.CompilerParams(dimension_semantics=("arbitrary",), vmem_limit_bytes=VMEM_LIMIT),
        name="mix_ffn",
    )(x, o_nsa, o_gdn, wo, gf, wgu, wd, gl)


def _compress_rows(load_rows, wlo_ref, whi_ref, pelo_ref, pehi_ref, nc):
    acc_lo = jnp.zeros((nc, 2 * LANES), F32)
    acc_hi = jnp.zeros((nc, 2 * LANES), F32)
    for l in range(CMP_STRIDE):
        x = load_rows(l)
        acc_lo = acc_lo + _dot((x + pelo_ref[l:l + 1, :]).astype(BF16), wlo_ref[l])
        acc_hi = acc_hi + _dot((x + pehi_ref[l:l + 1, :]).astype(BF16), whi_ref[l])
    nxt = pltpu.roll(acc_hi, nc - 1, 0)
    row = lax.broadcasted_iota(jnp.int32, (nc, 2 * LANES), 0)
    return jnp.where(row < nc - 1, acc_lo + nxt, 0.0)


def _compress_kernel(k_ref, v_ref, wlo_ref, whi_ref, pelo_ref, pehi_ref, ckv_ref, *, nc):
    load = lambda l: jnp.concatenate([k_ref[pl.ds(l, nc, stride=CMP_STRIDE), :],
                                      v_ref[pl.ds(l, nc, stride=CMP_STRIDE), :]], axis=1)
    ckv_ref[...] = _compress_rows(load, wlo_ref, whi_ref, pelo_ref, pehi_ref, nc).astype(BF16)


def _compress(kv, wlo, whi, pelo, pehi, *, batch, seq):
    nc = seq // CMP_STRIDE
    return pl.pallas_call(
        functools.partial(_compress_kernel, nc=nc),
        grid=(batch,),
        in_specs=[pl.BlockSpec((seq, LANES), lambda b: (b, 0)), pl.BlockSpec((seq, LANES), lambda b: (b, 1)),
                  _const_spec(wlo.shape), _const_spec(whi.shape), _const_spec(pelo.shape), _const_spec(pehi.shape)],
        out_specs=pl.BlockSpec((nc, 2 * LANES), lambda b: (b, 0)),
        out_shape=jax.ShapeDtypeStruct((batch * nc, 2 * LANES), BF16),
        compiler_params=pltpu.CompilerParams(dimension_semantics=("arbitrary",), vmem_limit_bytes=VMEM_LIMIT),
        name="compress",
    )(kv, kv, wlo, whi, pelo, pehi)


def _cmp_softmax(s, valid):
    s = jnp.where(valid, s, NEG)
    m = jnp.max(s, axis=-1, keepdims=True)
    e = jnp.exp(s - m)
    return jnp.where(valid, e / jnp.sum(e, axis=-1, keepdims=True), 0.0)


def _select_blocks(score, blk, axis):
    sel = jnp.zeros(score.shape, F32)
    n = score.shape[axis]
    for _ in range(SEL_TOPK):
        mx = jnp.max(score, axis=axis, keepdims=True)
        idx = jnp.min(jnp.where(score == mx, blk, n), axis=axis, keepdims=True)
        hit = blk == idx
        sel = jnp.where(hit & (mx > NEG / 2), 1.0, sel)
        score = jnp.where(hit, -3e38, score)
    return sel


def _nsa_prompt_kernel(q_ref, small_ref, kaug_ref, vaug_ref, kwin_ref, vwin_ref, ckv_ref, mt_ref, o_ref,
                       *, seq, kc):
    nc = seq // CMP_STRIDE
    n_cmp = nc - 1
    qb = Q_BLOCK
    rows = NSA_REP * qb
    start = pl.program_id(1) * qb
    wlen = WINDOW + qb

    lane = lax.broadcasted_iota(jnp.int32, (qb, LANES), 1)
    qpos_r = start + lax.broadcasted_iota(jnp.int32, (rows, 1), 0) % qb
    gates = _sigmoid(small_ref[...])

    ncol = lax.broadcasted_iota(jnp.int32, (1, nc), 1)
    cmp_valid = (ncol * CMP_STRIDE + CMP_LEN - 1 <= qpos_r) & (ncol < n_cmp)

    blk_t = lax.broadcasted_iota(jnp.int32, (LANES, qb), 0)
    qpos_t = start + lax.broadcasted_iota(jnp.int32, (LANES, qb), 1)
    cur_t = qpos_t // SEL_LEN
    blk_valid = blk_t * SEL_LEN <= qpos_t
    forced = (blk_t == 0) | (blk_t == cur_t) | (blk_t == cur_t - 1)

    wbase = pl.multiple_of(jnp.maximum(start - WINDOW, 0), qb)
    wpos = wbase + lax.broadcasted_iota(jnp.int32, (1, wlen), 1)
    win_valid = (wpos <= qpos_r) & (wpos >= qpos_r - WINDOW)

    n_chunks = (start + kc - 1) // kc
    tail_ok = start + lax.broadcasted_iota(jnp.int32, (1, qb), 1) <= qpos_r

    groups = range(NSA_KV_HEADS)
    qs, q_augs, o_cmps, m0s, acc0s = [], [], [], [], []
    for g in groups:
        mine = (lane >= HEAD_DIM) == (g == 1)
        q = jnp.concatenate(
            [jnp.where(mine, q_ref[:, r * LANES:(r + 1) * LANES], 0.0).astype(BF16) for r in range(NSA_REP)], axis=0)

        p = _cmp_softmax(_dot_nt(q, ckv_ref[:, 0:LANES]), cmp_valid)
        o_cmps.append(_dot(p.astype(BF16), ckv_ref[:, LANES:2 * LANES]))

        psum = p[0:qb] + p[qb:2 * qb] + p[2 * qb:3 * qb] + p[3 * qb:4 * qb]
        imp_t = sum(_dot_nt(mt_ref[...], piece) for piece in _split3(psum))
        score = jnp.where(blk_valid, imp_t + jnp.where(forced, SEL_BONUS, 0.0), NEG)
        sel_t = _select_blocks(score, blk_t, 0)
        bias = jnp.where((sel_t > 0.5) & (blk_t * SEL_LEN < start), 0.0, NEG).T.astype(BF16)
        s_tail = jnp.where(tail_ok, _dot_nt(q, kaug_ref[pl.ds(start, qb), LANES:2 * LANES]), NEG)
        m0 = jnp.max(s_tail, axis=-1, keepdims=True)
        qs.append(q)
        q_augs.append(jnp.concatenate([jnp.concatenate([bias] * NSA_REP, axis=0), q], axis=1))
        m0s.append(m0)
        acc0s.append(_dot(jnp.exp(s_tail - m0).astype(BF16), vaug_ref[pl.ds(start, qb), :]))

    def sel_step(c, carry):
        off = pl.multiple_of(c * kc, kc)
        k = kaug_ref[pl.ds(off, kc), :]
        v = vaug_ref[pl.ds(off, kc), :]
        out = []
        for g in groups:
            m, acc = carry[g]
            s = _dot_nt(q_augs[g], k)
            m_new = jnp.maximum(m, jnp.max(s, axis=-1, keepdims=True))
            e = jnp.exp(s - m_new).astype(BF16)
            out.append((m_new, jnp.exp(m - m_new) * acc + _dot(e, v)))
        return tuple(out)

    sel_out = lax.fori_loop(0, n_chunks, sel_step, tuple(zip(m0s, acc0s)))

    outs = [None] * NSA_REP
    for g in groups:
        acc = sel_out[g][1]
        o_sel = acc[:, 0:LANES] / acc[:, LANES:2 * LANES]

        s = jnp.where(win_valid, _dot_nt(qs[g], kwin_ref[pl.ds(wbase, wlen), :]), NEG)
        e = jnp.exp(s - jnp.max(s, axis=-1, keepdims=True))
        pw = e / jnp.sum(e, axis=-1, keepdims=True)
        o_win = _dot(pw.astype(BF16), vwin_ref[pl.ds(wbase, wlen), :])

        for r in range(NSA_REP):
            c0 = (g * NSA_REP + r) * 3
            rs = slice(r * qb, (r + 1) * qb)
            o = (gates[:, c0:c0 + 1] * o_cmps[g][rs] + gates[:, c0 + 1:c0 + 2] * o_sel[rs]
                 + gates[:, c0 + 2:c0 + 3] * o_win[rs])
            outs[r] = o if g == 0 else jnp.where(lane < HEAD_DIM, outs[r], o)

    for r in range(NSA_REP):
        o_ref[:, r * LANES:(r + 1) * LANES] = outs[r].astype(BF16)


def _nsa_prompt(q, small, kaug, vsel, kwin, vwin, ckv, mt, *, batch, seq, kc):
    nb = seq // Q_BLOCK
    nc = seq // CMP_STRIDE
    blk_spec = lambda w: pl.BlockSpec((Q_BLOCK, w), lambda b, i: (b * nb + i, 0))
    seq_spec = lambda w: pl.BlockSpec((seq, w), lambda b, i: (b, 0))
    return pl.pallas_call(
        functools.partial(_nsa_prompt_kernel, seq=seq, kc=kc),
        grid=(batch, nb),
        in_specs=[blk_spec(NSA_Q_W), blk_spec(LANES), seq_spec(2 * LANES), seq_spec(2 * LANES), seq_spec(LANES),
                  seq_spec(LANES), pl.BlockSpec((nc, 2 * LANES), lambda b, i: (b, 0)), _const_spec(mt.shape)],
        out_specs=blk_spec(NSA_Q_W),
        out_shape=jax.ShapeDtypeStruct((batch * seq, NSA_Q_W), BF16),
        compiler_params=pltpu.CompilerParams(dimension_semantics=("arbitrary", "arbitrary"),
                                             vmem_limit_bytes=VMEM_LIMIT),
        name="nsa_prompt",
    )(q, small, kaug, vsel, kwin, vwin, ckv, mt)


def _nsa_sample_kernel(pt_ref, *refs, past, pps):
    del pt_ref
    pages = refs[:pps]
    (q_ref, kvn_ref, small_ref, win_ref, perm_ref, wlo_ref, whi_ref, pelo_ref, pehi_ref, mt_ref, exp_ref,
     o_ref, xs_ref, s_ref, vs_ref) = refs[pps:]
    step = pl.program_id(1)
    n_pages = past // PAGE_SIZE
    cpp = PAGE_SIZE // CMP_STRIDE
    heads = NSA_HEADS
    row = lax.broadcasted_iota(jnp.int32, (heads, LANES), 0)
    lane = lax.broadcasted_iota(jnp.int32, (heads, LANES), 1)
    mine = (lane >= HEAD_DIM) == (row >= NSA_REP)
    rr = row % NSA_REP
    qsel = jnp.zeros((heads, LANES), F32)
    for r in range(NSA_REP):
        qsel = jnp.where(rr == r, q_ref[:, r * LANES:(r + 1) * LANES].astype(F32), qsel)
    qf = jnp.where(mine, qsel, 0.0)
    q = qf.astype(BF16)

    nc = past // CMP_STRIDE
    cw = 2 * LANES

    def take_pages(first_page):
        for k in range(pps):
            pg = first_page + k
            page = pages[k]
            tok = slice(pg * PAGE_SIZE, (pg + 1) * PAGE_SIZE)
            s_ref[:, tok] = _dot(q, page[2 * LANES:3 * LANES, :].astype(BF16))
            vs_ref[:, tok] = page[3 * LANES:4 * LANES, :].astype(BF16)
            x = _dot_nt(perm_ref[...], page[0:2 * LANES, :].astype(BF16))
            for l in range(CMP_STRIDE):
                xs_ref[pg * cpp:(pg + 1) * cpp, l * cw:(l + 1) * cw] = x[l * cpp:(l + 1) * cpp, :]

    for st in range(n_pages // pps):
        pl.when(step == st)(functools.partial(take_pages, st * pps))

    @pl.when(step == pl.num_programs(1) - 1)
    def _():
        n_cmp = nc - 1
        nsp = mt_ref.shape[0]
        cur = past // SEL_LEN

        for l in range(CMP_STRIDE):
            xs_ref[nc:nc + SUBLANES, l * cw:(l + 1) * cw] = jnp.broadcast_to(pelo_ref[l:l + 1, :], (SUBLANES, cw))
            xs_ref[nc + SUBLANES:nc + 2 * SUBLANES, l * cw:(l + 1) * cw] = jnp.broadcast_to(pehi_ref[l:l + 1, :],
                                                                                            (SUBLANES, cw))
        x16 = xs_ref[...].astype(BF16)
        acc_lo = _dot(x16, wlo_ref[...])
        acc_hi = _dot(x16, whi_ref[...])
        bias = acc_lo[nc:nc + 1] + acc_hi[nc + SUBLANES:nc + SUBLANES + 1]
        nxt = pltpu.roll(acc_hi[0:nc], nc - 1, 0)
        crow = lax.broadcasted_iota(jnp.int32, (nc, 2 * LANES), 0)
        ckv = jnp.where(crow < nc - 1, acc_lo[0:nc] + nxt + bias, 0.0).astype(BF16)

        ncol = lax.broadcasted_iota(jnp.int32, (1, nc), 1)
        cmp_valid = (ncol * CMP_STRIDE + CMP_LEN - 1 <= past) & (ncol < n_cmp)
        p = _cmp_softmax(_dot_nt(q, ckv[:, 0:LANES]), cmp_valid)
        o_cmp = _dot(p.astype(BF16), ckv[:, LANES:2 * LANES])

        prow = lax.broadcasted_iota(jnp.int32, (heads, nc), 0)
        ps = [jnp.sum(jnp.where(prow // NSA_REP == g, p, 0.0), axis=0, keepdims=True) for g in range(NSA_KV_HEADS)]
        psum = jnp.where(prow < NSA_REP, ps[0], ps[1])
        imp = sum(_dot_nt(piece, mt_ref[...]) for piece in _split3(psum))
        blk = lax.broadcasted_iota(jnp.int32, (heads, nsp), 1)
        forced = (blk == 0) | (blk == cur) | (blk == cur - 1)
        score = jnp.where(blk * SEL_LEN <= past, imp + jnp.where(forced, SEL_BONUS, 0.0), NEG)
        score_t = score.T
        bi = lax.broadcasted_iota(jnp.int32, (nsp, nsp), 0)
        bj = lax.broadcasted_iota(jnp.int32, (nsp, nsp), 1)
        tie = jnp.where(bi < bj, 1.0, 0.0)
        picks = []
        for g in range(NSA_KV_HEADS):
            c = g * NSA_REP
            s_i, s_j = score_t[:, c:c + 1], score[c:c + 1, :]
            beats = jnp.where(s_i > s_j, 1.0, jnp.where(s_i == s_j, tie, 0.0))
            rank = jnp.sum(beats, axis=0, keepdims=True)
            picks.append(jnp.where(rank < SEL_TOPK, jnp.where(s_j > NEG / 2, 1.0, 0.0), 0.0))
        sel8 = jnp.where(lax.broadcasted_iota(jnp.int32, (heads, nsp), 0) < NSA_REP, picks[0], picks[1])
        keep = _dot(sel8.astype(BF16), exp_ref[...]) > 0.5

        kvn = kvn_ref[...]
        new_ok = sel8[:, cur:cur + 1] > 0.5
        s = jnp.where(keep, s_ref[...], NEG)
        s_new = jnp.where(new_ok, jnp.sum(qf * kvn[:, 2 * LANES:3 * LANES], axis=-1, keepdims=True), NEG)
        m = jnp.maximum(jnp.max(s, axis=-1, keepdims=True), s_new)
        e = jnp.exp(s - m)
        e_new = jnp.where(new_ok, jnp.exp(s_new - m), 0.0)
        o_sel = ((_dot_nt(e.astype(BF16), vs_ref[...]) + e_new * kvn[:, 3 * LANES:4 * LANES])
                 / (jnp.sum(e, axis=-1, keepdims=True) + e_new))

        nw = win_ref.shape[0] // 2
        s = _dot(q, win_ref[0:nw, :].astype(BF16))
        s_new = jnp.sum(qf * kvn[:, 4 * LANES:5 * LANES], axis=-1, keepdims=True)
        m = jnp.maximum(jnp.max(s, axis=-1, keepdims=True), s_new)
        e = jnp.exp(s - m)
        e_new = jnp.exp(s_new - m)
        l = jnp.sum(e, axis=-1, keepdims=True) + e_new
        o_win = (_dot_nt(e.astype(BF16), win_ref[nw:, :].astype(BF16))
                 + e_new * kvn[:, 5 * LANES:6 * LANES]) / l

        gates = _sigmoid(small_ref[...])
        gate = lambda br: jnp.sum(jnp.where(lane == row * 3 + br, gates, 0.0), axis=-1, keepdims=True)
        o = gate(0) * o_cmp + gate(1) * o_sel + gate(2) * o_win
        for r in range(NSA_REP):
            o_ref[:, r * LANES:(r + 1) * LANES] = jnp.where(
                lane[0:1] < HEAD_DIM, o[r:r + 1], o[NSA_REP + r:NSA_REP + r + 1]).astype(BF16)


def _nsa_sample(page_table, cache, q, kvn, small, win, wlo, whi, pelo, pehi, mt, *, pps):
    batch, n_pages = page_table.shape
    past = n_pages * PAGE_SIZE
    rows = cache.shape[1]
    cpp = PAGE_SIZE // CMP_STRIDE
    nsp = mt.shape[0]
    expand = jnp.asarray(np.arange(nsp)[:, None] == np.arange(past)[None, :] // SEL_LEN, BF16)
    tok = np.arange(PAGE_SIZE)
    perm = jnp.asarray((tok[None, :] % CMP_STRIDE) * cpp + tok[None, :] // CMP_STRIDE == tok[:, None], BF16)

    def page_spec(k):
        return pl.BlockSpec((None, rows, PAGE_SIZE), lambda b, s, pt: (pt[b * n_pages + s * pps + k], 0, 0))

    per_b = lambda shape: pl.BlockSpec((None,) + shape, lambda b, s, pt: (b, 0, 0))
    const = lambda a: pl.BlockSpec(a.shape, lambda b, s, pt: (0,) * a.ndim, pipeline_mode=pl.Buffered(1))
    grid_spec = pltpu.PrefetchScalarGridSpec(
        num_scalar_prefetch=1,
        grid=(batch, n_pages // pps),
        in_specs=[page_spec(k) for k in range(pps)] + [
            per_b((1, NSA_Q_W)), per_b((1, 6 * NSA_KV_W)), per_b((1, LANES)), per_b(win.shape[1:]),
            const(perm), const(wlo), const(whi), const(pelo), const(pehi), const(mt), const(expand)],
        out_specs=per_b((1, NSA_Q_W)),
        scratch_shapes=[pltpu.VMEM((past // CMP_STRIDE + 2 * SUBLANES, CMP_STRIDE * 2 * LANES), F32),
                        pltpu.VMEM((NSA_HEADS, past), F32),
                        pltpu.VMEM((LANES, past), BF16)],
    )
    return pl.pallas_call(
        functools.partial(_nsa_sample_kernel, past=past, pps=pps),
        grid_spec=grid_spec,
        out_shape=jax.ShapeDtypeStruct((batch, 1, NSA_Q_W), BF16),
        compiler_params=pltpu.CompilerParams(dimension_semantics=("arbitrary", "arbitrary"),
                                             vmem_limit_bytes=VMEM_LIMIT),
        name="nsa_sample",
    )(page_table.reshape(-1), *([cache] * pps), q, kvn, small, win, perm, wlo, whi, pelo, pehi, mt, expand)


def _l2norm(x):
    return x * lax.rsqrt(jnp.sum(x * x, axis=-1, keepdims=True) + EPS)


def _softplus(x):
    return jnp.maximum(x, 0.0) + jnp.log(1.0 + jnp.exp(-jnp.abs(x)))


def _gdn_kernel(conv_ref, z_ref, small_ref, smallt_ref, buf_ref, s0_ref, cw_ref, pcol_ref, prow_ref, nw_ref,
                tril_ref, triu_ref, o_ref, sout_ref, xs_ref, s_ref, *, tb, chunk, t_real):
    j = pl.program_id(1)
    pad = SUBLANES
    hist = CONV_W - 1

    @pl.when(j == 0)
    def _():
        xs_ref[0:pad, :] = jnp.zeros((pad, GDN_CONV_CH), F32)
        xs_ref[pad - hist:pad, :] = buf_ref[...]
        s_ref[...] = s0_ref[...]

    xs_ref[pad:pad + tb, :] = conv_ref[...]
    conv = xs_ref[pad - hist:pad - hist + tb, :] * cw_ref[0:1, :]
    for t in range(1, CONV_W):
        conv = conv + xs_ref[pad - hist + t:pad - hist + t + tb, :] * cw_ref[t:t + 1, :]
    xs_ref[pad - hist:pad, :] = xs_ref[pad + tb - hist:pad + tb, :]
    act = _silu(conv)

    small = small_ref[...]
    tok_c = j * tb + lax.broadcasted_iota(jnp.int32, (tb, LANES), 0)
    g_col = jnp.where(tok_c < t_real, -jnp.exp(pcol_ref[0:1, :]) * _softplus(small + pcol_ref[1:2, :]), 0.0)
    beta = jnp.where(tok_c < t_real, _sigmoid(small), 0.0)
    tok_r = j * tb + lax.broadcasted_iota(jnp.int32, (SUBLANES, tb), 1)
    g_row = jnp.where(tok_r < t_real,
                      -jnp.exp(prow_ref[:, 0:1]) * _softplus(smallt_ref[...] + prow_ref[:, 1:2]), 0.0)
    gcum_col = sum(_dot(tril_ref[...], piece) for piece in _split3(g_col))
    gcum_row = sum(_dot(piece, triu_ref[...]) for piece in _split3(g_row))

    ii = lax.broadcasted_iota(jnp.int32, (chunk, chunk), 0)
    jj = lax.broadcasted_iota(jnp.int32, (chunk, chunk), 1)
    incl = ii >= jj
    eye = jnp.where(ii == jj, 1.0, 0.0)

    for ci in range(tb // chunk):
        r0 = ci * chunk
        rs = slice(r0, r0 + chunk)
        for h in range(GDN_HEADS):
            hs = slice(h * GDN_DK, (h + 1) * GDN_DK)
            gc = gcum_col[rs, A_COL + h:A_COL + h + 1]
            gr = gcum_row[h:h + 1, r0:r0 + chunk]
            glast = gcum_col[r0 + chunk - 1:r0 + chunk, A_COL + h:A_COL + h + 1]
            bt = beta[rs, B_COL + h:B_COL + h + 1]
            dec = jnp.where(incl, jnp.exp(jnp.where(incl, gc - gr, 0.0)), 0.0)
            qh = _l2norm(act[rs, hs]) * (GDN_DK ** -0.5)
            kh = _l2norm(act[rs, GDN_QK_W + h * GDN_DK:GDN_QK_W + (h + 1) * GDN_DK])
            vh = act[rs, 2 * GDN_QK_W + h * GDN_DV:2 * GDN_QK_W + (h + 1) * GDN_DV]
            kb = kh * bt
            k16 = kh.astype(BF16)
            lm = jnp.where(ii > jj, _dot_nt(kb.astype(BF16), k16) * dec, 0.0)
            ainv = eye - lm
            pw = lm
            n = 2
            while n < chunk:
                pw16 = pw.astype(BF16)
                pw = _dot(pw16, pw16)
                ainv = ainv + _dot(ainv.astype(BF16), pw.astype(BF16))
                n *= 2
            eg = jnp.exp(gc)
            rhs = jnp.concatenate([vh * bt, kb * eg], axis=1).astype(BF16)
            sol = _dot(ainv.astype(BF16), rhs)
            u = sol[:, 0:GDN_DV]
            w = sol[:, GDN_DV:]
            aqk = _dot_nt(qh.astype(BF16), k16) * dec
            s16 = s_ref[h].astype(BF16)
            vn = u - _dot(w.astype(BF16), s16)
            vn16 = vn.astype(BF16)
            o = _dot((qh * eg).astype(BF16), s16) + _dot(aqk.astype(BF16), vn16)
            kg = kh * jnp.exp(glast - gc)
            s_ref[h] = s_ref[h] * jnp.exp(glast) + lax.dot_general(kg.astype(BF16), vn16, TN_DIMS,
                                                                    preferred_element_type=F32)
            on = o * lax.rsqrt(jnp.mean(o * o, axis=-1, keepdims=True) + EPS) * nw_ref[...]
            o_ref[rs, hs] = (on * _silu(z_ref[rs, hs])).astype(BF16)

    sout_ref[...] = s_ref[...]


def _gdn(conv_in, z, small, smallt, buf, s0, cw, pcol, prow, nw, *, batch, seq, tb, chunk, t_real):
    nblk = seq // tb
    blk = np.arange(tb)
    same = (blk[:, None] // chunk) == (blk[None, :] // chunk)
    tril = jnp.asarray(same & (blk[:, None] >= blk[None, :]), BF16)
    triu = jnp.asarray(same & (blk[:, None] <= blk[None, :]), BF16)
    row_spec = lambda w: pl.BlockSpec((tb, w), lambda b, j: (b * nblk + j, 0))
    per_b = lambda shape: pl.BlockSpec((None,) + shape, lambda b, j: (b,) + (0,) * len(shape))
    return pl.pallas_call(
        functools.partial(_gdn_kernel, tb=tb, chunk=chunk, t_real=t_real),
        grid=(batch, nblk),
        in_specs=[row_spec(GDN_CONV_CH), row_spec(GDN_V_W), row_spec(LANES),
                  pl.BlockSpec((None, SUBLANES, tb), lambda b, j: (b, 0, j)),
                  per_b((CONV_W - 1, GDN_CONV_CH)), per_b((GDN_HEADS, GDN_DK, GDN_DV)),
                  _const_spec(cw.shape), _const_spec(pcol.shape), _const_spec(prow.shape), _const_spec(nw.shape),
                  _const_spec(tril.shape), _const_spec(triu.shape)],
        out_specs=(row_spec(GDN_V_W), per_b((GDN_HEADS, GDN_DK, GDN_DV))),
        out_shape=(jax.ShapeDtypeStruct((batch * seq, GDN_V_W), BF16),
                   jax.ShapeDtypeStruct((batch, GDN_HEADS, GDN_DK, GDN_DV), F32)),
        scratch_shapes=[pltpu.VMEM((SUBLANES + tb, GDN_CONV_CH), F32),
                        pltpu.VMEM((GDN_HEADS, GDN_DK, GDN_DV), F32)],
        compiler_params=pltpu.CompilerParams(dimension_semantics=("arbitrary", "arbitrary"),
                                             vmem_limit_bytes=VMEM_LIMIT),
        name="gdn",
    )(conv_in, z, small, smallt, buf, s0, cw, pcol, prow, nw, tril, triu)


GDN_CHUNK = 64
GDN_STACK = GDN_HEADS * GDN_CHUNK


def _stack_heads(x, col0, width):
    return jnp.concatenate([x[:, col0 + h * width:col0 + (h + 1) * width] for h in range(GDN_HEADS)], axis=0)


def _gdn_prep_kernel(conv_ref, small_ref, smallt_ref, buf_ref, cw_ref, pcol_ref, prow_ref, tril_ref, triu_ref,
                     u_ref, w_ref, qg_ref, kgt_ref, aqk_ref, gl_ref, xs_ref, *, tb):
    j = pl.program_id(1)
    pad = SUBLANES
    hist = CONV_W - 1
    ck = GDN_CHUNK
    st = GDN_STACK

    @pl.when(j == 0)
    def _():
        xs_ref[0:pad, :] = jnp.zeros((pad, GDN_CONV_CH), F32)
        xs_ref[pad - hist:pad, :] = buf_ref[...]

    xs_ref[pad:pad + tb, :] = conv_ref[...]
    conv = xs_ref[pad - hist:pad - hist + tb, :] * cw_ref[0:1, :]
    for t in range(1, CONV_W):
        conv = conv + xs_ref[pad - hist + t:pad - hist + t + tb, :] * cw_ref[t:t + 1, :]
    xs_ref[pad - hist:pad, :] = xs_ref[pad + tb - hist:pad + tb, :]
    act = _silu(conv)

    small = small_ref[...]
    g_col = -jnp.exp(pcol_ref[0:1, :]) * _softplus(small + pcol_ref[1:2, :])
    beta = _sigmoid(small)
    g_row = -jnp.exp(prow_ref[:, 0:1]) * _softplus(smallt_ref[...] + prow_ref[:, 1:2])
    gcum_col = sum(_dot(tril_ref[...], piece) for piece in _split3(g_col))
    gcum_row = sum(_dot(piece, triu_ref[...]) for piece in _split3(g_row))

    ii = lax.broadcasted_iota(jnp.int32, (st, st), 0)
    jj = lax.broadcasted_iota(jnp.int32, (st, st), 1)
    same_head = (ii // ck) == (jj // ck)
    incl = same_head & (ii >= jj)
    strict = same_head & (ii > jj)
    eye = jnp.where(ii == jj, 1.0, 0.0)
    hrow = lax.broadcasted_iota(jnp.int32, (SUBLANES, LANES), 0)

    for ci in range(tb // ck):
        r0 = ci * ck
        rs = slice(r0, r0 + ck)
        gc = _stack_heads(gcum_col[rs], A_COL, 1)
        bt = _stack_heads(beta[rs], B_COL, 1)
        gr = jnp.concatenate([gcum_row[h:h + 1, r0:r0 + ck] for h in range(GDN_HEADS)], axis=1)
        glast = [gcum_col[r0 + ck - 1:r0 + ck, A_COL + h:A_COL + h + 1] for h in range(GDN_HEADS)]
        gl_stack = jnp.concatenate([jnp.broadcast_to(x, (ck, 1)) for x in glast], axis=0)
        dec = jnp.where(incl, jnp.exp(jnp.where(incl, gc - gr, 0.0)), 0.0)
        qs = jnp.concatenate([_l2norm(act[rs, h * GDN_DK:(h + 1) * GDN_DK]) for h in range(GDN_HEADS)],
                             axis=0) * (GDN_DK ** -0.5)
        ks = jnp.concatenate([_l2norm(act[rs, GDN_QK_W + h * GDN_DK:GDN_QK_W + (h + 1) * GDN_DK])
                              for h in range(GDN_HEADS)], axis=0)
        vs = _stack_heads(act[rs], 2 * GDN_QK_W, GDN_DV)
        kb = ks * bt
        k16 = ks.astype(BF16)
        lm = jnp.where(strict, _dot_nt(kb.astype(BF16), k16) * dec, 0.0)
        ainv = eye - lm
        pw = lm
        n = 2
        while n < ck:
            pw16 = pw.astype(BF16)
            pw = _dot(pw16, pw16)
            ainv = ainv + _dot(ainv.astype(BF16), pw.astype(BF16))
            n *= 2
        eg = jnp.exp(gc)
        rhs = jnp.concatenate([vs * bt, kb * eg], axis=1).astype(BF16)
        sol = _dot(ainv.astype(BF16), rhs)
        orow = slice(ci * st, (ci + 1) * st)
        u_ref[orow, :] = sol[:, 0:GDN_DV]
        w_ref[orow, :] = sol[:, GDN_DV:].astype(BF16)
        qg_ref[orow, :] = (qs * eg).astype(BF16)
        aqk_ref[orow, :] = (_dot_nt(qs.astype(BF16), k16) * dec).astype(BF16)
        kgt_ref[ci * GDN_DK:(ci + 1) * GDN_DK, :] = (ks * jnp.exp(gl_stack - gc)).T.astype(BF16)
        gl = jnp.zeros((SUBLANES, LANES), F32)
        for h in range(GDN_HEADS):
            gl = jnp.where(hrow == h, jnp.exp(glast[h]), gl)
        gl_ref[ci * SUBLANES:(ci + 1) * SUBLANES, :] = gl


def _gdn_prep(conv_in, small, smallt, buf, cw, pcol, prow, *, batch, seq, tb):
    nblk = seq // tb
    ncb = tb // GDN_CHUNK
    blk = np.arange(tb)
    same = (blk[:, None] // GDN_CHUNK) == (blk[None, :] // GDN_CHUNK)
    tril = jnp.asarray(same & (blk[:, None] >= blk[None, :]), BF16)
    triu = jnp.asarray(same & (blk[:, None] <= blk[None, :]), BF16)
    row_spec = lambda r, w: pl.BlockSpec((r, w), lambda b, j: (b * nblk + j, 0))
    per_b = lambda shape: pl.BlockSpec((None,) + shape, lambda b, j: (b,) + (0,) * len(shape))
    n_chunks = batch * seq // GDN_CHUNK
    out_shape = (jax.ShapeDtypeStruct((n_chunks * GDN_STACK, GDN_DV), F32),
                 jax.ShapeDtypeStruct((n_chunks * GDN_STACK, GDN_DK), BF16),
                 jax.ShapeDtypeStruct((n_chunks * GDN_STACK, GDN_DK), BF16),
                 jax.ShapeDtypeStruct((n_chunks * GDN_DK, GDN_STACK), BF16),
                 jax.ShapeDtypeStruct((n_chunks * GDN_STACK, GDN_STACK), BF16),
                 jax.ShapeDtypeStruct((n_chunks * SUBLANES, LANES), F32))
    out_specs = (row_spec(ncb * GDN_STACK, GDN_DV), row_spec(ncb * GDN_STACK, GDN_DK),
                 row_spec(ncb * GDN_STACK, GDN_DK), row_spec(ncb * GDN_DK, GDN_STACK),
                 row_spec(ncb * GDN_STACK, GDN_STACK), row_spec(ncb * SUBLANES, LANES))
    return pl.pallas_call(
        functools.partial(_gdn_prep_kernel, tb=tb),
        grid=(batch, nblk),
        in_specs=[row_spec(tb, GDN_CONV_CH), row_spec(tb, LANES),
                  pl.BlockSpec((None, SUBLANES, tb), lambda b, j: (b, 0, j)),
                  per_b((CONV_W - 1, GDN_CONV_CH)),
                  _const_spec(cw.shape), _const_spec(pcol.shape), _const_spec(prow.shape),
                  _const_spec(tril.shape), _const_spec(triu.shape)],
        out_specs=out_specs,
        out_shape=out_shape,
        scratch_shapes=[pltpu.VMEM((SUBLANES + tb, GDN_CONV_CH), F32)],
        compiler_params=pltpu.CompilerParams(dimension_semantics=("arbitrary", "arbitrary"),
                                             vmem_limit_bytes=VMEM_LIMIT),
        name="gdn_prep",
    )(conv_in, small, smallt, buf, cw, pcol, prow, tril, triu)


def _gdn_scan_kernel(u_ref, w_ref, qg_ref, kgt_ref, aqk_ref, gl_ref, z_ref, s0_ref, nw_ref, o_ref, sout_ref, s_ref,
                     *, batch, ncb):
    ck = GDN_CHUNK
    st = GDN_STACK
    sw = GDN_HEADS * GDN_DK

    @pl.when(pl.program_id(0) == 0)
    def _():
        s_ref[...] = s0_ref[...]

    wide_mask = (lax.broadcasted_iota(jnp.int32, (st, sw), 0) // ck
                 == lax.broadcasted_iota(jnp.int32, (st, sw), 1) // GDN_DK)
    tall_mask = (lax.broadcasted_iota(jnp.int32, (sw, st), 0) // GDN_DK
                 == lax.broadcasted_iota(jnp.int32, (sw, st), 1) // ck)

    for ci in range(ncb):
        rows = slice(ci * st, (ci + 1) * st)
        for b in range(batch):
            w_bd = jnp.where(wide_mask, jnp.concatenate([w_ref[b, rows, :]] * GDN_HEADS, axis=1), 0.0)
            qg_bd = jnp.where(wide_mask, jnp.concatenate([qg_ref[b, rows, :]] * GDN_HEADS, axis=1), 0.0)
            kgt = kgt_ref[b, ci * GDN_DK:(ci + 1) * GDN_DK, :]
            kgt_bd = jnp.where(tall_mask, jnp.concatenate([kgt] * GDN_HEADS, axis=0), 0.0)
            gl = gl_ref[b, ci * SUBLANES:(ci + 1) * SUBLANES, :]
            gl_rows = jnp.concatenate([jnp.broadcast_to(gl[h:h + 1], (GDN_DK, GDN_DV)) for h in range(GDN_HEADS)],
                                      axis=0)
            s = s_ref[b]
            t1 = _dot(jnp.concatenate([w_bd, qg_bd], axis=0).astype(BF16), s.astype(BF16))
            vn16 = (u_ref[b, rows, :] - t1[0:st]).astype(BF16)
            o = t1[st:] + _dot(aqk_ref[b, rows, :], vn16)
            s_ref[b] = s * gl_rows + _dot(kgt_bd.astype(BF16), vn16)
            for h in range(GDN_HEADS):
                oh = o[h * ck:(h + 1) * ck]
                on = oh * lax.rsqrt(jnp.mean(oh * oh, axis=-1, keepdims=True) + EPS) * nw_ref[...]
                hs = slice(h * GDN_DV, (h + 1) * GDN_DV)
                o_ref[b, ci * ck:(ci + 1) * ck, hs] = (on * _silu(z_ref[b, ci * ck:(ci + 1) * ck, hs])).astype(BF16)

    sout_ref[...] = s_ref[...]


def _gdn_scan(u, w, qg, kgt, aqk, gl, z, s0, nw, *, batch, seq, tb):
    nblk = seq // tb
    ncb = tb // GDN_CHUNK
    cps = seq // GDN_CHUNK
    sw = GDN_HEADS * GDN_DK
    r3 = lambda a, rows_per_chunk: a.reshape(batch, cps * rows_per_chunk, a.shape[-1])
    blk = lambda rows, width: pl.BlockSpec((batch, rows, width), lambda j: (0, j, 0))
    full = pl.BlockSpec((batch, sw, GDN_DV), lambda j: (0, 0, 0))
    return pl.pallas_call(
        functools.partial(_gdn_scan_kernel, batch=batch, ncb=ncb),
        grid=(nblk,),
        in_specs=[blk(ncb * GDN_STACK, GDN_DV), blk(ncb * GDN_STACK, GDN_DK), blk(ncb * GDN_STACK, GDN_DK),
                  blk(ncb * GDN_DK, GDN_STACK), blk(ncb * GDN_STACK, GDN_STACK), blk(ncb * SUBLANES, LANES),
                  blk(tb, GDN_V_W), full, _const_spec(nw.shape)],
        out_specs=(blk(tb, GDN_V_W), full),
        out_shape=(jax.ShapeDtypeStruct((batch, seq, GDN_V_W), BF16),
                   jax.ShapeDtypeStruct((batch, sw, GDN_DV), F32)),
        scratch_shapes=[pltpu.VMEM((batch, sw, GDN_DV), F32)],
        compiler_params=pltpu.CompilerParams(dimension_semantics=("arbitrary",), vmem_limit_bytes=VMEM_LIMIT),
        name="gdn_scan",
    )(r3(u, GDN_STACK), r3(w, GDN_STACK), r3(qg, GDN_STACK), r3(kgt, GDN_DK), r3(aqk, GDN_STACK),
      r3(gl, SUBLANES), z.reshape(batch, seq, GDN_V_W), s0.reshape(batch, sw, GDN_DV), nw)


def _rope_tables(pos):
    half = HEAD_DIM // 2
    inv = jnp.power(ROPE_THETA, -jnp.arange(half, dtype=F32) * 2.0 / HEAD_DIM)
    ang = pos.astype(F32)[:, None] * inv[None, :]
    cos, sin = jnp.cos(ang), jnp.sin(ang)
    return jnp.tile(cos, (1, 4)), jnp.concatenate([-sin, sin, -sin, sin], axis=1)


def _cmp_to_sel_t(n_cmp, n_sel, rows, cols):
    cs = np.arange(cols)[None, :] * CMP_STRIDE
    ss = np.arange(rows)[:, None] * SEL_LEN
    hit = (cs < ss + SEL_LEN) & (cs + CMP_LEN > ss)
    hit &= (np.arange(cols)[None, :] < n_cmp) & (np.arange(rows)[:, None] < n_sel)
    return jnp.asarray(hit, BF16)


def _layer_weights(w_in, cmp_pe, cmp_w, conv_w, a_log, dt_bias, gdn_norm, w_out):
    cuts = np.cumsum([NSA_Q_W, 6 * NSA_KV_W, GATE_COLS, GDN_CONV_CH, GDN_V_W, GDN_HEADS]).tolist()
    wq, wkv, wgt, wconv, wz, wa, wb = jnp.split(w_in, cuts, axis=1)
    order = np.array([g * NSA_REP + r for r in range(NSA_REP) for g in range(NSA_KV_HEADS)])
    cols = (order[:, None] * HEAD_DIM + np.arange(HEAD_DIM)[None, :]).reshape(-1)
    wm = jnp.concatenate([wq[:, cols], wkv, wconv, wz], axis=1).astype(BF16)
    ws = jnp.concatenate([wgt, wa, wb], axis=1)
    ws = jnp.pad(ws, ((0, 0), (0, LANES - ws.shape[1]))).astype(BF16)
    wo = jnp.concatenate([w_out[:NSA_Q_W][cols], w_out[NSA_Q_W:]], axis=0).astype(BF16)

    def blockdiag(l0):
        wk, wv = cmp_w[0, l0:l0 + CMP_STRIDE], cmp_w[1, l0:l0 + CMP_STRIDE]
        z = jnp.zeros_like(wk)
        rows = [jnp.concatenate(r, axis=2) for r in ([wk, z, z, z], [z, wk, z, z], [z, z, wv, z], [z, z, z, wv])]
        return jnp.concatenate(rows, axis=1).astype(BF16)

    def pe_rows(l0):
        pk, pv = cmp_pe[0, l0:l0 + CMP_STRIDE], cmp_pe[1, l0:l0 + CMP_STRIDE]
        return jnp.concatenate([pk, pk, pv, pv], axis=1)

    pcol = jnp.zeros((2, LANES), F32).at[0, A_COL:A_COL + GDN_HEADS].set(a_log)
    pcol = pcol.at[1, A_COL:A_COL + GDN_HEADS].set(dt_bias)
    prow = jnp.zeros((SUBLANES, 2), F32).at[0:GDN_HEADS, 0].set(a_log).at[0:GDN_HEADS, 1].set(dt_bias)
    return dict(wm=wm, ws=ws, wo=wo, wlo=blockdiag(0), whi=blockdiag(CMP_STRIDE), pelo=pe_rows(0),
                pehi=pe_rows(CMP_STRIDE), cw=conv_w, pcol=pcol, prow=prow, nw=gdn_norm[None, :])


def kernel(x_prompt, x_sample, cache_nsa_kv, cache_nsa_win, state_gdn_S, state_gdn_conv, page_table, norm_mix, w_in,
           nsa_cmp_pe, nsa_cmp_w, gdn_conv_w, gdn_a_log, gdn_dt_bias, gdn_norm, w_out, norm_ffn, w_gate_up, w_down,
           norm_final):
    bp, tp, _ = x_prompt.shape
    bs, ts, _ = x_sample.shape
    depth = w_in.shape[0]
    n_pages = page_table.shape[1]
    past = n_pages * PAGE_SIZE
    assert depth == 1 and ts == 1, "one layer and one new token per sample row"
    assert tp % 512 == 0 and tp // SEL_LEN <= LANES and past % 512 == 0
    l = 0
    wts = _layer_weights(w_in[l], nsa_cmp_pe[l], nsa_cmp_w[l], gdn_conv_w[l], gdn_a_log[l], gdn_dt_bias[l],
                         gdn_norm[l], w_out[l])
    g_mix, g_ffn, g_fin = norm_mix[l][None, :], norm_ffn[l][None, :], norm_final[None, :]
    wgu, wd = w_gate_up[l].astype(BF16), w_down[l].astype(BF16)
    hist = CONV_W - 1

    xp = x_prompt.reshape(bp * tp, D_MODEL)
    cos_p, sin_p = _rope_tables(jnp.arange(tp, dtype=jnp.int32))
    q, kv, conv_in, z, small, kaug, vsel, kwin, vwin = _proj(xp, g_mix, wts["wm"], wts["ws"], cos_p, sin_p, tm=256)
    ckv = _compress(kv, wts["wlo"], wts["whi"], wts["pelo"], wts["pehi"], batch=bp, seq=tp)
    nc = tp // CMP_STRIDE
    mt_p = _cmp_to_sel_t(nc - 1, tp // SEL_LEN, LANES, nc)
    o_nsa = _nsa_prompt(q, small, kaug, vsel, kwin, vwin, ckv, mt_p, batch=bp, seq=tp, kc=512)
    smallt = small[:, A_COL:A_COL + SUBLANES].reshape(bp, tp, SUBLANES).transpose(0, 2, 1)
    prep = _gdn_prep(conv_in, small, smallt, jnp.zeros((bp, hist, GDN_CONV_CH), F32), wts["cw"], wts["pcol"],
                     wts["prow"], batch=bp, seq=tp, tb=256)
    o_gdn, s_p = _gdn_scan(*prep, z, jnp.zeros((bp, GDN_HEADS, GDN_DK, GDN_DV), F32), wts["nw"],
                           batch=bp, seq=tp, tb=256)
    o_gdn = o_gdn.reshape(bp * tp, GDN_V_W)
    s_p = s_p.reshape(bp, GDN_HEADS, GDN_DK, GDN_DV)
    y_p = _mix_ffn(xp, o_nsa, o_gdn, wts["wo"], g_ffn, wgu, wd, g_fin, tm=256)
    kv_p = kv.reshape(bp, tp, 3, 2, NSA_KV_HEADS, HEAD_DIM)
    conv_p = conv_in.reshape(bp, tp, GDN_CONV_CH)[:, tp - hist:]

    xs = x_sample.reshape(bs, D_MODEL)
    cos_s, sin_s = _rope_tables(jnp.full((bs,), past, jnp.int32))
    q, kv, conv_in, z, small, _, _, _, _ = _proj(xs, g_mix, wts["wm"], wts["ws"], cos_s, sin_s, tm=bs)
    n_sel = past // SEL_LEN + 1
    nsp = -(-n_sel // LANES) * LANES
    mt_s = _cmp_to_sel_t(past // CMP_STRIDE - 1, n_sel, nsp, past // CMP_STRIDE)
    cache = cache_nsa_kv[l].transpose(0, 2, 3, 4, 1).reshape(-1, 4 * NSA_KV_W, PAGE_SIZE)
    win = cache_nsa_win[l].transpose(0, 2, 3, 4, 1).reshape(bs, 2 * NSA_KV_W, -1)
    flat = lambda w: w.reshape(-1, w.shape[-1])
    o_nsa = _nsa_sample(page_table, cache, q[:, None], kv[:, None], small[:, None], win, flat(wts["wlo"]),
                        flat(wts["whi"]), wts["pelo"], wts["pehi"], mt_s, pps=8).reshape(bs, NSA_Q_W)
    pad_rows = lambda a: jnp.pad(a[:, None], ((0, 0), (0, SUBLANES - 1), (0, 0))).reshape(bs * SUBLANES, -1)
    smallt = jnp.pad(small[:, A_COL:A_COL + SUBLANES][:, :, None], ((0, 0), (0, 0), (0, SUBLANES - 1)))
    o_gdn, s_s = _gdn(pad_rows(conv_in), pad_rows(z), pad_rows(small), smallt, state_gdn_conv[l], state_gdn_S[l],
                      wts["cw"], wts["pcol"], wts["prow"], wts["nw"], batch=bs, seq=SUBLANES, tb=SUBLANES,
                      chunk=SUBLANES, t_real=1)
    o_gdn = o_gdn.reshape(bs, SUBLANES, GDN_V_W)[:, 0]
    y_s = _mix_ffn(xs, o_nsa, o_gdn, wts["wo"], g_ffn, wgu, wd, g_fin, tm=bs)
    kv_s = kv.reshape(bs, 1, 3, 2, NSA_KV_HEADS, HEAD_DIM)
    win_s = jnp.concatenate([cache_nsa_win[l], kv_s[:, :, 2]], axis=1)[:, -min(WINDOW, past + 1):]
    conv_s = jnp.concatenate([state_gdn_conv[l], conv_in[:, None]], axis=1)[:, -hist:]

    return (y_p.reshape(bp, tp, D_MODEL), y_s.reshape(bs, 1, D_MODEL),
            kv_p[:, :, 0:2].reshape(bp, tp, 4, NSA_KV_HEADS, HEAD_DIM)[None],
            kv_p[:, tp - min(WINDOW, tp):, 2][None], s_p[None], conv_p[None],
            kv_s[:, :, 0:2].reshape(bs, 1, 4, NSA_KV_HEADS, HEAD_DIM)[None], win_s[None], s_s[None], conv_s[None])
```

```python
import functools

import numpy as np
import jax
import jax.numpy as jnp
from jax import lax
from jax.experimental import pallas as pl
from jax.experimental.pallas import tpu as pltpu

F32 = jnp.float32
BF16 = jnp.bfloat16

D_MODEL = 1024
PAGE_SIZE = 128
HEAD_DIM = 64
NSA_HEADS = 8
NSA_KV_HEADS = 2
NSA_REP = NSA_HEADS // NSA_KV_HEADS
CMP_LEN = 32
CMP_STRIDE = 16
SEL_LEN = 64
SEL_TOPK = 16
WINDOW = 512
Q_BLOCK = 128
ROPE_THETA = 10000.0
GDN_HEADS = 4
GDN_DK = 128
GDN_DV = 128
CONV_W = 4
NSA_Q_W = NSA_HEADS * HEAD_DIM
NSA_KV_W = NSA_KV_HEADS * HEAD_DIM
GDN_QK_W = GDN_HEADS * GDN_DK
GDN_V_W = GDN_HEADS * GDN_DV
GDN_CONV_CH = 2 * GDN_QK_W + GDN_V_W
NEG = -1e30
LOG2E = 1.4426950408889634
SEL_BONUS = 1e4
EPS = 1e-6

LANES = 128
SUBLANES = 8
VMEM_LIMIT = 56 * 1024 * 1024

GATE_COLS = 3 * NSA_HEADS
A_COL = GATE_COLS
B_COL = GATE_COLS + GDN_HEADS

NT_DIMS = (((1,), (1,)), ((), ()))
TN_DIMS = (((0,), (0,)), ((), ()))


def _dot(a, b):
    return jnp.dot(a, b, preferred_element_type=F32)


def _dot_nt(a, b):
    return lax.dot_general(a, b, NT_DIMS, preferred_element_type=F32)


def _sigmoid(x):
    return 1.0 / (1.0 + jnp.exp(-x))


def _silu(x):
    return x * _sigmoid(x)


def _split3(x):
    p1 = x.astype(BF16)
    r1 = x - p1.astype(F32)
    p2 = r1.astype(BF16)
    p3 = (r1 - p2.astype(F32)).astype(BF16)
    return p1, p2, p3


def _const_spec(shape):
    nd = len(shape)
    return pl.BlockSpec(shape, lambda *_: (0,) * nd, pipeline_mode=pl.Buffered(1))


def _proj_kernel(x_ref, g_ref, wm_ref, ws_ref, cos_ref, sin_ref,
                 q_ref, kv_ref, conv_ref, z_ref, small_ref, kaug_ref, vaug_ref, kwin_ref, vwaug_ref,
                 *, tm, pos_rows):
    x = x_ref[...]
    ms = jnp.mean(x * x, axis=-1, keepdims=True)
    xn = (x * lax.rsqrt(ms + EPS) * g_ref[...]).astype(BF16)
    cos = cos_ref[...]
    sin = sin_ref[...]
    lane = lax.broadcasted_iota(jnp.int32, (tm, LANES), 1)
    low_half = (lane % HEAD_DIM) < (HEAD_DIM // 2)

    def rope(v):
        rot = jnp.where(low_half, pltpu.roll(v, LANES - HEAD_DIM // 2, 1), pltpu.roll(v, HEAD_DIM // 2, 1))
        return v * cos + rot * sin

    q = _dot(xn, wm_ref[:, 0:NSA_Q_W])
    for r in range(NSA_REP):
        sl = slice(r * LANES, (r + 1) * LANES)
        q_ref[:, sl] = (rope(q[:, sl]) * (HEAD_DIM ** -0.5 * LOG2E)).astype(BF16)

    kv = _dot(xn, wm_ref[:, NSA_Q_W:NSA_Q_W + 6 * NSA_KV_W])
    group0 = lane < HEAD_DIM
    for c in range(6):
        sl = slice(c * LANES, (c + 1) * LANES)
        blk = kv[:, sl]
        if c % 2 == 0:
            blk = rope(blk)
        kv_ref[:, sl] = blk
        if c == 2:
            kaug_ref[:, LANES:2 * LANES] = blk.astype(BF16)
        elif c == 4:
            kwin_ref[...] = blk.astype(BF16)
        elif c in (3, 5):
            v_ref = vaug_ref if c == 3 else vwaug_ref
            v_ref[:, 0:LANES] = jnp.where(group0, blk, 1.0).astype(BF16)
            v_ref[:, LANES:2 * LANES] = jnp.where(group0, 1.0, blk).astype(BF16)

    row0 = (pl.program_id(0) * tm) % pos_rows
    rows = row0 + lax.broadcasted_iota(jnp.int32, (tm, LANES), 0)
    kaug_ref[:, 0:LANES] = jnp.where(rows // SEL_LEN == lane, 1.0, 0.0).astype(BF16)

    c0 = NSA_Q_W + 6 * NSA_KV_W
    conv_ref[...] = _dot(xn, wm_ref[:, c0:c0 + GDN_CONV_CH])
    z_ref[...] = _dot(xn, wm_ref[:, c0 + GDN_CONV_CH:c0 + GDN_CONV_CH + GDN_V_W])
    small_ref[...] = _dot(xn, ws_ref[...])


def _proj(x, g, wm, ws, cos, sin, *, tm):
    rows = x.shape[0]
    pos_rows = cos.shape[0]
    n_pos_blk = pos_rows // tm
    grid = (rows // tm,)
    row_spec = lambda w: pl.BlockSpec((tm, w), lambda i: (i, 0))
    pos_spec = pl.BlockSpec((tm, LANES), lambda i: (i % n_pos_blk, 0))
    out_shape = (
        jax.ShapeDtypeStruct((rows, NSA_Q_W), BF16),
        jax.ShapeDtypeStruct((rows, 6 * NSA_KV_W), F32),
        jax.ShapeDtypeStruct((rows, GDN_CONV_CH), F32),
        jax.ShapeDtypeStruct((rows, GDN_V_W), F32),
        jax.ShapeDtypeStruct((rows, LANES), F32),
        jax.ShapeDtypeStruct((rows, 2 * LANES), BF16),
        jax.ShapeDtypeStruct((rows, 2 * LANES), BF16),
        jax.ShapeDtypeStruct((rows, LANES), BF16),
        jax.ShapeDtypeStruct((rows, 2 * LANES), BF16),
    )
    return pl.pallas_call(
        functools.partial(_proj_kernel, tm=tm, pos_rows=pos_rows),
        grid=grid,
        in_specs=[row_spec(D_MODEL), _const_spec((1, D_MODEL)), _const_spec(wm.shape), _const_spec(ws.shape),
                  pos_spec, pos_spec],
        out_specs=tuple(row_spec(s.shape[1]) for s in out_shape),
        out_shape=out_shape,
        compiler_params=pltpu.CompilerParams(dimension_semantics=("arbitrary",), vmem_limit_bytes=VMEM_LIMIT),
        name="proj",
    )(x, g, wm, ws, cos, sin)


def _mix_ffn_kernel(x_ref, on_ref, og_ref, wo_ref, gf_ref, wgu_ref, wd_ref, gl_ref, y_ref, *, d_ff):
    h = x_ref[...] + _dot(on_ref[...], wo_ref[0:NSA_Q_W, :]) + _dot(og_ref[...], wo_ref[NSA_Q_W:, :])
    ms = jnp.mean(h * h, axis=-1, keepdims=True)
    hn = (h * lax.rsqrt(ms + EPS) * gf_ref[...]).astype(BF16)
    gate = _dot(hn, wgu_ref[:, 0:d_ff])
    up = _dot(hn, wgu_ref[:, d_ff:])
    act = (_silu(gate) * up).astype(BF16)
    h = h + _dot(act, wd_ref[...])
    ms = jnp.mean(h * h, axis=-1, keepdims=True)
    y_ref[...] = h * lax.rsqrt(ms + EPS) * gl_ref[...]


def _mix_ffn(x, o_nsa, o_gdn, wo, gf, wgu, wd, gl, *, tm):
    rows = x.shape[0]
    d_ff = wd.shape[0]
    row_spec = lambda w: pl.BlockSpec((tm, w), lambda i: (i, 0))
    return pl.pallas_call(
        functools.partial(_mix_ffn_kernel, d_ff=d_ff),
        grid=(rows // tm,),
        in_specs=[row_spec(D_MODEL), row_spec(NSA_Q_W), row_spec(GDN_V_W), _const_spec(wo.shape),
                  _const_spec((1, D_MODEL)), _const_spec(wgu.shape), _const_spec(wd.shape),
                  _const_spec((1, D_MODEL))],
        out_specs=row_spec(D_MODEL),
        out_shape=jax.ShapeDtypeStruct((rows, D_MODEL), F32),
        compiler_params=pltpu.CompilerParams(dimension_semantics=("arbitrary",), vmem_limit_bytes=VMEM_LIMIT),
        name="mix_ffn",
    )(x, o_nsa, o_gdn, wo, gf, wgu, wd, gl)


def _compress_rows(load_rows, wlo_ref, whi_ref, pelo_ref, pehi_ref, nc):
    acc_lo = jnp.zeros((nc, 2 * LANES), F32)
    acc_hi = jnp.zeros((nc, 2 * LANES), F32)
    for l in range(CMP_STRIDE):
        x = load_rows(l)
        acc_lo = acc_lo + _dot((x + pelo_ref[l:l + 1, :]).astype(BF16), wlo_ref[l])
        acc_hi = acc_hi + _dot((x + pehi_ref[l:l + 1, :]).astype(BF16), whi_ref[l])
    nxt = pltpu.roll(acc_hi, nc - 1, 0)
    row = lax.broadcasted_iota(jnp.int32, (nc, 2 * LANES), 0)
    return jnp.where(row < nc - 1, acc_lo + nxt, 0.0)


def _compress_kernel(k_ref, v_ref, wlo_ref, whi_ref, pelo_ref, pehi_ref, ckv_ref, *, nc):
    load = lambda l: jnp.concatenate([k_ref[pl.ds(l, nc, stride=CMP_STRIDE), :],
                                      v_ref[pl.ds(l, nc, stride=CMP_STRIDE), :]], axis=1)
    ckv_ref[...] = _compress_rows(load, wlo_ref, whi_ref, pelo_ref, pehi_ref, nc).astype(BF16)


def _compress(kv, wlo, whi, pelo, pehi, *, batch, seq):
    nc = seq // CMP_STRIDE
    return pl.pallas_call(
        functools.partial(_compress_kernel, nc=nc),
        grid=(batch,),
        in_specs=[pl.BlockSpec((seq, LANES), lambda b: (b, 0)), pl.BlockSpec((seq, LANES), lambda b: (b, 1)),
                  _const_spec(wlo.shape), _const_spec(whi.shape), _const_spec(pelo.shape), _const_spec(pehi.shape)],
        out_specs=pl.BlockSpec((nc, 2 * LANES), lambda b: (b, 0)),
        out_shape=jax.ShapeDtypeStruct((batch * nc, 2 * LANES), BF16),
        compiler_params=pltpu.CompilerParams(dimension_semantics=("arbitrary",), vmem_limit_bytes=VMEM_LIMIT),
        name="compress",
    )(kv, kv, wlo, whi, pelo, pehi)


def _masked_softmax2(s):
    m = jnp.maximum(jnp.max(s, axis=-1, keepdims=True), NEG / 8)
    e = jnp.exp2(s - m)
    return e * (1.0 / jnp.maximum(jnp.sum(e, axis=-1, keepdims=True), 1e-30))


def _select_blocks(score, sel, blk, axis, rounds):
    n = score.shape[axis]
    for _ in range(rounds):
        mx = jnp.max(score, axis=axis, keepdims=True)
        idx = jnp.min(jnp.where(score == mx, blk, n), axis=axis, keepdims=True)
        hit = blk == idx
        sel = jnp.where(hit, jnp.where(mx > NEG / 2, 1.0, sel), sel)
        score = jnp.where(hit, -3e38, score)
    return sel


def _nsa_prompt_kernel(q_ref, small_ref, kaug_ref, vaug_ref, kwin_ref, vwaug_ref, ckv_ref, mt_ref, o_ref,
                       *, seq, kc):
    nc = seq // CMP_STRIDE
    qb = Q_BLOCK
    start = pl.program_id(1) * qb
    wlen = WINDOW + qb
    groups = range(NSA_KV_HEADS)
    stack = lambda x: jnp.concatenate([x] * NSA_REP, axis=0)

    lane = lax.broadcasted_iota(jnp.int32, (qb, LANES), 1)
    gates = _sigmoid(small_ref[...])

    qrow_c = lax.broadcasted_iota(jnp.int32, (qb, nc), 0)
    ncol = lax.broadcasted_iota(jnp.int32, (qb, nc), 1)
    cmp_bias = stack(jnp.where(ncol * CMP_STRIDE + (CMP_LEN - 1) - qrow_c <= start, 0.0, NEG))
    tail_bias = stack(jnp.where(lane <= lax.broadcasted_iota(jnp.int32, (qb, qb), 0), 0.0, NEG))
    wbase = pl.multiple_of(jnp.maximum(start - WINDOW, 0), qb)
    back = (lax.broadcasted_iota(jnp.int32, (qb, wlen), 0) - lax.broadcasted_iota(jnp.int32, (qb, wlen), 1)
            + (start - wbase))
    win_bias = stack(jnp.where(back >= 0, jnp.where(back <= WINDOW, 0.0, NEG), NEG))

    qs, o_cmps, imps = [], [], []
    for g in groups:
        mine = (lane >= HEAD_DIM) == (g == 1)
        q = jnp.concatenate(
            [jnp.where(mine, q_ref[:, r * LANES:(r + 1) * LANES], 0.0).astype(BF16) for r in range(NSA_REP)], axis=0)
        p = _masked_softmax2(_dot_nt(q, ckv_ref[:, 0:LANES]) + cmp_bias)
        o_cmps.append(_dot(p.astype(BF16), ckv_ref[:, LANES:2 * LANES]))
        psum = p[0:qb] + p[qb:2 * qb] + p[2 * qb:3 * qb] + p[3 * qb:4 * qb]
        imps.append(sum(_dot_nt(mt_ref[...], piece) for piece in _split3(psum)))
        qs.append(q)

    blk_t = lax.broadcasted_iota(jnp.int32, (LANES, 2 * qb), 0)
    qpos_t = start + lax.broadcasted_iota(jnp.int32, (LANES, 2 * qb), 1) % qb
    cur_t = qpos_t // SEL_LEN
    visible = blk_t * SEL_LEN <= qpos_t
    forced = (blk_t == 0) | (blk_t == cur_t) | (blk_t == cur_t - 1)
    imp_t = jnp.concatenate(imps, axis=1)
    sel_t = _select_blocks(jnp.where(visible, jnp.where(forced, NEG, imp_t), NEG),
                           jnp.where(visible, jnp.where(forced, 1.0, 0.0), 0.0), blk_t, 0, SEL_TOPK - 3)
    bias_t = jnp.where(blk_t * SEL_LEN < start, jnp.where(sel_t > 0.5, 0.0, NEG), NEG).T.astype(BF16)
    n_chunks = (start + kc - 1) // kc
    q_augs, m0s, acc0s = [], [], []
    for g in groups:
        gl = slice(g * LANES, (g + 1) * LANES)
        s_tail = _dot_nt(qs[g], kaug_ref[pl.ds(start, qb), LANES:2 * LANES]) + tail_bias
        m0 = jnp.max(s_tail, axis=-1, keepdims=True)
        q_augs.append(jnp.concatenate([stack(bias_t[g * qb:(g + 1) * qb]), qs[g]], axis=1))
        m0s.append(m0)
        acc0s.append(_dot(jnp.exp2(s_tail - m0).astype(BF16), vaug_ref[pl.ds(start, qb), gl]))

    def sel_step(c, carry):
        off = pl.multiple_of(c * kc, kc)
        k = kaug_ref[pl.ds(off, kc), :]
        out = []
        for g in groups:
            m, acc = carry[g]
            s = _dot_nt(q_augs[g], k)
            m_new = jnp.maximum(m, jnp.max(s, axis=-1, keepdims=True))
            e = jnp.exp2(s - m_new).astype(BF16)
            out.append((m_new, jnp.exp2(m - m_new) * acc + _dot(e, vaug_ref[pl.ds(off, kc), g * LANES:(g + 1) * LANES])))
        return tuple(out)

    sel_out = lax.fori_loop(0, n_chunks, sel_step, tuple(zip(m0s, acc0s)))

    outs = [None] * NSA_REP
    for g in groups:
        acc = sel_out[g][1]
        o_sel = acc / pltpu.roll(acc, HEAD_DIM, 1)

        s = _dot_nt(qs[g], kwin_ref[pl.ds(wbase, wlen), :]) + win_bias
        e = jnp.exp2(s - jnp.max(s, axis=-1, keepdims=True)).astype(BF16)
        acc = _dot(e, vwaug_ref[pl.ds(wbase, wlen), g * LANES:(g + 1) * LANES])
        o_win = acc / pltpu.roll(acc, HEAD_DIM, 1)

        for r in range(NSA_REP):
            c0 = (g * NSA_REP + r) * 3
            rs = slice(r * qb, (r + 1) * qb)
            o = (gates[:, c0:c0 + 1] * o_cmps[g][rs] + gates[:, c0 + 1:c0 + 2] * o_sel[rs]
                 + gates[:, c0 + 2:c0 + 3] * o_win[rs])
            outs[r] = o if g == 0 else jnp.where(lane < HEAD_DIM, outs[r], o)

    for r in range(NSA_REP):
        o_ref[:, r * LANES:(r + 1) * LANES] = outs[r].astype(BF16)


def _nsa_prompt(q, small, kaug, vaug, kwin, vwaug, ckv, mt, *, batch, seq, kc):
    nb = seq // Q_BLOCK
    nc = seq // CMP_STRIDE
    blk_spec = lambda w: pl.BlockSpec((Q_BLOCK, w), lambda b, i: (b * nb + i, 0))
    seq_spec = lambda w: pl.BlockSpec((seq, w), lambda b, i: (b, 0))
    return pl.pallas_call(
        functools.partial(_nsa_prompt_kernel, seq=seq, kc=kc),
        grid=(batch, nb),
        in_specs=[blk_spec(NSA_Q_W), blk_spec(LANES), seq_spec(2 * LANES), seq_spec(2 * LANES), seq_spec(LANES),
                  seq_spec(2 * LANES), pl.BlockSpec((nc, 2 * LANES), lambda b, i: (b, 0)), _const_spec(mt.shape)],
        out_specs=blk_spec(NSA_Q_W),
        out_shape=jax.ShapeDtypeStruct((batch * seq, NSA_Q_W), BF16),
        compiler_params=pltpu.CompilerParams(dimension_semantics=("arbitrary", "arbitrary"),
                                             vmem_limit_bytes=VMEM_LIMIT),
        name="nsa_prompt",
    )(q, small, kaug, vaug, kwin, vwaug, ckv, mt)


def _nsa_sample_kernel(pt_ref, *refs, past, pps):
    del pt_ref
    pages = refs[:pps]
    (q_ref, kvn_ref, small_ref, win_ref, perm_ref, wlo_ref, whi_ref, pelo_ref, pehi_ref, mt_ref, exp_ref,
     o_ref, xs_ref, s_ref, vs_ref) = refs[pps:]
    step = pl.program_id(1)
    n_pages = past // PAGE_SIZE
    cpp = PAGE_SIZE // CMP_STRIDE
    heads = NSA_HEADS
    row = lax.broadcasted_iota(jnp.int32, (heads, LANES), 0)
    lane = lax.broadcasted_iota(jnp.int32, (heads, LANES), 1)
    mine = (lane >= HEAD_DIM) == (row >= NSA_REP)
    rr = row % NSA_REP
    qsel = jnp.zeros((heads, LANES), F32)
    for r in range(NSA_REP):
        qsel = jnp.where(rr == r, q_ref[:, r * LANES:(r + 1) * LANES].astype(F32), qsel)
    qf = jnp.where(mine, qsel, 0.0)
    q = qf.astype(BF16)

    nc = past // CMP_STRIDE
    cw = 2 * LANES

    def take_pages(first_page):
        for k in range(pps):
            pg = first_page + k
            page = pages[k]
            tok = slice(pg * PAGE_SIZE, (pg + 1) * PAGE_SIZE)
            s_ref[:, tok] = _dot(q, page[2 * LANES:3 * LANES, :].astype(BF16))
            vs_ref[:, tok] = page[3 * LANES:4 * LANES, :].astype(BF16)
            x = _dot_nt(perm_ref[...], page[0:2 * LANES, :].astype(BF16))
            for l in range(CMP_STRIDE):
                xs_ref[pg * cpp:(pg + 1) * cpp, l * cw:(l + 1) * cw] = x[l * cpp:(l + 1) * cpp, :]

    for st in range(n_pages // pps):
        pl.when(step == st)(functools.partial(take_pages, st * pps))

    @pl.when(step == pl.num_programs(1) - 1)
    def _():
        n_cmp = nc - 1
        nsp = mt_ref.shape[0]
        cur = past // SEL_LEN

        for l in range(CMP_STRIDE):
            xs_ref[nc:nc + SUBLANES, l * cw:(l + 1) * cw] = jnp.broadcast_to(pelo_ref[l:l + 1, :], (SUBLANES, cw))
            xs_ref[nc + SUBLANES:nc + 2 * SUBLANES, l * cw:(l + 1) * cw] = jnp.broadcast_to(pehi_ref[l:l + 1, :],
                                                                                            (SUBLANES, cw))
        x16 = xs_ref[...].astype(BF16)
        acc_lo = _dot(x16, wlo_ref[...])
        acc_hi = _dot(x16, whi_ref[...])
        bias = acc_lo[nc:nc + 1] + acc_hi[nc + SUBLANES:nc + SUBLANES + 1]
        nxt = pltpu.roll(acc_hi[0:nc], nc - 1, 0)
        crow = lax.broadcasted_iota(jnp.int32, (nc, 2 * LANES), 0)
        ckv = jnp.where(crow < nc - 1, acc_lo[0:nc] + nxt + bias, 0.0).astype(BF16)

        ncol = lax.broadcasted_iota(jnp.int32, (1, nc), 1)
        cmp_valid = (ncol * CMP_STRIDE + CMP_LEN - 1 <= past) & (ncol < n_cmp)
        p = _masked_softmax2(jnp.where(cmp_valid, _dot_nt(q, ckv[:, 0:LANES]), NEG))
        o_cmp = _dot(p.astype(BF16), ckv[:, LANES:2 * LANES])

        prow = lax.broadcasted_iota(jnp.int32, (heads, nc), 0)
        ps = [jnp.sum(jnp.where(prow // NSA_REP == g, p, 0.0), axis=0, keepdims=True) for g in range(NSA_KV_HEADS)]
        psum = jnp.where(prow < NSA_REP, ps[0], ps[1])
        imp = sum(_dot_nt(piece, mt_ref[...]) for piece in _split3(psum))
        blk = lax.broadcasted_iota(jnp.int32, (heads, nsp), 1)
        forced = (blk == 0) | (blk == cur) | (blk == cur - 1)
        score = jnp.where(blk * SEL_LEN <= past, imp + jnp.where(forced, SEL_BONUS, 0.0), NEG)
        score_t = score.T
        bi = lax.broadcasted_iota(jnp.int32, (nsp, nsp), 0)
        bj = lax.broadcasted_iota(jnp.int32, (nsp, nsp), 1)
        tie = jnp.where(bi < bj, 1.0, 0.0)
        picks = []
        for g in range(NSA_KV_HEADS):
            c = g * NSA_REP
            s_i, s_j = score_t[:, c:c + 1], score[c:c + 1, :]
            beats = jnp.where(s_i > s_j, 1.0, jnp.where(s_i == s_j, tie, 0.0))
            rank = jnp.sum(beats, axis=0, keepdims=True)
            picks.append(jnp.where(rank < SEL_TOPK, jnp.where(s_j > NEG / 2, 1.0, 0.0), 0.0))
        sel8 = jnp.where(lax.broadcasted_iota(jnp.int32, (heads, nsp), 0) < NSA_REP, picks[0], picks[1])
        keep = _dot(sel8.astype(BF16), exp_ref[...]) > 0.5

        kvn = kvn_ref[...]
        new_ok = sel8[:, cur:cur + 1] > 0.5
        s = jnp.where(keep, s_ref[...], NEG)
        s_new = jnp.where(new_ok, jnp.sum(qf * kvn[:, 2 * LANES:3 * LANES], axis=-1, keepdims=True), NEG)
        m = jnp.maximum(jnp.max(s, axis=-1, keepdims=True), s_new)
        e = jnp.exp2(s - m)
        e_new = jnp.where(new_ok, jnp.exp2(s_new - m), 0.0)
        o_sel = ((_dot_nt(e.astype(BF16), vs_ref[...]) + e_new * kvn[:, 3 * LANES:4 * LANES])
                 / (jnp.sum(e, axis=-1, keepdims=True) + e_new))

        nw = win_ref.shape[0] // 2
        s = _dot(q, win_ref[0:nw, :].astype(BF16))
        s_new = jnp.sum(qf * kvn[:, 4 * LANES:5 * LANES], axis=-1, keepdims=True)
        m = jnp.maximum(jnp.max(s, axis=-1, keepdims=True), s_new)
        e = jnp.exp2(s - m)
        e_new = jnp.exp2(s_new - m)
        l = jnp.sum(e, axis=-1, keepdims=True) + e_new
        o_win = (_dot_nt(e.astype(BF16), win_ref[nw:, :].astype(BF16))
                 + e_new * kvn[:, 5 * LANES:6 * LANES]) / l

        gates = _sigmoid(small_ref[...])
        gate = lambda br: jnp.sum(jnp.where(lane == row * 3 + br, gates, 0.0), axis=-1, keepdims=True)
        o = gate(0) * o_cmp + gate(1) * o_sel + gate(2) * o_win
        for r in range(NSA_REP):
            o_ref[:, r * LANES:(r + 1) * LANES] = jnp.where(
                lane[0:1] < HEAD_DIM, o[r:r + 1], o[NSA_REP + r:NSA_REP + r + 1]).astype(BF16)


def _nsa_sample(page_table, cache, q, kvn, small, win, wlo, whi, pelo, pehi, mt, *, pps):
    batch, n_pages = page_table.shape
    past = n_pages * PAGE_SIZE
    rows = cache.shape[1]
    cpp = PAGE_SIZE // CMP_STRIDE
    nsp = mt.shape[0]
    expand = jnp.asarray(np.arange(nsp)[:, None] == np.arange(past)[None, :] // SEL_LEN, BF16)
    tok = np.arange(PAGE_SIZE)
    perm = jnp.asarray((tok[None, :] % CMP_STRIDE) * cpp + tok[None, :] // CMP_STRIDE == tok[:, None], BF16)

    def page_spec(k):
        return pl.BlockSpec((None, rows, PAGE_SIZE), lambda b, s, pt: (pt[b * n_pages + s * pps + k], 0, 0))

    per_b = lambda shape: pl.BlockSpec((None,) + shape, lambda b, s, pt: (b, 0, 0))
    const = lambda a: pl.BlockSpec(a.shape, lambda b, s, pt: (0,) * a.ndim, pipeline_mode=pl.Buffered(1))
    grid_spec = pltpu.PrefetchScalarGridSpec(
        num_scalar_prefetch=1,
        grid=(batch, n_pages // pps),
        in_specs=[page_spec(k) for k in range(pps)] + [
            per_b((1, NSA_Q_W)), per_b((1, 6 * NSA_KV_W)), per_b((1, LANES)), per_b(win.shape[1:]),
            const(perm), const(wlo), const(whi), const(pelo), const(pehi), const(mt), const(expand)],
        out_specs=per_b((1, NSA_Q_W)),
        scratch_shapes=[pltpu.VMEM((past // CMP_STRIDE + 2 * SUBLANES, CMP_STRIDE * 2 * LANES), F32),
                        pltpu.VMEM((NSA_HEADS, past), F32),
                        pltpu.VMEM((LANES, past), BF16)],
    )
    return pl.pallas_call(
        functools.partial(_nsa_sample_kernel, past=past, pps=pps),
        grid_spec=grid_spec,
        out_shape=jax.ShapeDtypeStruct((batch, 1, NSA_Q_W), BF16),
        compiler_params=pltpu.CompilerParams(dimension_semantics=("arbitrary", "arbitrary"),
                                             vmem_limit_bytes=VMEM_LIMIT),
        name="nsa_sample",
    )(page_table.reshape(-1), *([cache] * pps), q, kvn, small, win, perm, wlo, whi, pelo, pehi, mt, expand)


def _l2norm(x):
    return x * lax.rsqrt(jnp.sum(x * x, axis=-1, keepdims=True) + EPS)


def _softplus(x):
    return jnp.maximum(x, 0.0) + jnp.log(1.0 + jnp.exp(-jnp.abs(x)))


def _gdn_kernel(conv_ref, z_ref, small_ref, smallt_ref, buf_ref, s0_ref, cw_ref, pcol_ref, prow_ref, nw_ref,
                tril_ref, triu_ref, o_ref, sout_ref, xs_ref, s_ref, *, tb, chunk, t_real):
    j = pl.program_id(1)
    pad = SUBLANES
    hist = CONV_W - 1

    @pl.when(j == 0)
    def _():
        xs_ref[0:pad, :] = jnp.zeros((pad, GDN_CONV_CH), F32)
        xs_ref[pad - hist:pad, :] = buf_ref[...]
        s_ref[...] = s0_ref[...]

    xs_ref[pad:pad + tb, :] = conv_ref[...]
    conv = xs_ref[pad - hist:pad - hist + tb, :] * cw_ref[0:1, :]
    for t in range(1, CONV_W):
        conv = conv + xs_ref[pad - hist + t:pad - hist + t + tb, :] * cw_ref[t:t + 1, :]
    xs_ref[pad - hist:pad, :] = xs_ref[pad + tb - hist:pad + tb, :]
    act = _silu(conv)

    small = small_ref[...]
    tok_c = j * tb + lax.broadcasted_iota(jnp.int32, (tb, LANES), 0)
    g_col = jnp.where(tok_c < t_real, -jnp.exp(pcol_ref[0:1, :]) * _softplus(small + pcol_ref[1:2, :]), 0.0)
    beta = jnp.where(tok_c < t_real, _sigmoid(small), 0.0)
    tok_r = j * tb + lax.broadcasted_iota(jnp.int32, (SUBLANES, tb), 1)
    g_row = jnp.where(tok_r < t_real,
                      -jnp.exp(prow_ref[:, 0:1]) * _softplus(smallt_ref[...] + prow_ref[:, 1:2]), 0.0)
    gcum_col = sum(_dot(tril_ref[...], piece) for piece in _split3(g_col))
    gcum_row = sum(_dot(piece, triu_ref[...]) for piece in _split3(g_row))

    ii = lax.broadcasted_iota(jnp.int32, (chunk, chunk), 0)
    jj = lax.broadcasted_iota(jnp.int32, (chunk, chunk), 1)
    incl = ii >= jj
    eye = jnp.where(ii == jj, 1.0, 0.0)

    for ci in range(tb // chunk):
        r0 = ci * chunk
        rs = slice(r0, r0 + chunk)
        for h in range(GDN_HEADS):
            hs = slice(h * GDN_DK, (h + 1) * GDN_DK)
            gc = gcum_col[rs, A_COL + h:A_COL + h + 1]
            gr = gcum_row[h:h + 1, r0:r0 + chunk]
            glast = gcum_col[r0 + chunk - 1:r0 + chunk, A_COL + h:A_COL + h + 1]
            bt = beta[rs, B_COL + h:B_COL + h + 1]
            dec = jnp.where(incl, jnp.exp(jnp.where(incl, gc - gr, 0.0)), 0.0)
            qh = _l2norm(act[rs, hs]) * (GDN_DK ** -0.5)
            kh = _l2norm(act[rs, GDN_QK_W + h * GDN_DK:GDN_QK_W + (h + 1) * GDN_DK])
            vh = act[rs, 2 * GDN_QK_W + h * GDN_DV:2 * GDN_QK_W + (h + 1) * GDN_DV]
            kb = kh * bt
            k16 = kh.astype(BF16)
            lm = jnp.where(ii > jj, _dot_nt(kb.astype(BF16), k16) * dec, 0.0)
            ainv = eye - lm
            pw = lm
            n = 2
            while n < chunk:
                pw16 = pw.astype(BF16)
                pw = _dot(pw16, pw16)
                ainv = ainv + _dot(ainv.astype(BF16), pw.astype(BF16))
                n *= 2
            eg = jnp.exp(gc)
            rhs = jnp.concatenate([vh * bt, kb * eg], axis=1).astype(BF16)
            sol = _dot(ainv.astype(BF16), rhs)
            u = sol[:, 0:GDN_DV]
            w = sol[:, GDN_DV:]
            aqk = _dot_nt(qh.astype(BF16), k16) * dec
            s16 = s_ref[h].astype(BF16)
            vn = u - _dot(w.astype(BF16), s16)
            vn16 = vn.astype(BF16)
            o = _dot((qh * eg).astype(BF16), s16) + _dot(aqk.astype(BF16), vn16)
            kg = kh * jnp.exp(glast - gc)
            s_ref[h] = s_ref[h] * jnp.exp(glast) + lax.dot_general(kg.astype(BF16), vn16, TN_DIMS,
                                                                    preferred_element_type=F32)
            on = o * lax.rsqrt(jnp.mean(o * o, axis=-1, keepdims=True) + EPS) * nw_ref[...]
            o_ref[rs, hs] = (on * _silu(z_ref[rs, hs])).astype(BF16)

    sout_ref[...] = s_ref[...]


def _gdn(conv_in, z, small, smallt, buf, s0, cw, pcol, prow, nw, *, batch, seq, tb, chunk, t_real):
    nblk = seq // tb
    blk = np.arange(tb)
    same = (blk[:, None] // chunk) == (blk[None, :] // chunk)
    tril = jnp.asarray(same & (blk[:, None] >= blk[None, :]), BF16)
    triu = jnp.asarray(same & (blk[:, None] <= blk[None, :]), BF16)
    row_spec = lambda w: pl.BlockSpec((tb, w), lambda b, j: (b * nblk + j, 0))
    per_b = lambda shape: pl.BlockSpec((None,) + shape, lambda b, j: (b,) + (0,) * len(shape))
    return pl.pallas_call(
        functools.partial(_gdn_kernel, tb=tb, chunk=chunk, t_real=t_real),
        grid=(batch, nblk),
        in_specs=[row_spec(GDN_CONV_CH), row_spec(GDN_V_W), row_spec(LANES),
                  pl.BlockSpec((None, SUBLANES, tb), lambda b, j: (b, 0, j)),
                  per_b((CONV_W - 1, GDN_CONV_CH)), per_b((GDN_HEADS, GDN_DK, GDN_DV)),
                  _const_spec(cw.shape), _const_spec(pcol.shape), _const_spec(prow.shape), _const_spec(nw.shape),
                  _const_spec(tril.shape), _const_spec(triu.shape)],
        out_specs=(row_spec(GDN_V_W), per_b((GDN_HEADS, GDN_DK, GDN_DV))),
        out_shape=(jax.ShapeDtypeStruct((batch * seq, GDN_V_W), BF16),
                   jax.ShapeDtypeStruct((batch, GDN_HEADS, GDN_DK, GDN_DV), F32)),
        scratch_shapes=[pltpu.VMEM((SUBLANES + tb, GDN_CONV_CH), F32),
                        pltpu.VMEM((GDN_HEADS, GDN_DK, GDN_DV), F32)],
        compiler_params=pltpu.CompilerParams(dimension_semantics=("arbitrary", "arbitrary"),
                                             vmem_limit_bytes=VMEM_LIMIT),
        name="gdn",
    )(conv_in, z, small, smallt, buf, s0, cw, pcol, prow, nw, tril, triu)


GDN_CHUNK = 64
GDN_STACK = GDN_HEADS * GDN_CHUNK


def _stack_heads(x, col0, width):
    return jnp.concatenate([x[:, col0 + h * width:col0 + (h + 1) * width] for h in range(GDN_HEADS)], axis=0)


def _gdn_prep_kernel(conv_ref, small_ref, smallt_ref, buf_ref, cw_ref, pcol_ref, prow_ref, tril_ref, triu_ref,
                     u_ref, w_ref, qg_ref, kgt_ref, aqk_ref, gl_ref, xs_ref, *, tb):
    j = pl.program_id(1)
    pad = SUBLANES
    hist = CONV_W - 1
    ck = GDN_CHUNK
    st = GDN_STACK

    @pl.when(j == 0)
    def _():
        xs_ref[0:pad, :] = jnp.zeros((pad, GDN_CONV_CH), F32)
        xs_ref[pad - hist:pad, :] = buf_ref[...]

    xs_ref[pad:pad + tb, :] = conv_ref[...]
    conv = xs_ref[pad - hist:pad - hist + tb, :] * cw_ref[0:1, :]
    for t in range(1, CONV_W):
        conv = conv + xs_ref[pad - hist + t:pad - hist + t + tb, :] * cw_ref[t:t + 1, :]
    xs_ref[pad - hist:pad, :] = xs_ref[pad + tb - hist:pad + tb, :]
    act = _silu(conv)

    small = small_ref[...]
    g_col = -jnp.exp(pcol_ref[0:1, :]) * _softplus(small + pcol_ref[1:2, :])
    beta = _sigmoid(small)
    g_row = -jnp.exp(prow_ref[:, 0:1]) * _softplus(smallt_ref[...] + prow_ref[:, 1:2])
    gcum_col = sum(_dot(tril_ref[...], piece) for piece in _split3(g_col))
    gcum_row = sum(_dot(piece, triu_ref[...]) for piece in _split3(g_row))

    ii = lax.broadcasted_iota(jnp.int32, (st, st), 0)
    jj = lax.broadcasted_iota(jnp.int32, (st, st), 1)
    same_head = (ii // ck) == (jj // ck)
    incl = same_head & (ii >= jj)
    strict = same_head & (ii > jj)
    eye = jnp.where(ii == jj, 1.0, 0.0)
    hrow = lax.broadcasted_iota(jnp.int32, (SUBLANES, LANES), 0)

    for ci in range(tb // ck):
        r0 = ci * ck
        rs = slice(r0, r0 + ck)
        gc = _stack_heads(gcum_col[rs], A_COL, 1)
        bt = _stack_heads(beta[rs], B_COL, 1)
        gr = jnp.concatenate([gcum_row[h:h + 1, r0:r0 + ck] for h in range(GDN_HEADS)], axis=1)
        glast = [gcum_col[r0 + ck - 1:r0 + ck, A_COL + h:A_COL + h + 1] for h in range(GDN_HEADS)]
        gl_stack = jnp.concatenate([jnp.broadcast_to(x, (ck, 1)) for x in glast], axis=0)
        dec = jnp.where(incl, jnp.exp(jnp.where(incl, gc - gr, 0.0)), 0.0)
        qs = jnp.concatenate([_l2norm(act[rs, h * GDN_DK:(h + 1) * GDN_DK]) for h in range(GDN_HEADS)],
                             axis=0) * (GDN_DK ** -0.5)
        ks = jnp.concatenate([_l2norm(act[rs, GDN_QK_W + h * GDN_DK:GDN_QK_W + (h + 1) * GDN_DK])
                              for h in range(GDN_HEADS)], axis=0)
        vs = _stack_heads(act[rs], 2 * GDN_QK_W, GDN_DV)
        kb = ks * bt
        k16 = ks.astype(BF16)
        lm = jnp.where(strict, _dot_nt(kb.astype(BF16), k16) * dec, 0.0)
        ainv = eye - lm
        pw = lm
        n = 2
        while n < ck:
            pw16 = pw.astype(BF16)
            pw = _dot(pw16, pw16)
            ainv = ainv + _dot(ainv.astype(BF16), pw.astype(BF16))
            n *= 2
        eg = jnp.exp(gc)
        rhs = jnp.concatenate([vs * bt, kb * eg], axis=1).astype(BF16)
        sol = _dot(ainv.astype(BF16), rhs)
        orow = slice(ci * st, (ci + 1) * st)
        u_ref[orow, :] = sol[:, 0:GDN_DV]
        w_ref[orow, :] = sol[:, GDN_DV:].astype(BF16)
        qg_ref[orow, :] = (qs * eg).astype(BF16)
        aqk_ref[orow, :] = (_dot_nt(qs.astype(BF16), k16) * dec).astype(BF16)
        kgt_ref[ci * GDN_DK:(ci + 1) * GDN_DK, :] = (ks * jnp.exp(gl_stack - gc)).T.astype(BF16)
        gl = jnp.zeros((SUBLANES, LANES), F32)
        for h in range(GDN_HEADS):
            gl = jnp.where(hrow == h, jnp.exp(glast[h]), gl)
        gl_ref[ci * SUBLANES:(ci + 1) * SUBLANES, :] = gl


def _gdn_prep(conv_in, small, smallt, buf, cw, pcol, prow, *, batch, seq, tb):
    nblk = seq // tb
    ncb = tb // GDN_CHUNK
    blk = np.arange(tb)
    same = (blk[:, None] // GDN_CHUNK) == (blk[None, :] // GDN_CHUNK)
    tril = jnp.asarray(same & (blk[:, None] >= blk[None, :]), BF16)
    triu = jnp.asarray(same & (blk[:, None] <= blk[None, :]), BF16)
    row_spec = lambda r, w: pl.BlockSpec((r, w), lambda b, j: (b * nblk + j, 0))
    per_b = lambda shape: pl.BlockSpec((None,) + shape, lambda b, j: (b,) + (0,) * len(shape))
    n_chunks = batch * seq // GDN_CHUNK
    out_shape = (jax.ShapeDtypeStruct((n_chunks * GDN_STACK, GDN_DV), F32),
                 jax.ShapeDtypeStruct((n_chunks * GDN_STACK, GDN_DK), BF16),
                 jax.ShapeDtypeStruct((n_chunks * GDN_STACK, GDN_DK), BF16),
                 jax.ShapeDtypeStruct((n_chunks * GDN_DK, GDN_STACK), BF16),
                 jax.ShapeDtypeStruct((n_chunks * GDN_STACK, GDN_STACK), BF16),
                 jax.ShapeDtypeStruct((n_chunks * SUBLANES, LANES), F32))
    out_specs = (row_spec(ncb * GDN_STACK, GDN_DV), row_spec(ncb * GDN_STACK, GDN_DK),
                 row_spec(ncb * GDN_STACK, GDN_DK), row_spec(ncb * GDN_DK, GDN_STACK),
                 row_spec(ncb * GDN_STACK, GDN_STACK), row_spec(ncb * SUBLANES, LANES))
    return pl.pallas_call(
        functools.partial(_gdn_prep_kernel, tb=tb),
        grid=(batch, nblk),
        in_specs=[row_spec(tb, GDN_CONV_CH), row_spec(tb, LANES),
                  pl.BlockSpec((None, SUBLANES, tb), lambda b, j: (b, 0, j)),
                  per_b((CONV_W - 1, GDN_CONV_CH)),
                  _const_spec(cw.shape), _const_spec(pcol.shape), _const_spec(prow.shape),
                  _const_spec(tril.shape), _const_spec(triu.shape)],
        out_specs=out_specs,
        out_shape=out_shape,
        scratch_shapes=[pltpu.VMEM((SUBLANES + tb, GDN_CONV_CH), F32)],
        compiler_params=pltpu.CompilerParams(dimension_semantics=("arbitrary", "arbitrary"),
                                             vmem_limit_bytes=VMEM_LIMIT),
        name="gdn_prep",
    )(conv_in, small, smallt, buf, cw, pcol, prow, tril, triu)


def _gdn_scan_kernel(u_ref, w_ref, qg_ref, kgt_ref, aqk_ref, gl_ref, z_ref, s0_ref, nw_ref, o_ref, sout_ref, s_ref,
                     *, batch, ncb):
    ck = GDN_CHUNK
    st = GDN_STACK
    sw = GDN_HEADS * GDN_DK

    @pl.when(pl.program_id(0) == 0)
    def _():
        s_ref[...] = s0_ref[...]

    wide_mask = (lax.broadcasted_iota(jnp.int32, (st, sw), 0) // ck
                 == lax.broadcasted_iota(jnp.int32, (st, sw), 1) // GDN_DK)
    tall_mask = (lax.broadcasted_iota(jnp.int32, (sw, st), 0) // GDN_DK
                 == lax.broadcasted_iota(jnp.int32, (sw, st), 1) // ck)

    for ci in range(ncb):
        rows = slice(ci * st, (ci + 1) * st)
        for b in range(batch):
            w_bd = jnp.where(wide_mask, jnp.concatenate([w_ref[b, rows, :]] * GDN_HEADS, axis=1), 0.0)
            qg_bd = jnp.where(wide_mask, jnp.concatenate([qg_ref[b, rows, :]] * GDN_HEADS, axis=1), 0.0)
            kgt = kgt_ref[b, ci * GDN_DK:(ci + 1) * GDN_DK, :]
            kgt_bd = jnp.where(tall_mask, jnp.concatenate([kgt] * GDN_HEADS, axis=0), 0.0)
            gl = gl_ref[b, ci * SUBLANES:(ci + 1) * SUBLANES, :]
            gl_rows = jnp.concatenate([jnp.broadcast_to(gl[h:h + 1], (GDN_DK, GDN_DV)) for h in range(GDN_HEADS)],
                                      axis=0)
            s = s_ref[b]
            t1 = _dot(jnp.concatenate([w_bd, qg_bd], axis=0).astype(BF16), s.astype(BF16))
            vn16 = (u_ref[b, rows, :] - t1[0:st]).astype(BF16)
            o = t1[st:] + _dot(aqk_ref[b, rows, :], vn16)
            s_ref[b] = s * gl_rows + _dot(kgt_bd.astype(BF16), vn16)
            for h in range(GDN_HEADS):
                oh = o[h * ck:(h + 1) * ck]
                on = oh * lax.rsqrt(jnp.mean(oh * oh, axis=-1, keepdims=True) + EPS) * nw_ref[...]
                hs = slice(h * GDN_DV, (h + 1) * GDN_DV)
                o_ref[b, ci * ck:(ci + 1) * ck, hs] = (on * _silu(z_ref[b, ci * ck:(ci + 1) * ck, hs])).astype(BF16)

    sout_ref[...] = s_ref[...]


def _gdn_scan(u, w, qg, kgt, aqk, gl, z, s0, nw, *, batch, seq, tb):
    nblk = seq // tb
    ncb = tb // GDN_CHUNK
    cps = seq // GDN_CHUNK
    sw = GDN_HEADS * GDN_DK
    r3 = lambda a, rows_per_chunk: a.reshape(batch, cps * rows_per_chunk, a.shape[-1])
    blk = lambda rows, width: pl.BlockSpec((batch, rows, width), lambda j: (0, j, 0))
    full = pl.BlockSpec((batch, sw, GDN_DV), lambda j: (0, 0, 0))
    return pl.pallas_call(
        functools.partial(_gdn_scan_kernel, batch=batch, ncb=ncb),
        grid=(nblk,),
        in_specs=[blk(ncb * GDN_STACK, GDN_DV), blk(ncb * GDN_STACK, GDN_DK), blk(ncb * GDN_STACK, GDN_DK),
                  blk(ncb * GDN_DK, GDN_STACK), blk(ncb * GDN_STACK, GDN_STACK), blk(ncb * SUBLANES, LANES),
                  blk(tb, GDN_V_W), full, _const_spec(nw.shape)],
        out_specs=(blk(tb, GDN_V_W), full),
        out_shape=(jax.ShapeDtypeStruct((batch, seq, GDN_V_W), BF16),
                   jax.ShapeDtypeStruct((batch, sw, GDN_DV), F32)),
        scratch_shapes=[pltpu.VMEM((batch, sw, GDN_DV), F32)],
        compiler_params=pltpu.CompilerParams(dimension_semantics=("arbitrary",), vmem_limit_bytes=VMEM_LIMIT),
        name="gdn_scan",
    )(r3(u, GDN_STACK), r3(w, GDN_STACK), r3(qg, GDN_STACK), r3(kgt, GDN_DK), r3(aqk, GDN_STACK),
      r3(gl, SUBLANES), z.reshape(batch, seq, GDN_V_W), s0.reshape(batch, sw, GDN_DV), nw)


def _rope_tables(pos):
    half = HEAD_DIM // 2
    inv = jnp.power(ROPE_THETA, -jnp.arange(half, dtype=F32) * 2.0 / HEAD_DIM)
    ang = pos.astype(F32)[:, None] * inv[None, :]
    cos, sin = jnp.cos(ang), jnp.sin(ang)
    return jnp.tile(cos, (1, 4)), jnp.concatenate([-sin, sin, -sin, sin], axis=1)


def _cmp_to_sel_t(n_cmp, n_sel, rows, cols):
    cs = np.arange(cols)[None, :] * CMP_STRIDE
    ss = np.arange(rows)[:, None] * SEL_LEN
    hit = (cs < ss + SEL_LEN) & (cs + CMP_LEN > ss)
    hit &= (np.arange(cols)[None, :] < n_cmp) & (np.arange(rows)[:, None] < n_sel)
    return jnp.asarray(hit, BF16)


def _layer_weights(w_in, cmp_pe, cmp_w, conv_w, a_log, dt_bias, gdn_norm, w_out):
    cuts = np.cumsum([NSA_Q_W, 6 * NSA_KV_W, GATE_COLS, GDN_CONV_CH, GDN_V_W, GDN_HEADS]).tolist()
    wq, wkv, wgt, wconv, wz, wa, wb = jnp.split(w_in, cuts, axis=1)
    order = np.array([g * NSA_REP + r for r in range(NSA_REP) for g in range(NSA_KV_HEADS)])
    cols = (order[:, None] * HEAD_DIM + np.arange(HEAD_DIM)[None, :]).reshape(-1)
    wm = jnp.concatenate([wq[:, cols], wkv, wconv, wz], axis=1).astype(BF16)
    ws = jnp.concatenate([wgt, wa, wb], axis=1)
    ws = jnp.pad(ws, ((0, 0), (0, LANES - ws.shape[1]))).astype(BF16)
    wo = jnp.concatenate([w_out[:NSA_Q_W][cols], w_out[NSA_Q_W:]], axis=0).astype(BF16)

    def blockdiag(l0):
        wk, wv = cmp_w[0, l0:l0 + CMP_STRIDE], cmp_w[1, l0:l0 + CMP_STRIDE]
        z = jnp.zeros_like(wk)
        rows = [jnp.concatenate(r, axis=2) for r in ([wk, z, z, z], [z, wk, z, z], [z, z, wv, z], [z, z, z, wv])]
        return jnp.concatenate(rows, axis=1).astype(BF16)

    def pe_rows(l0):
        pk, pv = cmp_pe[0, l0:l0 + CMP_STRIDE], cmp_pe[1, l0:l0 + CMP_STRIDE]
        return jnp.concatenate([pk, pk, pv, pv], axis=1)

    pcol = jnp.zeros((2, LANES), F32).at[0, A_COL:A_COL + GDN_HEADS].set(a_log)
    pcol = pcol.at[1, A_COL:A_COL + GDN_HEADS].set(dt_bias)
    prow = jnp.zeros((SUBLANES, 2), F32).at[0:GDN_HEADS, 0].set(a_log).at[0:GDN_HEADS, 1].set(dt_bias)
    return dict(wm=wm, ws=ws, wo=wo, wlo=blockdiag(0), whi=blockdiag(CMP_STRIDE), pelo=pe_rows(0),
                pehi=pe_rows(CMP_STRIDE), cw=conv_w, pcol=pcol, prow=prow, nw=gdn_norm[None, :])


def kernel(x_prompt, x_sample, cache_nsa_kv, cache_nsa_win, state_gdn_S, state_gdn_conv, page_table, norm_mix, w_in,
           nsa_cmp_pe, nsa_cmp_w, gdn_conv_w, gdn_a_log, gdn_dt_bias, gdn_norm, w_out, norm_ffn, w_gate_up, w_down,
           norm_final):
    bp, tp, _ = x_prompt.shape
    bs, ts, _ = x_sample.shape
    depth = w_in.shape[0]
    n_pages = page_table.shape[1]
    past = n_pages * PAGE_SIZE
    assert depth == 1 and ts == 1, "one layer and one new token per sample row"
    assert tp % 512 == 0 and tp // SEL_LEN <= LANES and past % 512 == 0
    l = 0
    wts = _layer_weights(w_in[l], nsa_cmp_pe[l], nsa_cmp_w[l], gdn_conv_w[l], gdn_a_log[l], gdn_dt_bias[l],
                         gdn_norm[l], w_out[l])
    g_mix, g_ffn, g_fin = norm_mix[l][None, :], norm_ffn[l][None, :], norm_final[None, :]
    wgu, wd = w_gate_up[l].astype(BF16), w_down[l].astype(BF16)
    hist = CONV_W - 1

    xp = x_prompt.reshape(bp * tp, D_MODEL)
    cos_p, sin_p = _rope_tables(jnp.arange(tp, dtype=jnp.int32))
    q, kv, conv_in, z, small, kaug, vaug, kwin, vwaug = _proj(xp, g_mix, wts["wm"], wts["ws"], cos_p, sin_p, tm=256)
    ckv = _compress(kv, wts["wlo"], wts["whi"], wts["pelo"], wts["pehi"], batch=bp, seq=tp)
    nc = tp // CMP_STRIDE
    mt_p = _cmp_to_sel_t(nc - 1, tp // SEL_LEN, LANES, nc)
    o_nsa = _nsa_prompt(q, small, kaug, vaug, kwin, vwaug, ckv, mt_p, batch=bp, seq=tp, kc=512)
    smallt = small[:, A_COL:A_COL + SUBLANES].reshape(bp, tp, SUBLANES).transpose(0, 2, 1)
    prep = _gdn_prep(conv_in, small, smallt, jnp.zeros((bp, hist, GDN_CONV_CH), F32), wts["cw"], wts["pcol"],
                     wts["prow"], batch=bp, seq=tp, tb=256)
    o_gdn, s_p = _gdn_scan(*prep, z, jnp.zeros((bp, GDN_HEADS, GDN_DK, GDN_DV), F32), wts["nw"],
                           batch=bp, seq=tp, tb=256)
    o_gdn = o_gdn.reshape(bp * tp, GDN_V_W)
    s_p = s_p.reshape(bp, GDN_HEADS, GDN_DK, GDN_DV)
    y_p = _mix_ffn(xp, o_nsa, o_gdn, wts["wo"], g_ffn, wgu, wd, g_fin, tm=256)
    kv_p = kv.reshape(bp, tp, 3, 2, NSA_KV_HEADS, HEAD_DIM)
    conv_p = conv_in.reshape(bp, tp, GDN_CONV_CH)[:, tp - hist:]

    xs = x_sample.reshape(bs, D_MODEL)
    cos_s, sin_s = _rope_tables(jnp.full((bs,), past, jnp.int32))
    q, kv, conv_in, z, small, _, _, _, _ = _proj(xs, g_mix, wts["wm"], wts["ws"], cos_s, sin_s, tm=bs)
    n_sel = past // SEL_LEN + 1
    nsp = -(-n_sel // LANES) * LANES
    mt_s = _cmp_to_sel_t(past // CMP_STRIDE - 1, n_sel, nsp, past // CMP_STRIDE)
    cache = cache_nsa_kv[l].transpose(0, 2, 3, 4, 1).reshape(-1, 4 * NSA_KV_W, PAGE_SIZE)
    win = cache_nsa_win[l].transpose(0, 2, 3, 4, 1).reshape(bs, 2 * NSA_KV_W, -1)
    flat = lambda w: w.reshape(-1, w.shape[-1])
    o_nsa = _nsa_sample(page_table, cache, q[:, None], kv[:, None], small[:, None], win, flat(wts["wlo"]),
                        flat(wts["whi"]), wts["pelo"], wts["pehi"], mt_s, pps=min(32, n_pages)).reshape(bs, NSA_Q_W)
    pad_rows = lambda a: jnp.pad(a[:, None], ((0, 0), (0, SUBLANES - 1), (0, 0))).reshape(bs * SUBLANES, -1)
    smallt = jnp.pad(small[:, A_COL:A_COL + SUBLANES][:, :, None], ((0, 0), (0, 0), (0, SUBLANES - 1)))
    o_gdn, s_s = _gdn(pad_rows(conv_in), pad_rows(z), pad_rows(small), smallt, state_gdn_conv[l], state_gdn_S[l],
                      wts["cw"], wts["pcol"], wts["prow"], wts["nw"], batch=bs, seq=SUBLANES, tb=SUBLANES,
                      chunk=SUBLANES, t_real=1)
    o_gdn = o_gdn.reshape(bs, SUBLANES, GDN_V_W)[:, 0]
    y_s = _mix_ffn(xs, o_nsa, o_gdn, wts["wo"], g_ffn, wgu, wd, g_fin, tm=bs)
    kv_s = kv.reshape(bs, 1, 3, 2, NSA_KV_HEADS, HEAD_DIM)
    win_s = jnp.concatenate([cache_nsa_win[l], kv_s[:, :, 2]], axis=1)[:, -min(WINDOW, past + 1):]
    conv_s = jnp.concatenate([state_gdn_conv[l], conv_in[:, None]], axis=1)[:, -hist:]

    return (y_p.reshape(bp, tp, D_MODEL), y_s.reshape(bs, 1, D_MODEL),
            kv_p[:, :, 0:2].reshape(bp, tp, 4, NSA_KV_HEADS, HEAD_DIM)[None],
            kv_p[:, tp - min(WINDOW, tp):, 2][None], s_p[None], conv_p[None],
            kv_s[:, :, 0:2].reshape(bs, 1, 4, NSA_KV_HEADS, HEAD_DIM)[None], win_s[None], s_s[None], conv_s[None])
```

```python
import functools

import numpy as np
import jax
import jax.numpy as jnp
from jax import lax
from jax.experimental import pallas as pl
from jax.experimental.pallas import tpu as pltpu

F32 = jnp.float32
BF16 = jnp.bfloat16

D_MODEL = 1024
PAGE_SIZE = 128
HEAD_DIM = 64
NSA_HEADS = 8
NSA_KV_HEADS = 2
NSA_REP = NSA_HEADS // NSA_KV_HEADS
CMP_LEN = 32
CMP_STRIDE = 16
SEL_LEN = 64
SEL_TOPK = 16
WINDOW = 512
Q_BLOCK = 128
ROPE_THETA = 10000.0
GDN_HEADS = 4
GDN_DK = 128
GDN_DV = 128
CONV_W = 4
NSA_Q_W = NSA_HEADS * HEAD_DIM
NSA_KV_W = NSA_KV_HEADS * HEAD_DIM
GDN_QK_W = GDN_HEADS * GDN_DK
GDN_V_W = GDN_HEADS * GDN_DV
GDN_CONV_CH = 2 * GDN_QK_W + GDN_V_W
NEG = -1e30
LOG2E = 1.4426950408889634
SEL_BONUS = 1e4
EPS = 1e-6

LANES = 128
SUBLANES = 8
VMEM_LIMIT = 56 * 1024 * 1024

GATE_COLS = 3 * NSA_HEADS
A_COL = GATE_COLS
B_COL = GATE_COLS + GDN_HEADS

NT_DIMS = (((1,), (1,)), ((), ()))
TN_DIMS = (((0,), (0,)), ((), ()))


def _dot(a, b):
    return jnp.dot(a, b, preferred_element_type=F32)


def _dot_nt(a, b):
    return lax.dot_general(a, b, NT_DIMS, preferred_element_type=F32)


def _sigmoid(x):
    return 1.0 / (1.0 + jnp.exp(-x))


def _silu(x):
    return x * _sigmoid(x)


def _split3(x):
    p1 = x.astype(BF16)
    r1 = x - p1.astype(F32)
    p2 = r1.astype(BF16)
    p3 = (r1 - p2.astype(F32)).astype(BF16)
    return p1, p2, p3


def _const_spec(shape):
    nd = len(shape)
    return pl.BlockSpec(shape, lambda *_: (0,) * nd, pipeline_mode=pl.Buffered(1))


def _proj_kernel(x_ref, g_ref, wm_ref, ws_ref, wkvt_ref, cos_ref, sin_ref, cost_ref, sint_ref,
                 q_ref, kv_ref, conv_ref, z_ref, small_ref, kaug_ref, vaug_ref, kwin_ref, vwaug_ref,
                 kv4t_ref, kvwt_ref, *, tm, pos_rows):
    x = x_ref[...]
    ms = jnp.mean(x * x, axis=-1, keepdims=True)
    xn = (x * lax.rsqrt(ms + EPS) * g_ref[...]).astype(BF16)
    cos = cos_ref[...]
    sin = sin_ref[...]
    lane = lax.broadcasted_iota(jnp.int32, (tm, LANES), 1)
    low_half = (lane % HEAD_DIM) < (HEAD_DIM // 2)

    def rope(v):
        rot = jnp.where(low_half, pltpu.roll(v, LANES - HEAD_DIM // 2, 1), pltpu.roll(v, HEAD_DIM // 2, 1))
        return v * cos + rot * sin

    q = _dot(xn, wm_ref[:, 0:NSA_Q_W])
    for r in range(NSA_REP):
        sl = slice(r * LANES, (r + 1) * LANES)
        q_ref[:, sl] = (rope(q[:, sl]) * (HEAD_DIM ** -0.5 * LOG2E)).astype(BF16)

    kv = _dot(xn, wm_ref[:, NSA_Q_W:NSA_Q_W + 6 * NSA_KV_W])
    group0 = lane < HEAD_DIM
    for c in range(6):
        sl = slice(c * LANES, (c + 1) * LANES)
        blk = kv[:, sl]
        if c % 2 == 0:
            blk = rope(blk)
        kv_ref[:, sl] = blk
        if c == 2:
            kaug_ref[:, LANES:2 * LANES] = blk.astype(BF16)
        elif c == 4:
            kwin_ref[...] = blk.astype(BF16)
        elif c in (3, 5):
            v_ref = vaug_ref if c == 3 else vwaug_ref
            v_ref[:, 0:LANES] = jnp.where(group0, blk, 1.0).astype(BF16)
            v_ref[:, LANES:2 * LANES] = jnp.where(group0, 1.0, blk).astype(BF16)

    row0 = (pl.program_id(0) * tm) % pos_rows
    rows = row0 + lax.broadcasted_iota(jnp.int32, (tm, LANES), 0)
    kaug_ref[:, 0:LANES] = jnp.where(rows // SEL_LEN == lane, 1.0, 0.0).astype(BF16)

    kvt = _dot_nt(wkvt_ref[...], xn)
    cos_t = cost_ref[...]
    sin_t = sint_ref[...]
    half = HEAD_DIM // 2
    for c in range(6):
        blk = kvt[c * LANES:(c + 1) * LANES]
        if c % 2 == 0:
            parts = []
            for g in range(NSA_KV_HEADS):
                x1 = blk[g * HEAD_DIM:g * HEAD_DIM + half]
                x2 = blk[g * HEAD_DIM + half:(g + 1) * HEAD_DIM]
                parts += [x1 * cos_t - x2 * sin_t, x2 * cos_t + x1 * sin_t]
            blk = jnp.concatenate(parts, axis=0)
        dst = kv4t_ref if c < 4 else kvwt_ref
        dst[(c % 4) * LANES:(c % 4 + 1) * LANES, :] = blk

    c0 = NSA_Q_W + 6 * NSA_KV_W
    conv_ref[...] = _dot(xn, wm_ref[:, c0:c0 + GDN_CONV_CH])
    z_ref[...] = _dot(xn, wm_ref[:, c0 + GDN_CONV_CH:c0 + GDN_CONV_CH + GDN_V_W])
    small_ref[...] = _dot(xn, ws_ref[...])


def _proj(x, g, wm, ws, wkvt, cos, sin, cos_t, sin_t, *, tm):
    rows = x.shape[0]
    pos_rows = cos.shape[0]
    n_pos_blk = pos_rows // tm
    n_seq = rows // pos_rows
    grid = (rows // tm,)
    row_spec = lambda w: pl.BlockSpec((tm, w), lambda i: (i, 0))
    pos_spec = pl.BlockSpec((tm, LANES), lambda i: (i % n_pos_blk, 0))
    pos_t_spec = pl.BlockSpec((HEAD_DIM // 2, tm), lambda i: (0, i % n_pos_blk))
    tok_minor = lambda r: pl.BlockSpec((None, r, tm), lambda i: (i // n_pos_blk, 0, i % n_pos_blk))
    out_shape = (
        jax.ShapeDtypeStruct((rows, NSA_Q_W), BF16),
        jax.ShapeDtypeStruct((rows, 6 * NSA_KV_W), F32),
        jax.ShapeDtypeStruct((rows, GDN_CONV_CH), F32),
        jax.ShapeDtypeStruct((rows, GDN_V_W), F32),
        jax.ShapeDtypeStruct((rows, LANES), F32),
        jax.ShapeDtypeStruct((rows, 2 * LANES), BF16),
        jax.ShapeDtypeStruct((rows, 2 * LANES), BF16),
        jax.ShapeDtypeStruct((rows, LANES), BF16),
        jax.ShapeDtypeStruct((rows, 2 * LANES), BF16),
    )
    out_shape_t = (jax.ShapeDtypeStruct((n_seq, 4 * NSA_KV_W, pos_rows), F32),
                   jax.ShapeDtypeStruct((n_seq, 2 * NSA_KV_W, pos_rows), F32))
    return pl.pallas_call(
        functools.partial(_proj_kernel, tm=tm, pos_rows=pos_rows),
        grid=grid,
        in_specs=[row_spec(D_MODEL), _const_spec((1, D_MODEL)), _const_spec(wm.shape), _const_spec(ws.shape),
                  _const_spec(wkvt.shape), pos_spec, pos_spec, pos_t_spec, pos_t_spec],
        out_specs=tuple(row_spec(s.shape[1]) for s in out_shape) + (tok_minor(4 * NSA_KV_W), tok_minor(2 * NSA_KV_W)),
        out_shape=out_shape + out_shape_t,
        compiler_params=pltpu.CompilerParams(dimension_semantics=("arbitrary",), vmem_limit_bytes=VMEM_LIMIT),
        name="proj",
    )(x, g, wm, ws, wkvt, cos, sin, cos_t, sin_t)


def _mix_ffn_kernel(x_ref, on_ref, og_ref, wo_ref, gf_ref, wgu_ref, wd_ref, gl_ref, y_ref, *, d_ff):
    h = x_ref[...] + _dot(on_ref[...], wo_ref[0:NSA_Q_W, :]) + _dot(og_ref[...], wo_ref[NSA_Q_W:, :])
    ms = jnp.mean(h * h, axis=-1, keepdims=True)
    hn = (h * lax.rsqrt(ms + EPS) * gf_ref[...]).astype(BF16)
    gate = _dot(hn, wgu_ref[:, 0:d_ff])
    up = _dot(hn, wgu_ref[:, d_ff:])
    act = (_silu(gate) * up).astype(BF16)
    h = h + _dot(act, wd_ref[...])
    ms = jnp.mean(h * h, axis=-1, keepdims=True)
    y_ref[...] = h * lax.rsqrt(ms + EPS) * gl_ref[...]


def _mix_ffn(x, o_nsa, o_gdn, wo, gf, wgu, wd, gl, *, tm):
    rows = x.shape[0]
    d_ff = wd.shape[0]
    row_spec = lambda w: pl.BlockSpec((tm, w), lambda i: (i, 0))
    return pl.pallas_call(
        functools.partial(_mix_ffn_kernel, d_ff=d_ff),
        grid=(rows // tm,),
        in_specs=[row_spec(D_MODEL), row_spec(NSA_Q_W), row_spec(GDN_V_W), _const_spec(wo.shape),
                  _const_spec((1, D_MODEL)), _const_spec(wgu.shape), _const_spec(wd.shape),
                  _const_spec((1, D_MODEL))],
        out_specs=row_spec(D_MODEL),
        out_shape=jax.ShapeDtypeStruct((rows, D_MODEL), F32),
        compiler_params=pltpu.CompilerParams(dimension_semantics=("arbitrary",), vmem_limit_bytes=VMEM_LIMIT),
        name="mix_ffn",
    )(x, o_nsa, o_gdn, wo, gf, wgu, wd, gl)


def _compress_rows(load_rows, wlo_ref, whi_ref, pelo_ref, pehi_ref, nc):
    acc_lo = jnp.zeros((nc, 2 * LANES), F32)
    acc_hi = jnp.zeros((nc, 2 * LANES), F32)
    for l in range(CMP_STRIDE):
        x = load_rows(l)
        acc_lo = acc_lo + _dot((x + pelo_ref[l:l + 1, :]).astype(BF16), wlo_ref[l])
        acc_hi = acc_hi + _dot((x + pehi_ref[l:l + 1, :]).astype(BF16), whi_ref[l])
    nxt = pltpu.roll(acc_hi, nc - 1, 0)
    row = lax.broadcasted_iota(jnp.int32, (nc, 2 * LANES), 0)
    return jnp.where(row < nc - 1, acc_lo + nxt, 0.0)


def _compress_kernel(k_ref, v_ref, wlo_ref, whi_ref, pelo_ref, pehi_ref, ckv_ref, *, nc):
    load = lambda l: jnp.concatenate([k_ref[pl.ds(l, nc, stride=CMP_STRIDE), :],
                                      v_ref[pl.ds(l, nc, stride=CMP_STRIDE), :]], axis=1)
    ckv_ref[...] = _compress_rows(load, wlo_ref, whi_ref, pelo_ref, pehi_ref, nc).astype(BF16)


def _compress(kv, wlo, whi, pelo, pehi, *, batch, seq):
    nc = seq // CMP_STRIDE
    return pl.pallas_call(
        functools.partial(_compress_kernel, nc=nc),
        grid=(batch,),
        in_specs=[pl.BlockSpec((seq, LANES), lambda b: (b, 0)), pl.BlockSpec((seq, LANES), lambda b: (b, 1)),
                  _const_spec(wlo.shape), _const_spec(whi.shape), _const_spec(pelo.shape), _const_spec(pehi.shape)],
        out_specs=pl.BlockSpec((nc, 2 * LANES), lambda b: (b, 0)),
        out_shape=jax.ShapeDtypeStruct((batch * nc, 2 * LANES), BF16),
        compiler_params=pltpu.CompilerParams(dimension_semantics=("arbitrary",), vmem_limit_bytes=VMEM_LIMIT),
        name="compress",
    )(kv, kv, wlo, whi, pelo, pehi)


def _masked_softmax2(s):
    m = jnp.maximum(jnp.max(s, axis=-1, keepdims=True), NEG / 8)
    e = jnp.exp2(s - m)
    return e * (1.0 / jnp.maximum(jnp.sum(e, axis=-1, keepdims=True), 1e-30))


def _select_blocks(score, sel, blk, axis, rounds):
    n = score.shape[axis]
    for _ in range(rounds):
        mx = jnp.max(score, axis=axis, keepdims=True)
        idx = jnp.min(jnp.where(score == mx, blk, n), axis=axis, keepdims=True)
        hit = blk == idx
        sel = jnp.where(hit, jnp.where(mx > NEG / 2, 1.0, sel), sel)
        score = jnp.where(hit, -3e38, score)
    return sel


def _nsa_prompt_kernel(q_ref, small_ref, kaug_ref, vaug_ref, kwin_ref, vwaug_ref, ckv_ref, mt_ref, o_ref,
                       *, seq, kc):
    nc = seq // CMP_STRIDE
    qb = Q_BLOCK
    start = pl.program_id(1) * qb
    wlen = WINDOW + qb
    groups = range(NSA_KV_HEADS)
    stack = lambda x: jnp.concatenate([x] * NSA_REP, axis=0)

    lane = lax.broadcasted_iota(jnp.int32, (qb, LANES), 1)
    gates = _sigmoid(small_ref[...])

    qrow_c = lax.broadcasted_iota(jnp.int32, (qb, nc), 0)
    ncol = lax.broadcasted_iota(jnp.int32, (qb, nc), 1)
    cmp_bias = stack(jnp.where(ncol * CMP_STRIDE + (CMP_LEN - 1) - qrow_c <= start, 0.0, NEG))
    tail_bias = stack(jnp.where(lane <= lax.broadcasted_iota(jnp.int32, (qb, qb), 0), 0.0, NEG))
    wbase = pl.multiple_of(jnp.maximum(start - WINDOW, 0), qb)
    back = (lax.broadcasted_iota(jnp.int32, (qb, wlen), 0) - lax.broadcasted_iota(jnp.int32, (qb, wlen), 1)
            + (start - wbase))
    win_bias = stack(jnp.where(back >= 0, jnp.where(back <= WINDOW, 0.0, NEG), NEG))

    qs, o_cmps, imps = [], [], []
    for g in groups:
        mine = (lane >= HEAD_DIM) == (g == 1)
        q = jnp.concatenate(
            [jnp.where(mine, q_ref[:, r * LANES:(r + 1) * LANES], 0.0).astype(BF16) for r in range(NSA_REP)], axis=0)
        p = _masked_softmax2(_dot_nt(q, ckv_ref[:, 0:LANES]) + cmp_bias)
        o_cmps.append(_dot(p.astype(BF16), ckv_ref[:, LANES:2 * LANES]))
        psum = p[0:qb] + p[qb:2 * qb] + p[2 * qb:3 * qb] + p[3 * qb:4 * qb]
        imps.append(sum(_dot_nt(mt_ref[...], piece) for piece in _split3(psum)))
        qs.append(q)

    blk_t = lax.broadcasted_iota(jnp.int32, (LANES, 2 * qb), 0)
    qpos_t = start + lax.broadcasted_iota(jnp.int32, (LANES, 2 * qb), 1) % qb
    cur_t = qpos_t // SEL_LEN
    visible = blk_t * SEL_LEN <= qpos_t
    forced = (blk_t == 0) | (blk_t == cur_t) | (blk_t == cur_t - 1)
    imp_t = jnp.concatenate(imps, axis=1)
    sel_t = _select_blocks(jnp.where(visible, jnp.where(forced, NEG, imp_t), NEG),
                           jnp.where(visible, jnp.where(forced, 1.0, 0.0), 0.0), blk_t, 0, SEL_TOPK - 3)
    bias_t = jnp.where(blk_t * SEL_LEN < start, jnp.where(sel_t > 0.5, 0.0, NEG), NEG).T.astype(BF16)
    n_chunks = (start + kc - 1) // kc
    q_augs, m0s, acc0s = [], [], []
    for g in groups:
        gl = slice(g * LANES, (g + 1) * LANES)
        s_tail = _dot_nt(qs[g], kaug_ref[pl.ds(start, qb), LANES:2 * LANES]) + tail_bias
        m0 = jnp.max(s_tail, axis=-1, keepdims=True)
        q_augs.append(jnp.concatenate([stack(bias_t[g * qb:(g + 1) * qb]), qs[g]], axis=1))
        m0s.append(m0)
        acc0s.append(_dot(jnp.exp2(s_tail - m0).astype(BF16), vaug_ref[pl.ds(start, qb), gl]))

    def sel_step(c, carry):
        off = pl.multiple_of(c * kc, kc)
        k = kaug_ref[pl.ds(off, kc), :]
        out = []
        for g in groups:
            m, acc = carry[g]
            s = _dot_nt(q_augs[g], k)
            m_new = jnp.maximum(m, jnp.max(s, axis=-1, keepdims=True))
            e = jnp.exp2(s - m_new).astype(BF16)
            out.append((m_new, jnp.exp2(m - m_new) * acc + _dot(e, vaug_ref[pl.ds(off, kc), g * LANES:(g + 1) * LANES])))
        return tuple(out)

    carry = lax.fori_loop(0, n_chunks // 2, lambda c, x: sel_step(2 * c + 1, sel_step(2 * c, x)),
                          tuple(zip(m0s, acc0s)))
    sel_out = lax.cond(n_chunks % 2 == 1, lambda x: sel_step(n_chunks - 1, x), lambda x: x, carry)

    outs = [None] * NSA_REP
    for g in groups:
        acc = sel_out[g][1]
        o_sel = acc / pltpu.roll(acc, HEAD_DIM, 1)

        s = _dot_nt(qs[g], kwin_ref[pl.ds(wbase, wlen), :]) + win_bias
        e = jnp.exp2(s - jnp.max(s, axis=-1, keepdims=True)).astype(BF16)
        acc = _dot(e, vwaug_ref[pl.ds(wbase, wlen), g * LANES:(g + 1) * LANES])
        o_win = acc / pltpu.roll(acc, HEAD_DIM, 1)

        for r in range(NSA_REP):
            c0 = (g * NSA_REP + r) * 3
            rs = slice(r * qb, (r + 1) * qb)
            o = (gates[:, c0:c0 + 1] * o_cmps[g][rs] + gates[:, c0 + 1:c0 + 2] * o_sel[rs]
                 + gates[:, c0 + 2:c0 + 3] * o_win[rs])
            outs[r] = o if g == 0 else jnp.where(lane < HEAD_DIM, outs[r], o)

    for r in range(NSA_REP):
        o_ref[:, r * LANES:(r + 1) * LANES] = outs[r].astype(BF16)


def _nsa_prompt(q, small, kaug, vaug, kwin, vwaug, ckv, mt, *, batch, seq, kc):
    nb = seq // Q_BLOCK
    nc = seq // CMP_STRIDE
    blk_spec = lambda w: pl.BlockSpec((Q_BLOCK, w), lambda b, i: (b * nb + i, 0))
    seq_spec = lambda w: pl.BlockSpec((seq, w), lambda b, i: (b, 0))
    return pl.pallas_call(
        functools.partial(_nsa_prompt_kernel, seq=seq, kc=kc),
        grid=(batch, nb),
        in_specs=[blk_spec(NSA_Q_W), blk_spec(LANES), seq_spec(2 * LANES), seq_spec(2 * LANES), seq_spec(LANES),
                  seq_spec(2 * LANES), pl.BlockSpec((nc, 2 * LANES), lambda b, i: (b, 0)), _const_spec(mt.shape)],
        out_specs=blk_spec(NSA_Q_W),
        out_shape=jax.ShapeDtypeStruct((batch * seq, NSA_Q_W), BF16),
        compiler_params=pltpu.CompilerParams(dimension_semantics=("arbitrary", "arbitrary"),
                                             vmem_limit_bytes=VMEM_LIMIT),
        name="nsa_prompt",
    )(q, small, kaug, vaug, kwin, vwaug, ckv, mt)


def _nsa_sample_kernel(pt_ref, *refs, past, pps):
    del pt_ref
    pages = refs[:pps]
    (q_ref, kvn_ref, small_ref, win_ref, perm_ref, wlo_ref, whi_ref, pelo_ref, pehi_ref, mt_ref, exp_ref,
     o_ref, xs_ref, s_ref, vs_ref) = refs[pps:]
    step = pl.program_id(1)
    n_pages = past // PAGE_SIZE
    cpp = PAGE_SIZE // CMP_STRIDE
    heads = NSA_HEADS
    row = lax.broadcasted_iota(jnp.int32, (heads, LANES), 0)
    lane = lax.broadcasted_iota(jnp.int32, (heads, LANES), 1)
    mine = (lane >= HEAD_DIM) == (row >= NSA_REP)
    rr = row % NSA_REP
    qsel = jnp.zeros((heads, LANES), F32)
    for r in range(NSA_REP):
        qsel = jnp.where(rr == r, q_ref[:, r * LANES:(r + 1) * LANES].astype(F32), qsel)
    qf = jnp.where(mine, qsel, 0.0)
    q = qf.astype(BF16)

    nc = past // CMP_STRIDE
    cw = 2 * LANES

    def take_pages(first_page):
        for k in range(pps):
            pg = first_page + k
            page = pages[k]
            tok = slice(pg * PAGE_SIZE, (pg + 1) * PAGE_SIZE)
            s_ref[:, tok] = _dot(q, page[2 * LANES:3 * LANES, :].astype(BF16))
            vs_ref[:, tok] = page[3 * LANES:4 * LANES, :].astype(BF16)
            x = _dot_nt(perm_ref[...], page[0:2 * LANES, :].astype(BF16))
            for l in range(CMP_STRIDE):
                xs_ref[pg * cpp:(pg + 1) * cpp, l * cw:(l + 1) * cw] = x[l * cpp:(l + 1) * cpp, :]

    for st in range(n_pages // pps):
        pl.when(step == st)(functools.partial(take_pages, st * pps))

    @pl.when(step == pl.num_programs(1) - 1)
    def _():
        n_cmp = nc - 1
        nsp = mt_ref.shape[0]
        cur = past // SEL_LEN

        for l in range(CMP_STRIDE):
            xs_ref[nc:nc + SUBLANES, l * cw:(l + 1) * cw] = jnp.broadcast_to(pelo_ref[l:l + 1, :], (SUBLANES, cw))
            xs_ref[nc + SUBLANES:nc + 2 * SUBLANES, l * cw:(l + 1) * cw] = jnp.broadcast_to(pehi_ref[l:l + 1, :],
                                                                                            (SUBLANES, cw))
        x16 = xs_ref[...].astype(BF16)
        acc_lo = _dot(x16, wlo_ref[...])
        acc_hi = _dot(x16, whi_ref[...])
        bias = acc_lo[nc:nc + 1] + acc_hi[nc + SUBLANES:nc + SUBLANES + 1]
        nxt = pltpu.roll(acc_hi[0:nc], nc - 1, 0)
        crow = lax.broadcasted_iota(jnp.int32, (nc, 2 * LANES), 0)
        ckv = jnp.where(crow < nc - 1, acc_lo[0:nc] + nxt + bias, 0.0).astype(BF16)

        ncol = lax.broadcasted_iota(jnp.int32, (1, nc), 1)
        cmp_valid = (ncol * CMP_STRIDE + CMP_LEN - 1 <= past) & (ncol < n_cmp)
        p = _masked_softmax2(jnp.where(cmp_valid, _dot_nt(q, ckv[:, 0:LANES]), NEG))
        o_cmp = _dot(p.astype(BF16), ckv[:, LANES:2 * LANES])

        prow = lax.broadcasted_iota(jnp.int32, (heads, nc), 0)
        ps = [jnp.sum(jnp.where(prow // NSA_REP == g, p, 0.0), axis=0, keepdims=True) for g in range(NSA_KV_HEADS)]
        psum = jnp.where(prow < NSA_REP, ps[0], ps[1])
        imp = sum(_dot_nt(piece, mt_ref[...]) for piece in _split3(psum))
        blk = lax.broadcasted_iota(jnp.int32, (heads, nsp), 1)
        forced = (blk == 0) | (blk == cur) | (blk == cur - 1)
        score = jnp.where(blk * SEL_LEN <= past, imp + jnp.where(forced, SEL_BONUS, 0.0), NEG)
        score_t = score.T
        bi = lax.broadcasted_iota(jnp.int32, (nsp, nsp), 0)
        bj = lax.broadcasted_iota(jnp.int32, (nsp, nsp), 1)
        tie = jnp.where(bi < bj, 1.0, 0.0)
        picks = []
        for g in range(NSA_KV_HEADS):
            c = g * NSA_REP
            s_i, s_j = score_t[:, c:c + 1], score[c:c + 1, :]
            beats = jnp.where(s_i > s_j, 1.0, jnp.where(s_i == s_j, tie, 0.0))
            rank = jnp.sum(beats, axis=0, keepdims=True)
            picks.append(jnp.where(rank < SEL_TOPK, jnp.where(s_j > NEG / 2, 1.0, 0.0), 0.0))
        sel8 = jnp.where(lax.broadcasted_iota(jnp.int32, (heads, nsp), 0) < NSA_REP, picks[0], picks[1])
        keep = _dot(sel8.astype(BF16), exp_ref[...]) > 0.5

        kvn = kvn_ref[...]
        new_ok = sel8[:, cur:cur + 1] > 0.5
        s = jnp.where(keep, s_ref[...], NEG)
        s_new = jnp.where(new_ok, jnp.sum(qf * kvn[:, 2 * LANES:3 * LANES], axis=-1, keepdims=True), NEG)
        m = jnp.maximum(jnp.max(s, axis=-1, keepdims=True), s_new)
        e = jnp.exp2(s - m)
        e_new = jnp.where(new_ok, jnp.exp2(s_new - m), 0.0)
        o_sel = ((_dot_nt(e.astype(BF16), vs_ref[...]) + e_new * kvn[:, 3 * LANES:4 * LANES])
                 / (jnp.sum(e, axis=-1, keepdims=True) + e_new))

        nw = win_ref.shape[0] // 2
        s = _dot(q, win_ref[0:nw, :].astype(BF16))
        s_new = jnp.sum(qf * kvn[:, 4 * LANES:5 * LANES], axis=-1, keepdims=True)
        m = jnp.maximum(jnp.max(s, axis=-1, keepdims=True), s_new)
        e = jnp.exp2(s - m)
        e_new = jnp.exp2(s_new - m)
        l = jnp.sum(e, axis=-1, keepdims=True) + e_new
        o_win = (_dot_nt(e.astype(BF16), win_ref[nw:, :].astype(BF16))
                 + e_new * kvn[:, 5 * LANES:6 * LANES]) / l

        gates = _sigmoid(small_ref[...])
        gate = lambda br: jnp.sum(jnp.where(lane == row * 3 + br, gates, 0.0), axis=-1, keepdims=True)
        o = gate(0) * o_cmp + gate(1) * o_sel + gate(2) * o_win
        for r in range(NSA_REP):
            o_ref[:, r * LANES:(r + 1) * LANES] = jnp.where(
                lane[0:1] < HEAD_DIM, o[r:r + 1], o[NSA_REP + r:NSA_REP + r + 1]).astype(BF16)


def _nsa_sample(page_table, cache, q, kvn, small, win, wlo, whi, pelo, pehi, mt, *, pps):
    batch, n_pages = page_table.shape
    past = n_pages * PAGE_SIZE
    rows = cache.shape[1]
    cpp = PAGE_SIZE // CMP_STRIDE
    nsp = mt.shape[0]
    expand = jnp.asarray(np.arange(nsp)[:, None] == np.arange(past)[None, :] // SEL_LEN, BF16)
    tok = np.arange(PAGE_SIZE)
    perm = jnp.asarray((tok[None, :] % CMP_STRIDE) * cpp + tok[None, :] // CMP_STRIDE == tok[:, None], BF16)

    def page_spec(k):
        return pl.BlockSpec((None, rows, PAGE_SIZE), lambda b, s, pt: (pt[b * n_pages + s * pps + k], 0, 0))

    per_b = lambda shape: pl.BlockSpec((None,) + shape, lambda b, s, pt: (b, 0, 0))
    const = lambda a: pl.BlockSpec(a.shape, lambda b, s, pt: (0,) * a.ndim, pipeline_mode=pl.Buffered(1))
    grid_spec = pltpu.PrefetchScalarGridSpec(
        num_scalar_prefetch=1,
        grid=(batch, n_pages // pps),
        in_specs=[page_spec(k) for k in range(pps)] + [
            per_b((1, NSA_Q_W)), per_b((1, 6 * NSA_KV_W)), per_b((1, LANES)), per_b(win.shape[1:]),
            const(perm), const(wlo), const(whi), const(pelo), const(pehi), const(mt), const(expand)],
        out_specs=per_b((1, NSA_Q_W)),
        scratch_shapes=[pltpu.VMEM((past // CMP_STRIDE + 2 * SUBLANES, CMP_STRIDE * 2 * LANES), F32),
                        pltpu.VMEM((NSA_HEADS, past), F32),
                        pltpu.VMEM((LANES, past), BF16)],
    )
    return pl.pallas_call(
        functools.partial(_nsa_sample_kernel, past=past, pps=pps),
        grid_spec=grid_spec,
        out_shape=jax.ShapeDtypeStruct((batch, 1, NSA_Q_W), BF16),
        compiler_params=pltpu.CompilerParams(dimension_semantics=("arbitrary", "arbitrary"),
                                             vmem_limit_bytes=VMEM_LIMIT),
        name="nsa_sample",
    )(page_table.reshape(-1), *([cache] * pps), q, kvn, small, win, perm, wlo, whi, pelo, pehi, mt, expand)


def _l2norm(x):
    return x * lax.rsqrt(jnp.sum(x * x, axis=-1, keepdims=True) + EPS)


def _softplus(x):
    return jnp.maximum(x, 0.0) + jnp.log(1.0 + jnp.exp(-jnp.abs(x)))


def _gdn_step_kernel(x_ref, z_ref, small_ref, buf_ref, s0_ref, cw_ref, pcol_ref, nw_ref, o_ref, sout_ref):
    hist = CONV_W - 1
    conv = jnp.sum(buf_ref[...] * cw_ref[0:hist, :], axis=0, keepdims=True) + x_ref[...] * cw_ref[hist:CONV_W, :]
    act = _silu(conv)
    small = small_ref[...]
    g_all = -jnp.exp(pcol_ref[0:1, :]) * _softplus(small + pcol_ref[1:2, :])
    beta_all = _sigmoid(small)
    row = lax.broadcasted_iota(jnp.int32, (SUBLANES, GDN_DK), 0)
    for h in range(GDN_HEADS):
        hs = slice(h * GDN_DK, (h + 1) * GDN_DK)
        qh = _l2norm(act[:, hs]) * (GDN_DK ** -0.5)
        kh = _l2norm(act[:, GDN_QK_W + h * GDN_DK:GDN_QK_W + (h + 1) * GDN_DK])
        vh = act[:, 2 * GDN_QK_W + h * GDN_DV:2 * GDN_QK_W + (h + 1) * GDN_DV]
        eg = jnp.exp(g_all[:, A_COL + h:A_COL + h + 1])
        bt = beta_all[:, B_COL + h:B_COL + h + 1]
        s = s0_ref[h]
        kq = jnp.where(row == 0, kh, jnp.where(row == 1, qh, 0.0)).astype(BF16)
        ks_qs = _dot(kq, s.astype(BF16))
        vn = bt * (vh - eg * ks_qs[0:1])
        o = eg * ks_qs[1:2] + jnp.sum(qh * kh, axis=-1, keepdims=True) * vn
        k8 = jnp.where(row == 0, kh, 0.0).astype(BF16)
        vn8 = jnp.where(row == 0, vn, 0.0).astype(BF16)
        sout_ref[h] = s * eg + lax.dot_general(k8, vn8, TN_DIMS, preferred_element_type=F32)
        on = o * lax.rsqrt(jnp.mean(o * o, axis=-1, keepdims=True) + EPS) * nw_ref[...]
        o_ref[:, hs] = (on * _silu(z_ref[:, hs])).astype(BF16)


def _gdn_step(x, z, small, buf, s0, cw, pcol, nw):
    batch = x.shape[0]
    per_b = lambda shape: pl.BlockSpec((None,) + shape, lambda b: (b,) + (0,) * len(shape))
    state = (GDN_HEADS, GDN_DK, GDN_DV)
    return pl.pallas_call(
        _gdn_step_kernel,
        grid=(batch,),
        in_specs=[per_b((1, GDN_CONV_CH)), per_b((1, GDN_V_W)), per_b((1, LANES)), per_b((CONV_W - 1, GDN_CONV_CH)),
                  per_b(state), _const_spec(cw.shape), _const_spec(pcol.shape), _const_spec(nw.shape)],
        out_specs=(per_b((1, GDN_V_W)), per_b(state)),
        out_shape=(jax.ShapeDtypeStruct((batch, 1, GDN_V_W), BF16),
                   jax.ShapeDtypeStruct((batch,) + state, F32)),
        compiler_params=pltpu.CompilerParams(dimension_semantics=("arbitrary",), vmem_limit_bytes=VMEM_LIMIT),
        name="gdn_step",
    )(x[:, None], z[:, None], small[:, None], buf, s0, cw, pcol, nw)


GDN_CHUNK = 64
GDN_STACK = GDN_HEADS * GDN_CHUNK


def _stack_heads(x, col0, width):
    return jnp.concatenate([x[:, col0 + h * width:col0 + (h + 1) * width] for h in range(GDN_HEADS)], axis=0)


def _gdn_prep_kernel(conv_ref, small_ref, smallt_ref, buf_ref, cw_ref, pcol_ref, prow_ref, tril_ref, triu_ref,
                     u_ref, w_ref, qg_ref, kgt_ref, aqk_ref, gl_ref, xs_ref, *, tb):
    j = pl.program_id(1)
    pad = SUBLANES
    hist = CONV_W - 1
    ck = GDN_CHUNK
    st = GDN_STACK

    @pl.when(j == 0)
    def _():
        xs_ref[0:pad, :] = jnp.zeros((pad, GDN_CONV_CH), F32)
        xs_ref[pad - hist:pad, :] = buf_ref[...]

    xs_ref[pad:pad + tb, :] = conv_ref[...]
    conv = xs_ref[pad - hist:pad - hist + tb, :] * cw_ref[0:1, :]
    for t in range(1, CONV_W):
        conv = conv + xs_ref[pad - hist + t:pad - hist + t + tb, :] * cw_ref[t:t + 1, :]
    xs_ref[pad - hist:pad, :] = xs_ref[pad + tb - hist:pad + tb, :]
    act = _silu(conv)

    small = small_ref[...]
    g_col = -jnp.exp(pcol_ref[0:1, :]) * _softplus(small + pcol_ref[1:2, :])
    beta = _sigmoid(small)
    g_row = -jnp.exp(prow_ref[:, 0:1]) * _softplus(smallt_ref[...] + prow_ref[:, 1:2])
    gcum_col = sum(_dot(tril_ref[...], piece) for piece in _split3(g_col))
    gcum_row = sum(_dot(piece, triu_ref[...]) for piece in _split3(g_row))

    ii = lax.broadcasted_iota(jnp.int32, (st, st), 0)
    jj = lax.broadcasted_iota(jnp.int32, (st, st), 1)
    same_head = (ii // ck) == (jj // ck)
    incl = same_head & (ii >= jj)
    strict = same_head & (ii > jj)
    eye = jnp.where(ii == jj, 1.0, 0.0)
    hrow = lax.broadcasted_iota(jnp.int32, (SUBLANES, LANES), 0)

    for ci in range(tb // ck):
        r0 = ci * ck
        rs = slice(r0, r0 + ck)
        gc = _stack_heads(gcum_col[rs], A_COL, 1)
        bt = _stack_heads(beta[rs], B_COL, 1)
        gr = jnp.concatenate([gcum_row[h:h + 1, r0:r0 + ck] for h in range(GDN_HEADS)], axis=1)
        glast = [gcum_col[r0 + ck - 1:r0 + ck, A_COL + h:A_COL + h + 1] for h in range(GDN_HEADS)]
        gl_stack = jnp.concatenate([jnp.broadcast_to(x, (ck, 1)) for x in glast], axis=0)
        dec = jnp.where(incl, jnp.exp(jnp.where(incl, gc - gr, 0.0)), 0.0)
        qs = jnp.concatenate([_l2norm(act[rs, h * GDN_DK:(h + 1) * GDN_DK]) for h in range(GDN_HEADS)],
                             axis=0) * (GDN_DK ** -0.5)
        ks = jnp.concatenate([_l2norm(act[rs, GDN_QK_W + h * GDN_DK:GDN_QK_W + (h + 1) * GDN_DK])
                              for h in range(GDN_HEADS)], axis=0)
        vs = _stack_heads(act[rs], 2 * GDN_QK_W, GDN_DV)
        kb = ks * bt
        k16 = ks.astype(BF16)
        lm = jnp.where(strict, _dot_nt(kb.astype(BF16), k16) * dec, 0.0)
        ainv = eye - lm
        pw = lm
        n = 2
        while n < ck:
            pw16 = pw.astype(BF16)
            pw = _dot(pw16, pw16)
            ainv = ainv + _dot(ainv.astype(BF16), pw.astype(BF16))
            n *= 2
        eg = jnp.exp(gc)
        rhs = jnp.concatenate([vs * bt, kb * eg], axis=1).astype(BF16)
        sol = _dot(ainv.astype(BF16), rhs)
        orow = slice(ci * st, (ci + 1) * st)
        u_ref[orow, :] = sol[:, 0:GDN_DV]
        w_ref[orow, :] = sol[:, GDN_DV:].astype(BF16)
        qg_ref[orow, :] = (qs * eg).astype(BF16)
        aqk_ref[orow, :] = (_dot_nt(qs.astype(BF16), k16) * dec).astype(BF16)
        kgt_ref[ci * GDN_DK:(ci + 1) * GDN_DK, :] = (ks * jnp.exp(gl_stack - gc)).T.astype(BF16)
        gl = jnp.zeros((SUBLANES, LANES), F32)
        for h in range(GDN_HEADS):
            gl = jnp.where(hrow == h, jnp.exp(glast[h]), gl)
        gl_ref[ci * SUBLANES:(ci + 1) * SUBLANES, :] = gl


def _gdn_prep(conv_in, small, smallt, buf, cw, pcol, prow, *, batch, seq, tb):
    nblk = seq // tb
    ncb = tb // GDN_CHUNK
    blk = np.arange(tb)
    same = (blk[:, None] // GDN_CHUNK) == (blk[None, :] // GDN_CHUNK)
    tril = jnp.asarray(same & (blk[:, None] >= blk[None, :]), BF16)
    triu = jnp.asarray(same & (blk[:, None] <= blk[None, :]), BF16)
    row_spec = lambda r, w: pl.BlockSpec((r, w), lambda b, j: (b * nblk + j, 0))
    per_b = lambda shape: pl.BlockSpec((None,) + shape, lambda b, j: (b,) + (0,) * len(shape))
    n_chunks = batch * seq // GDN_CHUNK
    out_shape = (jax.ShapeDtypeStruct((n_chunks * GDN_STACK, GDN_DV), F32),
                 jax.ShapeDtypeStruct((n_chunks * GDN_STACK, GDN_DK), BF16),
                 jax.ShapeDtypeStruct((n_chunks * GDN_STACK, GDN_DK), BF16),
                 jax.ShapeDtypeStruct((n_chunks * GDN_DK, GDN_STACK), BF16),
                 jax.ShapeDtypeStruct((n_chunks * GDN_STACK, GDN_STACK), BF16),
                 jax.ShapeDtypeStruct((n_chunks * SUBLANES, LANES), F32))
    out_specs = (row_spec(ncb * GDN_STACK, GDN_DV), row_spec(ncb * GDN_STACK, GDN_DK),
                 row_spec(ncb * GDN_STACK, GDN_DK), row_spec(ncb * GDN_DK, GDN_STACK),
                 row_spec(ncb * GDN_STACK, GDN_STACK), row_spec(ncb * SUBLANES, LANES))
    return pl.pallas_call(
        functools.partial(_gdn_prep_kernel, tb=tb),
        grid=(batch, nblk),
        in_specs=[row_spec(tb, GDN_CONV_CH), row_spec(tb, LANES),
                  pl.BlockSpec((None, SUBLANES, tb), lambda b, j: (b, 0, j)),
                  per_b((CONV_W - 1, GDN_CONV_CH)),
                  _const_spec(cw.shape), _const_spec(pcol.shape), _const_spec(prow.shape),
                  _const_spec(tril.shape), _const_spec(triu.shape)],
        out_specs=out_specs,
        out_shape=out_shape,
        scratch_shapes=[pltpu.VMEM((SUBLANES + tb, GDN_CONV_CH), F32)],
        compiler_params=pltpu.CompilerParams(dimension_semantics=("arbitrary", "arbitrary"),
                                             vmem_limit_bytes=VMEM_LIMIT),
        name="gdn_prep",
    )(conv_in, small, smallt, buf, cw, pcol, prow, tril, triu)


def _gdn_scan_kernel(u_ref, w_ref, qg_ref, kgt_ref, aqk_ref, gl_ref, z_ref, s0_ref, nw_ref, o_ref, sout_ref, s_ref,
                     *, batch, ncb):
    ck = GDN_CHUNK
    st = GDN_STACK
    sw = GDN_HEADS * GDN_DK

    @pl.when(pl.program_id(0) == 0)
    def _():
        s_ref[...] = s0_ref[...]

    wide_mask = (lax.broadcasted_iota(jnp.int32, (st, sw), 0) // ck
                 == lax.broadcasted_iota(jnp.int32, (st, sw), 1) // GDN_DK)
    tall_mask = (lax.broadcasted_iota(jnp.int32, (sw, st), 0) // GDN_DK
                 == lax.broadcasted_iota(jnp.int32, (sw, st), 1) // ck)

    for ci in range(ncb):
        rows = slice(ci * st, (ci + 1) * st)
        for b in range(batch):
            w_bd = jnp.where(wide_mask, jnp.concatenate([w_ref[b, rows, :]] * GDN_HEADS, axis=1), 0.0)
            qg_bd = jnp.where(wide_mask, jnp.concatenate([qg_ref[b, rows, :]] * GDN_HEADS, axis=1), 0.0)
            kgt = kgt_ref[b, ci * GDN_DK:(ci + 1) * GDN_DK, :]
            kgt_bd = jnp.where(tall_mask, jnp.concatenate([kgt] * GDN_HEADS, axis=0), 0.0)
            gl = gl_ref[b, ci * SUBLANES:(ci + 1) * SUBLANES, :]
            gl_rows = jnp.concatenate([jnp.broadcast_to(gl[h:h + 1], (GDN_DK, GDN_DV)) for h in range(GDN_HEADS)],
                                      axis=0)
            s = s_ref[b]
            t1 = _dot(jnp.concatenate([w_bd, qg_bd], axis=0).astype(BF16), s.astype(BF16))
            vn16 = (u_ref[b, rows, :] - t1[0:st]).astype(BF16)
            o = t1[st:] + _dot(aqk_ref[b, rows, :], vn16)
            s_ref[b] = s * gl_rows + _dot(kgt_bd.astype(BF16), vn16)
            for h in range(GDN_HEADS):
                oh = o[h * ck:(h + 1) * ck]
                on = oh * lax.rsqrt(jnp.mean(oh * oh, axis=-1, keepdims=True) + EPS) * nw_ref[...]
                hs = slice(h * GDN_DV, (h + 1) * GDN_DV)
                o_ref[b, ci * ck:(ci + 1) * ck, hs] = (on * _silu(z_ref[b, ci * ck:(ci + 1) * ck, hs])).astype(BF16)

    sout_ref[...] = s_ref[...]


def _gdn_scan(u, w, qg, kgt, aqk, gl, z, s0, nw, *, batch, seq, tb):
    nblk = seq // tb
    ncb = tb // GDN_CHUNK
    cps = seq // GDN_CHUNK
    sw = GDN_HEADS * GDN_DK
    r3 = lambda a, rows_per_chunk: a.reshape(batch, cps * rows_per_chunk, a.shape[-1])
    blk = lambda rows, width: pl.BlockSpec((batch, rows, width), lambda j: (0, j, 0))
    full = pl.BlockSpec((batch, sw, GDN_DV), lambda j: (0, 0, 0))
    return pl.pallas_call(
        functools.partial(_gdn_scan_kernel, batch=batch, ncb=ncb),
        grid=(nblk,),
        in_specs=[blk(ncb * GDN_STACK, GDN_DV), blk(ncb * GDN_STACK, GDN_DK), blk(ncb * GDN_STACK, GDN_DK),
                  blk(ncb * GDN_DK, GDN_STACK), blk(ncb * GDN_STACK, GDN_STACK), blk(ncb * SUBLANES, LANES),
                  blk(tb, GDN_V_W), full, _const_spec(nw.shape)],
        out_specs=(blk(tb, GDN_V_W), full),
        out_shape=(jax.ShapeDtypeStruct((batch, seq, GDN_V_W), BF16),
                   jax.ShapeDtypeStruct((batch, sw, GDN_DV), F32)),
        scratch_shapes=[pltpu.VMEM((batch, sw, GDN_DV), F32)],
        compiler_params=pltpu.CompilerParams(dimension_semantics=("arbitrary",), vmem_limit_bytes=VMEM_LIMIT),
        name="gdn_scan",
    )(r3(u, GDN_STACK), r3(w, GDN_STACK), r3(qg, GDN_STACK), r3(kgt, GDN_DK), r3(aqk, GDN_STACK),
      r3(gl, SUBLANES), z.reshape(batch, seq, GDN_V_W), s0.reshape(batch, sw, GDN_DV), nw)


def _rope_tables(pos):
    half = HEAD_DIM // 2
    inv = jnp.power(ROPE_THETA, -jnp.arange(half, dtype=F32) * 2.0 / HEAD_DIM)
    ang = pos.astype(F32)[:, None] * inv[None, :]
    cos, sin = jnp.cos(ang), jnp.sin(ang)
    return jnp.tile(cos, (1, 4)), jnp.concatenate([-sin, sin, -sin, sin], axis=1), cos.T, sin.T


def _cmp_to_sel_t(n_cmp, n_sel, rows, cols):
    cs = np.arange(cols)[None, :] * CMP_STRIDE
    ss = np.arange(rows)[:, None] * SEL_LEN
    hit = (cs < ss + SEL_LEN) & (cs + CMP_LEN > ss)
    hit &= (np.arange(cols)[None, :] < n_cmp) & (np.arange(rows)[:, None] < n_sel)
    return jnp.asarray(hit, BF16)


def _layer_weights(w_in, cmp_pe, cmp_w, conv_w, a_log, dt_bias, gdn_norm, w_out):
    cuts = np.cumsum([NSA_Q_W, 6 * NSA_KV_W, GATE_COLS, GDN_CONV_CH, GDN_V_W, GDN_HEADS]).tolist()
    wq, wkv, wgt, wconv, wz, wa, wb = jnp.split(w_in, cuts, axis=1)
    order = np.array([g * NSA_REP + r for r in range(NSA_REP) for g in range(NSA_KV_HEADS)])
    cols = (order[:, None] * HEAD_DIM + np.arange(HEAD_DIM)[None, :]).reshape(-1)
    wm = jnp.concatenate([wq[:, cols], wkv, wconv, wz], axis=1).astype(BF16)
    wkvt = wkv.T.astype(BF16)
    ws = jnp.concatenate([wgt, wa, wb], axis=1)
    ws = jnp.pad(ws, ((0, 0), (0, LANES - ws.shape[1]))).astype(BF16)
    wo = jnp.concatenate([w_out[:NSA_Q_W][cols], w_out[NSA_Q_W:]], axis=0).astype(BF16)

    def blockdiag(l0):
        wk, wv = cmp_w[0, l0:l0 + CMP_STRIDE], cmp_w[1, l0:l0 + CMP_STRIDE]
        z = jnp.zeros_like(wk)
        rows = [jnp.concatenate(r, axis=2) for r in ([wk, z, z, z], [z, wk, z, z], [z, z, wv, z], [z, z, z, wv])]
        return jnp.concatenate(rows, axis=1).astype(BF16)

    def pe_rows(l0):
        pk, pv = cmp_pe[0, l0:l0 + CMP_STRIDE], cmp_pe[1, l0:l0 + CMP_STRIDE]
        return jnp.concatenate([pk, pk, pv, pv], axis=1)

    pcol = jnp.zeros((2, LANES), F32).at[0, A_COL:A_COL + GDN_HEADS].set(a_log)
    pcol = pcol.at[1, A_COL:A_COL + GDN_HEADS].set(dt_bias)
    prow = jnp.zeros((SUBLANES, 2), F32).at[0:GDN_HEADS, 0].set(a_log).at[0:GDN_HEADS, 1].set(dt_bias)
    return dict(wm=wm, ws=ws, wkvt=wkvt, wo=wo, wlo=blockdiag(0), whi=blockdiag(CMP_STRIDE), pelo=pe_rows(0),
                pehi=pe_rows(CMP_STRIDE), cw=conv_w, pcol=pcol, prow=prow, nw=gdn_norm[None, :])


def kernel(x_prompt, x_sample, cache_nsa_kv, cache_nsa_win, state_gdn_S, state_gdn_conv, page_table, norm_mix, w_in,
           nsa_cmp_pe, nsa_cmp_w, gdn_conv_w, gdn_a_log, gdn_dt_bias, gdn_norm, w_out, norm_ffn, w_gate_up, w_down,
           norm_final):
    bp, tp, _ = x_prompt.shape
    bs, ts, _ = x_sample.shape
    depth = w_in.shape[0]
    n_pages = page_table.shape[1]
    past = n_pages * PAGE_SIZE
    assert depth == 1 and ts == 1, "one layer and one new token per sample row"
    assert tp % 512 == 0 and tp // SEL_LEN <= LANES and past % 512 == 0
    l = 0
    wts = _layer_weights(w_in[l], nsa_cmp_pe[l], nsa_cmp_w[l], gdn_conv_w[l], gdn_a_log[l], gdn_dt_bias[l],
                         gdn_norm[l], w_out[l])
    g_mix, g_ffn, g_fin = norm_mix[l][None, :], norm_ffn[l][None, :], norm_final[None, :]
    wgu, wd = w_gate_up[l].astype(BF16), w_down[l].astype(BF16)
    hist = CONV_W - 1

    xp = x_prompt.reshape(bp * tp, D_MODEL)
    q, kv, conv_in, z, small, kaug, vaug, kwin, vwaug, kv4t_p, kvwt_p = _proj(
        xp, g_mix, wts["wm"], wts["ws"], wts["wkvt"], *_rope_tables(jnp.arange(tp, dtype=jnp.int32)), tm=256)
    ckv = _compress(kv, wts["wlo"], wts["whi"], wts["pelo"], wts["pehi"], batch=bp, seq=tp)
    nc = tp // CMP_STRIDE
    mt_p = _cmp_to_sel_t(nc - 1, tp // SEL_LEN, LANES, nc)
    o_nsa = _nsa_prompt(q, small, kaug, vaug, kwin, vwaug, ckv, mt_p, batch=bp, seq=tp, kc=512)
    smallt = small[:, A_COL:A_COL + SUBLANES].reshape(bp, tp, SUBLANES).transpose(0, 2, 1)
    prep = _gdn_prep(conv_in, small, smallt, jnp.zeros((bp, hist, GDN_CONV_CH), F32), wts["cw"], wts["pcol"],
                     wts["prow"], batch=bp, seq=tp, tb=256)
    o_gdn, s_p = _gdn_scan(*prep, z, jnp.zeros((bp, GDN_HEADS, GDN_DK, GDN_DV), F32), wts["nw"],
                           batch=bp, seq=tp, tb=256)
    o_gdn = o_gdn.reshape(bp * tp, GDN_V_W)
    s_p = s_p.reshape(bp, GDN_HEADS, GDN_DK, GDN_DV)
    y_p = _mix_ffn(xp, o_nsa, o_gdn, wts["wo"], g_ffn, wgu, wd, g_fin, tm=256)
    tok_major = lambda a, comps: a.reshape(bp, comps, NSA_KV_HEADS, HEAD_DIM, -1).transpose(0, 4, 1, 2, 3)
    kv_p = tok_major(kv4t_p, 4)
    win_p = tok_major(kvwt_p[:, :, tp - min(WINDOW, tp):], 2)
    conv_p = conv_in.reshape(bp, tp, GDN_CONV_CH)[:, tp - hist:]

    xs = x_sample.reshape(bs, D_MODEL)
    q, kv, conv_in, z, small = _proj(xs, g_mix, wts["wm"], wts["ws"], wts["wkvt"],
                                     *_rope_tables(jnp.full((bs,), past, jnp.int32)), tm=bs)[:5]
    n_sel = past // SEL_LEN + 1
    nsp = -(-n_sel // LANES) * LANES
    mt_s = _cmp_to_sel_t(past // CMP_STRIDE - 1, n_sel, nsp, past // CMP_STRIDE)
    cache = cache_nsa_kv[l].transpose(0, 2, 3, 4, 1).reshape(-1, 4 * NSA_KV_W, PAGE_SIZE)
    win = cache_nsa_win[l].transpose(0, 2, 3, 4, 1).reshape(bs, 2 * NSA_KV_W, -1)
    flat = lambda w: w.reshape(-1, w.shape[-1])
    o_nsa = _nsa_sample(page_table, cache, q[:, None], kv[:, None], small[:, None], win, flat(wts["wlo"]),
                        flat(wts["whi"]), wts["pelo"], wts["pehi"], mt_s, pps=min(32, n_pages)).reshape(bs, NSA_Q_W)
    o_gdn, s_s = _gdn_step(conv_in, z, small, state_gdn_conv[l], state_gdn_S[l], wts["cw"], wts["pcol"], wts["nw"])
    o_gdn = o_gdn.reshape(bs, GDN_V_W)
    y_s = _mix_ffn(xs, o_nsa, o_gdn, wts["wo"], g_ffn, wgu, wd, g_fin, tm=bs)
    kv_s = kv.reshape(bs, 1, 3, 2, NSA_KV_HEADS, HEAD_DIM)
    win_s = jnp.concatenate([cache_nsa_win[l], kv_s[:, :, 2]], axis=1)[:, -min(WINDOW, past + 1):]
    conv_s = jnp.concatenate([state_gdn_conv[l], conv_in[:, None]], axis=1)[:, -hist:]

    return (y_p.reshape(bp, tp, D_MODEL), y_s.reshape(bs, 1, D_MODEL),
            kv_p[None], win_p[None], s_p[None], conv_p[None],
            kv_s[:, :, 0:2].reshape(bs, 1, 4, NSA_KV_HEADS, HEAD_DIM)[None], win_s[None], s_s[None], conv_s[None])
```

```python
import functools

import numpy as np
import jax
import jax.numpy as jnp
from jax import lax
from jax.experimental import pallas as pl
from jax.experimental.pallas import tpu as pltpu

F32 = jnp.float32
BF16 = jnp.bfloat16

D_MODEL = 1024
PAGE_SIZE = 128
HEAD_DIM = 64
NSA_HEADS = 8
NSA_KV_HEADS = 2
NSA_REP = NSA_HEADS // NSA_KV_HEADS
CMP_LEN = 32
CMP_STRIDE = 16
SEL_LEN = 64
SEL_TOPK = 16
WINDOW = 512
Q_BLOCK = 128
ROPE_THETA = 10000.0
GDN_HEADS = 4
GDN_DK = 128
GDN_DV = 128
CONV_W = 4
NSA_Q_W = NSA_HEADS * HEAD_DIM
NSA_KV_W = NSA_KV_HEADS * HEAD_DIM
GDN_QK_W = GDN_HEADS * GDN_DK
GDN_V_W = GDN_HEADS * GDN_DV
GDN_CONV_CH = 2 * GDN_QK_W + GDN_V_W
NEG = -1e30
LOG2E = 1.4426950408889634
SEL_BONUS = 1e4
EPS = 1e-6

LANES = 128
SUBLANES = 8
VMEM_LIMIT = 56 * 1024 * 1024

GATE_COLS = 3 * NSA_HEADS
A_COL = GATE_COLS
B_COL = GATE_COLS + GDN_HEADS

NT_DIMS = (((1,), (1,)), ((), ()))
TN_DIMS = (((0,), (0,)), ((), ()))


def _dot(a, b):
    return jnp.dot(a, b, preferred_element_type=F32)


def _dot_nt(a, b):
    return lax.dot_general(a, b, NT_DIMS, preferred_element_type=F32)


def _sigmoid(x):
    return 1.0 / (1.0 + jnp.exp(-x))


def _silu(x):
    return x * _sigmoid(x)


def _split3(x):
    p1 = x.astype(BF16)
    r1 = x - p1.astype(F32)
    p2 = r1.astype(BF16)
    p3 = (r1 - p2.astype(F32)).astype(BF16)
    return p1, p2, p3


def _const_spec(shape):
    nd = len(shape)
    return pl.BlockSpec(shape, lambda *_: (0,) * nd, pipeline_mode=pl.Buffered(1))


def _proj_kernel(x_ref, g_ref, wm_ref, ws_ref, wkvt_ref, cos_ref, sin_ref, cost_ref, sint_ref,
                 q_ref, kv_ref, conv_ref, z_ref, small_ref, kaug_ref, vaug_ref, kwin_ref, vwaug_ref,
                 kv4t_ref, kvwt_ref, *, tm, pos_rows):
    x = x_ref[...]
    ms = jnp.mean(x * x, axis=-1, keepdims=True)
    xn = (x * lax.rsqrt(ms + EPS) * g_ref[...]).astype(BF16)
    cos = cos_ref[...]
    sin = sin_ref[...]
    lane = lax.broadcasted_iota(jnp.int32, (tm, LANES), 1)
    low_half = (lane % HEAD_DIM) < (HEAD_DIM // 2)

    def rope(v):
        rot = jnp.where(low_half, pltpu.roll(v, LANES - HEAD_DIM // 2, 1), pltpu.roll(v, HEAD_DIM // 2, 1))
        return v * cos + rot * sin

    q = _dot(xn, wm_ref[:, 0:NSA_Q_W])
    for r in range(NSA_REP):
        sl = slice(r * LANES, (r + 1) * LANES)
        q_ref[:, sl] = (rope(q[:, sl]) * (HEAD_DIM ** -0.5 * LOG2E)).astype(BF16)

    kv = _dot(xn, wm_ref[:, NSA_Q_W:NSA_Q_W + 6 * NSA_KV_W])
    group0 = lane < HEAD_DIM
    for c in range(6):
        sl = slice(c * LANES, (c + 1) * LANES)
        blk = kv[:, sl]
        if c % 2 == 0:
            blk = rope(blk)
        kv_ref[:, sl] = blk
        if c == 2:
            kaug_ref[:, LANES:2 * LANES] = blk.astype(BF16)
        elif c == 4:
            kwin_ref[...] = blk.astype(BF16)
        elif c in (3, 5):
            v_ref = vaug_ref if c == 3 else vwaug_ref
            v_ref[:, 0:LANES] = jnp.where(group0, blk, 1.0).astype(BF16)
            v_ref[:, LANES:2 * LANES] = jnp.where(group0, 1.0, blk).astype(BF16)

    row0 = (pl.program_id(0) * tm) % pos_rows
    rows = row0 + lax.broadcasted_iota(jnp.int32, (tm, LANES), 0)
    kaug_ref[:, 0:LANES] = jnp.where(rows // SEL_LEN == lane, 1.0, 0.0).astype(BF16)

    kvt = _dot_nt(wkvt_ref[...], xn)
    cos_t = cost_ref[...]
    sin_t = sint_ref[...]
    half = HEAD_DIM // 2
    for c in range(6):
        blk = kvt[c * LANES:(c + 1) * LANES]
        if c % 2 == 0:
            parts = []
            for g in range(NSA_KV_HEADS):
                x1 = blk[g * HEAD_DIM:g * HEAD_DIM + half]
                x2 = blk[g * HEAD_DIM + half:(g + 1) * HEAD_DIM]
                parts += [x1 * cos_t - x2 * sin_t, x2 * cos_t + x1 * sin_t]
            blk = jnp.concatenate(parts, axis=0)
        dst = kv4t_ref if c < 4 else kvwt_ref
        dst[(c % 4) * LANES:(c % 4 + 1) * LANES, :] = blk

    c0 = NSA_Q_W + 6 * NSA_KV_W
    conv_ref[...] = _dot(xn, wm_ref[:, c0:c0 + GDN_CONV_CH])
    z_ref[...] = _dot(xn, wm_ref[:, c0 + GDN_CONV_CH:c0 + GDN_CONV_CH + GDN_V_W])
    small_ref[...] = _dot(xn, ws_ref[...])


def _proj(x, g, wm, ws, wkvt, cos, sin, cos_t, sin_t, *, tm):
    rows = x.shape[0]
    pos_rows = cos.shape[0]
    n_pos_blk = pos_rows // tm
    n_seq = rows // pos_rows
    grid = (rows // tm,)
    row_spec = lambda w: pl.BlockSpec((tm, w), lambda i: (i, 0))
    pos_spec = pl.BlockSpec((tm, LANES), lambda i: (i % n_pos_blk, 0))
    pos_t_spec = pl.BlockSpec((HEAD_DIM // 2, tm), lambda i: (0, i % n_pos_blk))
    tok_minor = lambda r: pl.BlockSpec((None, r, tm), lambda i: (i // n_pos_blk, 0, i % n_pos_blk))
    out_shape = (
        jax.ShapeDtypeStruct((rows, NSA_Q_W), BF16),
        jax.ShapeDtypeStruct((rows, 6 * NSA_KV_W), F32),
        jax.ShapeDtypeStruct((rows, GDN_CONV_CH), F32),
        jax.ShapeDtypeStruct((rows, GDN_V_W), F32),
        jax.ShapeDtypeStruct((rows, LANES), F32),
        jax.ShapeDtypeStruct((rows, 2 * LANES), BF16),
        jax.ShapeDtypeStruct((rows, 2 * LANES), BF16),
        jax.ShapeDtypeStruct((rows, LANES), BF16),
        jax.ShapeDtypeStruct((rows, 2 * LANES), BF16),
    )
    out_shape_t = (jax.ShapeDtypeStruct((n_seq, 4 * NSA_KV_W, pos_rows), F32),
                   jax.ShapeDtypeStruct((n_seq, 2 * NSA_KV_W, pos_rows), F32))
    return pl.pallas_call(
        functools.partial(_proj_kernel, tm=tm, pos_rows=pos_rows),
        grid=grid,
        in_specs=[row_spec(D_MODEL), _const_spec((1, D_MODEL)), _const_spec(wm.shape), _const_spec(ws.shape),
                  _const_spec(wkvt.shape), pos_spec, pos_spec, pos_t_spec, pos_t_spec],
        out_specs=tuple(row_spec(s.shape[1]) for s in out_shape) + (tok_minor(4 * NSA_KV_W), tok_minor(2 * NSA_KV_W)),
        out_shape=out_shape + out_shape_t,
        compiler_params=pltpu.CompilerParams(dimension_semantics=("arbitrary",), vmem_limit_bytes=VMEM_LIMIT),
        name="proj",
    )(x, g, wm, ws, wkvt, cos, sin, cos_t, sin_t)


def _mix_ffn_kernel(x_ref, on_ref, og_ref, wo_ref, gf_ref, wgu_ref, wd_ref, gl_ref, y_ref, *, d_ff):
    h = x_ref[...] + _dot(on_ref[...], wo_ref[0:NSA_Q_W, :]) + _dot(og_ref[...], wo_ref[NSA_Q_W:, :])
    ms = jnp.mean(h * h, axis=-1, keepdims=True)
    hn = (h * lax.rsqrt(ms + EPS) * gf_ref[...]).astype(BF16)
    gate = _dot(hn, wgu_ref[:, 0:d_ff])
    up = _dot(hn, wgu_ref[:, d_ff:])
    act = (_silu(gate) * up).astype(BF16)
    h = h + _dot(act, wd_ref[...])
    ms = jnp.mean(h * h, axis=-1, keepdims=True)
    y_ref[...] = h * lax.rsqrt(ms + EPS) * gl_ref[...]


def _mix_ffn(x, o_nsa, o_gdn, wo, gf, wgu, wd, gl, *, tm):
    rows = x.shape[0]
    d_ff = wd.shape[0]
    row_spec = lambda w: pl.BlockSpec((tm, w), lambda i: (i, 0))
    return pl.pallas_call(
        functools.partial(_mix_ffn_kernel, d_ff=d_ff),
        grid=(rows // tm,),
        in_specs=[row_spec(D_MODEL), row_spec(NSA_Q_W), row_spec(GDN_V_W), _const_spec(wo.shape),
                  _const_spec((1, D_MODEL)), _const_spec(wgu.shape), _const_spec(wd.shape),
                  _const_spec((1, D_MODEL))],
        out_specs=row_spec(D_MODEL),
        out_shape=jax.ShapeDtypeStruct((rows, D_MODEL), F32),
        compiler_params=pltpu.CompilerParams(dimension_semantics=("arbitrary",), vmem_limit_bytes=VMEM_LIMIT),
        name="mix_ffn",
    )(x, o_nsa, o_gdn, wo, gf, wgu, wd, gl)


def _compress_rows(load_rows, wlo_ref, whi_ref, pelo_ref, pehi_ref, nc):
    acc_lo = jnp.zeros((nc, 2 * LANES), F32)
    acc_hi = jnp.zeros((nc, 2 * LANES), F32)
    for l in range(CMP_STRIDE):
        x = load_rows(l)
        acc_lo = acc_lo + _dot((x + pelo_ref[l:l + 1, :]).astype(BF16), wlo_ref[l])
        acc_hi = acc_hi + _dot((x + pehi_ref[l:l + 1, :]).astype(BF16), whi_ref[l])
    nxt = pltpu.roll(acc_hi, nc - 1, 0)
    row = lax.broadcasted_iota(jnp.int32, (nc, 2 * LANES), 0)
    return jnp.where(row < nc - 1, acc_lo + nxt, 0.0)


def _compress_kernel(k_ref, v_ref, wlo_ref, whi_ref, pelo_ref, pehi_ref, ckv_ref, *, nc):
    load = lambda l: jnp.concatenate([k_ref[pl.ds(l, nc, stride=CMP_STRIDE), :],
                                      v_ref[pl.ds(l, nc, stride=CMP_STRIDE), :]], axis=1)
    ckv_ref[...] = _compress_rows(load, wlo_ref, whi_ref, pelo_ref, pehi_ref, nc).astype(BF16)


def _compress(kv, wlo, whi, pelo, pehi, *, batch, seq):
    nc = seq // CMP_STRIDE
    return pl.pallas_call(
        functools.partial(_compress_kernel, nc=nc),
        grid=(batch,),
        in_specs=[pl.BlockSpec((seq, LANES), lambda b: (b, 0)), pl.BlockSpec((seq, LANES), lambda b: (b, 1)),
                  _const_spec(wlo.shape), _const_spec(whi.shape), _const_spec(pelo.shape), _const_spec(pehi.shape)],
        out_specs=pl.BlockSpec((nc, 2 * LANES), lambda b: (b, 0)),
        out_shape=jax.ShapeDtypeStruct((batch * nc, 2 * LANES), BF16),
        compiler_params=pltpu.CompilerParams(dimension_semantics=("arbitrary",), vmem_limit_bytes=VMEM_LIMIT),
        name="compress",
    )(kv, kv, wlo, whi, pelo, pehi)


def _masked_softmax2(s):
    m = jnp.maximum(jnp.max(s, axis=-1, keepdims=True), NEG / 8)
    e = jnp.exp2(s - m)
    return e * (1.0 / jnp.maximum(jnp.sum(e, axis=-1, keepdims=True), 1e-30))


def _select_blocks(score, sel, blk, axis, rounds, filler=iter(())):
    n = score.shape[axis]
    for _ in range(rounds):
        mx = jnp.max(score, axis=axis, keepdims=True)
        idx = jnp.min(jnp.where(score == mx, blk, n), axis=axis, keepdims=True)
        hit = blk == idx
        sel = jnp.where(hit, jnp.where(mx > NEG / 2, 1.0, sel), sel)
        score = jnp.where(hit, -3e38, score)
        next(filler, None)
    for _ in filler:
        pass
    return sel


def _nsa_prompt_kernel(q_ref, small_ref, kaug_ref, vaug_ref, kwin_ref, vwaug_ref, ckv_ref, mt_ref, o_ref,
                       *, seq, kc):
    nc = seq // CMP_STRIDE
    qb = Q_BLOCK
    start = pl.program_id(1) * qb
    wlen = WINDOW + qb
    groups = range(NSA_KV_HEADS)
    stack = lambda x: jnp.concatenate([x] * NSA_REP, axis=0)

    lane = lax.broadcasted_iota(jnp.int32, (qb, LANES), 1)
    gates = _sigmoid(small_ref[...])

    qrow_c = lax.broadcasted_iota(jnp.int32, (qb, nc), 0)
    ncol = lax.broadcasted_iota(jnp.int32, (qb, nc), 1)
    cmp_bias = stack(jnp.where(ncol * CMP_STRIDE + (CMP_LEN - 1) - qrow_c <= start, 0.0, NEG))
    tail_bias = stack(jnp.where(lane <= lax.broadcasted_iota(jnp.int32, (qb, qb), 0), 0.0, NEG))
    wbase = pl.multiple_of(jnp.maximum(start - WINDOW, 0), qb)
    back = (lax.broadcasted_iota(jnp.int32, (qb, wlen), 0) - lax.broadcasted_iota(jnp.int32, (qb, wlen), 1)
            + (start - wbase))
    win_bias = stack(jnp.where(back >= 0, jnp.where(back <= WINDOW, 0.0, NEG), NEG))

    glanes = [slice(g * LANES, (g + 1) * LANES) for g in groups]
    qs = [jnp.concatenate([jnp.where((lane >= HEAD_DIM) == (g == 1), q_ref[:, r * LANES:(r + 1) * LANES], 0.0)
                           .astype(BF16) for r in range(NSA_REP)], axis=0) for g in groups]
    ps = [_masked_softmax2(_dot_nt(q, ckv_ref[:, 0:LANES]) + cmp_bias) for q in qs]
    psums = [p[0:qb] + p[qb:2 * qb] + p[2 * qb:3 * qb] + p[3 * qb:4 * qb] for p in ps]
    imp_t = jnp.concatenate([sum(_dot_nt(mt_ref[...], piece) for piece in _split3(psum)) for psum in psums], axis=1)
    o_cmps = [_dot(p.astype(BF16), ckv_ref[:, LANES:2 * LANES]) for p in ps]

    done = {}

    def independent_work():
        s = [_dot_nt(q, kwin_ref[pl.ds(wbase, wlen), :]) + win_bias for q in qs]
        yield
        mx = [jnp.max(x, axis=-1, keepdims=True) for x in s]
        yield
        e = []
        for g in groups:
            e.append(jnp.exp2(s[g] - mx[g]).astype(BF16))
            yield
        acc = [_dot(e[g], vwaug_ref[pl.ds(wbase, wlen), glanes[g]]) for g in groups]
        yield
        o_win = [a / pltpu.roll(a, HEAD_DIM, 1) for a in acc]
        yield
        rows_of = lambda x, r: x[r * qb:(r + 1) * qb]
        done["partial"] = []
        for g in groups:
            done["partial"].append(
                [gates[:, 3 * (g * NSA_REP + r):3 * (g * NSA_REP + r) + 1] * rows_of(o_cmps[g], r)
                 + gates[:, 3 * (g * NSA_REP + r) + 2:3 * (g * NSA_REP + r) + 3] * rows_of(o_win[g], r)
                 for r in range(NSA_REP)])
            yield
        s_tail = [_dot_nt(q, kaug_ref[pl.ds(start, qb), LANES:2 * LANES]) + tail_bias for q in qs]
        yield
        done["m0"] = [jnp.max(x, axis=-1, keepdims=True) for x in s_tail]
        yield
        done["acc0"] = [_dot(jnp.exp2(s_tail[g] - done["m0"][g]).astype(BF16), vaug_ref[pl.ds(start, qb), glanes[g]])
                        for g in groups]
        yield

    blk_t = lax.broadcasted_iota(jnp.int32, (LANES, 2 * qb), 0)
    qpos_t = start + lax.broadcasted_iota(jnp.int32, (LANES, 2 * qb), 1) % qb
    cur_t = qpos_t // SEL_LEN
    visible = blk_t * SEL_LEN <= qpos_t
    forced = (blk_t == 0) | (blk_t == cur_t) | (blk_t == cur_t - 1)
    sel_t = _select_blocks(jnp.where(visible, jnp.where(forced, NEG, imp_t), NEG),
                           jnp.where(visible, jnp.where(forced, 1.0, 0.0), 0.0), blk_t, 0, SEL_TOPK - 3,
                           independent_work())
    partial, m0s, acc0s = done["partial"], done["m0"], done["acc0"]
    bias_t = jnp.where(blk_t * SEL_LEN < start, jnp.where(sel_t > 0.5, 0.0, NEG), NEG).T.astype(BF16)
    n_chunks = (start + kc - 1) // kc
    q_augs = [jnp.concatenate([stack(bias_t[g * qb:(g + 1) * qb]), qs[g]], axis=1) for g in groups]

    def sel_step(c, carry):
        off = pl.multiple_of(c * kc, kc)
        k = kaug_ref[pl.ds(off, kc), :]
        out = []
        for g in groups:
            m, acc = carry[g]
            s = _dot_nt(q_augs[g], k)
            m_new = jnp.maximum(m, jnp.max(s, axis=-1, keepdims=True))
            e = jnp.exp2(s - m_new).astype(BF16)
            out.append((m_new, jnp.exp2(m - m_new) * acc + _dot(e, vaug_ref[pl.ds(off, kc), g * LANES:(g + 1) * LANES])))
        return tuple(out)

    carry = lax.fori_loop(0, n_chunks // 2, lambda c, x: sel_step(2 * c + 1, sel_step(2 * c, x)),
                          tuple(zip(m0s, acc0s)))
    sel_out = lax.cond(n_chunks % 2 == 1, lambda x: sel_step(n_chunks - 1, x), lambda x: x, carry)

    outs = [None] * NSA_REP
    for g in groups:
        acc = sel_out[g][1]
        o_sel = acc / pltpu.roll(acc, HEAD_DIM, 1)
        for r in range(NSA_REP):
            c0 = (g * NSA_REP + r) * 3
            o = partial[g][r] + gates[:, c0 + 1:c0 + 2] * o_sel[r * qb:(r + 1) * qb]
            outs[r] = o if g == 0 else jnp.where(lane < HEAD_DIM, outs[r], o)

    for r in range(NSA_REP):
        o_ref[:, r * LANES:(r + 1) * LANES] = outs[r].astype(BF16)


def _nsa_prompt(q, small, kaug, vaug, kwin, vwaug, ckv, mt, *, batch, seq, kc):
    nb = seq // Q_BLOCK
    nc = seq // CMP_STRIDE
    blk_spec = lambda w: pl.BlockSpec((Q_BLOCK, w), lambda b, i: (b * nb + i, 0))
    seq_spec = lambda w: pl.BlockSpec((seq, w), lambda b, i: (b, 0))
    return pl.pallas_call(
        functools.partial(_nsa_prompt_kernel, seq=seq, kc=kc),
        grid=(batch, nb),
        in_specs=[blk_spec(NSA_Q_W), blk_spec(LANES), seq_spec(2 * LANES), seq_spec(2 * LANES), seq_spec(LANES),
                  seq_spec(2 * LANES), pl.BlockSpec((nc, 2 * LANES), lambda b, i: (b, 0)), _const_spec(mt.shape)],
        out_specs=blk_spec(NSA_Q_W),
        out_shape=jax.ShapeDtypeStruct((batch * seq, NSA_Q_W), BF16),
        compiler_params=pltpu.CompilerParams(dimension_semantics=("arbitrary", "arbitrary"),
                                             vmem_limit_bytes=VMEM_LIMIT),
        name="nsa_prompt",
    )(q, small, kaug, vaug, kwin, vwaug, ckv, mt)


def _nsa_sample_kernel(pt_ref, *refs, past, pps):
    del pt_ref
    pages = refs[:pps]
    (q_ref, kvn_ref, small_ref, win_ref, perm_ref, wlo_ref, whi_ref, pelo_ref, pehi_ref, mt_ref, exp_ref,
     o_ref, xs_ref, s_ref, vs_ref) = refs[pps:]
    step = pl.program_id(1)
    n_pages = past // PAGE_SIZE
    cpp = PAGE_SIZE // CMP_STRIDE
    heads = NSA_HEADS
    row = lax.broadcasted_iota(jnp.int32, (heads, LANES), 0)
    lane = lax.broadcasted_iota(jnp.int32, (heads, LANES), 1)
    mine = (lane >= HEAD_DIM) == (row >= NSA_REP)
    rr = row % NSA_REP
    qsel = jnp.zeros((heads, LANES), F32)
    for r in range(NSA_REP):
        qsel = jnp.where(rr == r, q_ref[:, r * LANES:(r + 1) * LANES].astype(F32), qsel)
    qf = jnp.where(mine, qsel, 0.0)
    q = qf.astype(BF16)

    nc = past // CMP_STRIDE
    cw = 2 * LANES

    def take_pages(first_page):
        for k in range(pps):
            pg = first_page + k
            page = pages[k]
            tok = slice(pg * PAGE_SIZE, (pg + 1) * PAGE_SIZE)
            s_ref[:, tok] = _dot(q, page[2 * LANES:3 * LANES, :].astype(BF16))
            vs_ref[:, tok] = page[3 * LANES:4 * LANES, :].astype(BF16)
            x = _dot_nt(perm_ref[...], page[0:2 * LANES, :].astype(BF16))
            for l in range(CMP_STRIDE):
                xs_ref[pg * cpp:(pg + 1) * cpp, l * cw:(l + 1) * cw] = x[l * cpp:(l + 1) * cpp, :]

    for st in range(n_pages // pps):
        pl.when(step == st)(functools.partial(take_pages, st * pps))

    @pl.when(step == pl.num_programs(1) - 1)
    def _():
        n_cmp = nc - 1
        nsp = mt_ref.shape[0]
        cur = past // SEL_LEN

        for l in range(CMP_STRIDE):
            xs_ref[nc:nc + SUBLANES, l * cw:(l + 1) * cw] = jnp.broadcast_to(pelo_ref[l:l + 1, :], (SUBLANES, cw))
            xs_ref[nc + SUBLANES:nc + 2 * SUBLANES, l * cw:(l + 1) * cw] = jnp.broadcast_to(pehi_ref[l:l + 1, :],
                                                                                            (SUBLANES, cw))
        x16 = xs_ref[...].astype(BF16)
        acc_lo = _dot(x16, wlo_ref[...])
        acc_hi = _dot(x16, whi_ref[...])
        bias = acc_lo[nc:nc + 1] + acc_hi[nc + SUBLANES:nc + SUBLANES + 1]
        nxt = pltpu.roll(acc_hi[0:nc], nc - 1, 0)
        crow = lax.broadcasted_iota(jnp.int32, (nc, 2 * LANES), 0)
        ckv = jnp.where(crow < nc - 1, acc_lo[0:nc] + nxt + bias, 0.0).astype(BF16)

        ncol = lax.broadcasted_iota(jnp.int32, (1, nc), 1)
        cmp_valid = (ncol * CMP_STRIDE + CMP_LEN - 1 <= past) & (ncol < n_cmp)
        p = _masked_softmax2(jnp.where(cmp_valid, _dot_nt(q, ckv[:, 0:LANES]), NEG))
        o_cmp = _dot(p.astype(BF16), ckv[:, LANES:2 * LANES])

        prow = lax.broadcasted_iota(jnp.int32, (heads, nc), 0)
        ps = [jnp.sum(jnp.where(prow // NSA_REP == g, p, 0.0), axis=0, keepdims=True) for g in range(NSA_KV_HEADS)]
        psum = jnp.where(prow < NSA_REP, ps[0], ps[1])
        imp = sum(_dot_nt(piece, mt_ref[...]) for piece in _split3(psum))
        blk = lax.broadcasted_iota(jnp.int32, (heads, nsp), 1)
        forced = (blk == 0) | (blk == cur) | (blk == cur - 1)
        score = jnp.where(blk * SEL_LEN <= past, imp + jnp.where(forced, SEL_BONUS, 0.0), NEG)
        score_t = score.T
        bi = lax.broadcasted_iota(jnp.int32, (nsp, nsp), 0)
        bj = lax.broadcasted_iota(jnp.int32, (nsp, nsp), 1)
        tie = jnp.where(bi < bj, 1.0, 0.0)
        picks = []
        for g in range(NSA_KV_HEADS):
            c = g * NSA_REP
            s_i, s_j = score_t[:, c:c + 1], score[c:c + 1, :]
            beats = jnp.where(s_i > s_j, 1.0, jnp.where(s_i == s_j, tie, 0.0))
            rank = jnp.sum(beats, axis=0, keepdims=True)
            picks.append(jnp.where(rank < SEL_TOPK, jnp.where(s_j > NEG / 2, 1.0, 0.0), 0.0))
        sel8 = jnp.where(lax.broadcasted_iota(jnp.int32, (heads, nsp), 0) < NSA_REP, picks[0], picks[1])
        keep = _dot(sel8.astype(BF16), exp_ref[...]) > 0.5

        kvn = kvn_ref[...]
        new_ok = sel8[:, cur:cur + 1] > 0.5
        s = jnp.where(keep, s_ref[...], NEG)
        s_new = jnp.where(new_ok, jnp.sum(qf * kvn[:, 2 * LANES:3 * LANES], axis=-1, keepdims=True), NEG)
        m = jnp.maximum(jnp.max(s, axis=-1, keepdims=True), s_new)
        e = jnp.exp2(s - m)
        e_new = jnp.where(new_ok, jnp.exp2(s_new - m), 0.0)
        o_sel = ((_dot_nt(e.astype(BF16), vs_ref[...]) + e_new * kvn[:, 3 * LANES:4 * LANES])
                 / (jnp.sum(e, axis=-1, keepdims=True) + e_new))

        nw = win_ref.shape[0] // 2
        s = _dot(q, win_ref[0:nw, :].astype(BF16))
        s_new = jnp.sum(qf * kvn[:, 4 * LANES:5 * LANES], axis=-1, keepdims=True)
        m = jnp.maximum(jnp.max(s, axis=-1, keepdims=True), s_new)
        e = jnp.exp2(s - m)
        e_new = jnp.exp2(s_new - m)
        l = jnp.sum(e, axis=-1, keepdims=True) + e_new
        o_win = (_dot_nt(e.astype(BF16), win_ref[nw:, :].astype(BF16))
                 + e_new * kvn[:, 5 * LANES:6 * LANES]) / l

        gates = _sigmoid(small_ref[...])
        gate = lambda br: jnp.sum(jnp.where(lane == row * 3 + br, gates, 0.0), axis=-1, keepdims=True)
        o = gate(0) * o_cmp + gate(1) * o_sel + gate(2) * o_win
        for r in range(NSA_REP):
            o_ref[:, r * LANES:(r + 1) * LANES] = jnp.where(
                lane[0:1] < HEAD_DIM, o[r:r + 1], o[NSA_REP + r:NSA_REP + r + 1]).astype(BF16)


def _nsa_sample(page_table, cache, q, kvn, small, win, wlo, whi, pelo, pehi, mt, *, pps):
    batch, n_pages = page_table.shape
    past = n_pages * PAGE_SIZE
    rows = cache.shape[1]
    cpp = PAGE_SIZE // CMP_STRIDE
    nsp = mt.shape[0]
    expand = jnp.asarray(np.arange(nsp)[:, None] == np.arange(past)[None, :] // SEL_LEN, BF16)
    tok = np.arange(PAGE_SIZE)
    perm = jnp.asarray((tok[None, :] % CMP_STRIDE) * cpp + tok[None, :] // CMP_STRIDE == tok[:, None], BF16)

    def page_spec(k):
        return pl.BlockSpec((None, rows, PAGE_SIZE), lambda b, s, pt: (pt[b * n_pages + s * pps + k], 0, 0))

    per_b = lambda shape: pl.BlockSpec((None,) + shape, lambda b, s, pt: (b, 0, 0))
    const = lambda a: pl.BlockSpec(a.shape, lambda b, s, pt: (0,) * a.ndim, pipeline_mode=pl.Buffered(1))
    grid_spec = pltpu.PrefetchScalarGridSpec(
        num_scalar_prefetch=1,
        grid=(batch, n_pages // pps),
        in_specs=[page_spec(k) for k in range(pps)] + [
            per_b((1, NSA_Q_W)), per_b((1, 6 * NSA_KV_W)), per_b((1, LANES)), per_b(win.shape[1:]),
            const(perm), const(wlo), const(whi), const(pelo), const(pehi), const(mt), const(expand)],
        out_specs=per_b((1, NSA_Q_W)),
        scratch_shapes=[pltpu.VMEM((past // CMP_STRIDE + 2 * SUBLANES, CMP_STRIDE * 2 * LANES), F32),
                        pltpu.VMEM((NSA_HEADS, past), F32),
                        pltpu.VMEM((LANES, past), BF16)],
    )
    return pl.pallas_call(
        functools.partial(_nsa_sample_kernel, past=past, pps=pps),
        grid_spec=grid_spec,
        out_shape=jax.ShapeDtypeStruct((batch, 1, NSA_Q_W), BF16),
        compiler_params=pltpu.CompilerParams(dimension_semantics=("arbitrary", "arbitrary"),
                                             vmem_limit_bytes=VMEM_LIMIT),
        name="nsa_sample",
    )(page_table.reshape(-1), *([cache] * pps), q, kvn, small, win, perm, wlo, whi, pelo, pehi, mt, expand)


def _l2norm(x):
    return x * lax.rsqrt(jnp.sum(x * x, axis=-1, keepdims=True) + EPS)


def _softplus(x):
    return jnp.maximum(x, 0.0) + jnp.log(1.0 + jnp.exp(-jnp.abs(x)))


def _gdn_step_kernel(x_ref, z_ref, small_ref, buf_ref, s0_ref, cw_ref, pcol_ref, nw_ref, o_ref, sout_ref):
    hist = CONV_W - 1
    conv = jnp.sum(buf_ref[...] * cw_ref[0:hist, :], axis=0, keepdims=True) + x_ref[...] * cw_ref[hist:CONV_W, :]
    act = _silu(conv)
    small = small_ref[...]
    g_all = -jnp.exp(pcol_ref[0:1, :]) * _softplus(small + pcol_ref[1:2, :])
    beta_all = _sigmoid(small)
    row = lax.broadcasted_iota(jnp.int32, (SUBLANES, GDN_DK), 0)
    for h in range(GDN_HEADS):
        hs = slice(h * GDN_DK, (h + 1) * GDN_DK)
        qh = _l2norm(act[:, hs]) * (GDN_DK ** -0.5)
        kh = _l2norm(act[:, GDN_QK_W + h * GDN_DK:GDN_QK_W + (h + 1) * GDN_DK])
        vh = act[:, 2 * GDN_QK_W + h * GDN_DV:2 * GDN_QK_W + (h + 1) * GDN_DV]
        eg = jnp.exp(g_all[:, A_COL + h:A_COL + h + 1])
        bt = beta_all[:, B_COL + h:B_COL + h + 1]
        s = s0_ref[h]
        kq = jnp.where(row == 0, kh, jnp.where(row == 1, qh, 0.0)).astype(BF16)
        ks_qs = _dot(kq, s.astype(BF16))
        vn = bt * (vh - eg * ks_qs[0:1])
        o = eg * ks_qs[1:2] + jnp.sum(qh * kh, axis=-1, keepdims=True) * vn
        k8 = jnp.where(row == 0, kh, 0.0).astype(BF16)
        vn8 = jnp.where(row == 0, vn, 0.0).astype(BF16)
        sout_ref[h] = s * eg + lax.dot_general(k8, vn8, TN_DIMS, preferred_element_type=F32)
        on = o * lax.rsqrt(jnp.mean(o * o, axis=-1, keepdims=True) + EPS) * nw_ref[...]
        o_ref[:, hs] = (on * _silu(z_ref[:, hs])).astype(BF16)


def _gdn_step(x, z, small, buf, s0, cw, pcol, nw):
    batch = x.shape[0]
    per_b = lambda shape: pl.BlockSpec((None,) + shape, lambda b: (b,) + (0,) * len(shape))
    state = (GDN_HEADS, GDN_DK, GDN_DV)
    return pl.pallas_call(
        _gdn_step_kernel,
        grid=(batch,),
        in_specs=[per_b((1, GDN_CONV_CH)), per_b((1, GDN_V_W)), per_b((1, LANES)), per_b((CONV_W - 1, GDN_CONV_CH)),
                  per_b(state), _const_spec(cw.shape), _const_spec(pcol.shape), _const_spec(nw.shape)],
        out_specs=(per_b((1, GDN_V_W)), per_b(state)),
        out_shape=(jax.ShapeDtypeStruct((batch, 1, GDN_V_W), BF16),
                   jax.ShapeDtypeStruct((batch,) + state, F32)),
        compiler_params=pltpu.CompilerParams(dimension_semantics=("arbitrary",), vmem_limit_bytes=VMEM_LIMIT),
        name="gdn_step",
    )(x[:, None], z[:, None], small[:, None], buf, s0, cw, pcol, nw)


GDN_CHUNK = 64
GDN_STACK = GDN_HEADS * GDN_CHUNK


def _stack_heads(x, col0, width):
    return jnp.concatenate([x[:, col0 + h * width:col0 + (h + 1) * width] for h in range(GDN_HEADS)], axis=0)


def _gdn_prep_kernel(conv_ref, small_ref, smallt_ref, buf_ref, cw_ref, pcol_ref, prow_ref, tril_ref, triu_ref,
                     u_ref, w_ref, qg_ref, kgt_ref, aqk_ref, gl_ref, xs_ref, *, tb):
    j = pl.program_id(1)
    pad = SUBLANES
    hist = CONV_W - 1
    ck = GDN_CHUNK
    st = GDN_STACK

    @pl.when(j == 0)
    def _():
        xs_ref[0:pad, :] = jnp.zeros((pad, GDN_CONV_CH), F32)
        xs_ref[pad - hist:pad, :] = buf_ref[...]

    xs_ref[pad:pad + tb, :] = conv_ref[...]
    xs = xs_ref[...]
    conv = xs[pad:] * cw_ref[hist:CONV_W, :]
    for t in range(hist):
        conv = conv + pltpu.roll(xs, hist - t, 0)[pad:] * cw_ref[t:t + 1, :]
    xs_ref[0:pad, :] = xs[tb:tb + pad]
    act = _silu(conv)

    small = small_ref[...]
    g_col = -jnp.exp(pcol_ref[0:1, :]) * _softplus(small + pcol_ref[1:2, :])
    beta = _sigmoid(small)
    g_row = -jnp.exp(prow_ref[:, 0:1]) * _softplus(smallt_ref[...] + prow_ref[:, 1:2])
    gcum_col = sum(_dot(tril_ref[...], piece) for piece in _split3(g_col))
    gcum_row = sum(_dot(piece, triu_ref[...]) for piece in _split3(g_row))

    ii = lax.broadcasted_iota(jnp.int32, (st, st), 0)
    jj = lax.broadcasted_iota(jnp.int32, (st, st), 1)
    same_head = (ii // ck) == (jj // ck)
    incl = same_head & (ii >= jj)
    strict = same_head & (ii > jj)
    eye = jnp.where(ii == jj, 1.0, 0.0)
    hrow = lax.broadcasted_iota(jnp.int32, (SUBLANES, LANES), 0)

    lms, rhss = [], []
    for ci in range(tb // ck):
        r0 = ci * ck
        rs = slice(r0, r0 + ck)
        gc = _stack_heads(gcum_col[rs], A_COL, 1)
        bt = _stack_heads(beta[rs], B_COL, 1)
        gr = jnp.concatenate([gcum_row[h:h + 1, r0:r0 + ck] for h in range(GDN_HEADS)], axis=1)
        glast = [gcum_col[r0 + ck - 1:r0 + ck, A_COL + h:A_COL + h + 1] for h in range(GDN_HEADS)]
        gl_stack = jnp.concatenate([jnp.broadcast_to(x, (ck, 1)) for x in glast], axis=0)
        dec = jnp.where(incl, jnp.exp(jnp.where(incl, gc - gr, 0.0)), 0.0)
        qs = jnp.concatenate([_l2norm(act[rs, h * GDN_DK:(h + 1) * GDN_DK]) for h in range(GDN_HEADS)],
                             axis=0) * (GDN_DK ** -0.5)
        ks = jnp.concatenate([_l2norm(act[rs, GDN_QK_W + h * GDN_DK:GDN_QK_W + (h + 1) * GDN_DK])
                              for h in range(GDN_HEADS)], axis=0)
        vs = _stack_heads(act[rs], 2 * GDN_QK_W, GDN_DV)
        kb = ks * bt
        k16 = ks.astype(BF16)
        lms.append(jnp.where(strict, _dot_nt(kb.astype(BF16), k16) * dec, 0.0))
        eg = jnp.exp(gc)
        rhss.append(jnp.concatenate([vs * bt, kb * eg], axis=1).astype(BF16))
        orow = slice(ci * st, (ci + 1) * st)
        qg_ref[orow, :] = (qs * eg).astype(BF16)
        aqk_ref[orow, :] = (_dot_nt(qs.astype(BF16), k16) * dec).astype(BF16)
        kgt_ref[ci * GDN_DK:(ci + 1) * GDN_DK, :] = (ks * jnp.exp(gl_stack - gc)).T.astype(BF16)
        gl = jnp.zeros((SUBLANES, LANES), F32)
        for h in range(GDN_HEADS):
            gl = jnp.where(hrow == h, jnp.exp(glast[h]), gl)
        gl_ref[ci * SUBLANES:(ci + 1) * SUBLANES, :] = gl

    ainvs = [eye - lm for lm in lms]
    pws = lms
    n = 2
    while n < ck:
        pw16s = [pw.astype(BF16) for pw in pws]
        pws = [_dot(pw16, pw16) for pw16 in pw16s]
        ainvs = [ainv + _dot(ainv.astype(BF16), pw.astype(BF16)) for ainv, pw in zip(ainvs, pws)]
        n *= 2
    for ci, (ainv, rhs) in enumerate(zip(ainvs, rhss)):
        sol = _dot(ainv.astype(BF16), rhs)
        orow = slice(ci * st, (ci + 1) * st)
        u_ref[orow, :] = sol[:, 0:GDN_DV]
        w_ref[orow, :] = sol[:, GDN_DV:].astype(BF16)


def _gdn_prep(conv_in, small, smallt, buf, cw, pcol, prow, *, batch, seq, tb):
    nblk = seq // tb
    ncb = tb // GDN_CHUNK
    blk = np.arange(tb)
    same = (blk[:, None] // GDN_CHUNK) == (blk[None, :] // GDN_CHUNK)
    tril = jnp.asarray(same & (blk[:, None] >= blk[None, :]), BF16)
    triu = jnp.asarray(same & (blk[:, None] <= blk[None, :]), BF16)
    row_spec = lambda r, w: pl.BlockSpec((r, w), lambda b, j: (b * nblk + j, 0))
    per_b = lambda shape: pl.BlockSpec((None,) + shape, lambda b, j: (b,) + (0,) * len(shape))
    n_chunks = batch * seq // GDN_CHUNK
    out_shape = (jax.ShapeDtypeStruct((n_chunks * GDN_STACK, GDN_DV), F32),
                 jax.ShapeDtypeStruct((n_chunks * GDN_STACK, GDN_DK), BF16),
                 jax.ShapeDtypeStruct((n_chunks * GDN_STACK, GDN_DK), BF16),
                 jax.ShapeDtypeStruct((n_chunks * GDN_DK, GDN_STACK), BF16),
                 jax.ShapeDtypeStruct((n_chunks * GDN_STACK, GDN_STACK), BF16),
                 jax.ShapeDtypeStruct((n_chunks * SUBLANES, LANES), F32))
    out_specs = (row_spec(ncb * GDN_STACK, GDN_DV), row_spec(ncb * GDN_STACK, GDN_DK),
                 row_spec(ncb * GDN_STACK, GDN_DK), row_spec(ncb * GDN_DK, GDN_STACK),
                 row_spec(ncb * GDN_STACK, GDN_STACK), row_spec(ncb * SUBLANES, LANES))
    return pl.pallas_call(
        functools.partial(_gdn_prep_kernel, tb=tb),
        grid=(batch, nblk),
        in_specs=[row_spec(tb, GDN_CONV_CH), row_spec(tb, LANES),
                  pl.BlockSpec((None, SUBLANES, tb), lambda b, j: (b, 0, j)),
                  per_b((CONV_W - 1, GDN_CONV_CH)),
                  _const_spec(cw.shape), _const_spec(pcol.shape), _const_spec(prow.shape),
                  _const_spec(tril.shape), _const_spec(triu.shape)],
        out_specs=out_specs,
        out_shape=out_shape,
        scratch_shapes=[pltpu.VMEM((SUBLANES + tb, GDN_CONV_CH), F32)],
        compiler_params=pltpu.CompilerParams(dimension_semantics=("arbitrary", "arbitrary"),
                                             vmem_limit_bytes=VMEM_LIMIT),
        name="gdn_prep",
    )(conv_in, small, smallt, buf, cw, pcol, prow, tril, triu)


def _gdn_scan_kernel(u_ref, w_ref, qg_ref, kgt_ref, aqk_ref, gl_ref, z_ref, s0_ref, nw_ref, o_ref, sout_ref, s_ref,
                     *, batch, ncb):
    ck = GDN_CHUNK
    st = GDN_STACK
    sw = GDN_HEADS * GDN_DK

    @pl.when(pl.program_id(0) == 0)
    def _():
        s_ref[...] = s0_ref[...]

    wide_mask = (lax.broadcasted_iota(jnp.int32, (st, sw), 0) // ck
                 == lax.broadcasted_iota(jnp.int32, (st, sw), 1) // GDN_DK)
    tall_mask = (lax.broadcasted_iota(jnp.int32, (sw, st), 0) // GDN_DK
                 == lax.broadcasted_iota(jnp.int32, (sw, st), 1) // ck)

    seqs = range(batch)
    s = [s_ref[b] for b in seqs]
    for ci in range(ncb):
        rows = slice(ci * st, (ci + 1) * st)
        toks = slice(ci * ck, (ci + 1) * ck)
        lhs, kgt_bd, gl_rows = [], [], []
        for b in seqs:
            w_bd = jnp.where(wide_mask, jnp.concatenate([w_ref[b, rows, :]] * GDN_HEADS, axis=1), 0.0)
            qg_bd = jnp.where(wide_mask, jnp.concatenate([qg_ref[b, rows, :]] * GDN_HEADS, axis=1), 0.0)
            lhs.append(jnp.concatenate([w_bd, qg_bd], axis=0))
            kgt = kgt_ref[b, ci * GDN_DK:(ci + 1) * GDN_DK, :]
            kgt_bd.append(jnp.where(tall_mask, jnp.concatenate([kgt] * GDN_HEADS, axis=0), 0.0))
            gl = gl_ref[b, ci * SUBLANES:(ci + 1) * SUBLANES, :]
            gl_rows.append(jnp.concatenate(
                [jnp.broadcast_to(gl[h:h + 1], (GDN_DK, GDN_DV)) for h in range(GDN_HEADS)], axis=0))
        t1 = [_dot(lhs[b], s[b].astype(BF16)) for b in seqs]
        vn16 = [(u_ref[b, rows, :] - t1[b][0:st]).astype(BF16) for b in seqs]
        o = [t1[b][st:] + _dot(aqk_ref[b, rows, :], vn16[b]) for b in seqs]
        s = [s[b] * gl_rows[b] + _dot(kgt_bd[b], vn16[b]) for b in seqs]
        for b in seqs:
            for h in range(GDN_HEADS):
                oh = o[b][h * ck:(h + 1) * ck]
                on = oh * lax.rsqrt(jnp.mean(oh * oh, axis=-1, keepdims=True) + EPS) * nw_ref[...]
                hs = slice(h * GDN_DV, (h + 1) * GDN_DV)
                o_ref[b, toks, hs] = (on * _silu(z_ref[b, toks, hs])).astype(BF16)

    for b in seqs:
        s_ref[b] = s[b]
    sout_ref[...] = s_ref[...]


def _gdn_scan(u, w, qg, kgt, aqk, gl, z, s0, nw, *, batch, seq, tb):
    nblk = seq // tb
    ncb = tb // GDN_CHUNK
    cps = seq // GDN_CHUNK
    sw = GDN_HEADS * GDN_DK
    r3 = lambda a, rows_per_chunk: a.reshape(batch, cps * rows_per_chunk, a.shape[-1])
    blk = lambda rows, width: pl.BlockSpec((batch, rows, width), lambda j: (0, j, 0))
    full = pl.BlockSpec((batch, sw, GDN_DV), lambda j: (0, 0, 0))
    return pl.pallas_call(
        functools.partial(_gdn_scan_kernel, batch=batch, ncb=ncb),
        grid=(nblk,),
        in_specs=[blk(ncb * GDN_STACK, GDN_DV), blk(ncb * GDN_STACK, GDN_DK), blk(ncb * GDN_STACK, GDN_DK),
                  blk(ncb * GDN_DK, GDN_STACK), blk(ncb * GDN_STACK, GDN_STACK), blk(ncb * SUBLANES, LANES),
                  blk(tb, GDN_V_W), full, _const_spec(nw.shape)],
        out_specs=(blk(tb, GDN_V_W), full),
        out_shape=(jax.ShapeDtypeStruct((batch, seq, GDN_V_W), BF16),
                   jax.ShapeDtypeStruct((batch, sw, GDN_DV), F32)),
        scratch_shapes=[pltpu.VMEM((batch, sw, GDN_DV), F32)],
        compiler_params=pltpu.CompilerParams(dimension_semantics=("arbitrary",), vmem_limit_bytes=VMEM_LIMIT),
        name="gdn_scan",
    )(r3(u, GDN_STACK), r3(w, GDN_STACK), r3(qg, GDN_STACK), r3(kgt, GDN_DK), r3(aqk, GDN_STACK),
      r3(gl, SUBLANES), z.reshape(batch, seq, GDN_V_W), s0.reshape(batch, sw, GDN_DV), nw)


def _rope_tables(pos):
    half = HEAD_DIM // 2
    inv = jnp.power(ROPE_THETA, -jnp.arange(half, dtype=F32) * 2.0 / HEAD_DIM)
    ang = pos.astype(F32)[:, None] * inv[None, :]
    cos, sin = jnp.cos(ang), jnp.sin(ang)
    return jnp.tile(cos, (1, 4)), jnp.concatenate([-sin, sin, -sin, sin], axis=1), cos.T, sin.T


def _cmp_to_sel_t(n_cmp, n_sel, rows, cols):
    cs = np.arange(cols)[None, :] * CMP_STRIDE
    ss = np.arange(rows)[:, None] * SEL_LEN
    hit = (cs < ss + SEL_LEN) & (cs + CMP_LEN > ss)
    hit &= (np.arange(cols)[None, :] < n_cmp) & (np.arange(rows)[:, None] < n_sel)
    return jnp.asarray(hit, BF16)


def _layer_weights(w_in, cmp_pe, cmp_w, conv_w, a_log, dt_bias, gdn_norm, w_out):
    cuts = np.cumsum([NSA_Q_W, 6 * NSA_KV_W, GATE_COLS, GDN_CONV_CH, GDN_V_W, GDN_HEADS]).tolist()
    wq, wkv, wgt, wconv, wz, wa, wb = jnp.split(w_in, cuts, axis=1)
    order = np.array([g * NSA_REP + r for r in range(NSA_REP) for g in range(NSA_KV_HEADS)])
    cols = (order[:, None] * HEAD_DIM + np.arange(HEAD_DIM)[None, :]).reshape(-1)
    wm = jnp.concatenate([wq[:, cols], wkv, wconv, wz], axis=1).astype(BF16)
    wkvt = wkv.T.astype(BF16)
    ws = jnp.concatenate([wgt, wa, wb], axis=1)
    ws = jnp.pad(ws, ((0, 0), (0, LANES - ws.shape[1]))).astype(BF16)
    wo = jnp.concatenate([w_out[:NSA_Q_W][cols], w_out[NSA_Q_W:]], axis=0).astype(BF16)

    def blockdiag(l0):
        wk, wv = cmp_w[0, l0:l0 + CMP_STRIDE], cmp_w[1, l0:l0 + CMP_STRIDE]
        z = jnp.zeros_like(wk)
        rows = [jnp.concatenate(r, axis=2) for r in ([wk, z, z, z], [z, wk, z, z], [z, z, wv, z], [z, z, z, wv])]
        return jnp.concatenate(rows, axis=1).astype(BF16)

    def pe_rows(l0):
        pk, pv = cmp_pe[0, l0:l0 + CMP_STRIDE], cmp_pe[1, l0:l0 + CMP_STRIDE]
        return jnp.concatenate([pk, pk, pv, pv], axis=1)

    pcol = jnp.zeros((2, LANES), F32).at[0, A_COL:A_COL + GDN_HEADS].set(a_log)
    pcol = pcol.at[1, A_COL:A_COL + GDN_HEADS].set(dt_bias)
    prow = jnp.zeros((SUBLANES, 2), F32).at[0:GDN_HEADS, 0].set(a_log).at[0:GDN_HEADS, 1].set(dt_bias)
    return dict(wm=wm, ws=ws, wkvt=wkvt, wo=wo, wlo=blockdiag(0), whi=blockdiag(CMP_STRIDE), pelo=pe_rows(0),
                pehi=pe_rows(CMP_STRIDE), cw=conv_w, pcol=pcol, prow=prow, nw=gdn_norm[None, :])


def kernel(x_prompt, x_sample, cache_nsa_kv, cache_nsa_win, state_gdn_S, state_gdn_conv, page_table, norm_mix, w_in,
           nsa_cmp_pe, nsa_cmp_w, gdn_conv_w, gdn_a_log, gdn_dt_bias, gdn_norm, w_out, norm_ffn, w_gate_up, w_down,
           norm_final):
    bp, tp, _ = x_prompt.shape
    bs, ts, _ = x_sample.shape
    depth = w_in.shape[0]
    n_pages = page_table.shape[1]
    past = n_pages * PAGE_SIZE
    assert depth == 1 and ts == 1, "one layer and one new token per sample row"
    assert tp % 512 == 0 and tp // SEL_LEN <= LANES and past % 512 == 0
    l = 0
    wts = _layer_weights(w_in[l], nsa_cmp_pe[l], nsa_cmp_w[l], gdn_conv_w[l], gdn_a_log[l], gdn_dt_bias[l],
                         gdn_norm[l], w_out[l])
    g_mix, g_ffn, g_fin = norm_mix[l][None, :], norm_ffn[l][None, :], norm_final[None, :]
    wgu, wd = w_gate_up[l].astype(BF16), w_down[l].astype(BF16)
    hist = CONV_W - 1

    xp = x_prompt.reshape(bp * tp, D_MODEL)
    q, kv, conv_in, z, small, kaug, vaug, kwin, vwaug, kv4t_p, kvwt_p = _proj(
        xp, g_mix, wts["wm"], wts["ws"], wts["wkvt"], *_rope_tables(jnp.arange(tp, dtype=jnp.int32)), tm=256)
    ckv = _compress(kv, wts["wlo"], wts["whi"], wts["pelo"], wts["pehi"], batch=bp, seq=tp)
    nc = tp // CMP_STRIDE
    mt_p = _cmp_to_sel_t(nc - 1, tp // SEL_LEN, LANES, nc)
    o_nsa = _nsa_prompt(q, small, kaug, vaug, kwin, vwaug, ckv, mt_p, batch=bp, seq=tp, kc=512)
    smallt = small[:, A_COL:A_COL + SUBLANES].reshape(bp, tp, SUBLANES).transpose(0, 2, 1)
    prep = _gdn_prep(conv_in, small, smallt, jnp.zeros((bp, hist, GDN_CONV_CH), F32), wts["cw"], wts["pcol"],
                     wts["prow"], batch=bp, seq=tp, tb=256)
    o_gdn, s_p = _gdn_scan(*prep, z, jnp.zeros((bp, GDN_HEADS, GDN_DK, GDN_DV), F32), wts["nw"],
                           batch=bp, seq=tp, tb=256)
    o_gdn = o_gdn.reshape(bp * tp, GDN_V_W)
    s_p = s_p.reshape(bp, GDN_HEADS, GDN_DK, GDN_DV)
    y_p = _mix_ffn(xp, o_nsa, o_gdn, wts["wo"], g_ffn, wgu, wd, g_fin, tm=256)
    tok_major = lambda a, comps: a.reshape(bp, comps, NSA_KV_HEADS, HEAD_DIM, -1).transpose(0, 4, 1, 2, 3)
    kv_p = tok_major(kv4t_p, 4)
    win_p = tok_major(kvwt_p[:, :, tp - min(WINDOW, tp):], 2)
    conv_p = conv_in.reshape(bp, tp, GDN_CONV_CH)[:, tp - hist:]

    xs = x_sample.reshape(bs, D_MODEL)
    q, kv, conv_in, z, small = _proj(xs, g_mix, wts["wm"], wts["ws"], wts["wkvt"],
                                     *_rope_tables(jnp.full((bs,), past, jnp.int32)), tm=bs)[:5]
    n_sel = past // SEL_LEN + 1
    nsp = -(-n_sel // LANES) * LANES
    mt_s = _cmp_to_sel_t(past // CMP_STRIDE - 1, n_sel, nsp, past // CMP_STRIDE)
    cache = cache_nsa_kv[l].transpose(0, 2, 3, 4, 1).reshape(-1, 4 * NSA_KV_W, PAGE_SIZE)
    win = cache_nsa_win[l].transpose(0, 2, 3, 4, 1).reshape(bs, 2 * NSA_KV_W, -1)
    flat = lambda w: w.reshape(-1, w.shape[-1])
    o_nsa = _nsa_sample(page_table, cache, q[:, None], kv[:, None], small[:, None], win, flat(wts["wlo"]),
                        flat(wts["whi"]), wts["pelo"], wts["pehi"], mt_s, pps=min(32, n_pages)).reshape(bs, NSA_Q_W)
    o_gdn, s_s = _gdn_step(conv_in, z, small, state_gdn_conv[l], state_gdn_S[l], wts["cw"], wts["pcol"], wts["nw"])
    o_gdn = o_gdn.reshape(bs, GDN_V_W)
    y_s = _mix_ffn(xs, o_nsa, o_gdn, wts["wo"], g_ffn, wgu, wd, g_fin, tm=bs)
    kv_s = kv.reshape(bs, 1, 3, 2, NSA_KV_HEADS, HEAD_DIM)
    win_s = jnp.concatenate([cache_nsa_win[l], kv_s[:, :, 2]], axis=1)[:, -min(WINDOW, past + 1):]
    conv_s = jnp.concatenate([state_gdn_conv[l], conv_in[:, None]], axis=1)[:, -hist:]

    return (y_p.reshape(bp, tp, D_MODEL), y_s.reshape(bs, 1, D_MODEL),
            kv_p[None], win_p[None], s_p[None], conv_p[None],
            kv_s[:, :, 0:2].reshape(bs, 1, 4, NSA_KV_HEADS, HEAD_DIM)[None], win_s[None], s_s[None], conv_s[None])
```

```python
import functools

import numpy as np
import jax
import jax.numpy as jnp
from jax import lax
from jax.experimental import pallas as pl
from jax.experimental.pallas import tpu as pltpu

F32 = jnp.float32
BF16 = jnp.bfloat16

D_MODEL = 1024
PAGE_SIZE = 128
HEAD_DIM = 64
NSA_HEADS = 8
NSA_KV_HEADS = 2
NSA_REP = NSA_HEADS // NSA_KV_HEADS
CMP_LEN = 32
CMP_STRIDE = 16
SEL_LEN = 64
SEL_TOPK = 16
WINDOW = 512
Q_BLOCK = 128
ROPE_THETA = 10000.0
GDN_HEADS = 4
GDN_DK = 128
GDN_DV = 128
CONV_W = 4
NSA_Q_W = NSA_HEADS * HEAD_DIM
NSA_KV_W = NSA_KV_HEADS * HEAD_DIM
GDN_QK_W = GDN_HEADS * GDN_DK
GDN_V_W = GDN_HEADS * GDN_DV
GDN_CONV_CH = 2 * GDN_QK_W + GDN_V_W
NEG = -1e30
LOG2E = 1.4426950408889634
SEL_BONUS = 1e4
EPS = 1e-6

LANES = 128
SUBLANES = 8
VMEM_LIMIT = 56 * 1024 * 1024

GATE_COLS = 3 * NSA_HEADS
A_COL = GATE_COLS
B_COL = GATE_COLS + GDN_HEADS

NT_DIMS = (((1,), (1,)), ((), ()))
TN_DIMS = (((0,), (0,)), ((), ()))


def _dot(a, b):
    return jnp.dot(a, b, preferred_element_type=F32)


def _dot_nt(a, b):
    return lax.dot_general(a, b, NT_DIMS, preferred_element_type=F32)


def _sigmoid(x):
    return 1.0 / (1.0 + jnp.exp(-x))


def _silu(x):
    return x * _sigmoid(x)


def _split3(x):
    p1 = x.astype(BF16)
    r1 = x - p1.astype(F32)
    p2 = r1.astype(BF16)
    p3 = (r1 - p2.astype(F32)).astype(BF16)
    return p1, p2, p3


def _const_spec(shape):
    nd = len(shape)
    return pl.BlockSpec(shape, lambda *_: (0,) * nd, pipeline_mode=pl.Buffered(1))


def _proj_kernel(x_ref, g_ref, wm_ref, ws_ref, wkvt_ref, cos_ref, sin_ref, cost_ref, sint_ref,
                 q_ref, kv_ref, conv_ref, z_ref, small_ref, kaug_ref, vaug_ref, kwin_ref, vwaug_ref,
                 kv4t_ref, kvwt_ref, *, tm, pos_rows):
    x = x_ref[...]
    ms = jnp.mean(x * x, axis=-1, keepdims=True)
    xn = (x * lax.rsqrt(ms + EPS) * g_ref[...]).astype(BF16)
    cos = cos_ref[...]
    sin = sin_ref[...]
    lane = lax.broadcasted_iota(jnp.int32, (tm, LANES), 1)
    low_half = (lane % HEAD_DIM) < (HEAD_DIM // 2)

    def rope(v):
        rot = jnp.where(low_half, pltpu.roll(v, LANES - HEAD_DIM // 2, 1), pltpu.roll(v, HEAD_DIM // 2, 1))
        return v * cos + rot * sin

    q = _dot(xn, wm_ref[:, 0:NSA_Q_W])
    for r in range(NSA_REP):
        sl = slice(r * LANES, (r + 1) * LANES)
        q_ref[:, sl] = (rope(q[:, sl]) * (HEAD_DIM ** -0.5 * LOG2E)).astype(BF16)

    kv = _dot(xn, wm_ref[:, NSA_Q_W:NSA_Q_W + 6 * NSA_KV_W])
    group0 = lane < HEAD_DIM
    for c in range(6):
        sl = slice(c * LANES, (c + 1) * LANES)
        blk = kv[:, sl]
        if c % 2 == 0:
            blk = rope(blk)
        kv_ref[:, sl] = blk
        if c == 2:
            kaug_ref[:, LANES:2 * LANES] = blk.astype(BF16)
        elif c == 4:
            kwin_ref[...] = blk.astype(BF16)
        elif c in (3, 5):
            v_ref = vaug_ref if c == 3 else vwaug_ref
            v_ref[:, 0:LANES] = jnp.where(group0, blk, 1.0).astype(BF16)
            v_ref[:, LANES:2 * LANES] = jnp.where(group0, 1.0, blk).astype(BF16)

    row0 = (pl.program_id(0) * tm) % pos_rows
    rows = row0 + lax.broadcasted_iota(jnp.int32, (tm, LANES), 0)
    kaug_ref[:, 0:LANES] = jnp.where(rows // SEL_LEN == lane, 1.0, 0.0).astype(BF16)

    kvt = _dot_nt(wkvt_ref[...], xn)
    cos_t = cost_ref[...]
    sin_t = sint_ref[...]
    half = HEAD_DIM // 2
    for c in range(6):
        blk = kvt[c * LANES:(c + 1) * LANES]
        if c % 2 == 0:
            parts = []
            for g in range(NSA_KV_HEADS):
                x1 = blk[g * HEAD_DIM:g * HEAD_DIM + half]
                x2 = blk[g * HEAD_DIM + half:(g + 1) * HEAD_DIM]
                parts += [x1 * cos_t - x2 * sin_t, x2 * cos_t + x1 * sin_t]
            blk = jnp.concatenate(parts, axis=0)
        dst = kv4t_ref if c < 4 else kvwt_ref
        dst[(c % 4) * LANES:(c % 4 + 1) * LANES, :] = blk

    c0 = NSA_Q_W + 6 * NSA_KV_W
    conv_ref[...] = _dot(xn, wm_ref[:, c0:c0 + GDN_CONV_CH])
    z_ref[...] = _dot(xn, wm_ref[:, c0 + GDN_CONV_CH:c0 + GDN_CONV_CH + GDN_V_W])
    small_ref[...] = _dot(xn, ws_ref[...])


def _proj(x, g, wm, ws, wkvt, cos, sin, cos_t, sin_t, *, tm):
    rows = x.shape[0]
    pos_rows = cos.shape[0]
    n_pos_blk = pos_rows // tm
    n_seq = rows // pos_rows
    grid = (rows // tm,)
    row_spec = lambda w: pl.BlockSpec((tm, w), lambda i: (i, 0))
    pos_spec = pl.BlockSpec((tm, LANES), lambda i: (i % n_pos_blk, 0))
    pos_t_spec = pl.BlockSpec((HEAD_DIM // 2, tm), lambda i: (0, i % n_pos_blk))
    tok_minor = lambda r: pl.BlockSpec((None, r, tm), lambda i: (i // n_pos_blk, 0, i % n_pos_blk))
    out_shape = (
        jax.ShapeDtypeStruct((rows, NSA_Q_W), BF16),
        jax.ShapeDtypeStruct((rows, 6 * NSA_KV_W), F32),
        jax.ShapeDtypeStruct((rows, GDN_CONV_CH), F32),
        jax.ShapeDtypeStruct((rows, GDN_V_W), F32),
        jax.ShapeDtypeStruct((rows, LANES), F32),
        jax.ShapeDtypeStruct((rows, 2 * LANES), BF16),
        jax.ShapeDtypeStruct((rows, 2 * LANES), BF16),
        jax.ShapeDtypeStruct((rows, LANES), BF16),
        jax.ShapeDtypeStruct((rows, 2 * LANES), BF16),
    )
    out_shape_t = (jax.ShapeDtypeStruct((n_seq, 4 * NSA_KV_W, pos_rows), F32),
                   jax.ShapeDtypeStruct((n_seq, 2 * NSA_KV_W, pos_rows), F32))
    return pl.pallas_call(
        functools.partial(_proj_kernel, tm=tm, pos_rows=pos_rows),
        grid=grid,
        in_specs=[row_spec(D_MODEL), _const_spec((1, D_MODEL)), _const_spec(wm.shape), _const_spec(ws.shape),
                  _const_spec(wkvt.shape), pos_spec, pos_spec, pos_t_spec, pos_t_spec],
        out_specs=tuple(row_spec(s.shape[1]) for s in out_shape) + (tok_minor(4 * NSA_KV_W), tok_minor(2 * NSA_KV_W)),
        out_shape=out_shape + out_shape_t,
        compiler_params=pltpu.CompilerParams(dimension_semantics=("arbitrary",), vmem_limit_bytes=VMEM_LIMIT),
        name="proj",
    )(x, g, wm, ws, wkvt, cos, sin, cos_t, sin_t)


def _mix_ffn_kernel(x_ref, on_ref, og_ref, wo_ref, gf_ref, wgu_ref, wd_ref, gl_ref, y_ref, *, d_ff):
    h = x_ref[...] + _dot(on_ref[...], wo_ref[0:NSA_Q_W, :]) + _dot(og_ref[...], wo_ref[NSA_Q_W:, :])
    ms = jnp.mean(h * h, axis=-1, keepdims=True)
    hn = (h * lax.rsqrt(ms + EPS) * gf_ref[...]).astype(BF16)
    gate = _dot(hn, wgu_ref[:, 0:d_ff])
    up = _dot(hn, wgu_ref[:, d_ff:])
    act = (_silu(gate) * up).astype(BF16)
    h = h + _dot(act, wd_ref[...])
    ms = jnp.mean(h * h, axis=-1, keepdims=True)
    y_ref[...] = h * lax.rsqrt(ms + EPS) * gl_ref[...]


def _mix_ffn(x, o_nsa, o_gdn, wo, gf, wgu, wd, gl, *, tm):
    rows = x.shape[0]
    d_ff = wd.shape[0]
    row_spec = lambda w: pl.BlockSpec((tm, w), lambda i: (i, 0))
    return pl.pallas_call(
        functools.partial(_mix_ffn_kernel, d_ff=d_ff),
        grid=(rows // tm,),
        in_specs=[row_spec(D_MODEL), row_spec(NSA_Q_W), row_spec(GDN_V_W), _const_spec(wo.shape),
                  _const_spec((1, D_MODEL)), _const_spec(wgu.shape), _const_spec(wd.shape),
                  _const_spec((1, D_MODEL))],
        out_specs=row_spec(D_MODEL),
        out_shape=jax.ShapeDtypeStruct((rows, D_MODEL), F32),
        compiler_params=pltpu.CompilerParams(dimension_semantics=("arbitrary",), vmem_limit_bytes=VMEM_LIMIT),
        name="mix_ffn",
    )(x, o_nsa, o_gdn, wo, gf, wgu, wd, gl)


def _compress_rows(load_rows, wlo_ref, whi_ref, pelo_ref, pehi_ref, nc):
    acc_lo = jnp.zeros((nc, 2 * LANES), F32)
    acc_hi = jnp.zeros((nc, 2 * LANES), F32)
    for l in range(CMP_STRIDE):
        x = load_rows(l)
        acc_lo = acc_lo + _dot((x + pelo_ref[l:l + 1, :]).astype(BF16), wlo_ref[l])
        acc_hi = acc_hi + _dot((x + pehi_ref[l:l + 1, :]).astype(BF16), whi_ref[l])
    nxt = pltpu.roll(acc_hi, nc - 1, 0)
    row = lax.broadcasted_iota(jnp.int32, (nc, 2 * LANES), 0)
    return jnp.where(row < nc - 1, acc_lo + nxt, 0.0)


def _compress_kernel(k_ref, v_ref, wlo_ref, whi_ref, pelo_ref, pehi_ref, ckv_ref, *, nc):
    load = lambda l: jnp.concatenate([k_ref[pl.ds(l, nc, stride=CMP_STRIDE), :],
                                      v_ref[pl.ds(l, nc, stride=CMP_STRIDE), :]], axis=1)
    ckv_ref[...] = _compress_rows(load, wlo_ref, whi_ref, pelo_ref, pehi_ref, nc).astype(BF16)


def _compress(kv, wlo, whi, pelo, pehi, *, batch, seq):
    nc = seq // CMP_STRIDE
    return pl.pallas_call(
        functools.partial(_compress_kernel, nc=nc),
        grid=(batch,),
        in_specs=[pl.BlockSpec((seq, LANES), lambda b: (b, 0)), pl.BlockSpec((seq, LANES), lambda b: (b, 1)),
                  _const_spec(wlo.shape), _const_spec(whi.shape), _const_spec(pelo.shape), _const_spec(pehi.shape)],
        out_specs=pl.BlockSpec((nc, 2 * LANES), lambda b: (b, 0)),
        out_shape=jax.ShapeDtypeStruct((batch * nc, 2 * LANES), BF16),
        compiler_params=pltpu.CompilerParams(dimension_semantics=("arbitrary",), vmem_limit_bytes=VMEM_LIMIT),
        name="compress",
    )(kv, kv, wlo, whi, pelo, pehi)


def _masked_softmax2(s):
    m = jnp.maximum(jnp.max(s, axis=-1, keepdims=True), NEG / 8)
    e = jnp.exp2(s - m)
    return e * (1.0 / jnp.maximum(jnp.sum(e, axis=-1, keepdims=True), 1e-30))


def _select_blocks(score, sel, blk, axis, rounds, filler=iter(())):
    n = score.shape[axis]
    for _ in range(rounds):
        mx = jnp.max(score, axis=axis, keepdims=True)
        idx = jnp.min(jnp.where(score == mx, blk, n), axis=axis, keepdims=True)
        hit = blk == idx
        sel = jnp.where(hit, jnp.where(mx > NEG / 2, 1.0, sel), sel)
        score = jnp.where(hit, -3e38, score)
        next(filler, None)
    for _ in filler:
        pass
    return sel


def _nsa_prompt_kernel(q_ref, small_ref, kaug_ref, vaug_ref, kwin_ref, vwaug_ref, ckv_ref, mt_ref, o_ref,
                       *, seq, kc):
    nc = seq // CMP_STRIDE
    qb = Q_BLOCK
    start = pl.program_id(1) * qb
    wlen = WINDOW + qb
    groups = range(NSA_KV_HEADS)
    stack = lambda x: jnp.concatenate([x] * NSA_REP, axis=0)

    lane = lax.broadcasted_iota(jnp.int32, (qb, LANES), 1)
    gates = _sigmoid(small_ref[...])

    qrow_c = lax.broadcasted_iota(jnp.int32, (qb, nc), 0)
    ncol = lax.broadcasted_iota(jnp.int32, (qb, nc), 1)
    cmp_bias = stack(jnp.where(ncol * CMP_STRIDE + (CMP_LEN - 1) - qrow_c <= start, 0.0, NEG))
    tail_bias = stack(jnp.where(lane <= lax.broadcasted_iota(jnp.int32, (qb, qb), 0), 0.0, NEG))
    wbase = pl.multiple_of(jnp.maximum(start - WINDOW, 0), qb)
    back = (lax.broadcasted_iota(jnp.int32, (qb, wlen), 0) - lax.broadcasted_iota(jnp.int32, (qb, wlen), 1)
            + (start - wbase))
    win_bias = stack(jnp.where(back >= 0, jnp.where(back <= WINDOW, 0.0, NEG), NEG))

    glanes = [slice(g * LANES, (g + 1) * LANES) for g in groups]
    qs = [jnp.concatenate([jnp.where((lane >= HEAD_DIM) == (g == 1), q_ref[:, r * LANES:(r + 1) * LANES], 0.0)
                           .astype(BF16) for r in range(NSA_REP)], axis=0) for g in groups]
    ps = [_masked_softmax2(_dot_nt(q, ckv_ref[:, 0:LANES]) + cmp_bias) for q in qs]
    psums = [p[0:qb] + p[qb:2 * qb] + p[2 * qb:3 * qb] + p[3 * qb:4 * qb] for p in ps]
    imp_t = jnp.concatenate([sum(_dot_nt(mt_ref[...], piece) for piece in _split3(psum)) for psum in psums], axis=1)
    o_cmps = [_dot(p.astype(BF16), ckv_ref[:, LANES:2 * LANES]) for p in ps]

    done = {}

    def independent_work():
        s = [_dot_nt(q, kwin_ref[pl.ds(wbase, wlen), :]) + win_bias for q in qs]
        yield
        mx = [jnp.max(x, axis=-1, keepdims=True) for x in s]
        yield
        e = []
        for g in groups:
            e.append(jnp.exp2(s[g] - mx[g]).astype(BF16))
            yield
        acc = [_dot(e[g], vwaug_ref[pl.ds(wbase, wlen), glanes[g]]) for g in groups]
        yield
        o_win = [a / pltpu.roll(a, HEAD_DIM, 1) for a in acc]
        yield
        rows_of = lambda x, r: x[r * qb:(r + 1) * qb]
        done["partial"] = []
        for g in groups:
            done["partial"].append(
                [gates[:, 3 * (g * NSA_REP + r):3 * (g * NSA_REP + r) + 1] * rows_of(o_cmps[g], r)
                 + gates[:, 3 * (g * NSA_REP + r) + 2:3 * (g * NSA_REP + r) + 3] * rows_of(o_win[g], r)
                 for r in range(NSA_REP)])
            yield
        s_tail = [_dot_nt(q, kaug_ref[pl.ds(start, qb), LANES:2 * LANES]) + tail_bias for q in qs]
        yield
        done["m0"] = [jnp.max(x, axis=-1, keepdims=True) for x in s_tail]
        yield
        done["acc0"] = [_dot(jnp.exp2(s_tail[g] - done["m0"][g]).astype(BF16), vaug_ref[pl.ds(start, qb), glanes[g]])
                        for g in groups]
        yield

    blk_t = lax.broadcasted_iota(jnp.int32, (LANES, 2 * qb), 0)
    qpos_t = start + lax.broadcasted_iota(jnp.int32, (LANES, 2 * qb), 1) % qb
    cur_t = qpos_t // SEL_LEN
    visible = blk_t * SEL_LEN <= qpos_t
    forced = (blk_t == 0) | (blk_t == cur_t) | (blk_t == cur_t - 1)
    sel_t = _select_blocks(jnp.where(visible, jnp.where(forced, NEG, imp_t), NEG),
                           jnp.where(visible, jnp.where(forced, 1.0, 0.0), 0.0), blk_t, 0, SEL_TOPK - 3,
                           independent_work())
    partial, m0s, acc0s = done["partial"], done["m0"], done["acc0"]
    bias_t = jnp.where(blk_t * SEL_LEN < start, jnp.where(sel_t > 0.5, 0.0, NEG), NEG).T.astype(BF16)
    n_chunks = (start + kc - 1) // kc
    q_augs = [jnp.concatenate([stack(bias_t[g * qb:(g + 1) * qb]), qs[g]], axis=1) for g in groups]

    def sel_step(c, carry):
        off = pl.multiple_of(c * kc, kc)
        k = kaug_ref[pl.ds(off, kc), :]
        out = []
        for g in groups:
            m, acc = carry[g]
            s = _dot_nt(q_augs[g], k)
            m_new = jnp.maximum(m, jnp.max(s, axis=-1, keepdims=True))
            e = jnp.exp2(s - m_new).astype(BF16)
            out.append((m_new, jnp.exp2(m - m_new) * acc + _dot(e, vaug_ref[pl.ds(off, kc), g * LANES:(g + 1) * LANES])))
        return tuple(out)

    carry = lax.fori_loop(0, n_chunks // 2, lambda c, x: sel_step(2 * c + 1, sel_step(2 * c, x)),
                          tuple(zip(m0s, acc0s)))
    sel_out = lax.cond(n_chunks % 2 == 1, lambda x: sel_step(n_chunks - 1, x), lambda x: x, carry)

    outs = [None] * NSA_REP
    for g in groups:
        acc = sel_out[g][1]
        o_sel = acc / pltpu.roll(acc, HEAD_DIM, 1)
        for r in range(NSA_REP):
            c0 = (g * NSA_REP + r) * 3
            o = partial[g][r] + gates[:, c0 + 1:c0 + 2] * o_sel[r * qb:(r + 1) * qb]
            outs[r] = o if g == 0 else jnp.where(lane < HEAD_DIM, outs[r], o)

    for r in range(NSA_REP):
        o_ref[:, r * LANES:(r + 1) * LANES] = outs[r].astype(BF16)


def _nsa_prompt(q, small, kaug, vaug, kwin, vwaug, ckv, mt, *, batch, seq, kc):
    nb = seq // Q_BLOCK
    nc = seq // CMP_STRIDE
    blk_spec = lambda w: pl.BlockSpec((Q_BLOCK, w), lambda b, i: (b * nb + i, 0))
    seq_spec = lambda w: pl.BlockSpec((seq, w), lambda b, i: (b, 0))
    return pl.pallas_call(
        functools.partial(_nsa_prompt_kernel, seq=seq, kc=kc),
        grid=(batch, nb),
        in_specs=[blk_spec(NSA_Q_W), blk_spec(LANES), seq_spec(2 * LANES), seq_spec(2 * LANES), seq_spec(LANES),
                  seq_spec(2 * LANES), pl.BlockSpec((nc, 2 * LANES), lambda b, i: (b, 0)), _const_spec(mt.shape)],
        out_specs=blk_spec(NSA_Q_W),
        out_shape=jax.ShapeDtypeStruct((batch * seq, NSA_Q_W), BF16),
        compiler_params=pltpu.CompilerParams(dimension_semantics=("arbitrary", "arbitrary"),
                                             vmem_limit_bytes=VMEM_LIMIT),
        name="nsa_prompt",
    )(q, small, kaug, vaug, kwin, vwaug, ckv, mt)


def _nsa_sample_kernel(pt_ref, *refs, past, pps):
    del pt_ref
    pages = refs[:pps]
    (q_ref, kvn_ref, small_ref, win_ref, perm_ref, wk_ref, wv_ref, pelo_ref, pehi_ref, mt_ref, exp_ref,
     o_ref, xs_ref, s_ref, vs_ref) = refs[pps:]
    step = pl.program_id(1)
    n_pages = past // PAGE_SIZE
    cpp = PAGE_SIZE // CMP_STRIDE
    heads = NSA_HEADS
    row = lax.broadcasted_iota(jnp.int32, (heads, LANES), 0)
    lane = lax.broadcasted_iota(jnp.int32, (heads, LANES), 1)
    mine = (lane >= HEAD_DIM) == (row >= NSA_REP)
    rr = row % NSA_REP
    qsel = jnp.zeros((heads, LANES), F32)
    for r in range(NSA_REP):
        qsel = jnp.where(rr == r, q_ref[:, r * LANES:(r + 1) * LANES].astype(F32), qsel)
    qf = jnp.where(mine, qsel, 0.0)
    q = qf.astype(BF16)

    nc = past // CMP_STRIDE

    def take_pages(first_page):
        for k in range(pps):
            pg = first_page + k
            page = pages[k]
            tok = slice(pg * PAGE_SIZE, (pg + 1) * PAGE_SIZE)
            s_ref[:, tok] = _dot(q, page[2 * LANES:3 * LANES, :].astype(BF16))
            vs_ref[:, tok] = page[3 * LANES:4 * LANES, :].astype(BF16)
            x = _dot_nt(perm_ref[...], page[0:2 * LANES, :].astype(BF16))
            for l in range(CMP_STRIDE):
                for c in range(2):
                    xs_ref[c, pg * cpp:(pg + 1) * cpp, l * LANES:(l + 1) * LANES] = (
                        x[l * cpp:(l + 1) * cpp, c * LANES:(c + 1) * LANES])

    for st in range(n_pages // pps):
        pl.when(step == st)(functools.partial(take_pages, st * pps))

    @pl.when(step == pl.num_programs(1) - 1)
    def _():
        n_cmp = nc - 1
        nsp = mt_ref.shape[0]
        cur = past // SEL_LEN

        crow = lax.broadcasted_iota(jnp.int32, (nc, LANES), 0)
        ckv = []
        for c, w_ref in enumerate((wk_ref, wv_ref)):
            cl = slice(c * LANES, (c + 1) * LANES)
            for l in range(CMP_STRIDE):
                ll = slice(l * LANES, (l + 1) * LANES)
                xs_ref[c, nc:nc + SUBLANES, ll] = jnp.broadcast_to(pelo_ref[l:l + 1, cl], (SUBLANES, LANES))
                xs_ref[c, nc + SUBLANES:nc + 2 * SUBLANES, ll] = jnp.broadcast_to(pehi_ref[l:l + 1, cl],
                                                                                    (SUBLANES, LANES))
            acc = _dot(xs_ref[c].astype(BF16), w_ref[...])
            bias = acc[nc:nc + 1, 0:LANES] + acc[nc + SUBLANES:nc + SUBLANES + 1, LANES:2 * LANES]
            nxt = pltpu.roll(acc[0:nc, LANES:2 * LANES], nc - 1, 0)
            ckv.append(jnp.where(crow < nc - 1, acc[0:nc, 0:LANES] + nxt + bias, 0.0).astype(BF16))
        ckv = jnp.concatenate(ckv, axis=1)

        ncol = lax.broadcasted_iota(jnp.int32, (1, nc), 1)
        cmp_valid = (ncol * CMP_STRIDE + CMP_LEN - 1 <= past) & (ncol < n_cmp)
        p = _masked_softmax2(jnp.where(cmp_valid, _dot_nt(q, ckv[:, 0:LANES]), NEG))
        o_cmp = _dot(p.astype(BF16), ckv[:, LANES:2 * LANES])

        prow = lax.broadcasted_iota(jnp.int32, (heads, nc), 0)
        ps = [jnp.sum(jnp.where(prow // NSA_REP == g, p, 0.0), axis=0, keepdims=True) for g in range(NSA_KV_HEADS)]
        psum = jnp.where(prow < NSA_REP, ps[0], ps[1])
        imp = sum(_dot_nt(piece, mt_ref[...]) for piece in _split3(psum))
        blk = lax.broadcasted_iota(jnp.int32, (heads, nsp), 1)
        forced = (blk == 0) | (blk == cur) | (blk == cur - 1)
        score = jnp.where(blk * SEL_LEN <= past, imp + jnp.where(forced, SEL_BONUS, 0.0), NEG)
        score_t = score.T
        bi = lax.broadcasted_iota(jnp.int32, (nsp, nsp), 0)
        bj = lax.broadcasted_iota(jnp.int32, (nsp, nsp), 1)
        tie = jnp.where(bi < bj, 1.0, 0.0)
        picks = []
        for g in range(NSA_KV_HEADS):
            c = g * NSA_REP
            s_i, s_j = score_t[:, c:c + 1], score[c:c + 1, :]
            beats = jnp.where(s_i > s_j, 1.0, jnp.where(s_i == s_j, tie, 0.0))
            rank = jnp.sum(beats, axis=0, keepdims=True)
            picks.append(jnp.where(rank < SEL_TOPK, jnp.where(s_j > NEG / 2, 1.0, 0.0), 0.0))
        sel8 = jnp.where(lax.broadcasted_iota(jnp.int32, (heads, nsp), 0) < NSA_REP, picks[0], picks[1])
        keep = _dot(sel8.astype(BF16), exp_ref[...]) > 0.5

        kvn = kvn_ref[...]
        new_ok = sel8[:, cur:cur + 1] > 0.5
        s = jnp.where(keep, s_ref[...], NEG)
        s_new = jnp.where(new_ok, jnp.sum(qf * kvn[:, 2 * LANES:3 * LANES], axis=-1, keepdims=True), NEG)
        m = jnp.maximum(jnp.max(s, axis=-1, keepdims=True), s_new)
        e = jnp.exp2(s - m)
        e_new = jnp.where(new_ok, jnp.exp2(s_new - m), 0.0)
        o_sel = ((_dot_nt(e.astype(BF16), vs_ref[...]) + e_new * kvn[:, 3 * LANES:4 * LANES])
                 / (jnp.sum(e, axis=-1, keepdims=True) + e_new))

        nw = win_ref.shape[0] // 2
        s = _dot(q, win_ref[0:nw, :].astype(BF16))
        s_new = jnp.sum(qf * kvn[:, 4 * LANES:5 * LANES], axis=-1, keepdims=True)
        m = jnp.maximum(jnp.max(s, axis=-1, keepdims=True), s_new)
        e = jnp.exp2(s - m)
        e_new = jnp.exp2(s_new - m)
        l = jnp.sum(e, axis=-1, keepdims=True) + e_new
        o_win = (_dot_nt(e.astype(BF16), win_ref[nw:, :].astype(BF16))
                 + e_new * kvn[:, 5 * LANES:6 * LANES]) / l

        gates = _sigmoid(small_ref[...])
        gate = lambda br: jnp.sum(jnp.where(lane == row * 3 + br, gates, 0.0), axis=-1, keepdims=True)
        o = gate(0) * o_cmp + gate(1) * o_sel + gate(2) * o_win
        for r in range(NSA_REP):
            o_ref[:, r * LANES:(r + 1) * LANES] = jnp.where(
                lane[0:1] < HEAD_DIM, o[r:r + 1], o[NSA_REP + r:NSA_REP + r + 1]).astype(BF16)


def _nsa_sample(page_table, cache, q, kvn, small, win, wlo, whi, pelo, pehi, mt, *, pps):
    batch, n_pages = page_table.shape
    past = n_pages * PAGE_SIZE
    rows = cache.shape[1]
    cpp = PAGE_SIZE // CMP_STRIDE
    nsp = mt.shape[0]
    expand = jnp.asarray(np.arange(nsp)[:, None] == np.arange(past)[None, :] // SEL_LEN, BF16)
    tok = np.arange(PAGE_SIZE)
    perm = jnp.asarray((tok[None, :] % CMP_STRIDE) * cpp + tok[None, :] // CMP_STRIDE == tok[:, None], BF16)

    def page_spec(k):
        return pl.BlockSpec((None, rows, PAGE_SIZE), lambda b, s, pt: (pt[b * n_pages + s * pps + k], 0, 0))

    per_b = lambda shape: pl.BlockSpec((None,) + shape, lambda b, s, pt: (b, 0, 0))
    const = lambda a: pl.BlockSpec(a.shape, lambda b, s, pt: (0,) * a.ndim, pipeline_mode=pl.Buffered(1))
    grid_spec = pltpu.PrefetchScalarGridSpec(
        num_scalar_prefetch=1,
        grid=(batch, n_pages // pps),
        in_specs=[page_spec(k) for k in range(pps)] + [
            per_b((1, NSA_Q_W)), per_b((1, 6 * NSA_KV_W)), per_b((1, LANES)), per_b(win.shape[1:]),
            const(perm), const(wlo), const(whi), const(pelo), const(pehi), const(mt), const(expand)],
        out_specs=per_b((1, NSA_Q_W)),
        scratch_shapes=[pltpu.VMEM((2, past // CMP_STRIDE + 2 * SUBLANES, CMP_STRIDE * LANES), F32),
                        pltpu.VMEM((NSA_HEADS, past), F32),
                        pltpu.VMEM((LANES, past), BF16)],
    )
    return pl.pallas_call(
        functools.partial(_nsa_sample_kernel, past=past, pps=pps),
        grid_spec=grid_spec,
        out_shape=jax.ShapeDtypeStruct((batch, 1, NSA_Q_W), BF16),
        compiler_params=pltpu.CompilerParams(dimension_semantics=("arbitrary", "arbitrary"),
                                             vmem_limit_bytes=VMEM_LIMIT),
        name="nsa_sample",
    )(page_table.reshape(-1), *([cache] * pps), q, kvn, small, win, perm, wlo, whi, pelo, pehi, mt, expand)


def _l2norm(x):
    return x * lax.rsqrt(jnp.sum(x * x, axis=-1, keepdims=True) + EPS)


def _softplus(x):
    return jnp.maximum(x, 0.0) + jnp.log(1.0 + jnp.exp(-jnp.abs(x)))


def _gdn_step_kernel(x_ref, z_ref, small_ref, buf_ref, s0_ref, cw_ref, pcol_ref, nw_ref, o_ref, sout_ref):
    hist = CONV_W - 1
    conv = jnp.sum(buf_ref[...] * cw_ref[0:hist, :], axis=0, keepdims=True) + x_ref[...] * cw_ref[hist:CONV_W, :]
    act = _silu(conv)
    small = small_ref[...]
    g_all = -jnp.exp(pcol_ref[0:1, :]) * _softplus(small + pcol_ref[1:2, :])
    beta_all = _sigmoid(small)
    row = lax.broadcasted_iota(jnp.int32, (SUBLANES, GDN_DK), 0)
    for h in range(GDN_HEADS):
        hs = slice(h * GDN_DK, (h + 1) * GDN_DK)
        qh = _l2norm(act[:, hs]) * (GDN_DK ** -0.5)
        kh = _l2norm(act[:, GDN_QK_W + h * GDN_DK:GDN_QK_W + (h + 1) * GDN_DK])
        vh = act[:, 2 * GDN_QK_W + h * GDN_DV:2 * GDN_QK_W + (h + 1) * GDN_DV]
        eg = jnp.exp(g_all[:, A_COL + h:A_COL + h + 1])
        bt = beta_all[:, B_COL + h:B_COL + h + 1]
        s = s0_ref[h]
        kq = jnp.where(row == 0, kh, jnp.where(row == 1, qh, 0.0)).astype(BF16)
        ks_qs = _dot(kq, s.astype(BF16))
        vn = bt * (vh - eg * ks_qs[0:1])
        o = eg * ks_qs[1:2] + jnp.sum(qh * kh, axis=-1, keepdims=True) * vn
        k8 = jnp.where(row == 0, kh, 0.0).astype(BF16)
        vn8 = jnp.where(row == 0, vn, 0.0).astype(BF16)
        sout_ref[h] = s * eg + lax.dot_general(k8, vn8, TN_DIMS, preferred_element_type=F32)
        on = o * lax.rsqrt(jnp.mean(o * o, axis=-1, keepdims=True) + EPS) * nw_ref[...]
        o_ref[:, hs] = (on * _silu(z_ref[:, hs])).astype(BF16)


def _gdn_step(x, z, small, buf, s0, cw, pcol, nw):
    batch = x.shape[0]
    per_b = lambda shape: pl.BlockSpec((None,) + shape, lambda b: (b,) + (0,) * len(shape))
    state = (GDN_HEADS, GDN_DK, GDN_DV)
    return pl.pallas_call(
        _gdn_step_kernel,
        grid=(batch,),
        in_specs=[per_b((1, GDN_CONV_CH)), per_b((1, GDN_V_W)), per_b((1, LANES)), per_b((CONV_W - 1, GDN_CONV_CH)),
                  per_b(state), _const_spec(cw.shape), _const_spec(pcol.shape), _const_spec(nw.shape)],
        out_specs=(per_b((1, GDN_V_W)), per_b(state)),
        out_shape=(jax.ShapeDtypeStruct((batch, 1, GDN_V_W), BF16),
                   jax.ShapeDtypeStruct((batch,) + state, F32)),
        compiler_params=pltpu.CompilerParams(dimension_semantics=("arbitrary",), vmem_limit_bytes=VMEM_LIMIT),
        name="gdn_step",
    )(x[:, None], z[:, None], small[:, None], buf, s0, cw, pcol, nw)


GDN_CHUNK = 64
GDN_STACK = GDN_HEADS * GDN_CHUNK


def _stack_heads(x, col0, width):
    return jnp.concatenate([x[:, col0 + h * width:col0 + (h + 1) * width] for h in range(GDN_HEADS)], axis=0)


def _gdn_prep_kernel(conv_ref, small_ref, smallt_ref, buf_ref, cw_ref, pcol_ref, prow_ref, tril_ref, triu_ref,
                     u_ref, w_ref, qg_ref, kgt_ref, aqk_ref, gl_ref, xs_ref, *, tb):
    j = pl.program_id(1)
    pad = SUBLANES
    hist = CONV_W - 1
    ck = GDN_CHUNK
    st = GDN_STACK

    @pl.when(j == 0)
    def _():
        xs_ref[0:pad, :] = jnp.zeros((pad, GDN_CONV_CH), F32)
        xs_ref[pad - hist:pad, :] = buf_ref[...]

    xs_ref[pad:pad + tb, :] = conv_ref[...]
    xs = xs_ref[...]
    conv = xs[pad:] * cw_ref[hist:CONV_W, :]
    for t in range(hist):
        conv = conv + pltpu.roll(xs, hist - t, 0)[pad:] * cw_ref[t:t + 1, :]
    xs_ref[0:pad, :] = xs[tb:tb + pad]
    act = _silu(conv)

    small = small_ref[...]
    g_col = -jnp.exp(pcol_ref[0:1, :]) * _softplus(small + pcol_ref[1:2, :])
    beta = _sigmoid(small)
    g_row = -jnp.exp(prow_ref[:, 0:1]) * _softplus(smallt_ref[...] + prow_ref[:, 1:2])
    gcum_col = sum(_dot(tril_ref[...], piece) for piece in _split3(g_col))
    gcum_row = sum(_dot(piece, triu_ref[...]) for piece in _split3(g_row))

    ii = lax.broadcasted_iota(jnp.int32, (st, st), 0)
    jj = lax.broadcasted_iota(jnp.int32, (st, st), 1)
    same_head = (ii // ck) == (jj // ck)
    incl = same_head & (ii >= jj)
    strict = same_head & (ii > jj)
    eye = jnp.where(ii == jj, 1.0, 0.0)
    hrow = lax.broadcasted_iota(jnp.int32, (SUBLANES, LANES), 0)

    lms, rhss = [], []
    for ci in range(tb // ck):
        r0 = ci * ck
        rs = slice(r0, r0 + ck)
        gc = _stack_heads(gcum_col[rs], A_COL, 1)
        bt = _stack_heads(beta[rs], B_COL, 1)
        gr = jnp.concatenate([gcum_row[h:h + 1, r0:r0 + ck] for h in range(GDN_HEADS)], axis=1)
        glast = [gcum_col[r0 + ck - 1:r0 + ck, A_COL + h:A_COL + h + 1] for h in range(GDN_HEADS)]
        gl_stack = jnp.concatenate([jnp.broadcast_to(x, (ck, 1)) for x in glast], axis=0)
        dec = jnp.where(incl, jnp.exp(jnp.where(incl, gc - gr, 0.0)), 0.0)
        qs = jnp.concatenate([_l2norm(act[rs, h * GDN_DK:(h + 1) * GDN_DK]) for h in range(GDN_HEADS)],
                             axis=0) * (GDN_DK ** -0.5)
        ks = jnp.concatenate([_l2norm(act[rs, GDN_QK_W + h * GDN_DK:GDN_QK_W + (h + 1) * GDN_DK])
                              for h in range(GDN_HEADS)], axis=0)
        vs = _stack_heads(act[rs], 2 * GDN_QK_W, GDN_DV)
        kb = ks * bt
        k16 = ks.astype(BF16)
        lms.append(jnp.where(strict, _dot_nt(kb.astype(BF16), k16) * dec, 0.0))
        eg = jnp.exp(gc)
        rhss.append(jnp.concatenate([vs * bt, kb * eg], axis=1).astype(BF16))
        orow = slice(ci * st, (ci + 1) * st)
        qg_ref[orow, :] = (qs * eg).astype(BF16)
        aqk_ref[orow, :] = (_dot_nt(qs.astype(BF16), k16) * dec).astype(BF16)
        kgt_ref[ci * GDN_DK:(ci + 1) * GDN_DK, :] = (ks * jnp.exp(gl_stack - gc)).T.astype(BF16)
        gl = jnp.zeros((SUBLANES, LANES), F32)
        for h in range(GDN_HEADS):
            gl = jnp.where(hrow == h, jnp.exp(glast[h]), gl)
        gl_ref[ci * SUBLANES:(ci + 1) * SUBLANES, :] = gl

    ainvs = [eye - lm for lm in lms]
    pws = lms
    n = 2
    while n < ck:
        pw16s = [pw.astype(BF16) for pw in pws]
        pws = [_dot(pw16, pw16) for pw16 in pw16s]
        ainvs = [ainv + _dot(ainv.astype(BF16), pw.astype(BF16)) for ainv, pw in zip(ainvs, pws)]
        n *= 2
    for ci, (ainv, rhs) in enumerate(zip(ainvs, rhss)):
        sol = _dot(ainv.astype(BF16), rhs)
        orow = slice(ci * st, (ci + 1) * st)
        u_ref[orow, :] = sol[:, 0:GDN_DV]
        w_ref[orow, :] = sol[:, GDN_DV:].astype(BF16)


def _gdn_prep(conv_in, small, smallt, buf, cw, pcol, prow, *, batch, seq, tb):
    nblk = seq // tb
    ncb = tb // GDN_CHUNK
    blk = np.arange(tb)
    same = (blk[:, None] // GDN_CHUNK) == (blk[None, :] // GDN_CHUNK)
    tril = jnp.asarray(same & (blk[:, None] >= blk[None, :]), BF16)
    triu = jnp.asarray(same & (blk[:, None] <= blk[None, :]), BF16)
    row_spec = lambda r, w: pl.BlockSpec((r, w), lambda b, j: (b * nblk + j, 0))
    per_b = lambda shape: pl.BlockSpec((None,) + shape, lambda b, j: (b,) + (0,) * len(shape))
    n_chunks = batch * seq // GDN_CHUNK
    out_shape = (jax.ShapeDtypeStruct((n_chunks * GDN_STACK, GDN_DV), F32),
                 jax.ShapeDtypeStruct((n_chunks * GDN_STACK, GDN_DK), BF16),
                 jax.ShapeDtypeStruct((n_chunks * GDN_STACK, GDN_DK), BF16),
                 jax.ShapeDtypeStruct((n_chunks * GDN_DK, GDN_STACK), BF16),
                 jax.ShapeDtypeStruct((n_chunks * GDN_STACK, GDN_STACK), BF16),
                 jax.ShapeDtypeStruct((n_chunks * SUBLANES, LANES), F32))
    out_specs = (row_spec(ncb * GDN_STACK, GDN_DV), row_spec(ncb * GDN_STACK, GDN_DK),
                 row_spec(ncb * GDN_STACK, GDN_DK), row_spec(ncb * GDN_DK, GDN_STACK),
                 row_spec(ncb * GDN_STACK, GDN_STACK), row_spec(ncb * SUBLANES, LANES))
    return pl.pallas_call(
        functools.partial(_gdn_prep_kernel, tb=tb),
        grid=(batch, nblk),
        in_specs=[row_spec(tb, GDN_CONV_CH), row_spec(tb, LANES),
                  pl.BlockSpec((None, SUBLANES, tb), lambda b, j: (b, 0, j)),
                  per_b((CONV_W - 1, GDN_CONV_CH)),
                  _const_spec(cw.shape), _const_spec(pcol.shape), _const_spec(prow.shape),
                  _const_spec(tril.shape), _const_spec(triu.shape)],
        out_specs=out_specs,
        out_shape=out_shape,
        scratch_shapes=[pltpu.VMEM((SUBLANES + tb, GDN_CONV_CH), F32)],
        compiler_params=pltpu.CompilerParams(dimension_semantics=("arbitrary", "arbitrary"),
                                             vmem_limit_bytes=VMEM_LIMIT),
        name="gdn_prep",
    )(conv_in, small, smallt, buf, cw, pcol, prow, tril, triu)


def _gdn_scan_kernel(u_ref, w_ref, qg_ref, kgt_ref, aqk_ref, gl_ref, z_ref, s0_ref, nw_ref, o_ref, sout_ref, s_ref,
                     *, batch, ncb):
    ck = GDN_CHUNK
    st = GDN_STACK
    sw = GDN_HEADS * GDN_DK

    @pl.when(pl.program_id(0) == 0)
    def _():
        s_ref[...] = s0_ref[...]

    wide_mask = (lax.broadcasted_iota(jnp.int32, (st, sw), 0) // ck
                 == lax.broadcasted_iota(jnp.int32, (st, sw), 1) // GDN_DK)
    tall_mask = (lax.broadcasted_iota(jnp.int32, (sw, st), 0) // GDN_DK
                 == lax.broadcasted_iota(jnp.int32, (sw, st), 1) // ck)

    seqs = range(batch)
    s = [s_ref[b] for b in seqs]
    for ci in range(ncb):
        rows = slice(ci * st, (ci + 1) * st)
        toks = slice(ci * ck, (ci + 1) * ck)
        lhs, kgt_bd, gl_rows = [], [], []
        for b in seqs:
            w_bd = jnp.where(wide_mask, jnp.concatenate([w_ref[b, rows, :]] * GDN_HEADS, axis=1), 0.0)
            qg_bd = jnp.where(wide_mask, jnp.concatenate([qg_ref[b, rows, :]] * GDN_HEADS, axis=1), 0.0)
            lhs.append(jnp.concatenate([w_bd, qg_bd], axis=0))
            kgt = kgt_ref[b, ci * GDN_DK:(ci + 1) * GDN_DK, :]
            kgt_bd.append(jnp.where(tall_mask, jnp.concatenate([kgt] * GDN_HEADS, axis=0), 0.0))
            gl = gl_ref[b, ci * SUBLANES:(ci + 1) * SUBLANES, :]
            gl_rows.append(jnp.concatenate(
                [jnp.broadcast_to(gl[h:h + 1], (GDN_DK, GDN_DV)) for h in range(GDN_HEADS)], axis=0))
        t1 = [_dot(lhs[b], s[b].astype(BF16)) for b in seqs]
        vn16 = [(u_ref[b, rows, :] - t1[b][0:st]).astype(BF16) for b in seqs]
        o = [t1[b][st:] + _dot(aqk_ref[b, rows, :], vn16[b]) for b in seqs]
        s = [s[b] * gl_rows[b] + _dot(kgt_bd[b], vn16[b]) for b in seqs]
        for b in seqs:
            for h in range(GDN_HEADS):
                oh = o[b][h * ck:(h + 1) * ck]
                on = oh * lax.rsqrt(jnp.mean(oh * oh, axis=-1, keepdims=True) + EPS) * nw_ref[...]
                hs = slice(h * GDN_DV, (h + 1) * GDN_DV)
                o_ref[b, toks, hs] = (on * _silu(z_ref[b, toks, hs])).astype(BF16)

    for b in seqs:
        s_ref[b] = s[b]
    sout_ref[...] = s_ref[...]


def _gdn_scan(u, w, qg, kgt, aqk, gl, z, s0, nw, *, batch, seq, tb):
    nblk = seq // tb
    ncb = tb // GDN_CHUNK
    cps = seq // GDN_CHUNK
    sw = GDN_HEADS * GDN_DK
    r3 = lambda a, rows_per_chunk: a.reshape(batch, cps * rows_per_chunk, a.shape[-1])
    blk = lambda rows, width: pl.BlockSpec((batch, rows, width), lambda j: (0, j, 0))
    full = pl.BlockSpec((batch, sw, GDN_DV), lambda j: (0, 0, 0))
    return pl.pallas_call(
        functools.partial(_gdn_scan_kernel, batch=batch, ncb=ncb),
        grid=(nblk,),
        in_specs=[blk(ncb * GDN_STACK, GDN_DV), blk(ncb * GDN_STACK, GDN_DK), blk(ncb * GDN_STACK, GDN_DK),
                  blk(ncb * GDN_DK, GDN_STACK), blk(ncb * GDN_STACK, GDN_STACK), blk(ncb * SUBLANES, LANES),
                  blk(tb, GDN_V_W), full, _const_spec(nw.shape)],
        out_specs=(blk(tb, GDN_V_W), full),
        out_shape=(jax.ShapeDtypeStruct((batch, seq, GDN_V_W), BF16),
                   jax.ShapeDtypeStruct((batch, sw, GDN_DV), F32)),
        scratch_shapes=[pltpu.VMEM((batch, sw, GDN_DV), F32)],
        compiler_params=pltpu.CompilerParams(dimension_semantics=("arbitrary",), vmem_limit_bytes=VMEM_LIMIT),
        name="gdn_scan",
    )(r3(u, GDN_STACK), r3(w, GDN_STACK), r3(qg, GDN_STACK), r3(kgt, GDN_DK), r3(aqk, GDN_STACK),
      r3(gl, SUBLANES), z.reshape(batch, seq, GDN_V_W), s0.reshape(batch, sw, GDN_DV), nw)


def _rope_tables(pos):
    half = HEAD_DIM // 2
    inv = np.power(ROPE_THETA, -np.arange(half, dtype=np.float64) * 2.0 / HEAD_DIM)
    ang = np.asarray(pos, np.float64)[:, None] * inv[None, :]
    cos, sin = np.cos(ang).astype(np.float32), np.sin(ang).astype(np.float32)
    return tuple(jnp.asarray(t) for t in (np.tile(cos, (1, 4)), np.concatenate([-sin, sin, -sin, sin], axis=1),
                                          cos.T, sin.T))


def _cmp_to_sel_t(n_cmp, n_sel, rows, cols):
    cs = np.arange(cols)[None, :] * CMP_STRIDE
    ss = np.arange(rows)[:, None] * SEL_LEN
    hit = (cs < ss + SEL_LEN) & (cs + CMP_LEN > ss)
    hit &= (np.arange(cols)[None, :] < n_cmp) & (np.arange(rows)[:, None] < n_sel)
    return jnp.asarray(hit, BF16)


def _layer_weights(w_in, cmp_pe, cmp_w, conv_w, a_log, dt_bias, gdn_norm, w_out):
    cuts = np.cumsum([NSA_Q_W, 6 * NSA_KV_W, GATE_COLS, GDN_CONV_CH, GDN_V_W, GDN_HEADS]).tolist()
    wq, wkv, wgt, wconv, wz, wa, wb = jnp.split(w_in, cuts, axis=1)
    order = np.array([g * NSA_REP + r for r in range(NSA_REP) for g in range(NSA_KV_HEADS)])
    cols = (order[:, None] * HEAD_DIM + np.arange(HEAD_DIM)[None, :]).reshape(-1)
    wm = jnp.concatenate([wq[:, cols], wkv, wconv, wz], axis=1).astype(BF16)
    wkvt = wkv.T.astype(BF16)
    ws = jnp.concatenate([wgt, wa, wb], axis=1)
    ws = jnp.pad(ws, ((0, 0), (0, LANES - ws.shape[1]))).astype(BF16)
    wo = jnp.concatenate([w_out[:NSA_Q_W][cols], w_out[NSA_Q_W:]], axis=0).astype(BF16)

    def blockdiag(l0):
        wk, wv = cmp_w[0, l0:l0 + CMP_STRIDE], cmp_w[1, l0:l0 + CMP_STRIDE]
        z = jnp.zeros_like(wk)
        rows = [jnp.concatenate(r, axis=2) for r in ([wk, z, z, z], [z, wk, z, z], [z, z, wv, z], [z, z, z, wv])]
        return jnp.concatenate(rows, axis=1).astype(BF16)

    def long_contraction(c):
        halves = []
        for l0 in (0, CMP_STRIDE):
            w = cmp_w[c, l0:l0 + CMP_STRIDE]
            z = jnp.zeros_like(w)
            halves.append(jnp.concatenate([jnp.concatenate([w, z], axis=2), jnp.concatenate([z, w], axis=2)], axis=1))
        return jnp.concatenate(halves, axis=2).reshape(CMP_STRIDE * LANES, 2 * LANES).astype(BF16)

    def pe_rows(l0):
        pk, pv = cmp_pe[0, l0:l0 + CMP_STRIDE], cmp_pe[1, l0:l0 + CMP_STRIDE]
        return jnp.concatenate([pk, pk, pv, pv], axis=1)

    pcol = jnp.zeros((2, LANES), F32).at[0, A_COL:A_COL + GDN_HEADS].set(a_log)
    pcol = pcol.at[1, A_COL:A_COL + GDN_HEADS].set(dt_bias)
    prow = jnp.zeros((SUBLANES, 2), F32).at[0:GDN_HEADS, 0].set(a_log).at[0:GDN_HEADS, 1].set(dt_bias)
    return dict(wm=wm, ws=ws, wkvt=wkvt, wo=wo, wk_long=long_contraction(0), wv_long=long_contraction(1),
                wlo=blockdiag(0), whi=blockdiag(CMP_STRIDE), pelo=pe_rows(0),
                pehi=pe_rows(CMP_STRIDE), cw=conv_w, pcol=pcol, prow=prow, nw=gdn_norm[None, :])


def kernel(x_prompt, x_sample, cache_nsa_kv, cache_nsa_win, state_gdn_S, state_gdn_conv, page_table, norm_mix, w_in,
           nsa_cmp_pe, nsa_cmp_w, gdn_conv_w, gdn_a_log, gdn_dt_bias, gdn_norm, w_out, norm_ffn, w_gate_up, w_down,
           norm_final):
    bp, tp, _ = x_prompt.shape
    bs, ts, _ = x_sample.shape
    depth = w_in.shape[0]
    n_pages = page_table.shape[1]
    past = n_pages * PAGE_SIZE
    assert depth == 1 and ts == 1, "one layer and one new token per sample row"
    assert tp % 512 == 0 and tp // SEL_LEN <= LANES and past % 512 == 0
    l = 0
    wts = _layer_weights(w_in[l], nsa_cmp_pe[l], nsa_cmp_w[l], gdn_conv_w[l], gdn_a_log[l], gdn_dt_bias[l],
                         gdn_norm[l], w_out[l])
    g_mix, g_ffn, g_fin = norm_mix[l][None, :], norm_ffn[l][None, :], norm_final[None, :]
    wgu, wd = w_gate_up[l].astype(BF16), w_down[l].astype(BF16)
    hist = CONV_W - 1

    xp = x_prompt.reshape(bp * tp, D_MODEL)
    q, kv, conv_in, z, small, kaug, vaug, kwin, vwaug, kv4t_p, kvwt_p = _proj(
        xp, g_mix, wts["wm"], wts["ws"], wts["wkvt"], *_rope_tables(np.arange(tp)), tm=256)
    ckv = _compress(kv, wts["wlo"], wts["whi"], wts["pelo"], wts["pehi"], batch=bp, seq=tp)
    nc = tp // CMP_STRIDE
    mt_p = _cmp_to_sel_t(nc - 1, tp // SEL_LEN, LANES, nc)
    o_nsa = _nsa_prompt(q, small, kaug, vaug, kwin, vwaug, ckv, mt_p, batch=bp, seq=tp, kc=512)
    smallt = small[:, A_COL:A_COL + SUBLANES].reshape(bp, tp, SUBLANES).transpose(0, 2, 1)
    prep = _gdn_prep(conv_in, small, smallt, jnp.zeros((bp, hist, GDN_CONV_CH), F32), wts["cw"], wts["pcol"],
                     wts["prow"], batch=bp, seq=tp, tb=256)
    o_gdn, s_p = _gdn_scan(*prep, z, jnp.zeros((bp, GDN_HEADS, GDN_DK, GDN_DV), F32), wts["nw"],
                           batch=bp, seq=tp, tb=256)
    o_gdn = o_gdn.reshape(bp * tp, GDN_V_W)
    s_p = s_p.reshape(bp, GDN_HEADS, GDN_DK, GDN_DV)
    y_p = _mix_ffn(xp, o_nsa, o_gdn, wts["wo"], g_ffn, wgu, wd, g_fin, tm=256)
    tok_major = lambda a, comps: a.reshape(bp, comps, NSA_KV_HEADS, HEAD_DIM, -1).transpose(0, 4, 1, 2, 3)
    kv_p = tok_major(kv4t_p, 4)
    win_p = tok_major(kvwt_p[:, :, tp - min(WINDOW, tp):], 2)
    conv_p = conv_in.reshape(bp, tp, GDN_CONV_CH)[:, tp - hist:]

    xs = x_sample.reshape(bs, D_MODEL)
    q, kv, conv_in, z, small = _proj(xs, g_mix, wts["wm"], wts["ws"], wts["wkvt"],
                                     *_rope_tables(np.full((bs,), past)), tm=bs)[:5]
    n_sel = past // SEL_LEN + 1
    nsp = -(-n_sel // LANES) * LANES
    mt_s = _cmp_to_sel_t(past // CMP_STRIDE - 1, n_sel, nsp, past // CMP_STRIDE)
    cache = cache_nsa_kv[l].transpose(0, 2, 3, 4, 1).reshape(-1, 4 * NSA_KV_W, PAGE_SIZE)
    win = cache_nsa_win[l].transpose(0, 2, 3, 4, 1).reshape(bs, 2 * NSA_KV_W, -1)
    o_nsa = _nsa_sample(page_table, cache, q[:, None], kv[:, None], small[:, None], win, wts["wk_long"],
                        wts["wv_long"], wts["pelo"], wts["pehi"], mt_s, pps=min(32, n_pages)).reshape(bs, NSA_Q_W)
    o_gdn, s_s = _gdn_step(conv_in, z, small, state_gdn_conv[l], state_gdn_S[l], wts["cw"], wts["pcol"], wts["nw"])
    o_gdn = o_gdn.reshape(bs, GDN_V_W)
    y_s = _mix_ffn(xs, o_nsa, o_gdn, wts["wo"], g_ffn, wgu, wd, g_fin, tm=bs)
    kv_s = kv.reshape(bs, 1, 3, 2, NSA_KV_HEADS, HEAD_DIM)
    win_s = jnp.concatenate([cache_nsa_win[l], kv_s[:, :, 2]], axis=1)[:, -min(WINDOW, past + 1):]
    conv_s = jnp.concatenate([state_gdn_conv[l], conv_in[:, None]], axis=1)[:, -hist:]

    return (y_p.reshape(bp, tp, D_MODEL), y_s.reshape(bs, 1, D_MODEL),
            kv_p[None], win_p[None], s_p[None], conv_p[None],
            kv_s[:, :, 0:2].reshape(bs, 1, 4, NSA_KV_HEADS, HEAD_DIM)[None], win_s[None], s_s[None], conv_s[None])
```

```python
import functools

import numpy as np
import jax
import jax.numpy as jnp
from jax import lax
from jax.experimental import pallas as pl
from jax.experimental.pallas import tpu as pltpu

F32 = jnp.float32
BF16 = jnp.bfloat16

D_MODEL = 1024
PAGE_SIZE = 128
HEAD_DIM = 64
NSA_HEADS = 8
NSA_KV_HEADS = 2
NSA_REP = NSA_HEADS // NSA_KV_HEADS
CMP_LEN = 32
CMP_STRIDE = 16
SEL_LEN = 64
SEL_TOPK = 16
WINDOW = 512
Q_BLOCK = 128
ROPE_THETA = 10000.0
GDN_HEADS = 4
GDN_DK = 128
GDN_DV = 128
CONV_W = 4
NSA_Q_W = NSA_HEADS * HEAD_DIM
NSA_KV_W = NSA_KV_HEADS * HEAD_DIM
GDN_QK_W = GDN_HEADS * GDN_DK
GDN_V_W = GDN_HEADS * GDN_DV
GDN_CONV_CH = 2 * GDN_QK_W + GDN_V_W
NEG = -1e30
LOG2E = 1.4426950408889634
SEL_BONUS = 1e4
EPS = 1e-6

LANES = 128
SUBLANES = 8
VMEM_LIMIT = 56 * 1024 * 1024

GATE_COLS = 3 * NSA_HEADS
A_COL = GATE_COLS
B_COL = GATE_COLS + GDN_HEADS

NT_DIMS = (((1,), (1,)), ((), ()))
TN_DIMS = (((0,), (0,)), ((), ()))


def _dot(a, b):
    return jnp.dot(a, b, preferred_element_type=F32)


def _dot_nt(a, b):
    return lax.dot_general(a, b, NT_DIMS, preferred_element_type=F32)


def _sigmoid(x):
    return 1.0 / (1.0 + jnp.exp(-x))


def _silu(x):
    return x * _sigmoid(x)


def _split3(x):
    p1 = x.astype(BF16)
    r1 = x - p1.astype(F32)
    p2 = r1.astype(BF16)
    p3 = (r1 - p2.astype(F32)).astype(BF16)
    return p1, p2, p3


def _const_spec(shape):
    nd = len(shape)
    return pl.BlockSpec(shape, lambda *_: (0,) * nd, pipeline_mode=pl.Buffered(1))


def _proj_kernel(x_ref, g_ref, wm_ref, ws_ref, wkvt_ref, cos_ref, sin_ref, cost_ref, sint_ref, cw_ref, buf_ref,
                 q_ref, kv_ref, conv_ref, z_ref, small_ref, kaug_ref, vaug_ref, kwin_ref, vwaug_ref,
                 kv4t_ref, kvwt_ref, act_ref, xs_ref, *, tm, pos_rows):
    pad = SUBLANES
    hist = CONV_W - 1
    c0 = NSA_Q_W + 6 * NSA_KV_W

    @pl.when(pl.program_id(0) % (pos_rows // tm) == 0)
    def _():
        xs_ref[0:pad, :] = jnp.zeros((pad, GDN_CONV_CH), F32)
        xs_ref[pad - hist:pad, :] = buf_ref[...]

    x = x_ref[...]
    ms = jnp.mean(x * x, axis=-1, keepdims=True)
    xn = (x * lax.rsqrt(ms + EPS) * g_ref[...]).astype(BF16)
    cos = cos_ref[...]
    sin = sin_ref[...]
    lane = lax.broadcasted_iota(jnp.int32, (tm, LANES), 1)
    low_half = (lane % HEAD_DIM) < (HEAD_DIM // 2)

    def rope(v):
        rot = jnp.where(low_half, pltpu.roll(v, LANES - HEAD_DIM // 2, 1), pltpu.roll(v, HEAD_DIM // 2, 1))
        return v * cos + rot * sin

    def delta_front_end(part):
        cols = slice(part * GDN_QK_W, (part + 1) * GDN_QK_W)
        conv_in = _dot(xn, wm_ref[:, c0 + part * GDN_QK_W:c0 + (part + 1) * GDN_QK_W])
        conv_ref[:, cols] = conv_in
        xs_ref[pad:pad + tm, cols] = conv_in
        xs = xs_ref[:, cols]
        conv = xs[pad:] * cw_ref[hist:CONV_W, cols]
        for t in range(hist):
            conv = conv + pltpu.roll(xs, hist - t, 0)[pad:] * cw_ref[t:t + 1, cols]
        xs_ref[0:pad, cols] = xs[tm:tm + pad]
        act = _silu(conv)
        for h in range(GDN_HEADS):
            hs = slice(h * GDN_DK, (h + 1) * GDN_DK)
            dst = act_ref.at[:, part * GDN_QK_W + h * GDN_DK:part * GDN_QK_W + (h + 1) * GDN_DK]
            if part == 0:
                dst[...] = _l2norm(act[:, hs]) * (GDN_DK ** -0.5)
            elif part == 1:
                dst[...] = _l2norm(act[:, hs])
            else:
                dst[...] = act[:, hs]

    delta_front_end(0)
    q = _dot(xn, wm_ref[:, 0:NSA_Q_W])
    for r in range(NSA_REP):
        sl = slice(r * LANES, (r + 1) * LANES)
        q_ref[:, sl] = (rope(q[:, sl]) * (HEAD_DIM ** -0.5 * LOG2E)).astype(BF16)

    delta_front_end(1)
    kv = _dot(xn, wm_ref[:, NSA_Q_W:NSA_Q_W + 6 * NSA_KV_W])
    group0 = lane < HEAD_DIM
    for c in range(6):
        sl = slice(c * LANES, (c + 1) * LANES)
        blk = kv[:, sl]
        if c % 2 == 0:
            blk = rope(blk)
        kv_ref[:, sl] = blk
        if c == 2:
            kaug_ref[:, LANES:2 * LANES] = blk.astype(BF16)
        elif c == 4:
            kwin_ref[...] = blk.astype(BF16)
        elif c in (3, 5):
            v_ref = vaug_ref if c == 3 else vwaug_ref
            v_ref[:, 0:LANES] = jnp.where(group0, blk, 1.0).astype(BF16)
            v_ref[:, LANES:2 * LANES] = jnp.where(group0, 1.0, blk).astype(BF16)

    row0 = (pl.program_id(0) * tm) % pos_rows
    rows = row0 + lax.broadcasted_iota(jnp.int32, (tm, LANES), 0)
    kaug_ref[:, 0:LANES] = jnp.where(rows // SEL_LEN == lane, 1.0, 0.0).astype(BF16)

    delta_front_end(2)
    kvt = _dot_nt(wkvt_ref[...], xn)
    cos_t = cost_ref[...]
    sin_t = sint_ref[...]
    half = HEAD_DIM // 2
    for c in range(6):
        blk = kvt[c * LANES:(c + 1) * LANES]
        if c % 2 == 0:
            parts = []
            for g in range(NSA_KV_HEADS):
                x1 = blk[g * HEAD_DIM:g * HEAD_DIM + half]
                x2 = blk[g * HEAD_DIM + half:(g + 1) * HEAD_DIM]
                parts += [x1 * cos_t - x2 * sin_t, x2 * cos_t + x1 * sin_t]
            blk = jnp.concatenate(parts, axis=0)
        dst = kv4t_ref if c < 4 else kvwt_ref
        dst[(c % 4) * LANES:(c % 4 + 1) * LANES, :] = blk

    z_ref[...] = _dot(xn, wm_ref[:, c0 + GDN_CONV_CH:c0 + GDN_CONV_CH + GDN_V_W])
    small_ref[...] = _dot(xn, ws_ref[...])


def _proj(x, g, wm, ws, wkvt, cw, buf, cos, sin, cos_t, sin_t, *, tm):
    rows = x.shape[0]
    pos_rows = cos.shape[0]
    n_pos_blk = pos_rows // tm
    n_seq = rows // pos_rows
    grid = (rows // tm,)
    row_spec = lambda w: pl.BlockSpec((tm, w), lambda i: (i, 0))
    pos_spec = pl.BlockSpec((tm, LANES), lambda i: (i % n_pos_blk, 0))
    pos_t_spec = pl.BlockSpec((HEAD_DIM // 2, tm), lambda i: (0, i % n_pos_blk))
    tok_minor = lambda r: pl.BlockSpec((None, r, tm), lambda i: (i // n_pos_blk, 0, i % n_pos_blk))
    out_shape = (
        jax.ShapeDtypeStruct((rows, NSA_Q_W), BF16),
        jax.ShapeDtypeStruct((rows, 6 * NSA_KV_W), F32),
        jax.ShapeDtypeStruct((rows, GDN_CONV_CH), F32),
        jax.ShapeDtypeStruct((rows, GDN_V_W), F32),
        jax.ShapeDtypeStruct((rows, LANES), F32),
        jax.ShapeDtypeStruct((rows, 2 * LANES), BF16),
        jax.ShapeDtypeStruct((rows, 2 * LANES), BF16),
        jax.ShapeDtypeStruct((rows, LANES), BF16),
        jax.ShapeDtypeStruct((rows, 2 * LANES), BF16),
    )
    out_shape_t = (jax.ShapeDtypeStruct((n_seq, 4 * NSA_KV_W, pos_rows), F32),
                   jax.ShapeDtypeStruct((n_seq, 2 * NSA_KV_W, pos_rows), F32),
                   jax.ShapeDtypeStruct((rows, GDN_CONV_CH), F32))
    return pl.pallas_call(
        functools.partial(_proj_kernel, tm=tm, pos_rows=pos_rows),
        grid=grid,
        in_specs=[row_spec(D_MODEL), _const_spec((1, D_MODEL)), _const_spec(wm.shape), _const_spec(ws.shape),
                  _const_spec(wkvt.shape), pos_spec, pos_spec, pos_t_spec, pos_t_spec, _const_spec(cw.shape),
                  pl.BlockSpec((None, CONV_W - 1, GDN_CONV_CH), lambda i: (i // n_pos_blk, 0, 0))],
        out_specs=(tuple(row_spec(s.shape[1]) for s in out_shape)
                   + (tok_minor(4 * NSA_KV_W), tok_minor(2 * NSA_KV_W), row_spec(GDN_CONV_CH))),
        out_shape=out_shape + out_shape_t,
        scratch_shapes=[pltpu.VMEM((SUBLANES + tm, GDN_CONV_CH), F32)],
        compiler_params=pltpu.CompilerParams(dimension_semantics=("arbitrary",), vmem_limit_bytes=VMEM_LIMIT),
        name="proj",
    )(x, g, wm, ws, wkvt, cos, sin, cos_t, sin_t, cw, buf)


def _mix_ffn_kernel(x_ref, on_ref, og_ref, wo_ref, gf_ref, wgu_ref, wd_ref, gl_ref, y_ref, *, d_ff):
    h = x_ref[...] + _dot(on_ref[...], wo_ref[0:NSA_Q_W, :]) + _dot(og_ref[...], wo_ref[NSA_Q_W:, :])
    ms = jnp.mean(h * h, axis=-1, keepdims=True)
    hn = (h * lax.rsqrt(ms + EPS) * gf_ref[...]).astype(BF16)
    gate = _dot(hn, wgu_ref[:, 0:d_ff])
    up = _dot(hn, wgu_ref[:, d_ff:])
    act = (_silu(gate) * up).astype(BF16)
    h = h + _dot(act, wd_ref[...])
    ms = jnp.mean(h * h, axis=-1, keepdims=True)
    y_ref[...] = h * lax.rsqrt(ms + EPS) * gl_ref[...]


def _mix_ffn(x, o_nsa, o_gdn, wo, gf, wgu, wd, gl, *, tm):
    rows = x.shape[0]
    d_ff = wd.shape[0]
    row_spec = lambda w: pl.BlockSpec((tm, w), lambda i: (i, 0))
    return pl.pallas_call(
        functools.partial(_mix_ffn_kernel, d_ff=d_ff),
        grid=(rows // tm,),
        in_specs=[row_spec(D_MODEL), row_spec(NSA_Q_W), row_spec(GDN_V_W), _const_spec(wo.shape),
                  _const_spec((1, D_MODEL)), _const_spec(wgu.shape), _const_spec(wd.shape),
                  _const_spec((1, D_MODEL))],
        out_specs=row_spec(D_MODEL),
        out_shape=jax.ShapeDtypeStruct((rows, D_MODEL), F32),
        compiler_params=pltpu.CompilerParams(dimension_semantics=("arbitrary",), vmem_limit_bytes=VMEM_LIMIT),
        name="mix_ffn",
    )(x, o_nsa, o_gdn, wo, gf, wgu, wd, gl)


def _compress_rows(load_rows, wlo_ref, whi_ref, pelo_ref, pehi_ref, nc):
    acc_lo = jnp.zeros((nc, 2 * LANES), F32)
    acc_hi = jnp.zeros((nc, 2 * LANES), F32)
    for l in range(CMP_STRIDE):
        x = load_rows(l)
        acc_lo = acc_lo + _dot((x + pelo_ref[l:l + 1, :]).astype(BF16), wlo_ref[l])
        acc_hi = acc_hi + _dot((x + pehi_ref[l:l + 1, :]).astype(BF16), whi_ref[l])
    nxt = pltpu.roll(acc_hi, nc - 1, 0)
    row = lax.broadcasted_iota(jnp.int32, (nc, 2 * LANES), 0)
    return jnp.where(row < nc - 1, acc_lo + nxt, 0.0)


def _compress_kernel(k_ref, v_ref, wlo_ref, whi_ref, pelo_ref, pehi_ref, ckv_ref, *, nc):
    load = lambda l: jnp.concatenate([k_ref[pl.ds(l, nc, stride=CMP_STRIDE), :],
                                      v_ref[pl.ds(l, nc, stride=CMP_STRIDE), :]], axis=1)
    ckv_ref[...] = _compress_rows(load, wlo_ref, whi_ref, pelo_ref, pehi_ref, nc).astype(BF16)


def _compress(kv, wlo, whi, pelo, pehi, *, batch, seq):
    nc = seq // CMP_STRIDE
    return pl.pallas_call(
        functools.partial(_compress_kernel, nc=nc),
        grid=(batch,),
        in_specs=[pl.BlockSpec((seq, LANES), lambda b: (b, 0)), pl.BlockSpec((seq, LANES), lambda b: (b, 1)),
                  _const_spec(wlo.shape), _const_spec(whi.shape), _const_spec(pelo.shape), _const_spec(pehi.shape)],
        out_specs=pl.BlockSpec((nc, 2 * LANES), lambda b: (b, 0)),
        out_shape=jax.ShapeDtypeStruct((batch * nc, 2 * LANES), BF16),
        compiler_params=pltpu.CompilerParams(dimension_semantics=("arbitrary",), vmem_limit_bytes=VMEM_LIMIT),
        name="compress",
    )(kv, kv, wlo, whi, pelo, pehi)


def _masked_softmax2(s):
    m = jnp.maximum(jnp.max(s, axis=-1, keepdims=True), NEG / 8)
    e = jnp.exp2(s - m)
    return e * (1.0 / jnp.maximum(jnp.sum(e, axis=-1, keepdims=True), 1e-30))


def _select_blocks(score, sel, blk, axis, rounds, filler=iter(())):
    n = score.shape[axis]
    for _ in range(rounds):
        mx = jnp.max(score, axis=axis, keepdims=True)
        idx = jnp.min(jnp.where(score == mx, blk, n), axis=axis, keepdims=True)
        hit = blk == idx
        sel = jnp.where(hit, jnp.where(mx > NEG / 2, 1.0, sel), sel)
        score = jnp.where(hit, -3e38, score)
        next(filler, None)
    for _ in filler:
        pass
    return sel


def _nsa_prompt_kernel(q_ref, small_ref, kaug_ref, vaug_ref, kwin_ref, vwaug_ref, ckv_ref, mt_ref, o_ref,
                       *, seq, kc):
    nc = seq // CMP_STRIDE
    qb = Q_BLOCK
    start = pl.program_id(1) * qb
    wlen = WINDOW + qb
    groups = range(NSA_KV_HEADS)
    stack = lambda x: jnp.concatenate([x] * NSA_REP, axis=0)

    lane = lax.broadcasted_iota(jnp.int32, (qb, LANES), 1)
    gates = _sigmoid(small_ref[...])

    qrow_c = lax.broadcasted_iota(jnp.int32, (qb, nc), 0)
    ncol = lax.broadcasted_iota(jnp.int32, (qb, nc), 1)
    cmp_bias = stack(jnp.where(ncol * CMP_STRIDE + (CMP_LEN - 1) - qrow_c <= start, 0.0, NEG))
    tail_bias = stack(jnp.where(lane <= lax.broadcasted_iota(jnp.int32, (qb, qb), 0), 0.0, NEG))
    wbase = pl.multiple_of(jnp.maximum(start - WINDOW, 0), qb)
    back = (lax.broadcasted_iota(jnp.int32, (qb, wlen), 0) - lax.broadcasted_iota(jnp.int32, (qb, wlen), 1)
            + (start - wbase))
    win_bias = stack(jnp.where(back >= 0, jnp.where(back <= WINDOW, 0.0, NEG), NEG))

    glanes = [slice(g * LANES, (g + 1) * LANES) for g in groups]
    qs = [jnp.concatenate([jnp.where((lane >= HEAD_DIM) == (g == 1), q_ref[:, r * LANES:(r + 1) * LANES], 0.0)
                           .astype(BF16) for r in range(NSA_REP)], axis=0) for g in groups]
    ps = [_masked_softmax2(_dot_nt(q, ckv_ref[:, 0:LANES]) + cmp_bias) for q in qs]
    psums = [p[0:qb] + p[qb:2 * qb] + p[2 * qb:3 * qb] + p[3 * qb:4 * qb] for p in ps]
    imp_t = jnp.concatenate([sum(_dot_nt(mt_ref[...], piece) for piece in _split3(psum)) for psum in psums], axis=1)
    o_cmps = [_dot(p.astype(BF16), ckv_ref[:, LANES:2 * LANES]) for p in ps]

    done = {}

    def independent_work():
        s = [_dot_nt(q, kwin_ref[pl.ds(wbase, wlen), :]) + win_bias for q in qs]
        yield
        mx = [jnp.max(x, axis=-1, keepdims=True) for x in s]
        yield
        e = []
        for g in groups:
            e.append(jnp.exp2(s[g] - mx[g]).astype(BF16))
            yield
        acc = [_dot(e[g], vwaug_ref[pl.ds(wbase, wlen), glanes[g]]) for g in groups]
        yield
        o_win = [a / pltpu.roll(a, HEAD_DIM, 1) for a in acc]
        yield
        rows_of = lambda x, r: x[r * qb:(r + 1) * qb]
        done["partial"] = []
        for g in groups:
            done["partial"].append(
                [gates[:, 3 * (g * NSA_REP + r):3 * (g * NSA_REP + r) + 1] * rows_of(o_cmps[g], r)
                 + gates[:, 3 * (g * NSA_REP + r) + 2:3 * (g * NSA_REP + r) + 3] * rows_of(o_win[g], r)
                 for r in range(NSA_REP)])
            yield
        s_tail = [_dot_nt(q, kaug_ref[pl.ds(start, qb), LANES:2 * LANES]) + tail_bias for q in qs]
        yield
        done["m0"] = [jnp.max(x, axis=-1, keepdims=True) for x in s_tail]
        yield
        done["acc0"] = [_dot(jnp.exp2(s_tail[g] - done["m0"][g]).astype(BF16), vaug_ref[pl.ds(start, qb), glanes[g]])
                        for g in groups]
        yield

    blk_t = lax.broadcasted_iota(jnp.int32, (LANES, 2 * qb), 0)
    qpos_t = start + lax.broadcasted_iota(jnp.int32, (LANES, 2 * qb), 1) % qb
    cur_t = qpos_t // SEL_LEN
    visible = blk_t * SEL_LEN <= qpos_t
    forced = (blk_t == 0) | (blk_t == cur_t) | (blk_t == cur_t - 1)
    sel_t = _select_blocks(jnp.where(visible, jnp.where(forced, NEG, imp_t), NEG),
                           jnp.where(visible, jnp.where(forced, 1.0, 0.0), 0.0), blk_t, 0, SEL_TOPK - 3,
                           independent_work())
    partial, m0s, acc0s = done["partial"], done["m0"], done["acc0"]
    bias_t = jnp.where(blk_t * SEL_LEN < start, jnp.where(sel_t > 0.5, 0.0, NEG), NEG).T.astype(BF16)
    n_chunks = (start + kc - 1) // kc
    q_augs = [jnp.concatenate([stack(bias_t[g * qb:(g + 1) * qb]), qs[g]], axis=1) for g in groups]

    def sel_step(c, carry):
        off = pl.multiple_of(c * kc, kc)
        k = kaug_ref[pl.ds(off, kc), :]
        out = []
        for g in groups:
            m, acc = carry[g]
            s = _dot_nt(q_augs[g], k)
            m_new = jnp.maximum(m, jnp.max(s, axis=-1, keepdims=True))
            e = jnp.exp2(s - m_new).astype(BF16)
            out.append((m_new, jnp.exp2(m - m_new) * acc + _dot(e, vaug_ref[pl.ds(off, kc), g * LANES:(g + 1) * LANES])))
        return tuple(out)

    carry = lax.fori_loop(0, n_chunks // 2, lambda c, x: sel_step(2 * c + 1, sel_step(2 * c, x)),
                          tuple(zip(m0s, acc0s)))
    sel_out = lax.cond(n_chunks % 2 == 1, lambda x: sel_step(n_chunks - 1, x), lambda x: x, carry)

    outs = [None] * NSA_REP
    for g in groups:
        acc = sel_out[g][1]
        o_sel = acc / pltpu.roll(acc, HEAD_DIM, 1)
        for r in range(NSA_REP):
            c0 = (g * NSA_REP + r) * 3
            o = partial[g][r] + gates[:, c0 + 1:c0 + 2] * o_sel[r * qb:(r + 1) * qb]
            outs[r] = o if g == 0 else jnp.where(lane < HEAD_DIM, outs[r], o)

    for r in range(NSA_REP):
        o_ref[:, r * LANES:(r + 1) * LANES] = outs[r].astype(BF16)


def _nsa_prompt(q, small, kaug, vaug, kwin, vwaug, ckv, mt, *, batch, seq, kc):
    nb = seq // Q_BLOCK
    nc = seq // CMP_STRIDE
    blk_spec = lambda w: pl.BlockSpec((Q_BLOCK, w), lambda b, i: (b * nb + i, 0))
    seq_spec = lambda w: pl.BlockSpec((seq, w), lambda b, i: (b, 0))
    return pl.pallas_call(
        functools.partial(_nsa_prompt_kernel, seq=seq, kc=kc),
        grid=(batch, nb),
        in_specs=[blk_spec(NSA_Q_W), blk_spec(LANES), seq_spec(2 * LANES), seq_spec(2 * LANES), seq_spec(LANES),
                  seq_spec(2 * LANES), pl.BlockSpec((nc, 2 * LANES), lambda b, i: (b, 0)), _const_spec(mt.shape)],
        out_specs=blk_spec(NSA_Q_W),
        out_shape=jax.ShapeDtypeStruct((batch * seq, NSA_Q_W), BF16),
        compiler_params=pltpu.CompilerParams(dimension_semantics=("arbitrary", "arbitrary"),
                                             vmem_limit_bytes=VMEM_LIMIT),
        name="nsa_prompt",
    )(q, small, kaug, vaug, kwin, vwaug, ckv, mt)


def _nsa_sample_kernel(pt_ref, *refs, past, pps):
    del pt_ref
    pages = refs[:pps]
    (q_ref, kvn_ref, small_ref, win_ref, perm_ref, wk_ref, wv_ref, pelo_ref, pehi_ref, mt_ref, exp_ref,
     o_ref, xs_ref, s_ref, vs_ref) = refs[pps:]
    step = pl.program_id(1)
    n_pages = past // PAGE_SIZE
    cpp = PAGE_SIZE // CMP_STRIDE
    heads = NSA_HEADS
    row = lax.broadcasted_iota(jnp.int32, (heads, LANES), 0)
    lane = lax.broadcasted_iota(jnp.int32, (heads, LANES), 1)
    mine = (lane >= HEAD_DIM) == (row >= NSA_REP)
    rr = row % NSA_REP
    qsel = jnp.zeros((heads, LANES), F32)
    for r in range(NSA_REP):
        qsel = jnp.where(rr == r, q_ref[:, r * LANES:(r + 1) * LANES].astype(F32), qsel)
    qf = jnp.where(mine, qsel, 0.0)
    q = qf.astype(BF16)

    nc = past // CMP_STRIDE

    def take_pages(first_page):
        for k in range(pps):
            pg = first_page + k
            page = pages[k]
            tok = slice(pg * PAGE_SIZE, (pg + 1) * PAGE_SIZE)
            s_ref[:, tok] = _dot(q, page[2 * LANES:3 * LANES, :].astype(BF16))
            vs_ref[:, tok] = page[3 * LANES:4 * LANES, :].astype(BF16)
            x = _dot_nt(perm_ref[...], page[0:2 * LANES, :].astype(BF16))
            for l in range(CMP_STRIDE):
                for c in range(2):
                    xs_ref[c, pg * cpp:(pg + 1) * cpp, l * LANES:(l + 1) * LANES] = (
                        x[l * cpp:(l + 1) * cpp, c * LANES:(c + 1) * LANES])

    for st in range(n_pages // pps):
        pl.when(step == st)(functools.partial(take_pages, st * pps))

    @pl.when(step == pl.num_programs(1) - 1)
    def _():
        n_cmp = nc - 1
        nsp = mt_ref.shape[0]
        cur = past // SEL_LEN

        crow = lax.broadcasted_iota(jnp.int32, (nc, LANES), 0)
        ckv = []
        for c, w_ref in enumerate((wk_ref, wv_ref)):
            cl = slice(c * LANES, (c + 1) * LANES)
            for l in range(CMP_STRIDE):
                ll = slice(l * LANES, (l + 1) * LANES)
                xs_ref[c, nc:nc + SUBLANES, ll] = jnp.broadcast_to(pelo_ref[l:l + 1, cl], (SUBLANES, LANES))
                xs_ref[c, nc + SUBLANES:nc + 2 * SUBLANES, ll] = jnp.broadcast_to(pehi_ref[l:l + 1, cl],
                                                                                    (SUBLANES, LANES))
            acc = _dot(xs_ref[c].astype(BF16), w_ref[...])
            bias = acc[nc:nc + 1, 0:LANES] + acc[nc + SUBLANES:nc + SUBLANES + 1, LANES:2 * LANES]
            nxt = pltpu.roll(acc[0:nc, LANES:2 * LANES], nc - 1, 0)
            ckv.append(jnp.where(crow < nc - 1, acc[0:nc, 0:LANES] + nxt + bias, 0.0).astype(BF16))
        ckv = jnp.concatenate(ckv, axis=1)

        ncol = lax.broadcasted_iota(jnp.int32, (1, nc), 1)
        cmp_valid = (ncol * CMP_STRIDE + CMP_LEN - 1 <= past) & (ncol < n_cmp)
        p = _masked_softmax2(jnp.where(cmp_valid, _dot_nt(q, ckv[:, 0:LANES]), NEG))
        o_cmp = _dot(p.astype(BF16), ckv[:, LANES:2 * LANES])

        prow = lax.broadcasted_iota(jnp.int32, (heads, nc), 0)
        ps = [jnp.sum(jnp.where(prow // NSA_REP == g, p, 0.0), axis=0, keepdims=True) for g in range(NSA_KV_HEADS)]
        psum = jnp.where(prow < NSA_REP, ps[0], ps[1])
        imp = sum(_dot_nt(piece, mt_ref[...]) for piece in _split3(psum))
        blk = lax.broadcasted_iota(jnp.int32, (heads, nsp), 1)
        forced = (blk == 0) | (blk == cur) | (blk == cur - 1)
        score = jnp.where(blk * SEL_LEN <= past, imp + jnp.where(forced, SEL_BONUS, 0.0), NEG)
        score_t = score.T
        bi = lax.broadcasted_iota(jnp.int32, (nsp, nsp), 0)
        bj = lax.broadcasted_iota(jnp.int32, (nsp, nsp), 1)
        tie = jnp.where(bi < bj, 1.0, 0.0)
        picks = []
        for g in range(NSA_KV_HEADS):
            c = g * NSA_REP
            s_i, s_j = score_t[:, c:c + 1], score[c:c + 1, :]
            beats = jnp.where(s_i > s_j, 1.0, jnp.where(s_i == s_j, tie, 0.0))
            rank = jnp.sum(beats, axis=0, keepdims=True)
            picks.append(jnp.where(rank < SEL_TOPK, jnp.where(s_j > NEG / 2, 1.0, 0.0), 0.0))
        sel8 = jnp.where(lax.broadcasted_iota(jnp.int32, (heads, nsp), 0) < NSA_REP, picks[0], picks[1])
        keep = _dot(sel8.astype(BF16), exp_ref[...]) > 0.5

        kvn = kvn_ref[...]
        new_ok = sel8[:, cur:cur + 1] > 0.5
        s = jnp.where(keep, s_ref[...], NEG)
        s_new = jnp.where(new_ok, jnp.sum(qf * kvn[:, 2 * LANES:3 * LANES], axis=-1, keepdims=True), NEG)
        m = jnp.maximum(jnp.max(s, axis=-1, keepdims=True), s_new)
        e = jnp.exp2(s - m)
        e_new = jnp.where(new_ok, jnp.exp2(s_new - m), 0.0)
        o_sel = ((_dot_nt(e.astype(BF16), vs_ref[...]) + e_new * kvn[:, 3 * LANES:4 * LANES])
                 / (jnp.sum(e, axis=-1, keepdims=True) + e_new))

        nw = win_ref.shape[0] // 2
        s = _dot(q, win_ref[0:nw, :].astype(BF16))
        s_new = jnp.sum(qf * kvn[:, 4 * LANES:5 * LANES], axis=-1, keepdims=True)
        m = jnp.maximum(jnp.max(s, axis=-1, keepdims=True), s_new)
        e = jnp.exp2(s - m)
        e_new = jnp.exp2(s_new - m)
        l = jnp.sum(e, axis=-1, keepdims=True) + e_new
        o_win = (_dot_nt(e.astype(BF16), win_ref[nw:, :].astype(BF16))
                 + e_new * kvn[:, 5 * LANES:6 * LANES]) / l

        gates = _sigmoid(small_ref[...])
        gate = lambda br: jnp.sum(jnp.where(lane == row * 3 + br, gates, 0.0), axis=-1, keepdims=True)
        o = gate(0) * o_cmp + gate(1) * o_sel + gate(2) * o_win
        for r in range(NSA_REP):
            o_ref[:, r * LANES:(r + 1) * LANES] = jnp.where(
                lane[0:1] < HEAD_DIM, o[r:r + 1], o[NSA_REP + r:NSA_REP + r + 1]).astype(BF16)


def _nsa_sample(page_table, cache, q, kvn, small, win, wlo, whi, pelo, pehi, mt, *, pps):
    batch, n_pages = page_table.shape
    past = n_pages * PAGE_SIZE
    rows = cache.shape[1]
    cpp = PAGE_SIZE // CMP_STRIDE
    nsp = mt.shape[0]
    expand = jnp.asarray(np.arange(nsp)[:, None] == np.arange(past)[None, :] // SEL_LEN, BF16)
    tok = np.arange(PAGE_SIZE)
    perm = jnp.asarray((tok[None, :] % CMP_STRIDE) * cpp + tok[None, :] // CMP_STRIDE == tok[:, None], BF16)

    def page_spec(k):
        return pl.BlockSpec((None, rows, PAGE_SIZE), lambda b, s, pt: (pt[b * n_pages + s * pps + k], 0, 0))

    per_b = lambda shape: pl.BlockSpec((None,) + shape, lambda b, s, pt: (b, 0, 0))
    const = lambda a: pl.BlockSpec(a.shape, lambda b, s, pt: (0,) * a.ndim, pipeline_mode=pl.Buffered(1))
    grid_spec = pltpu.PrefetchScalarGridSpec(
        num_scalar_prefetch=1,
        grid=(batch, n_pages // pps),
        in_specs=[page_spec(k) for k in range(pps)] + [
            per_b((1, NSA_Q_W)), per_b((1, 6 * NSA_KV_W)), per_b((1, LANES)), per_b(win.shape[1:]),
            const(perm), const(wlo), const(whi), const(pelo), const(pehi), const(mt), const(expand)],
        out_specs=per_b((1, NSA_Q_W)),
        scratch_shapes=[pltpu.VMEM((2, past // CMP_STRIDE + 2 * SUBLANES, CMP_STRIDE * LANES), F32),
                        pltpu.VMEM((NSA_HEADS, past), F32),
                        pltpu.VMEM((LANES, past), BF16)],
    )
    return pl.pallas_call(
        functools.partial(_nsa_sample_kernel, past=past, pps=pps),
        grid_spec=grid_spec,
        out_shape=jax.ShapeDtypeStruct((batch, 1, NSA_Q_W), BF16),
        compiler_params=pltpu.CompilerParams(dimension_semantics=("arbitrary", "arbitrary"),
                                             vmem_limit_bytes=VMEM_LIMIT),
        name="nsa_sample",
    )(page_table.reshape(-1), *([cache] * pps), q, kvn, small, win, perm, wlo, whi, pelo, pehi, mt, expand)


def _l2norm(x):
    return x * lax.rsqrt(jnp.sum(x * x, axis=-1, keepdims=True) + EPS)


def _softplus(x):
    return jnp.maximum(x, 0.0) + jnp.log(1.0 + jnp.exp(-jnp.abs(x)))


def _gdn_step_kernel(x_ref, z_ref, small_ref, buf_ref, s0_ref, cw_ref, pcol_ref, nw_ref, o_ref, sout_ref):
    hist = CONV_W - 1
    conv = jnp.sum(buf_ref[...] * cw_ref[0:hist, :], axis=0, keepdims=True) + x_ref[...] * cw_ref[hist:CONV_W, :]
    act = _silu(conv)
    small = small_ref[...]
    g_all = -jnp.exp(pcol_ref[0:1, :]) * _softplus(small + pcol_ref[1:2, :])
    beta_all = _sigmoid(small)
    row = lax.broadcasted_iota(jnp.int32, (SUBLANES, GDN_DK), 0)
    for h in range(GDN_HEADS):
        hs = slice(h * GDN_DK, (h + 1) * GDN_DK)
        qh = _l2norm(act[:, hs]) * (GDN_DK ** -0.5)
        kh = _l2norm(act[:, GDN_QK_W + h * GDN_DK:GDN_QK_W + (h + 1) * GDN_DK])
        vh = act[:, 2 * GDN_QK_W + h * GDN_DV:2 * GDN_QK_W + (h + 1) * GDN_DV]
        eg = jnp.exp(g_all[:, A_COL + h:A_COL + h + 1])
        bt = beta_all[:, B_COL + h:B_COL + h + 1]
        s = s0_ref[h]
        kq = jnp.where(row == 0, kh, jnp.where(row == 1, qh, 0.0)).astype(BF16)
        ks_qs = _dot(kq, s.astype(BF16))
        vn = bt * (vh - eg * ks_qs[0:1])
        o = eg * ks_qs[1:2] + jnp.sum(qh * kh, axis=-1, keepdims=True) * vn
        k8 = jnp.where(row == 0, kh, 0.0).astype(BF16)
        vn8 = jnp.where(row == 0, vn, 0.0).astype(BF16)
        sout_ref[h] = s * eg + lax.dot_general(k8, vn8, TN_DIMS, preferred_element_type=F32)
        on = o * lax.rsqrt(jnp.mean(o * o, axis=-1, keepdims=True) + EPS) * nw_ref[...]
        o_ref[:, hs] = (on * _silu(z_ref[:, hs])).astype(BF16)


def _gdn_step(x, z, small, buf, s0, cw, pcol, nw):
    batch = x.shape[0]
    per_b = lambda shape: pl.BlockSpec((None,) + shape, lambda b: (b,) + (0,) * len(shape))
    state = (GDN_HEADS, GDN_DK, GDN_DV)
    return pl.pallas_call(
        _gdn_step_kernel,
        grid=(batch,),
        in_specs=[per_b((1, GDN_CONV_CH)), per_b((1, GDN_V_W)), per_b((1, LANES)), per_b((CONV_W - 1, GDN_CONV_CH)),
                  per_b(state), _const_spec(cw.shape), _const_spec(pcol.shape), _const_spec(nw.shape)],
        out_specs=(per_b((1, GDN_V_W)), per_b(state)),
        out_shape=(jax.ShapeDtypeStruct((batch, 1, GDN_V_W), BF16),
                   jax.ShapeDtypeStruct((batch,) + state, F32)),
        compiler_params=pltpu.CompilerParams(dimension_semantics=("arbitrary",), vmem_limit_bytes=VMEM_LIMIT),
        name="gdn_step",
    )(x[:, None], z[:, None], small[:, None], buf, s0, cw, pcol, nw)


GDN_CHUNK = 64
GDN_STACK = GDN_HEADS * GDN_CHUNK


def _stack_heads(x, col0, width):
    return jnp.concatenate([x[:, col0 + h * width:col0 + (h + 1) * width] for h in range(GDN_HEADS)], axis=0)


def _gdn_prep_kernel(act_ref, small_ref, smallt_ref, pcol_ref, prow_ref, tril_ref, triu_ref,
                     u_ref, w_ref, qg_ref, kgt_ref, aqk_ref, gl_ref, *, tb):
    ck = GDN_CHUNK
    st = GDN_STACK
    act = act_ref[...]

    small = small_ref[...]
    g_col = -jnp.exp(pcol_ref[0:1, :]) * _softplus(small + pcol_ref[1:2, :])
    beta = _sigmoid(small)
    g_row = -jnp.exp(prow_ref[:, 0:1]) * _softplus(smallt_ref[...] + prow_ref[:, 1:2])
    gcum_col = sum(_dot(tril_ref[...], piece) for piece in _split3(g_col))
    gcum_row = sum(_dot(piece, triu_ref[...]) for piece in _split3(g_row))

    ii = lax.broadcasted_iota(jnp.int32, (st, st), 0)
    jj = lax.broadcasted_iota(jnp.int32, (st, st), 1)
    same_head = (ii // ck) == (jj // ck)
    incl = same_head & (ii >= jj)
    strict = same_head & (ii > jj)
    eye = jnp.where(ii == jj, 1.0, 0.0)
    hrow = lax.broadcasted_iota(jnp.int32, (SUBLANES, LANES), 0)

    lms, rhss = [], []
    for ci in range(tb // ck):
        r0 = ci * ck
        rs = slice(r0, r0 + ck)
        gc = _stack_heads(gcum_col[rs], A_COL, 1)
        bt = _stack_heads(beta[rs], B_COL, 1)
        gr = jnp.concatenate([gcum_row[h:h + 1, r0:r0 + ck] for h in range(GDN_HEADS)], axis=1)
        glast = [gcum_col[r0 + ck - 1:r0 + ck, A_COL + h:A_COL + h + 1] for h in range(GDN_HEADS)]
        gl_stack = jnp.concatenate([jnp.broadcast_to(x, (ck, 1)) for x in glast], axis=0)
        dec = jnp.where(incl, jnp.exp(jnp.where(incl, gc - gr, 0.0)), 0.0)
        qs = _stack_heads(act[rs], 0, GDN_DK)
        ks = _stack_heads(act[rs], GDN_QK_W, GDN_DK)
        vs = _stack_heads(act[rs], 2 * GDN_QK_W, GDN_DV)
        kb = ks * bt
        k16 = ks.astype(BF16)
        lms.append(jnp.where(strict, _dot_nt(kb.astype(BF16), k16) * dec, 0.0))
        eg = jnp.exp(gc)
        rhss.append(jnp.concatenate([vs * bt, kb * eg], axis=1).astype(BF16))
        orow = slice(ci * st, (ci + 1) * st)
        qg_ref[orow, :] = (qs * eg).astype(BF16)
        aqk_ref[orow, :] = (_dot_nt(qs.astype(BF16), k16) * dec).astype(BF16)
        kgt_ref[ci * GDN_DK:(ci + 1) * GDN_DK, :] = (ks * jnp.exp(gl_stack - gc)).T.astype(BF16)
        gl = jnp.zeros((SUBLANES, LANES), F32)
        for h in range(GDN_HEADS):
            gl = jnp.where(hrow == h, jnp.exp(glast[h]), gl)
        gl_ref[ci * SUBLANES:(ci + 1) * SUBLANES, :] = gl

    ainvs = [eye - lm for lm in lms]
    pws = lms
    n = 2
    while n < ck:
        pw16s = [pw.astype(BF16) for pw in pws]
        pws = [_dot(pw16, pw16) for pw16 in pw16s]
        ainvs = [ainv + _dot(ainv.astype(BF16), pw.astype(BF16)) for ainv, pw in zip(ainvs, pws)]
        n *= 2
    for ci, (ainv, rhs) in enumerate(zip(ainvs, rhss)):
        sol = _dot(ainv.astype(BF16), rhs)
        orow = slice(ci * st, (ci + 1) * st)
        u_ref[orow, :] = sol[:, 0:GDN_DV]
        w_ref[orow, :] = sol[:, GDN_DV:].astype(BF16)


def _gdn_prep(act, small, smallt, pcol, prow, *, batch, seq, tb):
    nblk = seq // tb
    ncb = tb // GDN_CHUNK
    blk = np.arange(tb)
    same = (blk[:, None] // GDN_CHUNK) == (blk[None, :] // GDN_CHUNK)
    tril = jnp.asarray(same & (blk[:, None] >= blk[None, :]), BF16)
    triu = jnp.asarray(same & (blk[:, None] <= blk[None, :]), BF16)
    row_spec = lambda r, w: pl.BlockSpec((r, w), lambda b, j: (b * nblk + j, 0))
    per_b = lambda shape: pl.BlockSpec((None,) + shape, lambda b, j: (b,) + (0,) * len(shape))
    n_chunks = batch * seq // GDN_CHUNK
    out_shape = (jax.ShapeDtypeStruct((n_chunks * GDN_STACK, GDN_DV), F32),
                 jax.ShapeDtypeStruct((n_chunks * GDN_STACK, GDN_DK), BF16),
                 jax.ShapeDtypeStruct((n_chunks * GDN_STACK, GDN_DK), BF16),
                 jax.ShapeDtypeStruct((n_chunks * GDN_DK, GDN_STACK), BF16),
                 jax.ShapeDtypeStruct((n_chunks * GDN_STACK, GDN_STACK), BF16),
                 jax.ShapeDtypeStruct((n_chunks * SUBLANES, LANES), F32))
    out_specs = (row_spec(ncb * GDN_STACK, GDN_DV), row_spec(ncb * GDN_STACK, GDN_DK),
                 row_spec(ncb * GDN_STACK, GDN_DK), row_spec(ncb * GDN_DK, GDN_STACK),
                 row_spec(ncb * GDN_STACK, GDN_STACK), row_spec(ncb * SUBLANES, LANES))
    return pl.pallas_call(
        functools.partial(_gdn_prep_kernel, tb=tb),
        grid=(batch, nblk),
        in_specs=[row_spec(tb, GDN_CONV_CH), row_spec(tb, LANES),
                  pl.BlockSpec((None, SUBLANES, tb), lambda b, j: (b, 0, j)),
                  _const_spec(pcol.shape), _const_spec(prow.shape),
                  _const_spec(tril.shape), _const_spec(triu.shape)],
        out_specs=out_specs,
        out_shape=out_shape,
        compiler_params=pltpu.CompilerParams(dimension_semantics=("arbitrary", "arbitrary"),
                                             vmem_limit_bytes=VMEM_LIMIT),
        name="gdn_prep",
    )(act, small, smallt, pcol, prow, tril, triu)


def _gdn_scan_kernel(u_ref, w_ref, qg_ref, kgt_ref, aqk_ref, gl_ref, z_ref, s0_ref, nw_ref, o_ref, sout_ref, s_ref,
                     *, batch, ncb):
    ck = GDN_CHUNK
    st = GDN_STACK
    sw = GDN_HEADS * GDN_DK

    @pl.when(pl.program_id(0) == 0)
    def _():
        s_ref[...] = s0_ref[...]

    wide_mask = (lax.broadcasted_iota(jnp.int32, (st, sw), 0) // ck
                 == lax.broadcasted_iota(jnp.int32, (st, sw), 1) // GDN_DK)
    tall_mask = (lax.broadcasted_iota(jnp.int32, (sw, st), 0) // GDN_DK
                 == lax.broadcasted_iota(jnp.int32, (sw, st), 1) // ck)

    seqs = range(batch)
    s = [s_ref[b] for b in seqs]
    for ci in range(ncb):
        rows = slice(ci * st, (ci + 1) * st)
        toks = slice(ci * ck, (ci + 1) * ck)
        lhs, kgt_bd, gl_rows = [], [], []
        for b in seqs:
            w_bd = jnp.where(wide_mask, jnp.concatenate([w_ref[b, rows, :]] * GDN_HEADS, axis=1), 0.0)
            qg_bd = jnp.where(wide_mask, jnp.concatenate([qg_ref[b, rows, :]] * GDN_HEADS, axis=1), 0.0)
            lhs.append(jnp.concatenate([w_bd, qg_bd], axis=0))
            kgt = kgt_ref[b, ci * GDN_DK:(ci + 1) * GDN_DK, :]
            kgt_bd.append(jnp.where(tall_mask, jnp.concatenate([kgt] * GDN_HEADS, axis=0), 0.0))
            gl = gl_ref[b, ci * SUBLANES:(ci + 1) * SUBLANES, :]
            gl_rows.append(jnp.concatenate(
                [jnp.broadcast_to(gl[h:h + 1], (GDN_DK, GDN_DV)) for h in range(GDN_HEADS)], axis=0))
        t1 = [_dot(lhs[b], s[b].astype(BF16)) for b in seqs]
        vn16 = [(u_ref[b, rows, :] - t1[b][0:st]).astype(BF16) for b in seqs]
        o = [t1[b][st:] + _dot(aqk_ref[b, rows, :], vn16[b]) for b in seqs]
        s = [s[b] * gl_rows[b] + _dot(kgt_bd[b], vn16[b]) for b in seqs]
        for b in seqs:
            for h in range(GDN_HEADS):
                oh = o[b][h * ck:(h + 1) * ck]
                on = oh * lax.rsqrt(jnp.mean(oh * oh, axis=-1, keepdims=True) + EPS) * nw_ref[...]
                hs = slice(h * GDN_DV, (h + 1) * GDN_DV)
                o_ref[b, toks, hs] = (on * _silu(z_ref[b, toks, hs])).astype(BF16)

    for b in seqs:
        s_ref[b] = s[b]
    sout_ref[...] = s_ref[...]


def _gdn_scan(u, w, qg, kgt, aqk, gl, z, s0, nw, *, batch, seq, tb):
    nblk = seq // tb
    ncb = tb // GDN_CHUNK
    cps = seq // GDN_CHUNK
    sw = GDN_HEADS * GDN_DK
    r3 = lambda a, rows_per_chunk: a.reshape(batch, cps * rows_per_chunk, a.shape[-1])
    blk = lambda rows, width: pl.BlockSpec((batch, rows, width), lambda j: (0, j, 0))
    full = pl.BlockSpec((batch, sw, GDN_DV), lambda j: (0, 0, 0))
    return pl.pallas_call(
        functools.partial(_gdn_scan_kernel, batch=batch, ncb=ncb),
        grid=(nblk,),
        in_specs=[blk(ncb * GDN_STACK, GDN_DV), blk(ncb * GDN_STACK, GDN_DK), blk(ncb * GDN_STACK, GDN_DK),
                  blk(ncb * GDN_DK, GDN_STACK), blk(ncb * GDN_STACK, GDN_STACK), blk(ncb * SUBLANES, LANES),
                  blk(tb, GDN_V_W), full, _const_spec(nw.shape)],
        out_specs=(blk(tb, GDN_V_W), full),
        out_shape=(jax.ShapeDtypeStruct((batch, seq, GDN_V_W), BF16),
                   jax.ShapeDtypeStruct((batch, sw, GDN_DV), F32)),
        scratch_shapes=[pltpu.VMEM((batch, sw, GDN_DV), F32)],
        compiler_params=pltpu.CompilerParams(dimension_semantics=("arbitrary",), vmem_limit_bytes=VMEM_LIMIT),
        name="gdn_scan",
    )(r3(u, GDN_STACK), r3(w, GDN_STACK), r3(qg, GDN_STACK), r3(kgt, GDN_DK), r3(aqk, GDN_STACK),
      r3(gl, SUBLANES), z.reshape(batch, seq, GDN_V_W), s0.reshape(batch, sw, GDN_DV), nw)


def _rope_tables(pos):
    half = HEAD_DIM // 2
    inv = np.power(ROPE_THETA, -np.arange(half, dtype=np.float64) * 2.0 / HEAD_DIM)
    ang = np.asarray(pos, np.float64)[:, None] * inv[None, :]
    cos, sin = np.cos(ang).astype(np.float32), np.sin(ang).astype(np.float32)
    return tuple(jnp.asarray(t) for t in (np.tile(cos, (1, 4)), np.concatenate([-sin, sin, -sin, sin], axis=1),
                                          cos.T, sin.T))


def _cmp_to_sel_t(n_cmp, n_sel, rows, cols):
    cs = np.arange(cols)[None, :] * CMP_STRIDE
    ss = np.arange(rows)[:, None] * SEL_LEN
    hit = (cs < ss + SEL_LEN) & (cs + CMP_LEN > ss)
    hit &= (np.arange(cols)[None, :] < n_cmp) & (np.arange(rows)[:, None] < n_sel)
    return jnp.asarray(hit, BF16)


def _layer_weights(w_in, cmp_pe, cmp_w, conv_w, a_log, dt_bias, gdn_norm, w_out):
    cuts = np.cumsum([NSA_Q_W, 6 * NSA_KV_W, GATE_COLS, GDN_CONV_CH, GDN_V_W, GDN_HEADS]).tolist()
    wq, wkv, wgt, wconv, wz, wa, wb = jnp.split(w_in, cuts, axis=1)
    order = np.array([g * NSA_REP + r for r in range(NSA_REP) for g in range(NSA_KV_HEADS)])
    cols = (order[:, None] * HEAD_DIM + np.arange(HEAD_DIM)[None, :]).reshape(-1)
    wm = jnp.concatenate([wq[:, cols], wkv, wconv, wz], axis=1).astype(BF16)
    wkvt = wkv.T.astype(BF16)
    ws = jnp.concatenate([wgt, wa, wb], axis=1)
    ws = jnp.pad(ws, ((0, 0), (0, LANES - ws.shape[1]))).astype(BF16)
    wo = jnp.concatenate([w_out[:NSA_Q_W][cols], w_out[NSA_Q_W:]], axis=0).astype(BF16)

    def blockdiag(l0):
        wk, wv = cmp_w[0, l0:l0 + CMP_STRIDE], cmp_w[1, l0:l0 + CMP_STRIDE]
        z = jnp.zeros_like(wk)
        rows = [jnp.concatenate(r, axis=2) for r in ([wk, z, z, z], [z, wk, z, z], [z, z, wv, z], [z, z, z, wv])]
        return jnp.concatenate(rows, axis=1).astype(BF16)

    def long_contraction(c):
        halves = []
        for l0 in (0, CMP_STRIDE):
            w = cmp_w[c, l0:l0 + CMP_STRIDE]
            z = jnp.zeros_like(w)
            halves.append(jnp.concatenate([jnp.concatenate([w, z], axis=2), jnp.concatenate([z, w], axis=2)], axis=1))
        return jnp.concatenate(halves, axis=2).reshape(CMP_STRIDE * LANES, 2 * LANES).astype(BF16)

    def pe_rows(l0):
        pk, pv = cmp_pe[0, l0:l0 + CMP_STRIDE], cmp_pe[1, l0:l0 + CMP_STRIDE]
        return jnp.concatenate([pk, pk, pv, pv], axis=1)

    pcol = jnp.zeros((2, LANES), F32).at[0, A_COL:A_COL + GDN_HEADS].set(a_log)
    pcol = pcol.at[1, A_COL:A_COL + GDN_HEADS].set(dt_bias)
    prow = jnp.zeros((SUBLANES, 2), F32).at[0:GDN_HEADS, 0].set(a_log).at[0:GDN_HEADS, 1].set(dt_bias)
    return dict(wm=wm, ws=ws, wkvt=wkvt, wo=wo, wk_long=long_contraction(0), wv_long=long_contraction(1),
                wlo=blockdiag(0), whi=blockdiag(CMP_STRIDE), pelo=pe_rows(0),
                pehi=pe_rows(CMP_STRIDE), cw=conv_w, pcol=pcol, prow=prow, nw=gdn_norm[None, :])


def kernel(x_prompt, x_sample, cache_nsa_kv, cache_nsa_win, state_gdn_S, state_gdn_conv, page_table, norm_mix, w_in,
           nsa_cmp_pe, nsa_cmp_w, gdn_conv_w, gdn_a_log, gdn_dt_bias, gdn_norm, w_out, norm_ffn, w_gate_up, w_down,
           norm_final):
    bp, tp, _ = x_prompt.shape
    bs, ts, _ = x_sample.shape
    depth = w_in.shape[0]
    n_pages = page_table.shape[1]
    past = n_pages * PAGE_SIZE
    assert depth == 1 and ts == 1, "one layer and one new token per sample row"
    assert tp % 512 == 0 and tp // SEL_LEN <= LANES and past % 512 == 0
    l = 0
    wts = _layer_weights(w_in[l], nsa_cmp_pe[l], nsa_cmp_w[l], gdn_conv_w[l], gdn_a_log[l], gdn_dt_bias[l],
                         gdn_norm[l], w_out[l])
    g_mix, g_ffn, g_fin = norm_mix[l][None, :], norm_ffn[l][None, :], norm_final[None, :]
    wgu, wd = w_gate_up[l].astype(BF16), w_down[l].astype(BF16)
    hist = CONV_W - 1

    xp = x_prompt.reshape(bp * tp, D_MODEL)
    q, kv, conv_in, z, small, kaug, vaug, kwin, vwaug, kv4t_p, kvwt_p, act = _proj(
        xp, g_mix, wts["wm"], wts["ws"], wts["wkvt"], wts["cw"], jnp.zeros((bp, hist, GDN_CONV_CH), F32),
        *_rope_tables(np.arange(tp)), tm=256)
    ckv = _compress(kv, wts["wlo"], wts["whi"], wts["pelo"], wts["pehi"], batch=bp, seq=tp)
    nc = tp // CMP_STRIDE
    mt_p = _cmp_to_sel_t(nc - 1, tp // SEL_LEN, LANES, nc)
    o_nsa = _nsa_prompt(q, small, kaug, vaug, kwin, vwaug, ckv, mt_p, batch=bp, seq=tp, kc=512)
    smallt = small[:, A_COL:A_COL + SUBLANES].reshape(bp, tp, SUBLANES).transpose(0, 2, 1)
    prep = _gdn_prep(act, small, smallt, wts["pcol"], wts["prow"], batch=bp, seq=tp, tb=256)
    o_gdn, s_p = _gdn_scan(*prep, z, jnp.zeros((bp, GDN_HEADS, GDN_DK, GDN_DV), F32), wts["nw"],
                           batch=bp, seq=tp, tb=256)
    o_gdn = o_gdn.reshape(bp * tp, GDN_V_W)
    s_p = s_p.reshape(bp, GDN_HEADS, GDN_DK, GDN_DV)
    y_p = _mix_ffn(xp, o_nsa, o_gdn, wts["wo"], g_ffn, wgu, wd, g_fin, tm=256)
    tok_major = lambda a, comps: a.reshape(bp, comps, NSA_KV_HEADS, HEAD_DIM, -1).transpose(0, 4, 1, 2, 3)
    kv_p = tok_major(kv4t_p, 4)
    win_p = tok_major(kvwt_p[:, :, tp - min(WINDOW, tp):], 2)
    conv_p = conv_in.reshape(bp, tp, GDN_CONV_CH)[:, tp - hist:]

    xs = x_sample.reshape(bs, D_MODEL)
    q, kv, conv_in, z, small = _proj(xs, g_mix, wts["wm"], wts["ws"], wts["wkvt"], wts["cw"],
                                     jnp.zeros((1, hist, GDN_CONV_CH), F32),
                                     *_rope_tables(np.full((bs,), past)), tm=bs)[:5]
    n_sel = past // SEL_LEN + 1
    nsp = -(-n_sel // LANES) * LANES
    mt_s = _cmp_to_sel_t(past // CMP_STRIDE - 1, n_sel, nsp, past // CMP_STRIDE)
    cache = cache_nsa_kv[l].transpose(0, 2, 3, 4, 1).reshape(-1, 4 * NSA_KV_W, PAGE_SIZE)
    win = cache_nsa_win[l].transpose(0, 2, 3, 4, 1).reshape(bs, 2 * NSA_KV_W, -1)
    o_nsa = _nsa_sample(page_table, cache, q[:, None], kv[:, None], small[:, None], win, wts["wk_long"],
                        wts["wv_long"], wts["pelo"], wts["pehi"], mt_s, pps=min(32, n_pages)).reshape(bs, NSA_Q_W)
    o_gdn, s_s = _gdn_step(conv_in, z, small, state_gdn_conv[l], state_gdn_S[l], wts["cw"], wts["pcol"], wts["nw"])
    o_gdn = o_gdn.reshape(bs, GDN_V_W)
    y_s = _mix_ffn(xs, o_nsa, o_gdn, wts["wo"], g_ffn, wgu, wd, g_fin, tm=bs)
    kv_s = kv.reshape(bs, 1, 3, 2, NSA_KV_HEADS, HEAD_DIM)
    win_s = jnp.concatenate([cache_nsa_win[l], kv_s[:, :, 2]], axis=1)[:, -min(WINDOW, past + 1):]
    conv_s = jnp.concatenate([state_gdn_conv[l], conv_in[:, None]], axis=1)[:, -hist:]

    return (y_p.reshape(bp, tp, D_MODEL), y_s.reshape(bs, 1, D_MODEL),
            kv_p[None], win_p[None], s_p[None], conv_p[None],
            kv_s[:, :, 0:2].reshape(bs, 1, 4, NSA_KV_HEADS, HEAD_DIM)[None], win_s[None], s_s[None], conv_s[None])
```

```python
import functools

import numpy as np
import jax
import jax.numpy as jnp
from jax import lax
from jax.experimental import pallas as pl
from jax.experimental.pallas import tpu as pltpu

F32 = jnp.float32
BF16 = jnp.bfloat16

D_MODEL = 1024
PAGE_SIZE = 128
HEAD_DIM = 64
NSA_HEADS = 8
NSA_KV_HEADS = 2
NSA_REP = NSA_HEADS // NSA_KV_HEADS
CMP_LEN = 32
CMP_STRIDE = 16
SEL_LEN = 64
SEL_TOPK = 16
WINDOW = 512
Q_BLOCK = 128
ROPE_THETA = 10000.0
GDN_HEADS = 4
GDN_DK = 128
GDN_DV = 128
CONV_W = 4
NSA_Q_W = NSA_HEADS * HEAD_DIM
NSA_KV_W = NSA_KV_HEADS * HEAD_DIM
GDN_QK_W = GDN_HEADS * GDN_DK
GDN_V_W = GDN_HEADS * GDN_DV
GDN_CONV_CH = 2 * GDN_QK_W + GDN_V_W
NEG = -1e30
SEL_UNROLL = 4
LOG2E = 1.4426950408889634
SEL_BONUS = 1e4
EPS = 1e-6

LANES = 128
SUBLANES = 8
VMEM_LIMIT = 56 * 1024 * 1024

GATE_COLS = 3 * NSA_HEADS
A_COL = GATE_COLS
B_COL = GATE_COLS + GDN_HEADS

NT_DIMS = (((1,), (1,)), ((), ()))
TN_DIMS = (((0,), (0,)), ((), ()))


def _dot(a, b):
    return jnp.dot(a, b, preferred_element_type=F32)


def _dot_nt(a, b):
    return lax.dot_general(a, b, NT_DIMS, preferred_element_type=F32)


def _sigmoid(x):
    return 1.0 / (1.0 + jnp.exp(-x))


def _silu(x):
    return x * _sigmoid(x)


def _split3(x):
    p1 = x.astype(BF16)
    r1 = x - p1.astype(F32)
    p2 = r1.astype(BF16)
    p3 = (r1 - p2.astype(F32)).astype(BF16)
    return p1, p2, p3


def _const_spec(shape):
    nd = len(shape)
    return pl.BlockSpec(shape, lambda *_: (0,) * nd, pipeline_mode=pl.Buffered(1))


def _proj_kernel(x_ref, g_ref, wm_ref, ws_ref, wkvt_ref, cos_ref, sin_ref, cost_ref, sint_ref, cw_ref, buf_ref,
                 q_ref, kv_ref, conv_ref, z_ref, small_ref, kaug_ref, vaug_ref, kwin_ref, vwaug_ref,
                 kv4t_ref, kvwt_ref, act_ref, xs_ref, *, tm, pos_rows):
    pad = SUBLANES
    hist = CONV_W - 1
    c0 = NSA_Q_W + 6 * NSA_KV_W

    @pl.when(pl.program_id(0) % (pos_rows // tm) == 0)
    def _():
        xs_ref[0:pad, :] = jnp.zeros((pad, GDN_CONV_CH), F32)
        xs_ref[pad - hist:pad, :] = buf_ref[...]

    x = x_ref[...]
    ms = jnp.mean(x * x, axis=-1, keepdims=True)
    xn = (x * lax.rsqrt(ms + EPS) * g_ref[...]).astype(BF16)
    cos = cos_ref[...]
    sin = sin_ref[...]
    lane = lax.broadcasted_iota(jnp.int32, (tm, LANES), 1)
    low_half = (lane % HEAD_DIM) < (HEAD_DIM // 2)

    def rope(v):
        rot = jnp.where(low_half, pltpu.roll(v, LANES - HEAD_DIM // 2, 1), pltpu.roll(v, HEAD_DIM // 2, 1))
        return v * cos + rot * sin

    def delta_front_end(part):
        cols = slice(part * GDN_QK_W, (part + 1) * GDN_QK_W)
        conv_in = _dot(xn, wm_ref[:, c0 + part * GDN_QK_W:c0 + (part + 1) * GDN_QK_W])
        conv_ref[:, cols] = conv_in
        xs_ref[pad:pad + tm, cols] = conv_in
        xs = xs_ref[:, cols]
        conv = xs[pad:] * cw_ref[hist:CONV_W, cols]
        for t in range(hist):
            conv = conv + pltpu.roll(xs, hist - t, 0)[pad:] * cw_ref[t:t + 1, cols]
        xs_ref[0:pad, cols] = xs[tm:tm + pad]
        act = _silu(conv)
        for h in range(GDN_HEADS):
            hs = slice(h * GDN_DK, (h + 1) * GDN_DK)
            dst = act_ref.at[:, part * GDN_QK_W + h * GDN_DK:part * GDN_QK_W + (h + 1) * GDN_DK]
            if part == 0:
                dst[...] = _l2norm(act[:, hs]) * (GDN_DK ** -0.5)
            elif part == 1:
                dst[...] = _l2norm(act[:, hs])
            else:
                dst[...] = act[:, hs]

    delta_front_end(0)
    q = _dot(xn, wm_ref[:, 0:NSA_Q_W])
    for r in range(NSA_REP):
        sl = slice(r * LANES, (r + 1) * LANES)
        q_ref[:, sl] = (rope(q[:, sl]) * (HEAD_DIM ** -0.5 * LOG2E)).astype(BF16)

    delta_front_end(1)
    kv = _dot(xn, wm_ref[:, NSA_Q_W:NSA_Q_W + 6 * NSA_KV_W])
    group0 = lane < HEAD_DIM
    for c in range(6):
        sl = slice(c * LANES, (c + 1) * LANES)
        blk = kv[:, sl]
        if c % 2 == 0:
            blk = rope(blk)
        kv_ref[:, sl] = blk
        if c == 2:
            kaug_ref[:, LANES:2 * LANES] = blk.astype(BF16)
        elif c == 4:
            kwin_ref[...] = blk.astype(BF16)
        elif c in (3, 5):
            v_ref = vaug_ref if c == 3 else vwaug_ref
            v_ref[:, 0:LANES] = jnp.where(group0, blk, 1.0).astype(BF16)
            v_ref[:, LANES:2 * LANES] = jnp.where(group0, 1.0, blk).astype(BF16)

    row0 = (pl.program_id(0) * tm) % pos_rows
    rows = row0 + lax.broadcasted_iota(jnp.int32, (tm, LANES), 0)
    kaug_ref[:, 0:LANES] = jnp.where(rows // SEL_LEN == lane, 1.0, 0.0).astype(BF16)

    delta_front_end(2)
    kvt = _dot_nt(wkvt_ref[...], xn)
    cos_t = cost_ref[...]
    sin_t = sint_ref[...]
    half = HEAD_DIM // 2
    for c in range(6):
        blk = kvt[c * LANES:(c + 1) * LANES]
        if c % 2 == 0:
            parts = []
            for g in range(NSA_KV_HEADS):
                x1 = blk[g * HEAD_DIM:g * HEAD_DIM + half]
                x2 = blk[g * HEAD_DIM + half:(g + 1) * HEAD_DIM]
                parts += [x1 * cos_t - x2 * sin_t, x2 * cos_t + x1 * sin_t]
            blk = jnp.concatenate(parts, axis=0)
        dst = kv4t_ref if c < 4 else kvwt_ref
        dst[(c % 4) * LANES:(c % 4 + 1) * LANES, :] = blk

    z_ref[...] = _dot(xn, wm_ref[:, c0 + GDN_CONV_CH:c0 + GDN_CONV_CH + GDN_V_W])
    small_ref[...] = _dot(xn, ws_ref[...])


def _proj(x, g, wm, ws, wkvt, cw, buf, cos, sin, cos_t, sin_t, *, tm):
    rows = x.shape[0]
    pos_rows = cos.shape[0]
    n_pos_blk = pos_rows // tm
    n_seq = rows // pos_rows
    grid = (rows // tm,)
    row_spec = lambda w: pl.BlockSpec((tm, w), lambda i: (i, 0))
    pos_spec = pl.BlockSpec((tm, LANES), lambda i: (i % n_pos_blk, 0))
    pos_t_spec = pl.BlockSpec((HEAD_DIM // 2, tm), lambda i: (0, i % n_pos_blk))
    tok_minor = lambda r: pl.BlockSpec((None, r, tm), lambda i: (i // n_pos_blk, 0, i % n_pos_blk))
    out_shape = (
        jax.ShapeDtypeStruct((rows, NSA_Q_W), BF16),
        jax.ShapeDtypeStruct((rows, 6 * NSA_KV_W), F32),
        jax.ShapeDtypeStruct((rows, GDN_CONV_CH), F32),
        jax.ShapeDtypeStruct((rows, GDN_V_W), F32),
        jax.ShapeDtypeStruct((rows, LANES), F32),
        jax.ShapeDtypeStruct((rows, 2 * LANES), BF16),
        jax.ShapeDtypeStruct((rows, 2 * LANES), BF16),
        jax.ShapeDtypeStruct((rows, LANES), BF16),
        jax.ShapeDtypeStruct((rows, 2 * LANES), BF16),
    )
    out_shape_t = (jax.ShapeDtypeStruct((n_seq, 4 * NSA_KV_W, pos_rows), F32),
                   jax.ShapeDtypeStruct((n_seq, 2 * NSA_KV_W, pos_rows), F32),
                   jax.ShapeDtypeStruct((rows, GDN_CONV_CH), F32))
    return pl.pallas_call(
        functools.partial(_proj_kernel, tm=tm, pos_rows=pos_rows),
        grid=grid,
        in_specs=[row_spec(D_MODEL), _const_spec((1, D_MODEL)), _const_spec(wm.shape), _const_spec(ws.shape),
                  _const_spec(wkvt.shape), pos_spec, pos_spec, pos_t_spec, pos_t_spec, _const_spec(cw.shape),
                  pl.BlockSpec((None, CONV_W - 1, GDN_CONV_CH), lambda i: (i // n_pos_blk, 0, 0))],
        out_specs=(tuple(row_spec(s.shape[1]) for s in out_shape)
                   + (tok_minor(4 * NSA_KV_W), tok_minor(2 * NSA_KV_W), row_spec(GDN_CONV_CH))),
        out_shape=out_shape + out_shape_t,
        scratch_shapes=[pltpu.VMEM((SUBLANES + tm, GDN_CONV_CH), F32)],
        compiler_params=pltpu.CompilerParams(dimension_semantics=("arbitrary",), vmem_limit_bytes=VMEM_LIMIT),
        name="proj",
    )(x, g, wm, ws, wkvt, cos, sin, cos_t, sin_t, cw, buf)


def _mix_ffn_kernel(x_ref, on_ref, og_ref, wo_ref, gf_ref, wgu_ref, wd_ref, gl_ref, y_ref, *, d_ff):
    h = x_ref[...] + _dot(on_ref[...], wo_ref[0:NSA_Q_W, :]) + _dot(og_ref[...], wo_ref[NSA_Q_W:, :])
    ms = jnp.mean(h * h, axis=-1, keepdims=True)
    hn = (h * lax.rsqrt(ms + EPS) * gf_ref[...]).astype(BF16)
    gate = _dot(hn, wgu_ref[:, 0:d_ff])
    up = _dot(hn, wgu_ref[:, d_ff:])
    act = (_silu(gate) * up).astype(BF16)
    h = h + _dot(act, wd_ref[...])
    ms = jnp.mean(h * h, axis=-1, keepdims=True)
    y_ref[...] = h * lax.rsqrt(ms + EPS) * gl_ref[...]


def _mix_ffn(x, o_nsa, o_gdn, wo, gf, wgu, wd, gl, *, tm):
    rows = x.shape[0]
    d_ff = wd.shape[0]
    row_spec = lambda w: pl.BlockSpec((tm, w), lambda i: (i, 0))
    return pl.pallas_call(
        functools.partial(_mix_ffn_kernel, d_ff=d_ff),
        grid=(rows // tm,),
        in_specs=[row_spec(D_MODEL), row_spec(NSA_Q_W), row_spec(GDN_V_W), _const_spec(wo.shape),
                  _const_spec((1, D_MODEL)), _const_spec(wgu.shape), _const_spec(wd.shape),
                  _const_spec((1, D_MODEL))],
        out_specs=row_spec(D_MODEL),
        out_shape=jax.ShapeDtypeStruct((rows, D_MODEL), F32),
        compiler_params=pltpu.CompilerParams(dimension_semantics=("arbitrary",), vmem_limit_bytes=VMEM_LIMIT),
        name="mix_ffn",
    )(x, o_nsa, o_gdn, wo, gf, wgu, wd, gl)


def _compress_rows(load_rows, wlo_ref, whi_ref, pelo_ref, pehi_ref, nc):
    acc_lo = jnp.zeros((nc, 2 * LANES), F32)
    acc_hi = jnp.zeros((nc, 2 * LANES), F32)
    for l in range(CMP_STRIDE):
        x = load_rows(l)
        acc_lo = acc_lo + _dot((x + pelo_ref[l:l + 1, :]).astype(BF16), wlo_ref[l])
        acc_hi = acc_hi + _dot((x + pehi_ref[l:l + 1, :]).astype(BF16), whi_ref[l])
    nxt = pltpu.roll(acc_hi, nc - 1, 0)
    row = lax.broadcasted_iota(jnp.int32, (nc, 2 * LANES), 0)
    return jnp.where(row < nc - 1, acc_lo + nxt, 0.0)


def _compress_kernel(k_ref, v_ref, wlo_ref, whi_ref, pelo_ref, pehi_ref, ckv_ref, *, nc):
    load = lambda l: jnp.concatenate([k_ref[pl.ds(l, nc, stride=CMP_STRIDE), :],
                                      v_ref[pl.ds(l, nc, stride=CMP_STRIDE), :]], axis=1)
    ckv_ref[...] = _compress_rows(load, wlo_ref, whi_ref, pelo_ref, pehi_ref, nc).astype(BF16)


def _compress(kv, wlo, whi, pelo, pehi, *, batch, seq):
    nc = seq // CMP_STRIDE
    return pl.pallas_call(
        functools.partial(_compress_kernel, nc=nc),
        grid=(batch,),
        in_specs=[pl.BlockSpec((seq, LANES), lambda b: (b, 0)), pl.BlockSpec((seq, LANES), lambda b: (b, 1)),
                  _const_spec(wlo.shape), _const_spec(whi.shape), _const_spec(pelo.shape), _const_spec(pehi.shape)],
        out_specs=pl.BlockSpec((nc, 2 * LANES), lambda b: (b, 0)),
        out_shape=jax.ShapeDtypeStruct((batch * nc, 2 * LANES), BF16),
        compiler_params=pltpu.CompilerParams(dimension_semantics=("arbitrary",), vmem_limit_bytes=VMEM_LIMIT),
        name="compress",
    )(kv, kv, wlo, whi, pelo, pehi)


def _masked_softmax2(s):
    m = jnp.maximum(jnp.max(s, axis=-1, keepdims=True), NEG / 8)
    e = jnp.exp2(s - m)
    return e * (1.0 / jnp.maximum(jnp.sum(e, axis=-1, keepdims=True), 1e-30))


def _select_blocks(score, sel, blk, axis, rounds, filler=iter(())):
    n = score.shape[axis]
    for _ in range(rounds):
        mx = jnp.max(score, axis=axis, keepdims=True)
        idx = jnp.min(jnp.where(score == mx, blk, n), axis=axis, keepdims=True)
        hit = blk == idx
        sel = jnp.where(hit, jnp.where(mx > NEG / 2, 1.0, sel), sel)
        score = jnp.where(hit, -3e38, score)
        next(filler, None)
    for _ in filler:
        pass
    return sel


def _nsa_prompt_kernel(q_ref, small_ref, kaug_ref, vaug_ref, kwin_ref, vwaug_ref, ckv_ref, mt_ref, o_ref,
                       *, seq, kc):
    nc = seq // CMP_STRIDE
    qb = Q_BLOCK
    start = pl.program_id(1) * qb
    wlen = WINDOW + qb
    groups = range(NSA_KV_HEADS)
    stack = lambda x: jnp.concatenate([x] * NSA_REP, axis=0)

    lane = lax.broadcasted_iota(jnp.int32, (qb, LANES), 1)
    gates = _sigmoid(small_ref[...])

    qrow_c = lax.broadcasted_iota(jnp.int32, (qb, nc), 0)
    ncol = lax.broadcasted_iota(jnp.int32, (qb, nc), 1)
    cmp_bias = stack(jnp.where(ncol * CMP_STRIDE + (CMP_LEN - 1) - qrow_c <= start, 0.0, NEG))
    tail_bias = stack(jnp.where(lane <= lax.broadcasted_iota(jnp.int32, (qb, qb), 0), 0.0, NEG))
    wbase = pl.multiple_of(jnp.maximum(start - WINDOW, 0), qb)
    back = (lax.broadcasted_iota(jnp.int32, (qb, wlen), 0) - lax.broadcasted_iota(jnp.int32, (qb, wlen), 1)
            + (start - wbase))
    win_bias = stack(jnp.where(back >= 0, jnp.where(back <= WINDOW, 0.0, NEG), NEG))

    glanes = [slice(g * LANES, (g + 1) * LANES) for g in groups]
    qs = [jnp.concatenate([jnp.where((lane >= HEAD_DIM) == (g == 1), q_ref[:, r * LANES:(r + 1) * LANES], 0.0)
                           .astype(BF16) for r in range(NSA_REP)], axis=0) for g in groups]
    ps = [_masked_softmax2(_dot_nt(q, ckv_ref[:, 0:LANES]) + cmp_bias) for q in qs]
    psums = [p[0:qb] + p[qb:2 * qb] + p[2 * qb:3 * qb] + p[3 * qb:4 * qb] for p in ps]
    imp_t = jnp.concatenate([sum(_dot_nt(mt_ref[...], piece) for piece in _split3(psum)) for psum in psums], axis=1)
    o_cmps = [_dot(p.astype(BF16), ckv_ref[:, LANES:2 * LANES]) for p in ps]

    done = {}

    def independent_work():
        s = [_dot_nt(q, kwin_ref[pl.ds(wbase, wlen), :]) + win_bias for q in qs]
        yield
        mx = [jnp.max(x, axis=-1, keepdims=True) for x in s]
        yield
        e = []
        for g in groups:
            e.append(jnp.exp2(s[g] - mx[g]).astype(BF16))
            yield
        acc = [_dot(e[g], vwaug_ref[pl.ds(wbase, wlen), glanes[g]]) for g in groups]
        yield
        o_win = [a / pltpu.roll(a, HEAD_DIM, 1) for a in acc]
        yield
        rows_of = lambda x, r: x[r * qb:(r + 1) * qb]
        done["partial"] = []
        for g in groups:
            done["partial"].append(
                [gates[:, 3 * (g * NSA_REP + r):3 * (g * NSA_REP + r) + 1] * rows_of(o_cmps[g], r)
                 + gates[:, 3 * (g * NSA_REP + r) + 2:3 * (g * NSA_REP + r) + 3] * rows_of(o_win[g], r)
                 for r in range(NSA_REP)])
            yield
        s_tail = [_dot_nt(q, kaug_ref[pl.ds(start, qb), LANES:2 * LANES]) + tail_bias for q in qs]
        yield
        done["m0"] = [jnp.max(x, axis=-1, keepdims=True) for x in s_tail]
        yield
        done["acc0"] = [_dot(jnp.exp2(s_tail[g] - done["m0"][g]).astype(BF16), vaug_ref[pl.ds(start, qb), glanes[g]])
                        for g in groups]
        yield

    blk_t = lax.broadcasted_iota(jnp.int32, (LANES, 2 * qb), 0)
    qpos_t = start + lax.broadcasted_iota(jnp.int32, (LANES, 2 * qb), 1) % qb
    cur_t = qpos_t // SEL_LEN
    visible = blk_t * SEL_LEN <= qpos_t
    forced = (blk_t == 0) | (blk_t == cur_t) | (blk_t == cur_t - 1)
    sel_t = _select_blocks(jnp.where(visible, jnp.where(forced, NEG, imp_t), NEG),
                           jnp.where(visible, jnp.where(forced, 1.0, 0.0), 0.0), blk_t, 0, SEL_TOPK - 3,
                           independent_work())
    partial, m0s, acc0s = done["partial"], done["m0"], done["acc0"]
    bias_t = jnp.where(blk_t * SEL_LEN < start, jnp.where(sel_t > 0.5, 0.0, NEG), NEG).T.astype(BF16)
    n_chunks = (start + kc - 1) // kc
    q_augs = [jnp.concatenate([stack(bias_t[g * qb:(g + 1) * qb]), qs[g]], axis=1) for g in groups]

    def sel_step(c, carry):
        off = pl.multiple_of(c * kc, kc)
        k = kaug_ref[pl.ds(off, kc), :]
        out = []
        for g in groups:
            m, acc = carry[g]
            s = _dot_nt(q_augs[g], k)
            m_new = jnp.maximum(m, jnp.max(s, axis=-1, keepdims=True))
            e = jnp.exp2(s - m_new).astype(BF16)
            out.append((m_new, jnp.exp2(m - m_new) * acc + _dot(e, vaug_ref[pl.ds(off, kc), g * LANES:(g + 1) * LANES])))
        return tuple(out)

    def sel_steps(first, count, x):
        for u in range(count):
            x = sel_step(first + u, x)
        return x

    sel_out = lax.fori_loop(0, n_chunks // SEL_UNROLL, lambda c, x: sel_steps(SEL_UNROLL * c, SEL_UNROLL, x),
                            tuple(zip(m0s, acc0s)))
    done_chunks = n_chunks // SEL_UNROLL * SEL_UNROLL
    count = SEL_UNROLL // 2
    while count >= 1:
        take = (n_chunks - done_chunks) >= count
        sel_out = lax.cond(take, functools.partial(sel_steps, done_chunks, count), lambda x: x, sel_out)
        done_chunks = done_chunks + jnp.where(take, count, 0)
        count //= 2

    outs = [None] * NSA_REP
    for g in groups:
        acc = sel_out[g][1]
        o_sel = acc / pltpu.roll(acc, HEAD_DIM, 1)
        for r in range(NSA_REP):
            c0 = (g * NSA_REP + r) * 3
            o = partial[g][r] + gates[:, c0 + 1:c0 + 2] * o_sel[r * qb:(r + 1) * qb]
            outs[r] = o if g == 0 else jnp.where(lane < HEAD_DIM, outs[r], o)

    for r in range(NSA_REP):
        o_ref[:, r * LANES:(r + 1) * LANES] = outs[r].astype(BF16)


def _nsa_prompt(q, small, kaug, vaug, kwin, vwaug, ckv, mt, *, batch, seq, kc):
    nb = seq // Q_BLOCK
    nc = seq // CMP_STRIDE
    blk_spec = lambda w: pl.BlockSpec((Q_BLOCK, w), lambda b, i: (b * nb + i, 0))
    seq_spec = lambda w: pl.BlockSpec((seq, w), lambda b, i: (b, 0))
    return pl.pallas_call(
        functools.partial(_nsa_prompt_kernel, seq=seq, kc=kc),
        grid=(batch, nb),
        in_specs=[blk_spec(NSA_Q_W), blk_spec(LANES), seq_spec(2 * LANES), seq_spec(2 * LANES), seq_spec(LANES),
                  seq_spec(2 * LANES), pl.BlockSpec((nc, 2 * LANES), lambda b, i: (b, 0)), _const_spec(mt.shape)],
        out_specs=blk_spec(NSA_Q_W),
        out_shape=jax.ShapeDtypeStruct((batch * seq, NSA_Q_W), BF16),
        compiler_params=pltpu.CompilerParams(dimension_semantics=("arbitrary", "arbitrary"),
                                             vmem_limit_bytes=VMEM_LIMIT),
        name="nsa_prompt",
    )(q, small, kaug, vaug, kwin, vwaug, ckv, mt)


def _nsa_sample_kernel(pt_ref, *refs, past, pps):
    del pt_ref
    pages = refs[:pps]
    (q_ref, kvn_ref, small_ref, win_ref, perm_ref, wk_ref, wv_ref, pelo_ref, pehi_ref, mt_ref, exp_ref,
     o_ref, xs_ref, s_ref, vs_ref) = refs[pps:]
    step = pl.program_id(1)
    n_pages = past // PAGE_SIZE
    cpp = PAGE_SIZE // CMP_STRIDE
    heads = NSA_HEADS
    row = lax.broadcasted_iota(jnp.int32, (heads, LANES), 0)
    lane = lax.broadcasted_iota(jnp.int32, (heads, LANES), 1)
    mine = (lane >= HEAD_DIM) == (row >= NSA_REP)
    rr = row % NSA_REP
    qsel = jnp.zeros((heads, LANES), F32)
    for r in range(NSA_REP):
        qsel = jnp.where(rr == r, q_ref[:, r * LANES:(r + 1) * LANES].astype(F32), qsel)
    qf = jnp.where(mine, qsel, 0.0)
    q = qf.astype(BF16)

    nc = past // CMP_STRIDE

    def take_pages(first_page):
        for k in range(pps):
            pg = first_page + k
            page = pages[k]
            tok = slice(pg * PAGE_SIZE, (pg + 1) * PAGE_SIZE)
            s_ref[:, tok] = _dot(q, page[2 * LANES:3 * LANES, :].astype(BF16))
            vs_ref[:, tok] = page[3 * LANES:4 * LANES, :].astype(BF16)
            x = _dot_nt(perm_ref[...], page[0:2 * LANES, :].astype(BF16))
            for l in range(CMP_STRIDE):
                for c in range(2):
                    xs_ref[c, pg * cpp:(pg + 1) * cpp, l * LANES:(l + 1) * LANES] = (
                        x[l * cpp:(l + 1) * cpp, c * LANES:(c + 1) * LANES])

    for st in range(n_pages // pps):
        pl.when(step == st)(functools.partial(take_pages, st * pps))

    @pl.when(step == pl.num_programs(1) - 1)
    def _():
        n_cmp = nc - 1
        nsp = mt_ref.shape[0]
        cur = past // SEL_LEN

        crow = lax.broadcasted_iota(jnp.int32, (nc, LANES), 0)
        ckv = []
        for c, w_ref in enumerate((wk_ref, wv_ref)):
            cl = slice(c * LANES, (c + 1) * LANES)
            for l in range(CMP_STRIDE):
                ll = slice(l * LANES, (l + 1) * LANES)
                xs_ref[c, nc:nc + SUBLANES, ll] = jnp.broadcast_to(pelo_ref[l:l + 1, cl], (SUBLANES, LANES))
                xs_ref[c, nc + SUBLANES:nc + 2 * SUBLANES, ll] = jnp.broadcast_to(pehi_ref[l:l + 1, cl],
                                                                                    (SUBLANES, LANES))
            acc = _dot(xs_ref[c].astype(BF16), w_ref[...])
            bias = acc[nc:nc + 1, 0:LANES] + acc[nc + SUBLANES:nc + SUBLANES + 1, LANES:2 * LANES]
            nxt = pltpu.roll(acc[0:nc, LANES:2 * LANES], nc - 1, 0)
            ckv.append(jnp.where(crow < nc - 1, acc[0:nc, 0:LANES] + nxt + bias, 0.0).astype(BF16))
        ckv = jnp.concatenate(ckv, axis=1)

        ncol = lax.broadcasted_iota(jnp.int32, (1, nc), 1)
        cmp_valid = (ncol * CMP_STRIDE + CMP_LEN - 1 <= past) & (ncol < n_cmp)
        p = _masked_softmax2(jnp.where(cmp_valid, _dot_nt(q, ckv[:, 0:LANES]), NEG))
        o_cmp = _dot(p.astype(BF16), ckv[:, LANES:2 * LANES])

        prow = lax.broadcasted_iota(jnp.int32, (heads, nc), 0)
        ps = [jnp.sum(jnp.where(prow // NSA_REP == g, p, 0.0), axis=0, keepdims=True) for g in range(NSA_KV_HEADS)]
        psum = jnp.where(prow < NSA_REP, ps[0], ps[1])
        imp = sum(_dot_nt(piece, mt_ref[...]) for piece in _split3(psum))
        blk = lax.broadcasted_iota(jnp.int32, (heads, nsp), 1)
        forced = (blk == 0) | (blk == cur) | (blk == cur - 1)
        score = jnp.where(blk * SEL_LEN <= past, imp + jnp.where(forced, SEL_BONUS, 0.0), NEG)
        score_t = score.T
        bi = lax.broadcasted_iota(jnp.int32, (nsp, nsp), 0)
        bj = lax.broadcasted_iota(jnp.int32, (nsp, nsp), 1)
        tie = jnp.where(bi < bj, 1.0, 0.0)
        picks = []
        for g in range(NSA_KV_HEADS):
            c = g * NSA_REP
            s_i, s_j = score_t[:, c:c + 1], score[c:c + 1, :]
            beats = jnp.where(s_i > s_j, 1.0, jnp.where(s_i == s_j, tie, 0.0))
            rank = jnp.sum(beats, axis=0, keepdims=True)
            picks.append(jnp.where(rank < SEL_TOPK, jnp.where(s_j > NEG / 2, 1.0, 0.0), 0.0))
        sel8 = jnp.where(lax.broadcasted_iota(jnp.int32, (heads, nsp), 0) < NSA_REP, picks[0], picks[1])
        keep = _dot(sel8.astype(BF16), exp_ref[...]) > 0.5

        kvn = kvn_ref[...]
        new_ok = sel8[:, cur:cur + 1] > 0.5
        s = jnp.where(keep, s_ref[...], NEG)
        s_new = jnp.where(new_ok, jnp.sum(qf * kvn[:, 2 * LANES:3 * LANES], axis=-1, keepdims=True), NEG)
        m = jnp.maximum(jnp.max(s, axis=-1, keepdims=True), s_new)
        e = jnp.exp2(s - m)
        e_new = jnp.where(new_ok, jnp.exp2(s_new - m), 0.0)
        o_sel = ((_dot_nt(e.astype(BF16), vs_ref[...]) + e_new * kvn[:, 3 * LANES:4 * LANES])
                 / (jnp.sum(e, axis=-1, keepdims=True) + e_new))

        nw = win_ref.shape[0] // 2
        s = _dot(q, win_ref[0:nw, :].astype(BF16))
        s_new = jnp.sum(qf * kvn[:, 4 * LANES:5 * LANES], axis=-1, keepdims=True)
        m = jnp.maximum(jnp.max(s, axis=-1, keepdims=True), s_new)
        e = jnp.exp2(s - m)
        e_new = jnp.exp2(s_new - m)
        l = jnp.sum(e, axis=-1, keepdims=True) + e_new
        o_win = (_dot_nt(e.astype(BF16), win_ref[nw:, :].astype(BF16))
                 + e_new * kvn[:, 5 * LANES:6 * LANES]) / l

        gates = _sigmoid(small_ref[...])
        gate = lambda br: jnp.sum(jnp.where(lane == row * 3 + br, gates, 0.0), axis=-1, keepdims=True)
        o = gate(0) * o_cmp + gate(1) * o_sel + gate(2) * o_win
        for r in range(NSA_REP):
            o_ref[:, r * LANES:(r + 1) * LANES] = jnp.where(
                lane[0:1] < HEAD_DIM, o[r:r + 1], o[NSA_REP + r:NSA_REP + r + 1]).astype(BF16)


def _nsa_sample(page_table, cache, q, kvn, small, win, wlo, whi, pelo, pehi, mt, *, pps):
    batch, n_pages = page_table.shape
    past = n_pages * PAGE_SIZE
    rows = cache.shape[1]
    cpp = PAGE_SIZE // CMP_STRIDE
    nsp = mt.shape[0]
    expand = jnp.asarray(np.arange(nsp)[:, None] == np.arange(past)[None, :] // SEL_LEN, BF16)
    tok = np.arange(PAGE_SIZE)
    perm = jnp.asarray((tok[None, :] % CMP_STRIDE) * cpp + tok[None, :] // CMP_STRIDE == tok[:, None], BF16)

    def page_spec(k):
        return pl.BlockSpec((None, rows, PAGE_SIZE), lambda b, s, pt: (pt[b * n_pages + s * pps + k], 0, 0))

    per_b = lambda shape: pl.BlockSpec((None,) + shape, lambda b, s, pt: (b, 0, 0))
    const = lambda a: pl.BlockSpec(a.shape, lambda b, s, pt: (0,) * a.ndim, pipeline_mode=pl.Buffered(1))
    grid_spec = pltpu.PrefetchScalarGridSpec(
        num_scalar_prefetch=1,
        grid=(batch, n_pages // pps),
        in_specs=[page_spec(k) for k in range(pps)] + [
            per_b((1, NSA_Q_W)), per_b((1, 6 * NSA_KV_W)), per_b((1, LANES)), per_b(win.shape[1:]),
            const(perm), const(wlo), const(whi), const(pelo), const(pehi), const(mt), const(expand)],
        out_specs=per_b((1, NSA_Q_W)),
        scratch_shapes=[pltpu.VMEM((2, past // CMP_STRIDE + 2 * SUBLANES, CMP_STRIDE * LANES), F32),
                        pltpu.VMEM((NSA_HEADS, past), F32),
                        pltpu.VMEM((LANES, past), BF16)],
    )
    return pl.pallas_call(
        functools.partial(_nsa_sample_kernel, past=past, pps=pps),
        grid_spec=grid_spec,
        out_shape=jax.ShapeDtypeStruct((batch, 1, NSA_Q_W), BF16),
        compiler_params=pltpu.CompilerParams(dimension_semantics=("arbitrary", "arbitrary"),
                                             vmem_limit_bytes=VMEM_LIMIT),
        name="nsa_sample",
    )(page_table.reshape(-1), *([cache] * pps), q, kvn, small, win, perm, wlo, whi, pelo, pehi, mt, expand)


def _l2norm(x):
    return x * lax.rsqrt(jnp.sum(x * x, axis=-1, keepdims=True) + EPS)


def _softplus(x):
    return jnp.maximum(x, 0.0) + jnp.log(1.0 + jnp.exp(-jnp.abs(x)))


def _gdn_step_kernel(x_ref, z_ref, small_ref, buf_ref, s0_ref, cw_ref, pcol_ref, nw_ref, o_ref, sout_ref):
    hist = CONV_W - 1
    conv = jnp.sum(buf_ref[...] * cw_ref[0:hist, :], axis=0, keepdims=True) + x_ref[...] * cw_ref[hist:CONV_W, :]
    act = _silu(conv)
    small = small_ref[...]
    g_all = -jnp.exp(pcol_ref[0:1, :]) * _softplus(small + pcol_ref[1:2, :])
    beta_all = _sigmoid(small)
    row = lax.broadcasted_iota(jnp.int32, (SUBLANES, GDN_DK), 0)
    for h in range(GDN_HEADS):
        hs = slice(h * GDN_DK, (h + 1) * GDN_DK)
        qh = _l2norm(act[:, hs]) * (GDN_DK ** -0.5)
        kh = _l2norm(act[:, GDN_QK_W + h * GDN_DK:GDN_QK_W + (h + 1) * GDN_DK])
        vh = act[:, 2 * GDN_QK_W + h * GDN_DV:2 * GDN_QK_W + (h + 1) * GDN_DV]
        eg = jnp.exp(g_all[:, A_COL + h:A_COL + h + 1])
        bt = beta_all[:, B_COL + h:B_COL + h + 1]
        s = s0_ref[h]
        kq = jnp.where(row == 0, kh, jnp.where(row == 1, qh, 0.0)).astype(BF16)
        ks_qs = _dot(kq, s.astype(BF16))
        vn = bt * (vh - eg * ks_qs[0:1])
        o = eg * ks_qs[1:2] + jnp.sum(qh * kh, axis=-1, keepdims=True) * vn
        k8 = jnp.where(row == 0, kh, 0.0).astype(BF16)
        vn8 = jnp.where(row == 0, vn, 0.0).astype(BF16)
        sout_ref[h] = s * eg + lax.dot_general(k8, vn8, TN_DIMS, preferred_element_type=F32)
        on = o * lax.rsqrt(jnp.mean(o * o, axis=-1, keepdims=True) + EPS) * nw_ref[...]
        o_ref[:, hs] = (on * _silu(z_ref[:, hs])).astype(BF16)


def _gdn_step(x, z, small, buf, s0, cw, pcol, nw):
    batch = x.shape[0]
    per_b = lambda shape: pl.BlockSpec((None,) + shape, lambda b: (b,) + (0,) * len(shape))
    state = (GDN_HEADS, GDN_DK, GDN_DV)
    return pl.pallas_call(
        _gdn_step_kernel,
        grid=(batch,),
        in_specs=[per_b((1, GDN_CONV_CH)), per_b((1, GDN_V_W)), per_b((1, LANES)), per_b((CONV_W - 1, GDN_CONV_CH)),
                  per_b(state), _const_spec(cw.shape), _const_spec(pcol.shape), _const_spec(nw.shape)],
        out_specs=(per_b((1, GDN_V_W)), per_b(state)),
        out_shape=(jax.ShapeDtypeStruct((batch, 1, GDN_V_W), BF16),
                   jax.ShapeDtypeStruct((batch,) + state, F32)),
        compiler_params=pltpu.CompilerParams(dimension_semantics=("arbitrary",), vmem_limit_bytes=VMEM_LIMIT),
        name="gdn_step",
    )(x[:, None], z[:, None], small[:, None], buf, s0, cw, pcol, nw)


GDN_CHUNK = 64
GDN_STACK = GDN_HEADS * GDN_CHUNK


def _stack_heads(x, col0, width):
    return jnp.concatenate([x[:, col0 + h * width:col0 + (h + 1) * width] for h in range(GDN_HEADS)], axis=0)


def _gdn_prep_kernel(act_ref, small_ref, smallt_ref, pcol_ref, prow_ref, tril_ref, triu_ref,
                     u_ref, w_ref, qg_ref, kgt_ref, aqk_ref, gl_ref, *, tb):
    ck = GDN_CHUNK
    st = GDN_STACK
    act = act_ref[...]

    small = small_ref[...]
    g_col = -jnp.exp(pcol_ref[0:1, :]) * _softplus(small + pcol_ref[1:2, :])
    beta = _sigmoid(small)
    g_row = -jnp.exp(prow_ref[:, 0:1]) * _softplus(smallt_ref[...] + prow_ref[:, 1:2])
    gcum_col = sum(_dot(tril_ref[...], piece) for piece in _split3(g_col))
    gcum_row = sum(_dot(piece, triu_ref[...]) for piece in _split3(g_row))

    ii = lax.broadcasted_iota(jnp.int32, (st, st), 0)
    jj = lax.broadcasted_iota(jnp.int32, (st, st), 1)
    same_head = (ii // ck) == (jj // ck)
    incl = same_head & (ii >= jj)
    strict = same_head & (ii > jj)
    eye = jnp.where(ii == jj, 1.0, 0.0)
    hrow = lax.broadcasted_iota(jnp.int32, (SUBLANES, LANES), 0)

    lms, rhss = [], []
    for ci in range(tb // ck):
        r0 = ci * ck
        rs = slice(r0, r0 + ck)
        gc = _stack_heads(gcum_col[rs], A_COL, 1)
        bt = _stack_heads(beta[rs], B_COL, 1)
        gr = jnp.concatenate([gcum_row[h:h + 1, r0:r0 + ck] for h in range(GDN_HEADS)], axis=1)
        glast = [gcum_col[r0 + ck - 1:r0 + ck, A_COL + h:A_COL + h + 1] for h in range(GDN_HEADS)]
        gl_stack = jnp.concatenate([jnp.broadcast_to(x, (ck, 1)) for x in glast], axis=0)
        dec = jnp.where(incl, jnp.exp(jnp.where(incl, gc - gr, 0.0)), 0.0)
        qs = _stack_heads(act[rs], 0, GDN_DK)
        ks = _stack_heads(act[rs], GDN_QK_W, GDN_DK)
        vs = _stack_heads(act[rs], 2 * GDN_QK_W, GDN_DV)
        kb = ks * bt
        k16 = ks.astype(BF16)
        lms.append(jnp.where(strict, _dot_nt(kb.astype(BF16), k16) * dec, 0.0))
        eg = jnp.exp(gc)
        rhss.append(jnp.concatenate([vs * bt, kb * eg], axis=1).astype(BF16))
        orow = slice(ci * st, (ci + 1) * st)
        qg_ref[orow, :] = (qs * eg).astype(BF16)
        aqk_ref[orow, :] = (_dot_nt(qs.astype(BF16), k16) * dec).astype(BF16)
        kgt_ref[ci * GDN_DK:(ci + 1) * GDN_DK, :] = (ks * jnp.exp(gl_stack - gc)).T.astype(BF16)
        gl = jnp.zeros((SUBLANES, LANES), F32)
        for h in range(GDN_HEADS):
            gl = jnp.where(hrow == h, jnp.exp(glast[h]), gl)
        gl_ref[ci * SUBLANES:(ci + 1) * SUBLANES, :] = gl

    ainvs = [eye - lm for lm in lms]
    pws = lms
    n = 2
    while n < ck:
        pw16s = [pw.astype(BF16) for pw in pws]
        pws = [_dot(pw16, pw16) for pw16 in pw16s]
        ainvs = [ainv + _dot(ainv.astype(BF16), pw.astype(BF16)) for ainv, pw in zip(ainvs, pws)]
        n *= 2
    for ci, (ainv, rhs) in enumerate(zip(ainvs, rhss)):
        sol = _dot(ainv.astype(BF16), rhs)
        orow = slice(ci * st, (ci + 1) * st)
        u_ref[orow, :] = sol[:, 0:GDN_DV]
        w_ref[orow, :] = sol[:, GDN_DV:].astype(BF16)


def _gdn_prep(act, small, smallt, pcol, prow, *, batch, seq, tb):
    nblk = seq // tb
    ncb = tb // GDN_CHUNK
    blk = np.arange(tb)
    same = (blk[:, None] // GDN_CHUNK) == (blk[None, :] // GDN_CHUNK)
    tril = jnp.asarray(same & (blk[:, None] >= blk[None, :]), BF16)
    triu = jnp.asarray(same & (blk[:, None] <= blk[None, :]), BF16)
    row_spec = lambda r, w: pl.BlockSpec((r, w), lambda b, j: (b * nblk + j, 0))
    per_b = lambda shape: pl.BlockSpec((None,) + shape, lambda b, j: (b,) + (0,) * len(shape))
    n_chunks = batch * seq // GDN_CHUNK
    out_shape = (jax.ShapeDtypeStruct((n_chunks * GDN_STACK, GDN_DV), F32),
                 jax.ShapeDtypeStruct((n_chunks * GDN_STACK, GDN_DK), BF16),
                 jax.ShapeDtypeStruct((n_chunks * GDN_STACK, GDN_DK), BF16),
                 jax.ShapeDtypeStruct((n_chunks * GDN_DK, GDN_STACK), BF16),
                 jax.ShapeDtypeStruct((n_chunks * GDN_STACK, GDN_STACK), BF16),
                 jax.ShapeDtypeStruct((n_chunks * SUBLANES, LANES), F32))
    out_specs = (row_spec(ncb * GDN_STACK, GDN_DV), row_spec(ncb * GDN_STACK, GDN_DK),
                 row_spec(ncb * GDN_STACK, GDN_DK), row_spec(ncb * GDN_DK, GDN_STACK),
                 row_spec(ncb * GDN_STACK, GDN_STACK), row_spec(ncb * SUBLANES, LANES))
    return pl.pallas_call(
        functools.partial(_gdn_prep_kernel, tb=tb),
        grid=(batch, nblk),
        in_specs=[row_spec(tb, GDN_CONV_CH), row_spec(tb, LANES),
                  pl.BlockSpec((None, SUBLANES, tb), lambda b, j: (b, 0, j)),
                  _const_spec(pcol.shape), _const_spec(prow.shape),
                  _const_spec(tril.shape), _const_spec(triu.shape)],
        out_specs=out_specs,
        out_shape=out_shape,
        compiler_params=pltpu.CompilerParams(dimension_semantics=("arbitrary", "arbitrary"),
                                             vmem_limit_bytes=VMEM_LIMIT),
        name="gdn_prep",
    )(act, small, smallt, pcol, prow, tril, triu)


def _gdn_scan_kernel(u_ref, w_ref, qg_ref, kgt_ref, aqk_ref, gl_ref, z_ref, s0_ref, nw_ref, o_ref, sout_ref, s_ref,
                     *, batch, ncb):
    ck = GDN_CHUNK
    st = GDN_STACK
    sw = GDN_HEADS * GDN_DK

    @pl.when(pl.program_id(0) == 0)
    def _():
        s_ref[...] = s0_ref[...]

    wide_mask = (lax.broadcasted_iota(jnp.int32, (st, sw), 0) // ck
                 == lax.broadcasted_iota(jnp.int32, (st, sw), 1) // GDN_DK)
    tall_mask = (lax.broadcasted_iota(jnp.int32, (sw, st), 0) // GDN_DK
                 == lax.broadcasted_iota(jnp.int32, (sw, st), 1) // ck)

    seqs = range(batch)
    s = [s_ref[b] for b in seqs]
    for ci in range(ncb):
        rows = slice(ci * st, (ci + 1) * st)
        toks = slice(ci * ck, (ci + 1) * ck)
        lhs, kgt_bd, gl_rows = [], [], []
        for b in seqs:
            w_bd = jnp.where(wide_mask, jnp.concatenate([w_ref[b, rows, :]] * GDN_HEADS, axis=1), 0.0)
            qg_bd = jnp.where(wide_mask, jnp.concatenate([qg_ref[b, rows, :]] * GDN_HEADS, axis=1), 0.0)
            lhs.append(jnp.concatenate([w_bd, qg_bd], axis=0))
            kgt = kgt_ref[b, ci * GDN_DK:(ci + 1) * GDN_DK, :]
            kgt_bd.append(jnp.where(tall_mask, jnp.concatenate([kgt] * GDN_HEADS, axis=0), 0.0))
            gl = gl_ref[b, ci * SUBLANES:(ci + 1) * SUBLANES, :]
            gl_rows.append(jnp.concatenate(
                [jnp.broadcast_to(gl[h:h + 1], (GDN_DK, GDN_DV)) for h in range(GDN_HEADS)], axis=0))
        t1 = [_dot(lhs[b], s[b].astype(BF16)) for b in seqs]
        vn16 = [(u_ref[b, rows, :] - t1[b][0:st]).astype(BF16) for b in seqs]
        o = [t1[b][st:] + _dot(aqk_ref[b, rows, :], vn16[b]) for b in seqs]
        s = [s[b] * gl_rows[b] + _dot(kgt_bd[b], vn16[b]) for b in seqs]
        for b in seqs:
            for h in range(GDN_HEADS):
                oh = o[b][h * ck:(h + 1) * ck]
                on = oh * lax.rsqrt(jnp.mean(oh * oh, axis=-1, keepdims=True) + EPS) * nw_ref[...]
                hs = slice(h * GDN_DV, (h + 1) * GDN_DV)
                o_ref[b, toks, hs] = (on * _silu(z_ref[b, toks, hs])).astype(BF16)

    for b in seqs:
        s_ref[b] = s[b]
    sout_ref[...] = s_ref[...]


def _gdn_scan(u, w, qg, kgt, aqk, gl, z, s0, nw, *, batch, seq, tb):
    nblk = seq // tb
    ncb = tb // GDN_CHUNK
    cps = seq // GDN_CHUNK
    sw = GDN_HEADS * GDN_DK
    r3 = lambda a, rows_per_chunk: a.reshape(batch, cps * rows_per_chunk, a.shape[-1])
    blk = lambda rows, width: pl.BlockSpec((batch, rows, width), lambda j: (0, j, 0))
    full = pl.BlockSpec((batch, sw, GDN_DV), lambda j: (0, 0, 0))
    return pl.pallas_call(
        functools.partial(_gdn_scan_kernel, batch=batch, ncb=ncb),
        grid=(nblk,),
        in_specs=[blk(ncb * GDN_STACK, GDN_DV), blk(ncb * GDN_STACK, GDN_DK), blk(ncb * GDN_STACK, GDN_DK),
                  blk(ncb * GDN_DK, GDN_STACK), blk(ncb * GDN_STACK, GDN_STACK), blk(ncb * SUBLANES, LANES),
                  blk(tb, GDN_V_W), full, _const_spec(nw.shape)],
        out_specs=(blk(tb, GDN_V_W), full),
        out_shape=(jax.ShapeDtypeStruct((batch, seq, GDN_V_W), BF16),
                   jax.ShapeDtypeStruct((batch, sw, GDN_DV), F32)),
        scratch_shapes=[pltpu.VMEM((batch, sw, GDN_DV), F32)],
        compiler_params=pltpu.CompilerParams(dimension_semantics=("arbitrary",), vmem_limit_bytes=VMEM_LIMIT),
        name="gdn_scan",
    )(r3(u, GDN_STACK), r3(w, GDN_STACK), r3(qg, GDN_STACK), r3(kgt, GDN_DK), r3(aqk, GDN_STACK),
      r3(gl, SUBLANES), z.reshape(batch, seq, GDN_V_W), s0.reshape(batch, sw, GDN_DV), nw)


def _rope_tables(pos):
    half = HEAD_DIM // 2
    inv = np.power(ROPE_THETA, -np.arange(half, dtype=np.float64) * 2.0 / HEAD_DIM)
    ang = np.asarray(pos, np.float64)[:, None] * inv[None, :]
    cos, sin = np.cos(ang).astype(np.float32), np.sin(ang).astype(np.float32)
    return tuple(jnp.asarray(t) for t in (np.tile(cos, (1, 4)), np.concatenate([-sin, sin, -sin, sin], axis=1),
                                          cos.T, sin.T))


def _cmp_to_sel_t(n_cmp, n_sel, rows, cols):
    cs = np.arange(cols)[None, :] * CMP_STRIDE
    ss = np.arange(rows)[:, None] * SEL_LEN
    hit = (cs < ss + SEL_LEN) & (cs + CMP_LEN > ss)
    hit &= (np.arange(cols)[None, :] < n_cmp) & (np.arange(rows)[:, None] < n_sel)
    return jnp.asarray(hit, BF16)


def _layer_weights(w_in, cmp_pe, cmp_w, conv_w, a_log, dt_bias, gdn_norm, w_out):
    cuts = np.cumsum([NSA_Q_W, 6 * NSA_KV_W, GATE_COLS, GDN_CONV_CH, GDN_V_W, GDN_HEADS]).tolist()
    wq, wkv, wgt, wconv, wz, wa, wb = jnp.split(w_in, cuts, axis=1)
    order = np.array([g * NSA_REP + r for r in range(NSA_REP) for g in range(NSA_KV_HEADS)])
    cols = (order[:, None] * HEAD_DIM + np.arange(HEAD_DIM)[None, :]).reshape(-1)
    wm = jnp.concatenate([wq[:, cols], wkv, wconv, wz], axis=1).astype(BF16)
    wkvt = wkv.T.astype(BF16)
    ws = jnp.concatenate([wgt, wa, wb], axis=1)
    ws = jnp.pad(ws, ((0, 0), (0, LANES - ws.shape[1]))).astype(BF16)
    wo = jnp.concatenate([w_out[:NSA_Q_W][cols], w_out[NSA_Q_W:]], axis=0).astype(BF16)

    def blockdiag(l0):
        wk, wv = cmp_w[0, l0:l0 + CMP_STRIDE], cmp_w[1, l0:l0 + CMP_STRIDE]
        z = jnp.zeros_like(wk)
        rows = [jnp.concatenate(r, axis=2) for r in ([wk, z, z, z], [z, wk, z, z], [z, z, wv, z], [z, z, z, wv])]
        return jnp.concatenate(rows, axis=1).astype(BF16)

    def long_contraction(c):
        halves = []
        for l0 in (0, CMP_STRIDE):
            w = cmp_w[c, l0:l0 + CMP_STRIDE]
            z = jnp.zeros_like(w)
            halves.append(jnp.concatenate([jnp.concatenate([w, z], axis=2), jnp.concatenate([z, w], axis=2)], axis=1))
        return jnp.concatenate(halves, axis=2).reshape(CMP_STRIDE * LANES, 2 * LANES).astype(BF16)

    def pe_rows(l0):
        pk, pv = cmp_pe[0, l0:l0 + CMP_STRIDE], cmp_pe[1, l0:l0 + CMP_STRIDE]
        return jnp.concatenate([pk, pk, pv, pv], axis=1)

    pcol = jnp.zeros((2, LANES), F32).at[0, A_COL:A_COL + GDN_HEADS].set(a_log)
    pcol = pcol.at[1, A_COL:A_COL + GDN_HEADS].set(dt_bias)
    prow = jnp.zeros((SUBLANES, 2), F32).at[0:GDN_HEADS, 0].set(a_log).at[0:GDN_HEADS, 1].set(dt_bias)
    return dict(wm=wm, ws=ws, wkvt=wkvt, wo=wo, wk_long=long_contraction(0), wv_long=long_contraction(1),
                wlo=blockdiag(0), whi=blockdiag(CMP_STRIDE), pelo=pe_rows(0),
                pehi=pe_rows(CMP_STRIDE), cw=conv_w, pcol=pcol, prow=prow, nw=gdn_norm[None, :])


def kernel(x_prompt, x_sample, cache_nsa_kv, cache_nsa_win, state_gdn_S, state_gdn_conv, page_table, norm_mix, w_in,
           nsa_cmp_pe, nsa_cmp_w, gdn_conv_w, gdn_a_log, gdn_dt_bias, gdn_norm, w_out, norm_ffn, w_gate_up, w_down,
           norm_final):
    bp, tp, _ = x_prompt.shape
    bs, ts, _ = x_sample.shape
    depth = w_in.shape[0]
    n_pages = page_table.shape[1]
    past = n_pages * PAGE_SIZE
    assert depth == 1 and ts == 1, "one layer and one new token per sample row"
    assert tp % 512 == 0 and tp // SEL_LEN <= LANES and past % 512 == 0
    l = 0
    wts = _layer_weights(w_in[l], nsa_cmp_pe[l], nsa_cmp_w[l], gdn_conv_w[l], gdn_a_log[l], gdn_dt_bias[l],
                         gdn_norm[l], w_out[l])
    g_mix, g_ffn, g_fin = norm_mix[l][None, :], norm_ffn[l][None, :], norm_final[None, :]
    wgu, wd = w_gate_up[l].astype(BF16), w_down[l].astype(BF16)
    hist = CONV_W - 1

    xp = x_prompt.reshape(bp * tp, D_MODEL)
    q, kv, conv_in, z, small, kaug, vaug, kwin, vwaug, kv4t_p, kvwt_p, act = _proj(
        xp, g_mix, wts["wm"], wts["ws"], wts["wkvt"], wts["cw"], jnp.zeros((bp, hist, GDN_CONV_CH), F32),
        *_rope_tables(np.arange(tp)), tm=256)
    ckv = _compress(kv, wts["wlo"], wts["whi"], wts["pelo"], wts["pehi"], batch=bp, seq=tp)
    nc = tp // CMP_STRIDE
    mt_p = _cmp_to_sel_t(nc - 1, tp // SEL_LEN, LANES, nc)
    o_nsa = _nsa_prompt(q, small, kaug, vaug, kwin, vwaug, ckv, mt_p, batch=bp, seq=tp, kc=512)
    smallt = small[:, A_COL:A_COL + SUBLANES].reshape(bp, tp, SUBLANES).transpose(0, 2, 1)
    prep = _gdn_prep(act, small, smallt, wts["pcol"], wts["prow"], batch=bp, seq=tp, tb=256)
    o_gdn, s_p = _gdn_scan(*prep, z, jnp.zeros((bp, GDN_HEADS, GDN_DK, GDN_DV), F32), wts["nw"],
                           batch=bp, seq=tp, tb=256)
    o_gdn = o_gdn.reshape(bp * tp, GDN_V_W)
    s_p = s_p.reshape(bp, GDN_HEADS, GDN_DK, GDN_DV)
    y_p = _mix_ffn(xp, o_nsa, o_gdn, wts["wo"], g_ffn, wgu, wd, g_fin, tm=512)
    tok_major = lambda a, comps: a.reshape(bp, comps, NSA_KV_HEADS, HEAD_DIM, -1).transpose(0, 4, 1, 2, 3)
    kv_p = tok_major(kv4t_p, 4)
    win_p = tok_major(kvwt_p[:, :, tp - min(WINDOW, tp):], 2)
    conv_p = conv_in.reshape(bp, tp, GDN_CONV_CH)[:, tp - hist:]

    xs = x_sample.reshape(bs, D_MODEL)
    q, kv, conv_in, z, small = _proj(xs, g_mix, wts["wm"], wts["ws"], wts["wkvt"], wts["cw"],
                                     jnp.zeros((1, hist, GDN_CONV_CH), F32),
                                     *_rope_tables(np.full((bs,), past)), tm=bs)[:5]
    n_sel = past // SEL_LEN + 1
    nsp = -(-n_sel // LANES) * LANES
    mt_s = _cmp_to_sel_t(past // CMP_STRIDE - 1, n_sel, nsp, past // CMP_STRIDE)
    cache = cache_nsa_kv[l].transpose(0, 2, 3, 4, 1).reshape(-1, 4 * NSA_KV_W, PAGE_SIZE)
    win = cache_nsa_win[l].transpose(0, 2, 3, 4, 1).reshape(bs, 2 * NSA_KV_W, -1)
    o_nsa = _nsa_sample(page_table, cache, q[:, None], kv[:, None], small[:, None], win, wts["wk_long"],
                        wts["wv_long"], wts["pelo"], wts["pehi"], mt_s, pps=min(32, n_pages)).reshape(bs, NSA_Q_W)
    o_gdn, s_s = _gdn_step(conv_in, z, small, state_gdn_conv[l], state_gdn_S[l], wts["cw"], wts["pcol"], wts["nw"])
    o_gdn = o_gdn.reshape(bs, GDN_V_W)
    y_s = _mix_ffn(xs, o_nsa, o_gdn, wts["wo"], g_ffn, wgu, wd, g_fin, tm=bs)
    kv_s = kv.reshape(bs, 1, 3, 2, NSA_KV_HEADS, HEAD_DIM)
    win_s = jnp.concatenate([cache_nsa_win[l], kv_s[:, :, 2]], axis=1)[:, -min(WINDOW, past + 1):]
    conv_s = jnp.concatenate([state_gdn_conv[l], conv_in[:, None]], axis=1)[:, -hist:]

    return (y_p.reshape(bp, tp, D_MODEL), y_s.reshape(bs, 1, D_MODEL),
            kv_p[None], win_p[None], s_p[None], conv_p[None],
            kv_s[:, :, 0:2].reshape(bs, 1, 4, NSA_KV_HEADS, HEAD_DIM)[None], win_s[None], s_s[None], conv_s[None])
```

```python
import functools

import numpy as np
import jax
import jax.numpy as jnp
from jax import lax
from jax.experimental import pallas as pl
from jax.experimental.pallas import tpu as pltpu

F32 = jnp.float32
BF16 = jnp.bfloat16

D_MODEL = 1024
PAGE_SIZE = 128
HEAD_DIM = 64
NSA_HEADS = 8
NSA_KV_HEADS = 2
NSA_REP = NSA_HEADS // NSA_KV_HEADS
CMP_LEN = 32
CMP_STRIDE = 16
SEL_LEN = 64
SEL_TOPK = 16
WINDOW = 512
Q_BLOCK = 128
ROPE_THETA = 10000.0
GDN_HEADS = 4
GDN_DK = 128
GDN_DV = 128
CONV_W = 4
NSA_Q_W = NSA_HEADS * HEAD_DIM
NSA_KV_W = NSA_KV_HEADS * HEAD_DIM
GDN_QK_W = GDN_HEADS * GDN_DK
GDN_V_W = GDN_HEADS * GDN_DV
GDN_CONV_CH = 2 * GDN_QK_W + GDN_V_W
NEG = -1e30
SEL_UNROLL = 4
LOG2E = 1.4426950408889634
SEL_BONUS = 1e4
EPS = 1e-6

LANES = 128
SUBLANES = 8
VMEM_LIMIT = 56 * 1024 * 1024

GATE_COLS = 3 * NSA_HEADS
A_COL = GATE_COLS
B_COL = GATE_COLS + GDN_HEADS

NT_DIMS = (((1,), (1,)), ((), ()))
TN_DIMS = (((0,), (0,)), ((), ()))


def _dot(a, b):
    return jnp.dot(a, b, preferred_element_type=F32)


def _dot_nt(a, b):
    return lax.dot_general(a, b, NT_DIMS, preferred_element_type=F32)


def _sigmoid(x):
    return 1.0 / (1.0 + jnp.exp(-x))


def _silu(x):
    return x * _sigmoid(x)


def _split3(x):
    p1 = x.astype(BF16)
    r1 = x - p1.astype(F32)
    p2 = r1.astype(BF16)
    p3 = (r1 - p2.astype(F32)).astype(BF16)
    return p1, p2, p3


def _const_spec(shape):
    nd = len(shape)
    return pl.BlockSpec(shape, lambda *_: (0,) * nd, pipeline_mode=pl.Buffered(1))


def _proj_kernel(x_ref, g_ref, wm_ref, ws_ref, wkvt_ref, cos_ref, sin_ref, cost_ref, sint_ref, cw_ref, buf_ref,
                 q_ref, kv_ref, conv_ref, z_ref, small_ref, kaug_ref, vaug_ref, kwin_ref, vwaug_ref,
                 kv4t_ref, kvwt_ref, act_ref, xs_ref, *, tm, pos_rows):
    pad = SUBLANES
    hist = CONV_W - 1
    c0 = NSA_Q_W + 6 * NSA_KV_W

    @pl.when(pl.program_id(0) % (pos_rows // tm) == 0)
    def _():
        xs_ref[0:pad, :] = jnp.zeros((pad, GDN_CONV_CH), F32)
        xs_ref[pad - hist:pad, :] = buf_ref[...]

    x = x_ref[...]
    ms = jnp.mean(x * x, axis=-1, keepdims=True)
    xn = (x * lax.rsqrt(ms + EPS) * g_ref[...]).astype(BF16)
    cos = cos_ref[...]
    sin = sin_ref[...]
    lane = lax.broadcasted_iota(jnp.int32, (tm, LANES), 1)
    low_half = (lane % HEAD_DIM) < (HEAD_DIM // 2)

    def rope(v):
        rot = jnp.where(low_half, pltpu.roll(v, LANES - HEAD_DIM // 2, 1), pltpu.roll(v, HEAD_DIM // 2, 1))
        return v * cos + rot * sin

    def delta_front_end(part):
        cols = slice(part * GDN_QK_W, (part + 1) * GDN_QK_W)
        conv_in = _dot(xn, wm_ref[:, c0 + part * GDN_QK_W:c0 + (part + 1) * GDN_QK_W])
        conv_ref[:, cols] = conv_in
        xs_ref[pad:pad + tm, cols] = conv_in
        xs = xs_ref[:, cols]
        conv = xs[pad:] * cw_ref[hist:CONV_W, cols]
        for t in range(hist):
            conv = conv + pltpu.roll(xs, hist - t, 0)[pad:] * cw_ref[t:t + 1, cols]
        xs_ref[0:pad, cols] = xs[tm:tm + pad]
        act = _silu(conv)
        for h in range(GDN_HEADS):
            hs = slice(h * GDN_DK, (h + 1) * GDN_DK)
            dst = act_ref.at[:, part * GDN_QK_W + h * GDN_DK:part * GDN_QK_W + (h + 1) * GDN_DK]
            if part == 0:
                dst[...] = _l2norm(act[:, hs]) * (GDN_DK ** -0.5)
            elif part == 1:
                dst[...] = _l2norm(act[:, hs])
            else:
                dst[...] = act[:, hs]

    delta_front_end(0)
    q = _dot(xn, wm_ref[:, 0:NSA_Q_W])
    for r in range(NSA_REP):
        sl = slice(r * LANES, (r + 1) * LANES)
        q_ref[:, sl] = (rope(q[:, sl]) * (HEAD_DIM ** -0.5 * LOG2E)).astype(BF16)

    delta_front_end(1)
    kv = _dot(xn, wm_ref[:, NSA_Q_W:NSA_Q_W + 6 * NSA_KV_W])
    group0 = lane < HEAD_DIM
    for c in range(6):
        sl = slice(c * LANES, (c + 1) * LANES)
        blk = kv[:, sl]
        if c % 2 == 0:
            blk = rope(blk)
        kv_ref[:, sl] = blk
        if c == 2:
            kaug_ref[:, LANES:2 * LANES] = blk.astype(BF16)
        elif c == 4:
            kwin_ref[...] = blk.astype(BF16)
        elif c in (3, 5):
            v_ref = vaug_ref if c == 3 else vwaug_ref
            v_ref[:, 0:LANES] = jnp.where(group0, blk, 1.0).astype(BF16)
            v_ref[:, LANES:2 * LANES] = jnp.where(group0, 1.0, blk).astype(BF16)

    row0 = (pl.program_id(0) * tm) % pos_rows
    rows = row0 + lax.broadcasted_iota(jnp.int32, (tm, LANES), 0)
    kaug_ref[:, 0:LANES] = jnp.where(rows // SEL_LEN == lane, 1.0, 0.0).astype(BF16)

    delta_front_end(2)
    kvt = _dot_nt(wkvt_ref[...], xn)
    cos_t = cost_ref[...]
    sin_t = sint_ref[...]
    half = HEAD_DIM // 2
    for c in range(6):
        blk = kvt[c * LANES:(c + 1) * LANES]
        if c % 2 == 0:
            parts = []
            for g in range(NSA_KV_HEADS):
                x1 = blk[g * HEAD_DIM:g * HEAD_DIM + half]
                x2 = blk[g * HEAD_DIM + half:(g + 1) * HEAD_DIM]
                parts += [x1 * cos_t - x2 * sin_t, x2 * cos_t + x1 * sin_t]
            blk = jnp.concatenate(parts, axis=0)
        dst = kv4t_ref if c < 4 else kvwt_ref
        dst[(c % 4) * LANES:(c % 4 + 1) * LANES, :] = blk

    z_ref[...] = _dot(xn, wm_ref[:, c0 + GDN_CONV_CH:c0 + GDN_CONV_CH + GDN_V_W])
    small_ref[...] = _dot(xn, ws_ref[...])


def _proj(x, g, wm, ws, wkvt, cw, buf, cos, sin, cos_t, sin_t, *, tm):
    rows = x.shape[0]
    pos_rows = cos.shape[0]
    n_pos_blk = pos_rows // tm
    n_seq = rows // pos_rows
    grid = (rows // tm,)
    row_spec = lambda w: pl.BlockSpec((tm, w), lambda i: (i, 0))
    pos_spec = pl.BlockSpec((tm, LANES), lambda i: (i % n_pos_blk, 0))
    pos_t_spec = pl.BlockSpec((HEAD_DIM // 2, tm), lambda i: (0, i % n_pos_blk))
    tok_minor = lambda r: pl.BlockSpec((None, r, tm), lambda i: (i // n_pos_blk, 0, i % n_pos_blk))
    out_shape = (
        jax.ShapeDtypeStruct((rows, NSA_Q_W), BF16),
        jax.ShapeDtypeStruct((rows, 6 * NSA_KV_W), F32),
        jax.ShapeDtypeStruct((rows, GDN_CONV_CH), F32),
        jax.ShapeDtypeStruct((rows, GDN_V_W), F32),
        jax.ShapeDtypeStruct((rows, LANES), F32),
        jax.ShapeDtypeStruct((rows, 2 * LANES), BF16),
        jax.ShapeDtypeStruct((rows, 2 * LANES), BF16),
        jax.ShapeDtypeStruct((rows, LANES), BF16),
        jax.ShapeDtypeStruct((rows, 2 * LANES), BF16),
    )
    out_shape_t = (jax.ShapeDtypeStruct((n_seq, 4 * NSA_KV_W, pos_rows), F32),
                   jax.ShapeDtypeStruct((n_seq, 2 * NSA_KV_W, pos_rows), F32),
                   jax.ShapeDtypeStruct((rows, GDN_CONV_CH), F32))
    return pl.pallas_call(
        functools.partial(_proj_kernel, tm=tm, pos_rows=pos_rows),
        grid=grid,
        in_specs=[row_spec(D_MODEL), _const_spec((1, D_MODEL)), _const_spec(wm.shape), _const_spec(ws.shape),
                  _const_spec(wkvt.shape), pos_spec, pos_spec, pos_t_spec, pos_t_spec, _const_spec(cw.shape),
                  pl.BlockSpec((None, CONV_W - 1, GDN_CONV_CH), lambda i: (i // n_pos_blk, 0, 0))],
        out_specs=(tuple(row_spec(s.shape[1]) for s in out_shape)
                   + (tok_minor(4 * NSA_KV_W), tok_minor(2 * NSA_KV_W), row_spec(GDN_CONV_CH))),
        out_shape=out_shape + out_shape_t,
        scratch_shapes=[pltpu.VMEM((SUBLANES + tm, GDN_CONV_CH), F32)],
        compiler_params=pltpu.CompilerParams(dimension_semantics=("arbitrary",), vmem_limit_bytes=VMEM_LIMIT),
        name="proj",
    )(x, g, wm, ws, wkvt, cos, sin, cos_t, sin_t, cw, buf)


def _mix_ffn_kernel(x_ref, on_ref, og_ref, wo_ref, gf_ref, wgu_ref, wd_ref, gl_ref, y_ref, *, d_ff):
    h = x_ref[...] + _dot(on_ref[...], wo_ref[0:NSA_Q_W, :]) + _dot(og_ref[...], wo_ref[NSA_Q_W:, :])
    ms = jnp.mean(h * h, axis=-1, keepdims=True)
    hn = (h * lax.rsqrt(ms + EPS) * gf_ref[...]).astype(BF16)
    gate = _dot(hn, wgu_ref[:, 0:d_ff])
    up = _dot(hn, wgu_ref[:, d_ff:])
    act = (_silu(gate) * up).astype(BF16)
    h = h + _dot(act, wd_ref[...])
    ms = jnp.mean(h * h, axis=-1, keepdims=True)
    y_ref[...] = h * lax.rsqrt(ms + EPS) * gl_ref[...]


def _mix_ffn(x, o_nsa, o_gdn, wo, gf, wgu, wd, gl, *, tm):
    rows = x.shape[0]
    d_ff = wd.shape[0]
    row_spec = lambda w: pl.BlockSpec((tm, w), lambda i: (i, 0))
    return pl.pallas_call(
        functools.partial(_mix_ffn_kernel, d_ff=d_ff),
        grid=(rows // tm,),
        in_specs=[row_spec(D_MODEL), row_spec(NSA_Q_W), row_spec(GDN_V_W), _const_spec(wo.shape),
                  _const_spec((1, D_MODEL)), _const_spec(wgu.shape), _const_spec(wd.shape),
                  _const_spec((1, D_MODEL))],
        out_specs=row_spec(D_MODEL),
        out_shape=jax.ShapeDtypeStruct((rows, D_MODEL), F32),
        compiler_params=pltpu.CompilerParams(dimension_semantics=("arbitrary",), vmem_limit_bytes=VMEM_LIMIT),
        name="mix_ffn",
    )(x, o_nsa, o_gdn, wo, gf, wgu, wd, gl)


def _compress_rows(load_rows, wlo_ref, whi_ref, pelo_ref, pehi_ref, nc):
    acc_lo = jnp.zeros((nc, 2 * LANES), F32)
    acc_hi = jnp.zeros((nc, 2 * LANES), F32)
    for l in range(CMP_STRIDE):
        x = load_rows(l)
        acc_lo = acc_lo + _dot((x + pelo_ref[l:l + 1, :]).astype(BF16), wlo_ref[l])
        acc_hi = acc_hi + _dot((x + pehi_ref[l:l + 1, :]).astype(BF16), whi_ref[l])
    nxt = pltpu.roll(acc_hi, nc - 1, 0)
    row = lax.broadcasted_iota(jnp.int32, (nc, 2 * LANES), 0)
    return jnp.where(row < nc - 1, acc_lo + nxt, 0.0)


def _compress_kernel(k_ref, v_ref, wlo_ref, whi_ref, pelo_ref, pehi_ref, ckv_ref, *, nc):
    load = lambda l: jnp.concatenate([k_ref[pl.ds(l, nc, stride=CMP_STRIDE), :],
                                      v_ref[pl.ds(l, nc, stride=CMP_STRIDE), :]], axis=1)
    ckv_ref[...] = _compress_rows(load, wlo_ref, whi_ref, pelo_ref, pehi_ref, nc).astype(BF16)


def _compress(kv, wlo, whi, pelo, pehi, *, batch, seq):
    nc = seq // CMP_STRIDE
    return pl.pallas_call(
        functools.partial(_compress_kernel, nc=nc),
        grid=(batch,),
        in_specs=[pl.BlockSpec((seq, LANES), lambda b: (b, 0)), pl.BlockSpec((seq, LANES), lambda b: (b, 1)),
                  _const_spec(wlo.shape), _const_spec(whi.shape), _const_spec(pelo.shape), _const_spec(pehi.shape)],
        out_specs=pl.BlockSpec((nc, 2 * LANES), lambda b: (b, 0)),
        out_shape=jax.ShapeDtypeStruct((batch * nc, 2 * LANES), BF16),
        compiler_params=pltpu.CompilerParams(dimension_semantics=("arbitrary",), vmem_limit_bytes=VMEM_LIMIT),
        name="compress",
    )(kv, kv, wlo, whi, pelo, pehi)


def _masked_softmax2(s):
    m = jnp.maximum(jnp.max(s, axis=-1, keepdims=True), NEG / 8)
    e = jnp.exp2(s - m)
    return e * (1.0 / jnp.maximum(jnp.sum(e, axis=-1, keepdims=True), 1e-30))


def _select_blocks(score, sel, blk, axis, rounds, filler=iter(())):
    n = score.shape[axis]
    for _ in range(rounds):
        mx = jnp.max(score, axis=axis, keepdims=True)
        idx = jnp.min(jnp.where(score == mx, blk, n), axis=axis, keepdims=True)
        hit = blk == idx
        sel = jnp.where(hit, jnp.where(mx > NEG / 2, 1.0, sel), sel)
        score = jnp.where(hit, -3e38, score)
        next(filler, None)
    for _ in filler:
        pass
    return sel


def _nsa_prompt_kernel(q_ref, small_ref, kaug_ref, vaug_ref, kwin_ref, vwaug_ref, ckv_ref, mt_ref, o_ref,
                       *, seq, kc):
    nc = seq // CMP_STRIDE
    qb = Q_BLOCK
    start = pl.program_id(1) * qb
    wlen = WINDOW + qb
    groups = range(NSA_KV_HEADS)
    stack = lambda x: jnp.concatenate([x] * NSA_REP, axis=0)

    lane = lax.broadcasted_iota(jnp.int32, (qb, LANES), 1)
    gates = _sigmoid(small_ref[...])

    qrow_c = lax.broadcasted_iota(jnp.int32, (qb, nc), 0)
    ncol = lax.broadcasted_iota(jnp.int32, (qb, nc), 1)
    cmp_bias = stack(jnp.where(ncol * CMP_STRIDE + (CMP_LEN - 1) - qrow_c <= start, 0.0, NEG))
    tail_bias = stack(jnp.where(lane <= lax.broadcasted_iota(jnp.int32, (qb, qb), 0), 0.0, NEG))
    wbase = pl.multiple_of(jnp.maximum(start - WINDOW, 0), qb)
    back = (lax.broadcasted_iota(jnp.int32, (qb, wlen), 0) - lax.broadcasted_iota(jnp.int32, (qb, wlen), 1)
            + (start - wbase))
    win_bias = stack(jnp.where(back >= 0, jnp.where(back <= WINDOW, 0.0, NEG), NEG))

    glanes = [slice(g * LANES, (g + 1) * LANES) for g in groups]
    qs = [jnp.concatenate([jnp.where((lane >= HEAD_DIM) == (g == 1), q_ref[:, r * LANES:(r + 1) * LANES], 0.0)
                           .astype(BF16) for r in range(NSA_REP)], axis=0) for g in groups]
    ps = [_masked_softmax2(_dot_nt(q, ckv_ref[:, 0:LANES]) + cmp_bias) for q in qs]
    psums = [p[0:qb] + p[qb:2 * qb] + p[2 * qb:3 * qb] + p[3 * qb:4 * qb] for p in ps]
    imp_t = jnp.concatenate([sum(_dot_nt(mt_ref[...], piece) for piece in _split3(psum)) for psum in psums], axis=1)
    o_cmps = [_dot(p.astype(BF16), ckv_ref[:, LANES:2 * LANES]) for p in ps]

    done = {}

    def independent_work():
        s = [_dot_nt(q, kwin_ref[pl.ds(wbase, wlen), :]) + win_bias for q in qs]
        yield
        mx = [jnp.max(x, axis=-1, keepdims=True) for x in s]
        yield
        e = []
        for g in groups:
            e.append(jnp.exp2(s[g] - mx[g]).astype(BF16))
            yield
        acc = [_dot(e[g], vwaug_ref[pl.ds(wbase, wlen), glanes[g]]) for g in groups]
        yield
        o_win = [a / pltpu.roll(a, HEAD_DIM, 1) for a in acc]
        yield
        rows_of = lambda x, r: x[r * qb:(r + 1) * qb]
        done["partial"] = []
        for g in groups:
            done["partial"].append(
                [gates[:, 3 * (g * NSA_REP + r):3 * (g * NSA_REP + r) + 1] * rows_of(o_cmps[g], r)
                 + gates[:, 3 * (g * NSA_REP + r) + 2:3 * (g * NSA_REP + r) + 3] * rows_of(o_win[g], r)
                 for r in range(NSA_REP)])
            yield
        s_tail = [_dot_nt(q, kaug_ref[pl.ds(start, qb), LANES:2 * LANES]) + tail_bias for q in qs]
        yield
        done["m0"] = [jnp.max(x, axis=-1, keepdims=True) for x in s_tail]
        yield
        done["acc0"] = [_dot(jnp.exp2(s_tail[g] - done["m0"][g]).astype(BF16), vaug_ref[pl.ds(start, qb), glanes[g]])
                        for g in groups]
        yield

    blk_t = lax.broadcasted_iota(jnp.int32, (LANES, 2 * qb), 0)
    qpos_t = start + lax.broadcasted_iota(jnp.int32, (LANES, 2 * qb), 1) % qb
    cur_t = qpos_t // SEL_LEN
    visible = blk_t * SEL_LEN <= qpos_t
    forced = (blk_t == 0) | (blk_t == cur_t) | (blk_t == cur_t - 1)
    sel_t = _select_blocks(jnp.where(visible, jnp.where(forced, NEG, imp_t), NEG),
                           jnp.where(visible, jnp.where(forced, 1.0, 0.0), 0.0), blk_t, 0, SEL_TOPK - 3,
                           independent_work())
    partial, m0s, acc0s = done["partial"], done["m0"], done["acc0"]
    bias_t = jnp.where(blk_t * SEL_LEN < start, jnp.where(sel_t > 0.5, 0.0, NEG), NEG).T.astype(BF16)
    n_chunks = (start + kc - 1) // kc
    q_augs = [jnp.concatenate([stack(bias_t[g * qb:(g + 1) * qb]), qs[g]], axis=1) for g in groups]

    def sel_step(c, carry):
        off = pl.multiple_of(c * kc, kc)
        k = kaug_ref[pl.ds(off, kc), :]
        out = []
        for g in groups:
            m, acc = carry[g]
            s = _dot_nt(q_augs[g], k)
            m_new = jnp.maximum(m, jnp.max(s, axis=-1, keepdims=True))
            e = jnp.exp2(s - m_new).astype(BF16)
            out.append((m_new, jnp.exp2(m - m_new) * acc + _dot(e, vaug_ref[pl.ds(off, kc), g * LANES:(g + 1) * LANES])))
        return tuple(out)

    def sel_steps(first, count, x):
        for u in range(count):
            x = sel_step(first + u, x)
        return x

    sel_out = lax.fori_loop(0, n_chunks // SEL_UNROLL, lambda c, x: sel_steps(SEL_UNROLL * c, SEL_UNROLL, x),
                            tuple(zip(m0s, acc0s)))
    done_chunks = n_chunks // SEL_UNROLL * SEL_UNROLL
    count = SEL_UNROLL // 2
    while count >= 1:
        take = (n_chunks - done_chunks) >= count
        sel_out = lax.cond(take, functools.partial(sel_steps, done_chunks, count), lambda x: x, sel_out)
        done_chunks = done_chunks + jnp.where(take, count, 0)
        count //= 2

    outs = [None] * NSA_REP
    for g in groups:
        acc = sel_out[g][1]
        o_sel = acc / pltpu.roll(acc, HEAD_DIM, 1)
        for r in range(NSA_REP):
            c0 = (g * NSA_REP + r) * 3
            o = partial[g][r] + gates[:, c0 + 1:c0 + 2] * o_sel[r * qb:(r + 1) * qb]
            outs[r] = o if g == 0 else jnp.where(lane < HEAD_DIM, outs[r], o)

    for r in range(NSA_REP):
        o_ref[:, r * LANES:(r + 1) * LANES] = outs[r].astype(BF16)


def _nsa_prompt(q, small, kaug, vaug, kwin, vwaug, ckv, mt, *, batch, seq, kc):
    nb = seq // Q_BLOCK
    nc = seq // CMP_STRIDE
    blk_spec = lambda w: pl.BlockSpec((Q_BLOCK, w), lambda b, i: (b * nb + i, 0))
    seq_spec = lambda w: pl.BlockSpec((seq, w), lambda b, i: (b, 0))
    return pl.pallas_call(
        functools.partial(_nsa_prompt_kernel, seq=seq, kc=kc),
        grid=(batch, nb),
        in_specs=[blk_spec(NSA_Q_W), blk_spec(LANES), seq_spec(2 * LANES), seq_spec(2 * LANES), seq_spec(LANES),
                  seq_spec(2 * LANES), pl.BlockSpec((nc, 2 * LANES), lambda b, i: (b, 0)), _const_spec(mt.shape)],
        out_specs=blk_spec(NSA_Q_W),
        out_shape=jax.ShapeDtypeStruct((batch * seq, NSA_Q_W), BF16),
        compiler_params=pltpu.CompilerParams(dimension_semantics=("arbitrary", "arbitrary"),
                                             vmem_limit_bytes=VMEM_LIMIT),
        name="nsa_prompt",
    )(q, small, kaug, vaug, kwin, vwaug, ckv, mt)


def _nsa_sample_kernel(pt_ref, *refs, past, pps, nb):
    del pt_ref
    pages = refs[:nb * pps]
    (q_ref, kvn_ref, small_ref, win_ref, perm_ref, wk_ref, wv_ref, pelo_ref, pehi_ref, mt_ref, exp_ref,
     o_ref, xs_ref, s_ref, vs_ref) = refs[nb * pps:]
    step = pl.program_id(1)
    n_pages = past // PAGE_SIZE
    cpp = PAGE_SIZE // CMP_STRIDE
    heads = NSA_HEADS
    seqs = range(nb)
    row = lax.broadcasted_iota(jnp.int32, (heads, LANES), 0)
    lane = lax.broadcasted_iota(jnp.int32, (heads, LANES), 1)
    mine = (lane >= HEAD_DIM) == (row >= NSA_REP)
    rr = row % NSA_REP
    qf = []
    for b in seqs:
        qsel = jnp.zeros((heads, LANES), F32)
        for r in range(NSA_REP):
            qsel = jnp.where(rr == r, q_ref[b, :, r * LANES:(r + 1) * LANES].astype(F32), qsel)
        qf.append(jnp.where(mine, qsel, 0.0))
    q = [x.astype(BF16) for x in qf]

    nc = past // CMP_STRIDE

    def take_pages(first_page):
        for k in range(pps):
            pg = first_page + k
            tok = slice(pg * PAGE_SIZE, (pg + 1) * PAGE_SIZE)
            for b in seqs:
                page = pages[b * pps + k]
                s_ref[b, :, tok] = _dot(q[b], page[2 * LANES:3 * LANES, :].astype(BF16))
                vs_ref[b, :, tok] = page[3 * LANES:4 * LANES, :].astype(BF16)
                x = _dot_nt(perm_ref[...], page[0:2 * LANES, :].astype(BF16))
                for l in range(CMP_STRIDE):
                    for c in range(2):
                        xs_ref[b, c, pg * cpp:(pg + 1) * cpp, l * LANES:(l + 1) * LANES] = (
                            x[l * cpp:(l + 1) * cpp, c * LANES:(c + 1) * LANES])

    for st in range(n_pages // pps):
        pl.when(step == st)(functools.partial(take_pages, st * pps))

    @pl.when(step == pl.num_programs(1) - 1)
    def _():
        n_cmp = nc - 1
        nsp = mt_ref.shape[0]
        cur = past // SEL_LEN

        crow = lax.broadcasted_iota(jnp.int32, (nc, LANES), 0)
        halves = []
        for c, w_ref in enumerate((wk_ref, wv_ref)):
            cl = slice(c * LANES, (c + 1) * LANES)
            for b in seqs:
                for l in range(CMP_STRIDE):
                    ll = slice(l * LANES, (l + 1) * LANES)
                    xs_ref[b, c, nc:nc + SUBLANES, ll] = jnp.broadcast_to(pelo_ref[l:l + 1, cl], (SUBLANES, LANES))
                    xs_ref[b, c, nc + SUBLANES:nc + 2 * SUBLANES, ll] = jnp.broadcast_to(pehi_ref[l:l + 1, cl],
                                                                                           (SUBLANES, LANES))
            accs = [_dot(xs_ref[b, c].astype(BF16), w_ref[...]) for b in seqs]
            halves.append([
                jnp.where(crow < nc - 1,
                          acc[0:nc, 0:LANES] + pltpu.roll(acc[0:nc, LANES:2 * LANES], nc - 1, 0)
                          + acc[nc:nc + 1, 0:LANES] + acc[nc + SUBLANES:nc + SUBLANES + 1, LANES:2 * LANES],
                          0.0).astype(BF16) for acc in accs])
        ck, cv = halves

        kvn = [kvn_ref[b] for b in seqs]
        nw = win_ref.shape[1] // 2
        s = [_dot(q[b], win_ref[b, 0:nw, :].astype(BF16)) for b in seqs]
        s_new = [jnp.sum(qf[b] * kvn[b][:, 4 * LANES:5 * LANES], axis=-1, keepdims=True) for b in seqs]
        m = [jnp.maximum(jnp.max(s[b], axis=-1, keepdims=True), s_new[b]) for b in seqs]
        e = [jnp.exp2(s[b] - m[b]) for b in seqs]
        e_new = [jnp.exp2(s_new[b] - m[b]) for b in seqs]
        o_win = [(_dot_nt(e[b].astype(BF16), win_ref[b, nw:, :].astype(BF16)) + e_new[b] * kvn[b][:, 5 * LANES:6 * LANES])
                 / (jnp.sum(e[b], axis=-1, keepdims=True) + e_new[b]) for b in seqs]

        ncol = lax.broadcasted_iota(jnp.int32, (1, nc), 1)
        cmp_valid = (ncol * CMP_STRIDE + CMP_LEN - 1 <= past) & (ncol < n_cmp)
        p = [_masked_softmax2(jnp.where(cmp_valid, _dot_nt(q[b], ck[b]), NEG)) for b in seqs]
        o_cmp = [_dot(p[b].astype(BF16), cv[b]) for b in seqs]

        prow = lax.broadcasted_iota(jnp.int32, (heads, nc), 0)
        psum = [jnp.where(prow < NSA_REP,
                          jnp.sum(jnp.where(prow < NSA_REP, p[b], 0.0), axis=0, keepdims=True),
                          jnp.sum(jnp.where(prow < NSA_REP, 0.0, p[b]), axis=0, keepdims=True)) for b in seqs]
        imp = [sum(_dot_nt(piece, mt_ref[...]) for piece in _split3(psum[b])) for b in seqs]
        blk = lax.broadcasted_iota(jnp.int32, (heads, nsp), 1)
        forced = (blk == 0) | (blk == cur) | (blk == cur - 1)
        score = [jnp.where(blk * SEL_LEN <= past, imp[b] + jnp.where(forced, SEL_BONUS, 0.0), NEG) for b in seqs]
        score_t = [x.T for x in score]
        bi = lax.broadcasted_iota(jnp.int32, (nsp, nsp), 0)
        bj = lax.broadcasted_iota(jnp.int32, (nsp, nsp), 1)
        tie = jnp.where(bi < bj, 1.0, 0.0)
        head_row = lax.broadcasted_iota(jnp.int32, (heads, nsp), 0)
        sel8 = []
        for b in seqs:
            picks = []
            for g in range(NSA_KV_HEADS):
                c = g * NSA_REP
                s_i, s_j = score_t[b][:, c:c + 1], score[b][c:c + 1, :]
                beats = jnp.where(s_i > s_j, 1.0, jnp.where(s_i == s_j, tie, 0.0))
                rank = jnp.sum(beats, axis=0, keepdims=True)
                picks.append(jnp.where(rank < SEL_TOPK, jnp.where(s_j > NEG / 2, 1.0, 0.0), 0.0))
            sel8.append(jnp.where(head_row < NSA_REP, picks[0], picks[1]))
        keep = [_dot(sel8[b].astype(BF16), exp_ref[...]) > 0.5 for b in seqs]

        new_ok = [sel8[b][:, cur:cur + 1] > 0.5 for b in seqs]
        s = [jnp.where(keep[b], s_ref[b], NEG) for b in seqs]
        s_new = [jnp.where(new_ok[b], jnp.sum(qf[b] * kvn[b][:, 2 * LANES:3 * LANES], axis=-1, keepdims=True), NEG)
                 for b in seqs]
        m = [jnp.maximum(jnp.max(s[b], axis=-1, keepdims=True), s_new[b]) for b in seqs]
        e = [jnp.exp2(s[b] - m[b]) for b in seqs]
        e_new = [jnp.where(new_ok[b], jnp.exp2(s_new[b] - m[b]), 0.0) for b in seqs]
        o_sel = [(_dot_nt(e[b].astype(BF16), vs_ref[b]) + e_new[b] * kvn[b][:, 3 * LANES:4 * LANES])
                 / (jnp.sum(e[b], axis=-1, keepdims=True) + e_new[b]) for b in seqs]

        for b in seqs:
            gates = _sigmoid(small_ref[b])
            gate = lambda br: jnp.sum(jnp.where(lane == row * 3 + br, gates, 0.0), axis=-1, keepdims=True)
            o = gate(0) * o_cmp[b] + gate(1) * o_sel[b] + gate(2) * o_win[b]
            for r in range(NSA_REP):
                o_ref[b, :, r * LANES:(r + 1) * LANES] = jnp.where(
                    lane[0:1] < HEAD_DIM, o[r:r + 1], o[NSA_REP + r:NSA_REP + r + 1]).astype(BF16)


def _nsa_sample(page_table, cache, q, kvn, small, win, wlo, whi, pelo, pehi, mt, *, pps, nb):
    batch, n_pages = page_table.shape
    past = n_pages * PAGE_SIZE
    rows = cache.shape[1]
    cpp = PAGE_SIZE // CMP_STRIDE
    nsp = mt.shape[0]
    expand = jnp.asarray(np.arange(nsp)[:, None] == np.arange(past)[None, :] // SEL_LEN, BF16)
    tok = np.arange(PAGE_SIZE)
    perm = jnp.asarray((tok[None, :] % CMP_STRIDE) * cpp + tok[None, :] // CMP_STRIDE == tok[:, None], BF16)

    def page_spec(r, k):
        return pl.BlockSpec((None, rows, PAGE_SIZE),
                            lambda b, s, pt: (pt[(b * nb + r) * n_pages + s * pps + k], 0, 0))

    per_b = lambda shape: pl.BlockSpec((nb,) + shape, lambda b, s, pt: (b, 0, 0))
    const = lambda a: pl.BlockSpec(a.shape, lambda b, s, pt: (0,) * a.ndim, pipeline_mode=pl.Buffered(1))
    grid_spec = pltpu.PrefetchScalarGridSpec(
        num_scalar_prefetch=1,
        grid=(batch // nb, n_pages // pps),
        in_specs=[page_spec(r, k) for r in range(nb) for k in range(pps)] + [
            per_b((1, NSA_Q_W)), per_b((1, 6 * NSA_KV_W)), per_b((1, LANES)), per_b(win.shape[1:]),
            const(perm), const(wlo), const(whi), const(pelo), const(pehi), const(mt), const(expand)],
        out_specs=per_b((1, NSA_Q_W)),
        scratch_shapes=[pltpu.VMEM((nb, 2, past // CMP_STRIDE + 2 * SUBLANES, CMP_STRIDE * LANES), F32),
                        pltpu.VMEM((nb, NSA_HEADS, past), F32),
                        pltpu.VMEM((nb, LANES, past), BF16)],
    )
    return pl.pallas_call(
        functools.partial(_nsa_sample_kernel, past=past, pps=pps, nb=nb),
        grid_spec=grid_spec,
        out_shape=jax.ShapeDtypeStruct((batch, 1, NSA_Q_W), BF16),
        compiler_params=pltpu.CompilerParams(dimension_semantics=("arbitrary", "arbitrary"),
                                             vmem_limit_bytes=VMEM_LIMIT),
        name="nsa_sample",
    )(page_table.reshape(-1), *([cache] * (nb * pps)), q, kvn, small, win, perm, wlo, whi, pelo, pehi, mt, expand)


def _l2norm(x):
    return x * lax.rsqrt(jnp.sum(x * x, axis=-1, keepdims=True) + EPS)


def _softplus(x):
    return jnp.maximum(x, 0.0) + jnp.log(1.0 + jnp.exp(-jnp.abs(x)))


def _gdn_step_kernel(x_ref, z_ref, small_ref, buf_ref, s0_ref, cw_ref, pcol_ref, nw_ref, o_ref, sout_ref):
    hist = CONV_W - 1
    conv = jnp.sum(buf_ref[...] * cw_ref[0:hist, :], axis=0, keepdims=True) + x_ref[...] * cw_ref[hist:CONV_W, :]
    act = _silu(conv)
    small = small_ref[...]
    g_all = -jnp.exp(pcol_ref[0:1, :]) * _softplus(small + pcol_ref[1:2, :])
    beta_all = _sigmoid(small)
    row = lax.broadcasted_iota(jnp.int32, (SUBLANES, GDN_DK), 0)
    for h in range(GDN_HEADS):
        hs = slice(h * GDN_DK, (h + 1) * GDN_DK)
        qh = _l2norm(act[:, hs]) * (GDN_DK ** -0.5)
        kh = _l2norm(act[:, GDN_QK_W + h * GDN_DK:GDN_QK_W + (h + 1) * GDN_DK])
        vh = act[:, 2 * GDN_QK_W + h * GDN_DV:2 * GDN_QK_W + (h + 1) * GDN_DV]
        eg = jnp.exp(g_all[:, A_COL + h:A_COL + h + 1])
        bt = beta_all[:, B_COL + h:B_COL + h + 1]
        s = s0_ref[h]
        kq = jnp.where(row == 0, kh, jnp.where(row == 1, qh, 0.0)).astype(BF16)
        ks_qs = _dot(kq, s.astype(BF16))
        vn = bt * (vh - eg * ks_qs[0:1])
        o = eg * ks_qs[1:2] + jnp.sum(qh * kh, axis=-1, keepdims=True) * vn
        k8 = jnp.where(row == 0, kh, 0.0).astype(BF16)
        vn8 = jnp.where(row == 0, vn, 0.0).astype(BF16)
        sout_ref[h] = s * eg + lax.dot_general(k8, vn8, TN_DIMS, preferred_element_type=F32)
        on = o * lax.rsqrt(jnp.mean(o * o, axis=-1, keepdims=True) + EPS) * nw_ref[...]
        o_ref[:, hs] = (on * _silu(z_ref[:, hs])).astype(BF16)


def _gdn_step(x, z, small, buf, s0, cw, pcol, nw):
    batch = x.shape[0]
    per_b = lambda shape: pl.BlockSpec((None,) + shape, lambda b: (b,) + (0,) * len(shape))
    state = (GDN_HEADS, GDN_DK, GDN_DV)
    return pl.pallas_call(
        _gdn_step_kernel,
        grid=(batch,),
        in_specs=[per_b((1, GDN_CONV_CH)), per_b((1, GDN_V_W)), per_b((1, LANES)), per_b((CONV_W - 1, GDN_CONV_CH)),
                  per_b(state), _const_spec(cw.shape), _const_spec(pcol.shape), _const_spec(nw.shape)],
        out_specs=(per_b((1, GDN_V_W)), per_b(state)),
        out_shape=(jax.ShapeDtypeStruct((batch, 1, GDN_V_W), BF16),
                   jax.ShapeDtypeStruct((batch,) + state, F32)),
        compiler_params=pltpu.CompilerParams(dimension_semantics=("arbitrary",), vmem_limit_bytes=VMEM_LIMIT),
        name="gdn_step",
    )(x[:, None], z[:, None], small[:, None], buf, s0, cw, pcol, nw)


GDN_CHUNK = 64
GDN_STACK = GDN_HEADS * GDN_CHUNK


def _stack_heads(x, col0, width):
    return jnp.concatenate([x[:, col0 + h * width:col0 + (h + 1) * width] for h in range(GDN_HEADS)], axis=0)


def _gdn_prep_kernel(act_ref, small_ref, smallt_ref, pcol_ref, prow_ref, tril_ref, triu_ref,
                     u_ref, w_ref, qg_ref, kgt_ref, aqk_ref, gl_ref, *, tb):
    ck = GDN_CHUNK
    st = GDN_STACK
    act = act_ref[...]

    small = small_ref[...]
    g_col = -jnp.exp(pcol_ref[0:1, :]) * _softplus(small + pcol_ref[1:2, :])
    beta = _sigmoid(small)
    g_row = -jnp.exp(prow_ref[:, 0:1]) * _softplus(smallt_ref[...] + prow_ref[:, 1:2])
    gcum_col = sum(_dot(tril_ref[...], piece) for piece in _split3(g_col))
    gcum_row = sum(_dot(piece, triu_ref[...]) for piece in _split3(g_row))

    ii = lax.broadcasted_iota(jnp.int32, (st, st), 0)
    jj = lax.broadcasted_iota(jnp.int32, (st, st), 1)
    same_head = (ii // ck) == (jj // ck)
    incl = same_head & (ii >= jj)
    strict = same_head & (ii > jj)
    eye = jnp.where(ii == jj, 1.0, 0.0)
    hrow = lax.broadcasted_iota(jnp.int32, (SUBLANES, LANES), 0)

    lms, rhss = [], []
    for ci in range(tb // ck):
        r0 = ci * ck
        rs = slice(r0, r0 + ck)
        gc = _stack_heads(gcum_col[rs], A_COL, 1)
        bt = _stack_heads(beta[rs], B_COL, 1)
        gr = jnp.concatenate([gcum_row[h:h + 1, r0:r0 + ck] for h in range(GDN_HEADS)], axis=1)
        glast = [gcum_col[r0 + ck - 1:r0 + ck, A_COL + h:A_COL + h + 1] for h in range(GDN_HEADS)]
        gl_stack = jnp.concatenate([jnp.broadcast_to(x, (ck, 1)) for x in glast], axis=0)
        dec = jnp.where(incl, jnp.exp(jnp.where(incl, gc - gr, 0.0)), 0.0)
        qs = _stack_heads(act[rs], 0, GDN_DK)
        ks = _stack_heads(act[rs], GDN_QK_W, GDN_DK)
        vs = _stack_heads(act[rs], 2 * GDN_QK_W, GDN_DV)
        kb = ks * bt
        k16 = ks.astype(BF16)
        lms.append(jnp.where(strict, _dot_nt(kb.astype(BF16), k16) * dec, 0.0))
        eg = jnp.exp(gc)
        rhss.append(jnp.concatenate([vs * bt, kb * eg], axis=1).astype(BF16))
        orow = slice(ci * st, (ci + 1) * st)
        qg_ref[orow, :] = (qs * eg).astype(BF16)
        aqk_ref[orow, :] = (_dot_nt(qs.astype(BF16), k16) * dec).astype(BF16)
        kgt_ref[ci * GDN_DK:(ci + 1) * GDN_DK, :] = (ks * jnp.exp(gl_stack - gc)).T.astype(BF16)
        gl = jnp.zeros((SUBLANES, LANES), F32)
        for h in range(GDN_HEADS):
            gl = jnp.where(hrow == h, jnp.exp(glast[h]), gl)
        gl_ref[ci * SUBLANES:(ci + 1) * SUBLANES, :] = gl

    ainvs = [eye - lm for lm in lms]
    pws = lms
    n = 2
    while n < ck:
        pw16s = [pw.astype(BF16) for pw in pws]
        pws = [_dot(pw16, pw16) for pw16 in pw16s]
        ainvs = [ainv + _dot(ainv.astype(BF16), pw.astype(BF16)) for ainv, pw in zip(ainvs, pws)]
        n *= 2
    for ci, (ainv, rhs) in enumerate(zip(ainvs, rhss)):
        sol = _dot(ainv.astype(BF16), rhs)
        orow = slice(ci * st, (ci + 1) * st)
        u_ref[orow, :] = sol[:, 0:GDN_DV]
        w_ref[orow, :] = sol[:, GDN_DV:].astype(BF16)


def _gdn_prep(act, small, smallt, pcol, prow, *, batch, seq, tb):
    nblk = seq // tb
    ncb = tb // GDN_CHUNK
    blk = np.arange(tb)
    same = (blk[:, None] // GDN_CHUNK) == (blk[None, :] // GDN_CHUNK)
    tril = jnp.asarray(same & (blk[:, None] >= blk[None, :]), BF16)
    triu = jnp.asarray(same & (blk[:, None] <= blk[None, :]), BF16)
    row_spec = lambda r, w: pl.BlockSpec((r, w), lambda b, j: (b * nblk + j, 0))
    per_b = lambda shape: pl.BlockSpec((None,) + shape, lambda b, j: (b,) + (0,) * len(shape))
    n_chunks = batch * seq // GDN_CHUNK
    out_shape = (jax.ShapeDtypeStruct((n_chunks * GDN_STACK, GDN_DV), F32),
                 jax.ShapeDtypeStruct((n_chunks * GDN_STACK, GDN_DK), BF16),
                 jax.ShapeDtypeStruct((n_chunks * GDN_STACK, GDN_DK), BF16),
                 jax.ShapeDtypeStruct((n_chunks * GDN_DK, GDN_STACK), BF16),
                 jax.ShapeDtypeStruct((n_chunks * GDN_STACK, GDN_STACK), BF16),
                 jax.ShapeDtypeStruct((n_chunks * SUBLANES, LANES), F32))
    out_specs = (row_spec(ncb * GDN_STACK, GDN_DV), row_spec(ncb * GDN_STACK, GDN_DK),
                 row_spec(ncb * GDN_STACK, GDN_DK), row_spec(ncb * GDN_DK, GDN_STACK),
                 row_spec(ncb * GDN_STACK, GDN_STACK), row_spec(ncb * SUBLANES, LANES))
    return pl.pallas_call(
        functools.partial(_gdn_prep_kernel, tb=tb),
        grid=(batch, nblk),
        in_specs=[row_spec(tb, GDN_CONV_CH), row_spec(tb, LANES),
                  pl.BlockSpec((None, SUBLANES, tb), lambda b, j: (b, 0, j)),
                  _const_spec(pcol.shape), _const_spec(prow.shape),
                  _const_spec(tril.shape), _const_spec(triu.shape)],
        out_specs=out_specs,
        out_shape=out_shape,
        compiler_params=pltpu.CompilerParams(dimension_semantics=("arbitrary", "arbitrary"),
                                             vmem_limit_bytes=VMEM_LIMIT),
        name="gdn_prep",
    )(act, small, smallt, pcol, prow, tril, triu)


def _gdn_scan_kernel(u_ref, w_ref, qg_ref, kgt_ref, aqk_ref, gl_ref, z_ref, s0_ref, nw_ref, o_ref, sout_ref, s_ref,
                     *, batch, ncb):
    ck = GDN_CHUNK
    st = GDN_STACK
    sw = GDN_HEADS * GDN_DK

    @pl.when(pl.program_id(0) == 0)
    def _():
        s_ref[...] = s0_ref[...]

    wide_mask = (lax.broadcasted_iota(jnp.int32, (st, sw), 0) // ck
                 == lax.broadcasted_iota(jnp.int32, (st, sw), 1) // GDN_DK)
    tall_mask = (lax.broadcasted_iota(jnp.int32, (sw, st), 0) // GDN_DK
                 == lax.broadcasted_iota(jnp.int32, (sw, st), 1) // ck)

    seqs = range(batch)
    s = [s_ref[b] for b in seqs]
    for ci in range(ncb):
        rows = slice(ci * st, (ci + 1) * st)
        toks = slice(ci * ck, (ci + 1) * ck)
        lhs, kgt_bd, gl_rows = [], [], []
        for b in seqs:
            w_bd = jnp.where(wide_mask, jnp.concatenate([w_ref[b, rows, :]] * GDN_HEADS, axis=1), 0.0)
            qg_bd = jnp.where(wide_mask, jnp.concatenate([qg_ref[b, rows, :]] * GDN_HEADS, axis=1), 0.0)
            lhs.append(jnp.concatenate([w_bd, qg_bd], axis=0))
            kgt = kgt_ref[b, ci * GDN_DK:(ci + 1) * GDN_DK, :]
            kgt_bd.append(jnp.where(tall_mask, jnp.concatenate([kgt] * GDN_HEADS, axis=0), 0.0))
            gl = gl_ref[b, ci * SUBLANES:(ci + 1) * SUBLANES, :]
            gl_rows.append(jnp.concatenate(
                [jnp.broadcast_to(gl[h:h + 1], (GDN_DK, GDN_DV)) for h in range(GDN_HEADS)], axis=0))
        t1 = [_dot(lhs[b], s[b].astype(BF16)) for b in seqs]
        vn16 = [(u_ref[b, rows, :] - t1[b][0:st]).astype(BF16) for b in seqs]
        o = [t1[b][st:] + _dot(aqk_ref[b, rows, :], vn16[b]) for b in seqs]
        s = [s[b] * gl_rows[b] + _dot(kgt_bd[b], vn16[b]) for b in seqs]
        for b in seqs:
            for h in range(GDN_HEADS):
                oh = o[b][h * ck:(h + 1) * ck]
                on = oh * lax.rsqrt(jnp.mean(oh * oh, axis=-1, keepdims=True) + EPS) * nw_ref[...]
                hs = slice(h * GDN_DV, (h + 1) * GDN_DV)
                o_ref[b, toks, hs] = (on * _silu(z_ref[b, toks, hs])).astype(BF16)

    for b in seqs:
        s_ref[b] = s[b]
    sout_ref[...] = s_ref[...]


def _gdn_scan(u, w, qg, kgt, aqk, gl, z, s0, nw, *, batch, seq, tb):
    nblk = seq // tb
    ncb = tb // GDN_CHUNK
    cps = seq // GDN_CHUNK
    sw = GDN_HEADS * GDN_DK
    r3 = lambda a, rows_per_chunk: a.reshape(batch, cps * rows_per_chunk, a.shape[-1])
    blk = lambda rows, width: pl.BlockSpec((batch, rows, width), lambda j: (0, j, 0))
    full = pl.BlockSpec((batch, sw, GDN_DV), lambda j: (0, 0, 0))
    return pl.pallas_call(
        functools.partial(_gdn_scan_kernel, batch=batch, ncb=ncb),
        grid=(nblk,),
        in_specs=[blk(ncb * GDN_STACK, GDN_DV), blk(ncb * GDN_STACK, GDN_DK), blk(ncb * GDN_STACK, GDN_DK),
                  blk(ncb * GDN_DK, GDN_STACK), blk(ncb * GDN_STACK, GDN_STACK), blk(ncb * SUBLANES, LANES),
                  blk(tb, GDN_V_W), full, _const_spec(nw.shape)],
        out_specs=(blk(tb, GDN_V_W), full),
        out_shape=(jax.ShapeDtypeStruct((batch, seq, GDN_V_W), BF16),
                   jax.ShapeDtypeStruct((batch, sw, GDN_DV), F32)),
        scratch_shapes=[pltpu.VMEM((batch, sw, GDN_DV), F32)],
        compiler_params=pltpu.CompilerParams(dimension_semantics=("arbitrary",), vmem_limit_bytes=VMEM_LIMIT),
        name="gdn_scan",
    )(r3(u, GDN_STACK), r3(w, GDN_STACK), r3(qg, GDN_STACK), r3(kgt, GDN_DK), r3(aqk, GDN_STACK),
      r3(gl, SUBLANES), z.reshape(batch, seq, GDN_V_W), s0.reshape(batch, sw, GDN_DV), nw)


def _rope_tables(pos):
    half = HEAD_DIM // 2
    inv = np.power(ROPE_THETA, -np.arange(half, dtype=np.float64) * 2.0 / HEAD_DIM)
    ang = np.asarray(pos, np.float64)[:, None] * inv[None, :]
    cos, sin = np.cos(ang).astype(np.float32), np.sin(ang).astype(np.float32)
    return tuple(jnp.asarray(t) for t in (np.tile(cos, (1, 4)), np.concatenate([-sin, sin, -sin, sin], axis=1),
                                          cos.T, sin.T))


def _cmp_to_sel_t(n_cmp, n_sel, rows, cols):
    cs = np.arange(cols)[None, :] * CMP_STRIDE
    ss = np.arange(rows)[:, None] * SEL_LEN
    hit = (cs < ss + SEL_LEN) & (cs + CMP_LEN > ss)
    hit &= (np.arange(cols)[None, :] < n_cmp) & (np.arange(rows)[:, None] < n_sel)
    return jnp.asarray(hit, BF16)


def _layer_weights(w_in, cmp_pe, cmp_w, conv_w, a_log, dt_bias, gdn_norm, w_out):
    cuts = np.cumsum([NSA_Q_W, 6 * NSA_KV_W, GATE_COLS, GDN_CONV_CH, GDN_V_W, GDN_HEADS]).tolist()
    wq, wkv, wgt, wconv, wz, wa, wb = jnp.split(w_in, cuts, axis=1)
    order = np.array([g * NSA_REP + r for r in range(NSA_REP) for g in range(NSA_KV_HEADS)])
    cols = (order[:, None] * HEAD_DIM + np.arange(HEAD_DIM)[None, :]).reshape(-1)
    wm = jnp.concatenate([wq[:, cols], wkv, wconv, wz], axis=1).astype(BF16)
    wkvt = wkv.T.astype(BF16)
    ws = jnp.concatenate([wgt, wa, wb], axis=1)
    ws = jnp.pad(ws, ((0, 0), (0, LANES - ws.shape[1]))).astype(BF16)
    wo = jnp.concatenate([w_out[:NSA_Q_W][cols], w_out[NSA_Q_W:]], axis=0).astype(BF16)

    def blockdiag(l0):
        wk, wv = cmp_w[0, l0:l0 + CMP_STRIDE], cmp_w[1, l0:l0 + CMP_STRIDE]
        z = jnp.zeros_like(wk)
        rows = [jnp.concatenate(r, axis=2) for r in ([wk, z, z, z], [z, wk, z, z], [z, z, wv, z], [z, z, z, wv])]
        return jnp.concatenate(rows, axis=1).astype(BF16)

    def long_contraction(c):
        halves = []
        for l0 in (0, CMP_STRIDE):
            w = cmp_w[c, l0:l0 + CMP_STRIDE]
            z = jnp.zeros_like(w)
            halves.append(jnp.concatenate([jnp.concatenate([w, z], axis=2), jnp.concatenate([z, w], axis=2)], axis=1))
        return jnp.concatenate(halves, axis=2).reshape(CMP_STRIDE * LANES, 2 * LANES).astype(BF16)

    def pe_rows(l0):
        pk, pv = cmp_pe[0, l0:l0 + CMP_STRIDE], cmp_pe[1, l0:l0 + CMP_STRIDE]
        return jnp.concatenate([pk, pk, pv, pv], axis=1)

    pcol = jnp.zeros((2, LANES), F32).at[0, A_COL:A_COL + GDN_HEADS].set(a_log)
    pcol = pcol.at[1, A_COL:A_COL + GDN_HEADS].set(dt_bias)
    prow = jnp.zeros((SUBLANES, 2), F32).at[0:GDN_HEADS, 0].set(a_log).at[0:GDN_HEADS, 1].set(dt_bias)
    return dict(wm=wm, ws=ws, wkvt=wkvt, wo=wo, wk_long=long_contraction(0), wv_long=long_contraction(1),
                wlo=blockdiag(0), whi=blockdiag(CMP_STRIDE), pelo=pe_rows(0),
                pehi=pe_rows(CMP_STRIDE), cw=conv_w, pcol=pcol, prow=prow, nw=gdn_norm[None, :])


def kernel(x_prompt, x_sample, cache_nsa_kv, cache_nsa_win, state_gdn_S, state_gdn_conv, page_table, norm_mix, w_in,
           nsa_cmp_pe, nsa_cmp_w, gdn_conv_w, gdn_a_log, gdn_dt_bias, gdn_norm, w_out, norm_ffn, w_gate_up, w_down,
           norm_final):
    bp, tp, _ = x_prompt.shape
    bs, ts, _ = x_sample.shape
    depth = w_in.shape[0]
    n_pages = page_table.shape[1]
    past = n_pages * PAGE_SIZE
    assert depth == 1 and ts == 1, "one layer and one new token per sample row"
    assert tp % 512 == 0 and tp // SEL_LEN <= LANES and past % 512 == 0
    l = 0
    wts = _layer_weights(w_in[l], nsa_cmp_pe[l], nsa_cmp_w[l], gdn_conv_w[l], gdn_a_log[l], gdn_dt_bias[l],
                         gdn_norm[l], w_out[l])
    g_mix, g_ffn, g_fin = norm_mix[l][None, :], norm_ffn[l][None, :], norm_final[None, :]
    wgu, wd = w_gate_up[l].astype(BF16), w_down[l].astype(BF16)
    hist = CONV_W - 1

    xp = x_prompt.reshape(bp * tp, D_MODEL)
    q, kv, conv_in, z, small, kaug, vaug, kwin, vwaug, kv4t_p, kvwt_p, act = _proj(
        xp, g_mix, wts["wm"], wts["ws"], wts["wkvt"], wts["cw"], jnp.zeros((bp, hist, GDN_CONV_CH), F32),
        *_rope_tables(np.arange(tp)), tm=256)
    ckv = _compress(kv, wts["wlo"], wts["whi"], wts["pelo"], wts["pehi"], batch=bp, seq=tp)
    nc = tp // CMP_STRIDE
    mt_p = _cmp_to_sel_t(nc - 1, tp // SEL_LEN, LANES, nc)
    o_nsa = _nsa_prompt(q, small, kaug, vaug, kwin, vwaug, ckv, mt_p, batch=bp, seq=tp, kc=512)
    smallt = small[:, A_COL:A_COL + SUBLANES].reshape(bp, tp, SUBLANES).transpose(0, 2, 1)
    prep = _gdn_prep(act, small, smallt, wts["pcol"], wts["prow"], batch=bp, seq=tp, tb=256)
    o_gdn, s_p = _gdn_scan(*prep, z, jnp.zeros((bp, GDN_HEADS, GDN_DK, GDN_DV), F32), wts["nw"],
                           batch=bp, seq=tp, tb=256)
    o_gdn = o_gdn.reshape(bp * tp, GDN_V_W)
    s_p = s_p.reshape(bp, GDN_HEADS, GDN_DK, GDN_DV)
    y_p = _mix_ffn(xp, o_nsa, o_gdn, wts["wo"], g_ffn, wgu, wd, g_fin, tm=512)
    tok_major = lambda a, comps: a.reshape(bp, comps, NSA_KV_HEADS, HEAD_DIM, -1).transpose(0, 4, 1, 2, 3)
    kv_p = tok_major(kv4t_p, 4)
    win_p = tok_major(kvwt_p[:, :, tp - min(WINDOW, tp):], 2)
    conv_p = conv_in.reshape(bp, tp, GDN_CONV_CH)[:, tp - hist:]

    xs = x_sample.reshape(bs, D_MODEL)
    q, kv, conv_in, z, small = _proj(xs, g_mix, wts["wm"], wts["ws"], wts["wkvt"], wts["cw"],
                                     jnp.zeros((1, hist, GDN_CONV_CH), F32),
                                     *_rope_tables(np.full((bs,), past)), tm=bs)[:5]
    n_sel = past // SEL_LEN + 1
    nsp = -(-n_sel // LANES) * LANES
    mt_s = _cmp_to_sel_t(past // CMP_STRIDE - 1, n_sel, nsp, past // CMP_STRIDE)
    cache = cache_nsa_kv[l].transpose(0, 2, 3, 4, 1).reshape(-1, 4 * NSA_KV_W, PAGE_SIZE)
    win = cache_nsa_win[l].transpose(0, 2, 3, 4, 1).reshape(bs, 2 * NSA_KV_W, -1)
    o_nsa = _nsa_sample(page_table, cache, q[:, None], kv[:, None], small[:, None], win, wts["wk_long"],
                        wts["wv_long"], wts["pelo"], wts["pehi"], mt_s, pps=min(16, n_pages), nb=2).reshape(bs, NSA_Q_W)
    o_gdn, s_s = _gdn_step(conv_in, z, small, state_gdn_conv[l], state_gdn_S[l], wts["cw"], wts["pcol"], wts["nw"])
    o_gdn = o_gdn.reshape(bs, GDN_V_W)
    y_s = _mix_ffn(xs, o_nsa, o_gdn, wts["wo"], g_ffn, wgu, wd, g_fin, tm=bs)
    kv_s = kv.reshape(bs, 1, 3, 2, NSA_KV_HEADS, HEAD_DIM)
    win_s = jnp.concatenate([cache_nsa_win[l], kv_s[:, :, 2]], axis=1)[:, -min(WINDOW, past + 1):]
    conv_s = jnp.concatenate([state_gdn_conv[l], conv_in[:, None]], axis=1)[:, -hist:]

    return (y_p.reshape(bp, tp, D_MODEL), y_s.reshape(bs, 1, D_MODEL),
            kv_p[None], win_p[None], s_p[None], conv_p[None],
            kv_s[:, :, 0:2].reshape(bs, 1, 4, NSA_KV_HEADS, HEAD_DIM)[None], win_s[None], s_s[None], conv_s[None])
```

```python
import functools

import numpy as np
import jax
import jax.numpy as jnp
from jax import lax
from jax.experimental import pallas as pl
from jax.experimental.pallas import tpu as pltpu

F32 = jnp.float32
BF16 = jnp.bfloat16

D_MODEL = 1024
PAGE_SIZE = 128
HEAD_DIM = 64
NSA_HEADS = 8
NSA_KV_HEADS = 2
NSA_REP = NSA_HEADS // NSA_KV_HEADS
CMP_LEN = 32
CMP_STRIDE = 16
SEL_LEN = 64
SEL_TOPK = 16
WINDOW = 512
Q_BLOCK = 128
ROPE_THETA = 10000.0
GDN_HEADS = 4
GDN_DK = 128
GDN_DV = 128
CONV_W = 4
NSA_Q_W = NSA_HEADS * HEAD_DIM
NSA_KV_W = NSA_KV_HEADS * HEAD_DIM
GDN_QK_W = GDN_HEADS * GDN_DK
GDN_V_W = GDN_HEADS * GDN_DV
GDN_CONV_CH = 2 * GDN_QK_W + GDN_V_W
NEG = -1e30
SEL_UNROLL = 4
LOG2E = 1.4426950408889634
SEL_BONUS = 1e4
EPS = 1e-6

LANES = 128
SUBLANES = 8
VMEM_LIMIT = 56 * 1024 * 1024

GATE_COLS = 3 * NSA_HEADS
A_COL = GATE_COLS
B_COL = GATE_COLS + GDN_HEADS

NT_DIMS = (((1,), (1,)), ((), ()))
TN_DIMS = (((0,), (0,)), ((), ()))


def _dot(a, b):
    return jnp.dot(a, b, preferred_element_type=F32)


def _dot_nt(a, b):
    return lax.dot_general(a, b, NT_DIMS, preferred_element_type=F32)


def _sigmoid(x):
    return 1.0 / (1.0 + jnp.exp(-x))


def _silu(x):
    return x * _sigmoid(x)


def _split3(x):
    p1 = x.astype(BF16)
    r1 = x - p1.astype(F32)
    p2 = r1.astype(BF16)
    p3 = (r1 - p2.astype(F32)).astype(BF16)
    return p1, p2, p3


def _const_spec(shape):
    nd = len(shape)
    return pl.BlockSpec(shape, lambda *_: (0,) * nd, pipeline_mode=pl.Buffered(1))


def _proj_kernel(x_ref, g_ref, wm_ref, ws_ref, wkvt_ref, cos_ref, sin_ref, cost_ref, sint_ref, cw_ref, buf_ref,
                 q_ref, kv_ref, conv_ref, z_ref, small_ref, kaug_ref, vaug_ref, kwin_ref, vwaug_ref,
                 kv4t_ref, kvwt_ref, act_ref, xs_ref, *, tm, pos_rows):
    pad = SUBLANES
    hist = CONV_W - 1
    c0 = NSA_Q_W + 6 * NSA_KV_W

    @pl.when(pl.program_id(0) % (pos_rows // tm) == 0)
    def _():
        xs_ref[0:pad, :] = jnp.zeros((pad, GDN_CONV_CH), F32)
        xs_ref[pad - hist:pad, :] = buf_ref[...]

    x = x_ref[...]
    ms = jnp.mean(x * x, axis=-1, keepdims=True)
    xn = (x * lax.rsqrt(ms + EPS) * g_ref[...]).astype(BF16)
    cos = cos_ref[...]
    sin = sin_ref[...]
    lane = lax.broadcasted_iota(jnp.int32, (tm, LANES), 1)
    low_half = (lane % HEAD_DIM) < (HEAD_DIM // 2)

    def rope(v):
        rot = jnp.where(low_half, pltpu.roll(v, LANES - HEAD_DIM // 2, 1), pltpu.roll(v, HEAD_DIM // 2, 1))
        return v * cos + rot * sin

    wide = 2 * LANES

    def delta_piece(i):
        cols = slice(i * wide, (i + 1) * wide)
        part = i * wide // GDN_QK_W
        conv_in = _dot(xn, wm_ref[:, c0 + i * wide:c0 + (i + 1) * wide])
        conv_ref[:, cols] = conv_in
        xs_ref[pad:pad + tm, cols] = conv_in
        xs = xs_ref[:, cols]
        conv = xs[pad:] * cw_ref[hist:CONV_W, cols]
        for t in range(hist):
            conv = conv + pltpu.roll(xs, hist - t, 0)[pad:] * cw_ref[t:t + 1, cols]
        xs_ref[0:pad, cols] = xs[tm:tm + pad]
        act = _silu(conv)
        for h in range(wide // GDN_DK):
            hs = slice(h * GDN_DK, (h + 1) * GDN_DK)
            dst = act_ref.at[:, i * wide + h * GDN_DK:i * wide + (h + 1) * GDN_DK]
            if part == 0:
                dst[...] = _l2norm(act[:, hs]) * (GDN_DK ** -0.5)
            elif part == 1:
                dst[...] = _l2norm(act[:, hs])
            else:
                dst[...] = act[:, hs]

    def q_piece(i):
        q = _dot(xn, wm_ref[:, i * wide:(i + 1) * wide])
        for r in range(2):
            q_ref[:, i * wide + r * LANES:i * wide + (r + 1) * LANES] = (
                rope(q[:, r * LANES:(r + 1) * LANES]) * (HEAD_DIM ** -0.5 * LOG2E)).astype(BF16)

    group0 = lane < HEAD_DIM

    def kv_piece(br):
        kv = _dot(xn, wm_ref[:, NSA_Q_W + br * wide:NSA_Q_W + (br + 1) * wide])
        k = rope(kv[:, 0:LANES])
        v = kv[:, LANES:wide]
        kv_ref[:, br * wide:br * wide + LANES] = k
        kv_ref[:, br * wide + LANES:(br + 1) * wide] = v
        if br == 1:
            kaug_ref[:, LANES:2 * LANES] = k.astype(BF16)
        elif br == 2:
            kwin_ref[...] = k.astype(BF16)
        if br > 0:
            v_ref = vaug_ref if br == 1 else vwaug_ref
            v_ref[:, 0:LANES] = jnp.where(group0, v, 1.0).astype(BF16)
            v_ref[:, LANES:2 * LANES] = jnp.where(group0, 1.0, v).astype(BF16)

    half = HEAD_DIM // 2

    def kvt_piece(br):
        kvt = _dot_nt(wkvt_ref[br * wide:(br + 1) * wide, :], xn)
        cos_t = cost_ref[...]
        sin_t = sint_ref[...]
        parts = []
        for g in range(NSA_KV_HEADS):
            x1 = kvt[g * HEAD_DIM:g * HEAD_DIM + half]
            x2 = kvt[g * HEAD_DIM + half:(g + 1) * HEAD_DIM]
            parts += [x1 * cos_t - x2 * sin_t, x2 * cos_t + x1 * sin_t]
        dst = kv4t_ref if br < 2 else kvwt_ref
        dst[(br % 2) * wide:(br % 2) * wide + LANES, :] = jnp.concatenate(parts, axis=0)
        dst[(br % 2) * wide + LANES:(br % 2 + 1) * wide, :] = kvt[LANES:wide]

    def z_piece(i):
        zc = c0 + GDN_CONV_CH
        z_ref[:, i * wide:(i + 1) * wide] = _dot(xn, wm_ref[:, zc + i * wide:zc + (i + 1) * wide])

    others = ([functools.partial(q_piece, i) for i in range(NSA_Q_W // wide)]
              + [functools.partial(kv_piece, br) for br in range(3)]
              + [functools.partial(kvt_piece, br) for br in range(3)]
              + [functools.partial(z_piece, i) for i in range(GDN_V_W // wide)])
    n_delta = GDN_CONV_CH // wide
    for i in range(max(n_delta, len(others))):
        if i < n_delta:
            delta_piece(i)
        for other in others[i * len(others) // n_delta:(i + 1) * len(others) // n_delta]:
            other()
    small_ref[...] = _dot(xn, ws_ref[...])

    row0 = (pl.program_id(0) * tm) % pos_rows
    rows = row0 + lax.broadcasted_iota(jnp.int32, (tm, LANES), 0)
    kaug_ref[:, 0:LANES] = jnp.where(rows // SEL_LEN == lane, 1.0, 0.0).astype(BF16)


def _proj(x, g, wm, ws, wkvt, cw, buf, cos, sin, cos_t, sin_t, *, tm):
    rows = x.shape[0]
    pos_rows = cos.shape[0]
    n_pos_blk = pos_rows // tm
    n_seq = rows // pos_rows
    grid = (rows // tm,)
    row_spec = lambda w: pl.BlockSpec((tm, w), lambda i: (i, 0))
    pos_spec = pl.BlockSpec((tm, LANES), lambda i: (i % n_pos_blk, 0))
    pos_t_spec = pl.BlockSpec((HEAD_DIM // 2, tm), lambda i: (0, i % n_pos_blk))
    tok_minor = lambda r: pl.BlockSpec((None, r, tm), lambda i: (i // n_pos_blk, 0, i % n_pos_blk))
    out_shape = (
        jax.ShapeDtypeStruct((rows, NSA_Q_W), BF16),
        jax.ShapeDtypeStruct((rows, 6 * NSA_KV_W), F32),
        jax.ShapeDtypeStruct((rows, GDN_CONV_CH), F32),
        jax.ShapeDtypeStruct((rows, GDN_V_W), F32),
        jax.ShapeDtypeStruct((rows, LANES), F32),
        jax.ShapeDtypeStruct((rows, 2 * LANES), BF16),
        jax.ShapeDtypeStruct((rows, 2 * LANES), BF16),
        jax.ShapeDtypeStruct((rows, LANES), BF16),
        jax.ShapeDtypeStruct((rows, 2 * LANES), BF16),
    )
    out_shape_t = (jax.ShapeDtypeStruct((n_seq, 4 * NSA_KV_W, pos_rows), F32),
                   jax.ShapeDtypeStruct((n_seq, 2 * NSA_KV_W, pos_rows), F32),
                   jax.ShapeDtypeStruct((rows, GDN_CONV_CH), F32))
    return pl.pallas_call(
        functools.partial(_proj_kernel, tm=tm, pos_rows=pos_rows),
        grid=grid,
        in_specs=[row_spec(D_MODEL), _const_spec((1, D_MODEL)), _const_spec(wm.shape), _const_spec(ws.shape),
                  _const_spec(wkvt.shape), pos_spec, pos_spec, pos_t_spec, pos_t_spec, _const_spec(cw.shape),
                  pl.BlockSpec((None, CONV_W - 1, GDN_CONV_CH), lambda i: (i // n_pos_blk, 0, 0))],
        out_specs=(tuple(row_spec(s.shape[1]) for s in out_shape)
                   + (tok_minor(4 * NSA_KV_W), tok_minor(2 * NSA_KV_W), row_spec(GDN_CONV_CH))),
        out_shape=out_shape + out_shape_t,
        scratch_shapes=[pltpu.VMEM((SUBLANES + tm, GDN_CONV_CH), F32)],
        compiler_params=pltpu.CompilerParams(dimension_semantics=("arbitrary",), vmem_limit_bytes=VMEM_LIMIT),
        name="proj",
    )(x, g, wm, ws, wkvt, cos, sin, cos_t, sin_t, cw, buf)


def _mix_ffn_kernel(x_ref, on_ref, og_ref, wo_ref, gf_ref, wgu_ref, wd_ref, gl_ref, y_ref, *, d_ff):
    h = x_ref[...] + _dot(on_ref[...], wo_ref[0:NSA_Q_W, :]) + _dot(og_ref[...], wo_ref[NSA_Q_W:, :])
    ms = jnp.mean(h * h, axis=-1, keepdims=True)
    hn = (h * lax.rsqrt(ms + EPS) * gf_ref[...]).astype(BF16)
    gate = _dot(hn, wgu_ref[:, 0:d_ff])
    up = _dot(hn, wgu_ref[:, d_ff:])
    act = (_silu(gate) * up).astype(BF16)
    h = h + _dot(act, wd_ref[...])
    ms = jnp.mean(h * h, axis=-1, keepdims=True)
    y_ref[...] = h * lax.rsqrt(ms + EPS) * gl_ref[...]


def _mix_ffn(x, o_nsa, o_gdn, wo, gf, wgu, wd, gl, *, tm):
    rows = x.shape[0]
    d_ff = wd.shape[0]
    row_spec = lambda w: pl.BlockSpec((tm, w), lambda i: (i, 0))
    return pl.pallas_call(
        functools.partial(_mix_ffn_kernel, d_ff=d_ff),
        grid=(rows // tm,),
        in_specs=[row_spec(D_MODEL), row_spec(NSA_Q_W), row_spec(GDN_V_W), _const_spec(wo.shape),
                  _const_spec((1, D_MODEL)), _const_spec(wgu.shape), _const_spec(wd.shape),
                  _const_spec((1, D_MODEL))],
        out_specs=row_spec(D_MODEL),
        out_shape=jax.ShapeDtypeStruct((rows, D_MODEL), F32),
        compiler_params=pltpu.CompilerParams(dimension_semantics=("arbitrary",), vmem_limit_bytes=VMEM_LIMIT),
        name="mix_ffn",
    )(x, o_nsa, o_gdn, wo, gf, wgu, wd, gl)


def _compress_rows(load_rows, wlo_ref, whi_ref, pelo_ref, pehi_ref, nc):
    acc_lo = jnp.zeros((nc, 2 * LANES), F32)
    acc_hi = jnp.zeros((nc, 2 * LANES), F32)
    for l in range(CMP_STRIDE):
        x = load_rows(l)
        acc_lo = acc_lo + _dot((x + pelo_ref[l:l + 1, :]).astype(BF16), wlo_ref[l])
        acc_hi = acc_hi + _dot((x + pehi_ref[l:l + 1, :]).astype(BF16), whi_ref[l])
    nxt = pltpu.roll(acc_hi, nc - 1, 0)
    row = lax.broadcasted_iota(jnp.int32, (nc, 2 * LANES), 0)
    return jnp.where(row < nc - 1, acc_lo + nxt, 0.0)


def _compress_kernel(k_ref, v_ref, wlo_ref, whi_ref, pelo_ref, pehi_ref, ckv_ref, *, nc):
    load = lambda l: jnp.concatenate([k_ref[pl.ds(l, nc, stride=CMP_STRIDE), :],
                                      v_ref[pl.ds(l, nc, stride=CMP_STRIDE), :]], axis=1)
    ckv_ref[...] = _compress_rows(load, wlo_ref, whi_ref, pelo_ref, pehi_ref, nc).astype(BF16)


def _compress(kv, wlo, whi, pelo, pehi, *, batch, seq):
    nc = seq // CMP_STRIDE
    return pl.pallas_call(
        functools.partial(_compress_kernel, nc=nc),
        grid=(batch,),
        in_specs=[pl.BlockSpec((seq, LANES), lambda b: (b, 0)), pl.BlockSpec((seq, LANES), lambda b: (b, 1)),
                  _const_spec(wlo.shape), _const_spec(whi.shape), _const_spec(pelo.shape), _const_spec(pehi.shape)],
        out_specs=pl.BlockSpec((nc, 2 * LANES), lambda b: (b, 0)),
        out_shape=jax.ShapeDtypeStruct((batch * nc, 2 * LANES), BF16),
        compiler_params=pltpu.CompilerParams(dimension_semantics=("arbitrary",), vmem_limit_bytes=VMEM_LIMIT),
        name="compress",
    )(kv, kv, wlo, whi, pelo, pehi)


def _masked_softmax2(s):
    m = jnp.maximum(jnp.max(s, axis=-1, keepdims=True), NEG / 8)
    e = jnp.exp2(s - m)
    return e * (1.0 / jnp.maximum(jnp.sum(e, axis=-1, keepdims=True), 1e-30))


def _select_blocks(score, sel, blk, axis, rounds, filler=iter(())):
    n = score.shape[axis]
    for _ in range(rounds):
        mx = jnp.max(score, axis=axis, keepdims=True)
        idx = jnp.min(jnp.where(score == mx, blk, n), axis=axis, keepdims=True)
        hit = blk == idx
        sel = jnp.where(hit, jnp.where(mx > NEG / 2, 1.0, sel), sel)
        score = jnp.where(hit, -3e38, score)
        next(filler, None)
    for _ in filler:
        pass
    return sel


def _nsa_prompt_kernel(q_ref, small_ref, kaug_ref, vaug_ref, kwin_ref, vwaug_ref, ckv_ref, mt_ref, o_ref,
                       *, seq, kc):
    nc = seq // CMP_STRIDE
    qb = Q_BLOCK
    start = pl.program_id(1) * qb
    wlen = WINDOW + qb
    groups = range(NSA_KV_HEADS)
    stack = lambda x: jnp.concatenate([x] * NSA_REP, axis=0)

    lane = lax.broadcasted_iota(jnp.int32, (qb, LANES), 1)
    gates = _sigmoid(small_ref[...])

    qrow_c = lax.broadcasted_iota(jnp.int32, (qb, nc), 0)
    ncol = lax.broadcasted_iota(jnp.int32, (qb, nc), 1)
    cmp_bias = stack(jnp.where(ncol * CMP_STRIDE + (CMP_LEN - 1) - qrow_c <= start, 0.0, NEG))
    tail_bias = stack(jnp.where(lane <= lax.broadcasted_iota(jnp.int32, (qb, qb), 0), 0.0, NEG))
    wbase = pl.multiple_of(jnp.maximum(start - WINDOW, 0), qb)
    back = (lax.broadcasted_iota(jnp.int32, (qb, wlen), 0) - lax.broadcasted_iota(jnp.int32, (qb, wlen), 1)
            + (start - wbase))
    win_bias = stack(jnp.where(back >= 0, jnp.where(back <= WINDOW, 0.0, NEG), NEG))

    glanes = [slice(g * LANES, (g + 1) * LANES) for g in groups]
    qs = [jnp.concatenate([jnp.where((lane >= HEAD_DIM) == (g == 1), q_ref[:, r * LANES:(r + 1) * LANES], 0.0)
                           .astype(BF16) for r in range(NSA_REP)], axis=0) for g in groups]
    ps = [_masked_softmax2(_dot_nt(q, ckv_ref[:, 0:LANES]) + cmp_bias) for q in qs]
    psums =[p[0:qb] + p[qb:2 * qb] + p[2 * qb:3 * qb] + p[3 * qb:4 * qb] for p in ps]
    imp_t = jnp.concatenate([sum(_dot_nt(mt_ref[...], piece) for piece in _split3(psum)) for psum in psums], axis=1)
    o_cmps = [_dot(p.astype(BF16), ckv_ref[:, LANES:2 * LANES]) for p in ps]

    done = {}

    def independent_work():
        s = [_dot_nt(q, kwin_ref[pl.ds(wbase, wlen), :]) + win_bias for q in qs]
        yield
        mx = [jnp.max(x, axis=-1, keepdims=True) for x in s]
        yield
        e = []
        for g in groups:
            e.append(jnp.exp2(s[g] - mx[g]).astype(BF16))
            yield
        acc = [_dot(e[g], vwaug_ref[pl.ds(wbase, wlen), glanes[g]]) for g in groups]
        yield
        o_win = [a / pltpu.roll(a, HEAD_DIM, 1) for a in acc]
        yield
        rows_of = lambda x, r: x[r * qb:(r + 1) * qb]
        done["partial"] = []
        for g in groups:
            done["partial"].append(
                [gates[:, 3 * (g * NSA_REP + r):3 * (g * NSA_REP + r) + 1] * rows_of(o_cmps[g], r)
                 + gates[:, 3 * (g * NSA_REP + r) + 2:3 * (g * NSA_REP + r) + 3] * rows_of(o_win[g], r)
                 for r in range(NSA_REP)])
            yield
        s_tail = [_dot_nt(q, kaug_ref[pl.ds(start, qb), LANES:2 * LANES]) + tail_bias for q in qs]
        yield
        done["m0"] = [jnp.max(x, axis=-1, keepdims=True) for x in s_tail]
        yield
        done["acc0"] = [_dot(jnp.exp2(s_tail[g] - done["m0"][g]).astype(BF16), vaug_ref[pl.ds(start, qb), glanes[g]])
                        for g in groups]
        yield

    blk_t = lax.broadcasted_iota(jnp.int32, (LANES, 2 * qb), 0)
    qpos_t = start + lax.broadcasted_iota(jnp.int32, (LANES, 2 * qb), 1) % qb
    cur_t = qpos_t // SEL_LEN
    visible = blk_t * SEL_LEN <= qpos_t
    forced = (blk_t == 0) | (blk_t == cur_t) | (blk_t == cur_t - 1)
    sel_t = _select_blocks(jnp.where(visible, jnp.where(forced, NEG, imp_t), NEG),
                           jnp.where(visible, jnp.where(forced, 1.0, 0.0), 0.0), blk_t, 0, SEL_TOPK - 3,
                           independent_work())
    partial, m0s, acc0s = done["partial"], done["m0"], done["acc0"]
    bias_t = jnp.where(blk_t * SEL_LEN < start, jnp.where(sel_t > 0.5, 0.0, NEG), NEG).T.astype(BF16)
    n_chunks = (start + kc - 1) // kc
    q_augs = [jnp.concatenate([stack(bias_t[g * qb:(g + 1) * qb]), qs[g]], axis=1) for g in groups]

    def sel_step(c, carry):
        off = pl.multiple_of(c * kc, kc)
        k = kaug_ref[pl.ds(off, kc), :]
        out = []
        for g in groups:
            m, acc = carry[g]
            s = _dot_nt(q_augs[g], k)
            m_new = jnp.maximum(m, jnp.max(s, axis=-1, keepdims=True))
            e = jnp.exp2(s - m_new).astype(BF16)
            out.append((m_new, jnp.exp2(m - m_new) * acc + _dot(e, vaug_ref[pl.ds(off, kc), g * LANES:(g + 1) * LANES])))
        return tuple(out)

    def sel_steps(first, count, x):
        for u in range(count):
            x = sel_step(first + u, x)
        return x

    sel_out = lax.fori_loop(0, n_chunks // SEL_UNROLL, lambda c, x: sel_steps(SEL_UNROLL * c, SEL_UNROLL, x),
                            tuple(zip(m0s, acc0s)))
    done_chunks = n_chunks // SEL_UNROLL * SEL_UNROLL
    count = SEL_UNROLL // 2
    while count >= 1:
        take = (n_chunks - done_chunks) >= count
        sel_out = lax.cond(take, functools.partial(sel_steps, done_chunks, count), lambda x: x, sel_out)
        done_chunks = done_chunks + jnp.where(take, count, 0)
        count //= 2

    outs = [None] * NSA_REP
    for g in groups:
        acc = sel_out[g][1]
        o_sel = acc / pltpu.roll(acc, HEAD_DIM, 1)
        for r in range(NSA_REP):
            c0 = (g * NSA_REP + r) * 3
            o = partial[g][r] + gates[:, c0 + 1:c0 + 2] * o_sel[r * qb:(r + 1) * qb]
            outs[r] = o if g == 0 else jnp.where(lane < HEAD_DIM, outs[r], o)

    for r in range(NSA_REP):
        o_ref[:, r * LANES:(r + 1) * LANES] = outs[r].astype(BF16)


def _nsa_prompt(q, small, kaug, vaug, kwin, vwaug, ckv, mt, *, batch, seq, kc):
    nb = seq // Q_BLOCK
    nc = seq // CMP_STRIDE
    blk_spec = lambda w: pl.BlockSpec((Q_BLOCK, w), lambda b, i: (b * nb + i, 0))
    seq_spec = lambda w: pl.BlockSpec((seq, w), lambda b, i: (b, 0))
    return pl.pallas_call(
        functools.partial(_nsa_prompt_kernel, seq=seq, kc=kc),
        grid=(batch, nb),
        in_specs=[blk_spec(NSA_Q_W), blk_spec(LANES), seq_spec(2 * LANES), seq_spec(2 * LANES), seq_spec(LANES),
                  seq_spec(2 * LANES), pl.BlockSpec((nc, 2 * LANES), lambda b, i: (b, 0)), _const_spec(mt.shape)],
        out_specs=blk_spec(NSA_Q_W),
        out_shape=jax.ShapeDtypeStruct((batch * seq, NSA_Q_W), BF16),
        compiler_params=pltpu.CompilerParams(dimension_semantics=("arbitrary", "arbitrary"),
                                             vmem_limit_bytes=VMEM_LIMIT),
        name="nsa_prompt",
    )(q, small, kaug, vaug, kwin, vwaug, ckv, mt)


def _nsa_sample_kernel(pt_ref, *refs, past, pps, nb):
    del pt_ref
    pages = refs[:nb * pps]
    (q_ref, kvn_ref, small_ref, win_ref, perm_ref, wk_ref, wv_ref, pelo_ref, pehi_ref, mt_ref, exp_ref,
     o_ref, xs_ref, s_ref, vs_ref) = refs[nb * pps:]
    step = pl.program_id(1)
    n_pages = past // PAGE_SIZE
    cpp = PAGE_SIZE // CMP_STRIDE
    heads = NSA_HEADS
    seqs = range(nb)
    row = lax.broadcasted_iota(jnp.int32, (heads, LANES), 0)
    lane = lax.broadcasted_iota(jnp.int32, (heads, LANES), 1)
    mine = (lane >= HEAD_DIM) == (row >= NSA_REP)
    rr = row % NSA_REP
    qf = []
    for b in seqs:
        qsel = jnp.zeros((heads, LANES), F32)
        for r in range(NSA_REP):
            qsel = jnp.where(rr == r, q_ref[b, :, r * LANES:(r + 1) * LANES].astype(F32), qsel)
        qf.append(jnp.where(mine, qsel, 0.0))
    q = [x.astype(BF16) for x in qf]

    nc = past // CMP_STRIDE

    def take_pages(first_page):
        for k in range(pps):
            pg = first_page + k
            tok = slice(pg * PAGE_SIZE, (pg + 1) * PAGE_SIZE)
            for b in seqs:
                page = pages[b * pps + k]
                s_ref[b, :, tok] = _dot(q[b], page[2 * LANES:3 * LANES, :].astype(BF16))
                vs_ref[b, :, tok] = page[3 * LANES:4 * LANES, :].astype(BF16)
                x = _dot_nt(perm_ref[...], page[0:2 * LANES, :].astype(BF16))
                for l in range(CMP_STRIDE):
                    for c in range(2):
                        xs_ref[b, c, pg * cpp:(pg + 1) * cpp, l * LANES:(l + 1) * LANES] = (
                            x[l * cpp:(l + 1) * cpp, c * LANES:(c + 1) * LANES])

    for st in range(n_pages // pps):
        pl.when(step == st)(functools.partial(take_pages, st * pps))

    @pl.when(step == pl.num_programs(1) - 1)
    def _():
        n_cmp = nc - 1
        nsp = mt_ref.shape[0]
        cur = past // SEL_LEN

        crow = lax.broadcasted_iota(jnp.int32, (nc, LANES), 0)
        halves = []
        for c, w_ref in enumerate((wk_ref, wv_ref)):
            cl = slice(c * LANES, (c + 1) * LANES)
            for b in seqs:
                for l in range(CMP_STRIDE):
                    ll = slice(l * LANES, (l + 1) * LANES)
                    xs_ref[b, c, nc:nc + SUBLANES, ll] = jnp.broadcast_to(pelo_ref[l:l + 1, cl], (SUBLANES, LANES))
                    xs_ref[b, c, nc + SUBLANES:nc + 2 * SUBLANES, ll] = jnp.broadcast_to(pehi_ref[l:l + 1, cl],
                                                                                           (SUBLANES, LANES))
            accs = [_dot(xs_ref[b, c].astype(BF16), w_ref[...]) for b in seqs]
            halves.append([
                jnp.where(crow < nc - 1,
                          acc[0:nc, 0:LANES] + pltpu.roll(acc[0:nc, LANES:2 * LANES], nc - 1, 0)
                          + acc[nc:nc + 1, 0:LANES] + acc[nc + SUBLANES:nc + SUBLANES + 1, LANES:2 * LANES],
                          0.0).astype(BF16) for acc in accs])
        ck, cv = halves

        kvn = [kvn_ref[b] for b in seqs]
        nw = win_ref.shape[1] // 2
        s = [_dot(q[b], win_ref[b, 0:nw, :].astype(BF16)) for b in seqs]
        s_new = [jnp.sum(qf[b] * kvn[b][:, 4 * LANES:5 * LANES], axis=-1, keepdims=True) for b in seqs]
        m = [jnp.maximum(jnp.max(s[b], axis=-1, keepdims=True), s_new[b]) for b in seqs]
        e = [jnp.exp2(s[b] - m[b]) for b in seqs]
        e_new = [jnp.exp2(s_new[b] - m[b]) for b in seqs]
        o_win = [(_dot_nt(e[b].astype(BF16), win_ref[b, nw:, :].astype(BF16)) + e_new[b] * kvn[b][:, 5 * LANES:6 * LANES])
                 / (jnp.sum(e[b], axis=-1, keepdims=True) + e_new[b]) for b in seqs]

        ncol = lax.broadcasted_iota(jnp.int32, (1, nc), 1)
        cmp_valid = (ncol * CMP_STRIDE + CMP_LEN - 1 <= past) & (ncol < n_cmp)
        p = [_masked_softmax2(jnp.where(cmp_valid, _dot_nt(q[b], ck[b]), NEG)) for b in seqs]
        o_cmp = [_dot(p[b].astype(BF16), cv[b]) for b in seqs]

        prow = lax.broadcasted_iota(jnp.int32, (heads, nc), 0)
        psum = [jnp.where(prow < NSA_REP,
                          jnp.sum(jnp.where(prow < NSA_REP, p[b], 0.0), axis=0, keepdims=True),
                          jnp.sum(jnp.where(prow < NSA_REP, 0.0, p[b]), axis=0, keepdims=True)) for b in seqs]
        imp = [sum(_dot_nt(piece, mt_ref[...]) for piece in _split3(psum[b])) for b in seqs]
        blk = lax.broadcasted_iota(jnp.int32, (heads, nsp), 1)
        forced = (blk == 0) | (blk == cur) | (blk == cur - 1)
        score = [jnp.where(blk * SEL_LEN <= past, imp[b] + jnp.where(forced, SEL_BONUS, 0.0), NEG) for b in seqs]
        score_t = [x.T for x in score]
        bi = lax.broadcasted_iota(jnp.int32, (nsp, nsp), 0)
        bj = lax.broadcasted_iota(jnp.int32, (nsp, nsp), 1)
        tie = jnp.where(bi < bj, 1.0, 0.0)
        head_row = lax.broadcasted_iota(jnp.int32, (heads, nsp), 0)
        sel8 = []
        for b in seqs:
            picks = []
            for g in range(NSA_KV_HEADS):
                c = g * NSA_REP
                s_i, s_j = score_t[b][:, c:c + 1], score[b][c:c + 1, :]
                beats = jnp.where(s_i > s_j, 1.0, jnp.where(s_i == s_j, tie, 0.0))
                rank = jnp.sum(beats, axis=0, keepdims=True)
                picks.append(jnp.where(rank < SEL_TOPK, jnp.where(s_j > NEG / 2, 1.0, 0.0), 0.0))
            sel8.append(jnp.where(head_row < NSA_REP, picks[0], picks[1]))
        keep = [_dot(sel8[b].astype(BF16), exp_ref[...]) > 0.5 for b in seqs]

        new_ok = [sel8[b][:, cur:cur + 1] > 0.5 for b in seqs]
        s = [jnp.where(keep[b], s_ref[b], NEG) for b in seqs]
        s_new = [jnp.where(new_ok[b], jnp.sum(qf[b] * kvn[b][:, 2 * LANES:3 * LANES], axis=-1, keepdims=True), NEG)
                 for b in seqs]
        m = [jnp.maximum(jnp.max(s[b], axis=-1, keepdims=True), s_new[b]) for b in seqs]
        e = [jnp.exp2(s[b] - m[b]) for b in seqs]
        e_new = [jnp.where(new_ok[b], jnp.exp2(s_new[b] - m[b]), 0.0) for b in seqs]
        o_sel = [(_dot_nt(e[b].astype(BF16), vs_ref[b]) + e_new[b] * kvn[b][:, 3 * LANES:4 * LANES])
                 / (jnp.sum(e[b], axis=-1, keepdims=True) + e_new[b]) for b in seqs]

        for b in seqs:
            gates = _sigmoid(small_ref[b])
            gate = lambda br: jnp.sum(jnp.where(lane == row * 3 + br, gates, 0.0), axis=-1, keepdims=True)
            o = gate(0) * o_cmp[b] + gate(1) * o_sel[b] + gate(2) * o_win[b]
            for r in range(NSA_REP):
                o_ref[b, :, r * LANES:(r + 1) * LANES] = jnp.where(
                    lane[0:1] < HEAD_DIM, o[r:r + 1], o[NSA_REP + r:NSA_REP + r + 1]).astype(BF16)


def _nsa_sample(page_table, cache, q, kvn, small, win, wlo, whi, pelo, pehi, mt, *, pps, nb):
    batch, n_pages = page_table.shape
    past = n_pages * PAGE_SIZE
    rows = cache.shape[1]
    cpp = PAGE_SIZE // CMP_STRIDE
    nsp = mt.shape[0]
    expand = jnp.asarray(np.arange(nsp)[:, None] == np.arange(past)[None, :] // SEL_LEN, BF16)
    tok = np.arange(PAGE_SIZE)
    perm = jnp.asarray((tok[None, :] % CMP_STRIDE) * cpp + tok[None, :] // CMP_STRIDE == tok[:, None], BF16)

    def page_spec(r, k):
        return pl.BlockSpec((None, rows, PAGE_SIZE),
                            lambda b, s, pt: (pt[(b * nb + r) * n_pages + s * pps + k], 0, 0))

    per_b = lambda shape: pl.BlockSpec((nb,) + shape, lambda b, s, pt: (b, 0, 0))
    const = lambda a: pl.BlockSpec(a.shape, lambda b, s, pt: (0,) * a.ndim, pipeline_mode=pl.Buffered(1))
    grid_spec = pltpu.PrefetchScalarGridSpec(
        num_scalar_prefetch=1,
        grid=(batch // nb, n_pages // pps),
        in_specs=[page_spec(r, k) for r in range(nb) for k in range(pps)] + [
            per_b((1, NSA_Q_W)), per_b((1, 6 * NSA_KV_W)), per_b((1, LANES)), per_b(win.shape[1:]),
            const(perm), const(wlo), const(whi), const(pelo), const(pehi), const(mt), const(expand)],
        out_specs=per_b((1, NSA_Q_W)),
        scratch_shapes=[pltpu.VMEM((nb, 2, past // CMP_STRIDE + 2 * SUBLANES, CMP_STRIDE * LANES), F32),
                        pltpu.VMEM((nb, NSA_HEADS, past), F32),
                        pltpu.VMEM((nb, LANES, past), BF16)],
    )
    return pl.pallas_call(
        functools.partial(_nsa_sample_kernel, past=past, pps=pps, nb=nb),
        grid_spec=grid_spec,
        out_shape=jax.ShapeDtypeStruct((batch, 1, NSA_Q_W), BF16),
        compiler_params=pltpu.CompilerParams(dimension_semantics=("arbitrary", "arbitrary"),
                                             vmem_limit_bytes=VMEM_LIMIT),
        name="nsa_sample",
    )(page_table.reshape(-1), *([cache] * (nb * pps)), q, kvn, small, win, perm, wlo, whi, pelo, pehi, mt, expand)


def _l2norm(x):
    return x * lax.rsqrt(jnp.sum(x * x, axis=-1, keepdims=True) + EPS)


def _softplus(x):
    return jnp.maximum(x, 0.0) + jnp.log(1.0 + jnp.exp(-jnp.abs(x)))


def _gdn_step_kernel(x_ref, z_ref, small_ref, buf_ref, s0_ref, cw_ref, pcol_ref, nw_ref, o_ref, sout_ref):
    hist = CONV_W - 1
    conv = jnp.sum(buf_ref[...] * cw_ref[0:hist, :], axis=0, keepdims=True) + x_ref[...] * cw_ref[hist:CONV_W, :]
    act = _silu(conv)
    small = small_ref[...]
    g_all = -jnp.exp(pcol_ref[0:1, :]) * _softplus(small + pcol_ref[1:2, :])
    beta_all = _sigmoid(small)
    row = lax.broadcasted_iota(jnp.int32, (SUBLANES, GDN_DK), 0)
    for h in range(GDN_HEADS):
        hs = slice(h * GDN_DK, (h + 1) * GDN_DK)
        qh = _l2norm(act[:, hs]) * (GDN_DK ** -0.5)
        kh = _l2norm(act[:, GDN_QK_W + h * GDN_DK:GDN_QK_W + (h + 1) * GDN_DK])
        vh = act[:, 2 * GDN_QK_W + h * GDN_DV:2 * GDN_QK_W + (h + 1) * GDN_DV]
        eg = jnp.exp(g_all[:, A_COL + h:A_COL + h + 1])
        bt = beta_all[:, B_COL + h:B_COL + h + 1]
        s = s0_ref[h]
        kq = jnp.where(row == 0, kh, jnp.where(row == 1, qh, 0.0)).astype(BF16)
        ks_qs = _dot(kq, s.astype(BF16))
        vn = bt * (vh - eg * ks_qs[0:1])
        o = eg * ks_qs[1:2] + jnp.sum(qh * kh, axis=-1, keepdims=True) * vn
        k8 = jnp.where(row == 0, kh, 0.0).astype(BF16)
        vn8 = jnp.where(row == 0, vn, 0.0).astype(BF16)
        sout_ref[h] = s * eg + lax.dot_general(k8, vn8, TN_DIMS, preferred_element_type=F32)
        on = o * lax.rsqrt(jnp.mean(o * o, axis=-1, keepdims=True) + EPS) * nw_ref[...]
        o_ref[:, hs] = (on * _silu(z_ref[:, hs])).astype(BF16)


def _gdn_step(x, z, small, buf, s0, cw, pcol, nw):
    batch = x.shape[0]
    per_b = lambda shape: pl.BlockSpec((None,) + shape, lambda b: (b,) + (0,) * len(shape))
    state = (GDN_HEADS, GDN_DK, GDN_DV)
    return pl.pallas_call(
        _gdn_step_kernel,
        grid=(batch,),
        in_specs=[per_b((1, GDN_CONV_CH)), per_b((1, GDN_V_W)), per_b((1, LANES)), per_b((CONV_W - 1, GDN_CONV_CH)),
                  per_b(state), _const_spec(cw.shape), _const_spec(pcol.shape), _const_spec(nw.shape)],
        out_specs=(per_b((1, GDN_V_W)), per_b(state)),
        out_shape=(jax.ShapeDtypeStruct((batch, 1, GDN_V_W), BF16),
                   jax.ShapeDtypeStruct((batch,) + state, F32)),
        compiler_params=pltpu.CompilerParams(dimension_semantics=("arbitrary",), vmem_limit_bytes=VMEM_LIMIT),
        name="gdn_step",
    )(x[:, None], z[:, None], small[:, None], buf, s0, cw, pcol, nw)


GDN_CHUNK = 64
GDN_STACK = GDN_HEADS * GDN_CHUNK


def _stack_heads(x, col0, width):
    return jnp.concatenate([x[:, col0 + h * width:col0 + (h + 1) * width] for h in range(GDN_HEADS)], axis=0)


def _gdn_prep_kernel(act_ref, small_ref, smallt_ref, pcol_ref, prow_ref, tril_ref, triu_ref,
                     u_ref, w_ref, qg_ref, kgt_ref, aqk_ref, gl_ref, *, tb):
    ck = GDN_CHUNK
    st = GDN_STACK
    act = act_ref[...]

    small = small_ref[...]
    g_col = -jnp.exp(pcol_ref[0:1, :]) * _softplus(small + pcol_ref[1:2, :])
    beta = _sigmoid(small)
    g_row = -jnp.exp(prow_ref[:, 0:1]) * _softplus(smallt_ref[...] + prow_ref[:, 1:2])
    gcum_col = sum(_dot(tril_ref[...], piece) for piece in _split3(g_col))
    gcum_row = sum(_dot(piece, triu_ref[...]) for piece in _split3(g_row))

    ii = lax.broadcasted_iota(jnp.int32, (st, st), 0)
    jj = lax.broadcasted_iota(jnp.int32, (st, st), 1)
    same_head = (ii // ck) == (jj // ck)
    incl = same_head & (ii >= jj)
    strict = same_head & (ii > jj)
    eye = jnp.where(ii == jj, 1.0, 0.0)
    hrow = lax.broadcasted_iota(jnp.int32, (SUBLANES, LANES), 0)

    lms, rhss = [], []
    for ci in range(tb // ck):
        r0 = ci * ck
        rs = slice(r0, r0 + ck)
        gc = _stack_heads(gcum_col[rs], A_COL, 1)
        bt = _stack_heads(beta[rs], B_COL, 1)
        gr = jnp.concatenate([gcum_row[h:h + 1, r0:r0 + ck] for h in range(GDN_HEADS)], axis=1)
        glast = [gcum_col[r0 + ck - 1:r0 + ck, A_COL + h:A_COL + h + 1] for h in range(GDN_HEADS)]
        gl_stack = jnp.concatenate([jnp.broadcast_to(x, (ck, 1)) for x in glast], axis=0)
        dec = jnp.where(incl, jnp.exp(jnp.where(incl, gc - gr, 0.0)), 0.0)
        qs = _stack_heads(act[rs], 0, GDN_DK)
        ks = _stack_heads(act[rs], GDN_QK_W, GDN_DK)
        vs = _stack_heads(act[rs], 2 * GDN_QK_W, GDN_DV)
        kb = ks * bt
        k16 = ks.astype(BF16)
        lms.append(jnp.where(strict, _dot_nt(kb.astype(BF16), k16) * dec, 0.0))
        eg = jnp.exp(gc)
        rhss.append(jnp.concatenate([vs * bt, kb * eg], axis=1).astype(BF16))
        orow = slice(ci * st, (ci + 1) * st)
        qg_ref[orow, :] = (qs * eg).astype(BF16)
        aqk_ref[orow, :] = (_dot_nt(qs.astype(BF16), k16) * dec).astype(BF16)
        kgt_ref[ci * GDN_DK:(ci + 1) * GDN_DK, :] = (ks * jnp.exp(gl_stack - gc)).T.astype(BF16)
        gl = jnp.zeros((SUBLANES, LANES), F32)
        for h in range(GDN_HEADS):
            gl = jnp.where(hrow == h, jnp.exp(glast[h]), gl)
        gl_ref[ci * SUBLANES:(ci + 1) * SUBLANES, :] = gl

    ainvs = [eye - lm for lm in lms]
    pws = lms
    n = 2
    while n < ck:
        pw16s = [pw.astype(BF16) for pw in pws]
        pws = [_dot(pw16, pw16) for pw16 in pw16s]
        ainvs = [ainv + _dot(ainv.astype(BF16), pw.astype(BF16)) for ainv, pw in zip(ainvs, pws)]
        n *= 2
    for ci, (ainv, rhs) in enumerate(zip(ainvs, rhss)):
        sol = _dot(ainv.astype(BF16), rhs)
        orow = slice(ci * st, (ci + 1) * st)
        u_ref[orow, :] = sol[:, 0:GDN_DV]
        w_ref[orow, :] = sol[:, GDN_DV:].astype(BF16)


def _gdn_prep(act, small, smallt, pcol, prow, *, batch, seq, tb):
    nblk = seq // tb
    ncb = tb // GDN_CHUNK
    blk = np.arange(tb)
    same = (blk[:, None] // GDN_CHUNK) == (blk[None, :] // GDN_CHUNK)
    tril = jnp.asarray(same & (blk[:, None] >= blk[None, :]), BF16)
    triu = jnp.asarray(same & (blk[:, None] <= blk[None, :]), BF16)
    row_spec = lambda r, w: pl.BlockSpec((r, w), lambda b, j: (b * nblk + j, 0))
    per_b = lambda shape: pl.BlockSpec((None,) + shape, lambda b, j: (b,) + (0,) * len(shape))
    n_chunks = batch * seq // GDN_CHUNK
    out_shape = (jax.ShapeDtypeStruct((n_chunks * GDN_STACK, GDN_DV), F32),
                 jax.ShapeDtypeStruct((n_chunks * GDN_STACK, GDN_DK), BF16),
                 jax.ShapeDtypeStruct((n_chunks * GDN_STACK, GDN_DK), BF16),
                 jax.ShapeDtypeStruct((n_chunks * GDN_DK, GDN_STACK), BF16),
                 jax.ShapeDtypeStruct((n_chunks * GDN_STACK, GDN_STACK), BF16),
                 jax.ShapeDtypeStruct((n_chunks * SUBLANES, LANES), F32))
    out_specs = (row_spec(ncb * GDN_STACK, GDN_DV), row_spec(ncb * GDN_STACK, GDN_DK),
                 row_spec(ncb * GDN_STACK, GDN_DK), row_spec(ncb * GDN_DK, GDN_STACK),
                 row_spec(ncb * GDN_STACK, GDN_STACK), row_spec(ncb * SUBLANES, LANES))
    return pl.pallas_call(
        functools.partial(_gdn_prep_kernel, tb=tb),
        grid=(batch, nblk),
        in_specs=[row_spec(tb, GDN_CONV_CH), row_spec(tb, LANES),
                  pl.BlockSpec((None, SUBLANES, tb), lambda b, j: (b, 0, j)),
                  _const_spec(pcol.shape), _const_spec(prow.shape),
                  _const_spec(tril.shape), _const_spec(triu.shape)],
        out_specs=out_specs,
        out_shape=out_shape,
        compiler_params=pltpu.CompilerParams(dimension_semantics=("arbitrary", "arbitrary"),
                                             vmem_limit_bytes=VMEM_LIMIT),
        name="gdn_prep",
    )(act, small, smallt, pcol, prow, tril, triu)


def _gdn_scan_kernel(u_ref, w_ref, qg_ref, kgt_ref, aqk_ref, gl_ref, z_ref, s0_ref, nw_ref, o_ref, sout_ref, s_ref,
                     *, batch, ncb):
    ck = GDN_CHUNK
    st = GDN_STACK
    sw = GDN_HEADS * GDN_DK

    @pl.when(pl.program_id(0) == 0)
    def _():
        s_ref[...] = s0_ref[...]

    tall_mask =(lax.broadcasted_iota(jnp.int32, (sw, st), 0) // GDN_DK
                 == lax.broadcasted_iota(jnp.int32, (sw, st), 1) // ck)

    seqs = range(batch)
    s = [s_ref[b] for b in seqs]
    for ci in range(ncb):
        rows = slice(ci * st, (ci + 1) * st)
        toks = slice(ci * ck, (ci + 1) * ck)
        lhs, kgt_bd, gl_rows = [], [], []
        for b in seqs:
            lhs.append([jnp.concatenate([w_ref[b, ci * st + h * ck:ci * st + (h + 1) * ck, :],
                                         qg_ref[b, ci * st + h * ck:ci * st + (h + 1) * ck, :]], axis=0)
                        for h in range(GDN_HEADS)])
            kgt = kgt_ref[b, ci * GDN_DK:(ci + 1) * GDN_DK, :]
            kgt_bd.append(jnp.where(tall_mask, jnp.concatenate([kgt] * GDN_HEADS, axis=0), 0.0))
            gl = gl_ref[b, ci * SUBLANES:(ci + 1) * SUBLANES, :]
            gl_rows.append(jnp.concatenate(
                [jnp.broadcast_to(gl[h:h + 1], (GDN_DK, GDN_DV)) for h in range(GDN_HEADS)], axis=0))
        s16 = [x.astype(BF16) for x in s]
        t1 = [[_dot(lhs[b][h], s16[b][h * GDN_DK:(h + 1) * GDN_DK]) for h in range(GDN_HEADS)] for b in seqs]
        w_s = [jnp.concatenate([t[0:ck] for t in t1[b]], axis=0) for b in seqs]
        q_s = [jnp.concatenate([t[ck:] for t in t1[b]], axis=0) for b in seqs]
        vn16 = [(u_ref[b, rows, :] - w_s[b]).astype(BF16) for b in seqs]
        o = [q_s[b] + _dot(aqk_ref[b, rows, :], vn16[b]) for b in seqs]
        s = [s[b] * gl_rows[b] + _dot(kgt_bd[b], vn16[b]) for b in seqs]
        for b in seqs:
            for h in range(GDN_HEADS):
                oh = o[b][h * ck:(h + 1) * ck]
                on = oh * lax.rsqrt(jnp.mean(oh * oh, axis=-1, keepdims=True) + EPS) * nw_ref[...]
                hs = slice(h * GDN_DV, (h + 1) * GDN_DV)
                o_ref[b, toks, hs] = (on * _silu(z_ref[b, toks, hs])).astype(BF16)

    for b in seqs:
        s_ref[b] = s[b]
    sout_ref[...] = s_ref[...]


def _gdn_scan(u, w, qg, kgt, aqk, gl, z, s0, nw, *, batch, seq, tb):
    nblk = seq // tb
    ncb = tb // GDN_CHUNK
    cps = seq // GDN_CHUNK
    sw = GDN_HEADS * GDN_DK
    r3 = lambda a, rows_per_chunk: a.reshape(batch, cps * rows_per_chunk, a.shape[-1])
    blk = lambda rows, width: pl.BlockSpec((batch, rows, width), lambda j: (0, j, 0))
    full = pl.BlockSpec((batch, sw, GDN_DV), lambda j: (0, 0, 0))
    return pl.pallas_call(
        functools.partial(_gdn_scan_kernel, batch=batch, ncb=ncb),
        grid=(nblk,),
        in_specs=[blk(ncb * GDN_STACK, GDN_DV), blk(ncb * GDN_STACK, GDN_DK), blk(ncb * GDN_STACK, GDN_DK),
                  blk(ncb * GDN_DK, GDN_STACK), blk(ncb * GDN_STACK, GDN_STACK), blk(ncb * SUBLANES, LANES),
                  blk(tb, GDN_V_W), full, _const_spec(nw.shape)],
        out_specs=(blk(tb, GDN_V_W), full),
        out_shape=(jax.ShapeDtypeStruct((batch, seq, GDN_V_W), BF16),
                   jax.ShapeDtypeStruct((batch, sw, GDN_DV), F32)),
        scratch_shapes=[pltpu.VMEM((batch, sw, GDN_DV), F32)],
        compiler_params=pltpu.CompilerParams(dimension_semantics=("arbitrary",), vmem_limit_bytes=VMEM_LIMIT),
        name="gdn_scan",
    )(r3(u, GDN_STACK), r3(w, GDN_STACK), r3(qg, GDN_STACK), r3(kgt, GDN_DK), r3(aqk, GDN_STACK),
      r3(gl, SUBLANES), z.reshape(batch, seq, GDN_V_W), s0.reshape(batch, sw, GDN_DV), nw)


def _rope_tables(pos):
    half = HEAD_DIM // 2
    inv = np.power(ROPE_THETA, -np.arange(half, dtype=np.float64) * 2.0 / HEAD_DIM)
    ang = np.asarray(pos, np.float64)[:, None] * inv[None, :]
    cos, sin = np.cos(ang).astype(np.float32), np.sin(ang).astype(np.float32)
    return tuple(jnp.asarray(t) for t in (np.tile(cos, (1, 4)), np.concatenate([-sin, sin, -sin, sin], axis=1),
                                          cos.T, sin.T))


def _cmp_to_sel_t(n_cmp, n_sel, rows, cols):
    cs = np.arange(cols)[None, :] * CMP_STRIDE
    ss = np.arange(rows)[:, None] * SEL_LEN
    hit = (cs < ss + SEL_LEN) & (cs + CMP_LEN > ss)
    hit &= (np.arange(cols)[None, :] < n_cmp) & (np.arange(rows)[:, None] < n_sel)
    return jnp.asarray(hit, BF16)


def _layer_weights(w_in, cmp_pe, cmp_w, conv_w, a_log, dt_bias, gdn_norm, w_out):
    cuts = np.cumsum([NSA_Q_W, 6 * NSA_KV_W, GATE_COLS, GDN_CONV_CH, GDN_V_W, GDN_HEADS]).tolist()
    wq, wkv, wgt, wconv, wz, wa, wb = jnp.split(w_in, cuts, axis=1)
    order = np.array([g * NSA_REP + r for r in range(NSA_REP) for g in range(NSA_KV_HEADS)])
    cols = (order[:, None] * HEAD_DIM + np.arange(HEAD_DIM)[None, :]).reshape(-1)
    wm = jnp.concatenate([wq[:, cols], wkv, wconv, wz], axis=1).astype(BF16)
    wkvt = wkv.T.astype(BF16)
    ws = jnp.concatenate([wgt, wa, wb], axis=1)
    ws = jnp.pad(ws, ((0, 0), (0, LANES - ws.shape[1]))).astype(BF16)
    wo = jnp.concatenate([w_out[:NSA_Q_W][cols], w_out[NSA_Q_W:]], axis=0).astype(BF16)

    def blockdiag(l0):
        wk, wv = cmp_w[0, l0:l0 + CMP_STRIDE], cmp_w[1, l0:l0 + CMP_STRIDE]
        z = jnp.zeros_like(wk)
        rows = [jnp.concatenate(r, axis=2) for r in ([wk, z, z, z], [z, wk, z, z], [z, z, wv, z], [z, z, z, wv])]
        return jnp.concatenate(rows, axis=1).astype(BF16)

    def long_contraction(c):
        halves = []
        for l0 in (0, CMP_STRIDE):
            w = cmp_w[c, l0:l0 + CMP_STRIDE]
            z = jnp.zeros_like(w)
            halves.append(jnp.concatenate([jnp.concatenate([w, z], axis=2), jnp.concatenate([z, w], axis=2)], axis=1))
        return jnp.concatenate(halves, axis=2).reshape(CMP_STRIDE * LANES, 2 * LANES).astype(BF16)

    def pe_rows(l0):
        pk, pv = cmp_pe[0, l0:l0 + CMP_STRIDE], cmp_pe[1, l0:l0 + CMP_STRIDE]
        return jnp.concatenate([pk, pk, pv, pv], axis=1)

    pcol = jnp.zeros((2, LANES), F32).at[0, A_COL:A_COL + GDN_HEADS].set(a_log)
    pcol = pcol.at[1, A_COL:A_COL + GDN_HEADS].set(dt_bias)
    prow = jnp.zeros((SUBLANES, 2), F32).at[0:GDN_HEADS, 0].set(a_log).at[0:GDN_HEADS, 1].set(dt_bias)
    return dict(wm=wm, ws=ws, wkvt=wkvt, wo=wo, wk_long=long_contraction(0), wv_long=long_contraction(1),
                wlo=blockdiag(0), whi=blockdiag(CMP_STRIDE), pelo=pe_rows(0),
                pehi=pe_rows(CMP_STRIDE), cw=conv_w, pcol=pcol, prow=prow, nw=gdn_norm[None, :])


def kernel(x_prompt, x_sample, cache_nsa_kv, cache_nsa_win, state_gdn_S, state_gdn_conv, page_table, norm_mix, w_in,
           nsa_cmp_pe, nsa_cmp_w, gdn_conv_w, gdn_a_log, gdn_dt_bias, gdn_norm, w_out, norm_ffn, w_gate_up, w_down,
           norm_final):
    bp, tp, _ = x_prompt.shape
    bs, ts, _ = x_sample.shape
    depth = w_in.shape[0]
    n_pages = page_table.shape[1]
    past = n_pages * PAGE_SIZE
    assert depth == 1 and ts == 1, "one layer and one new token per sample row"
    assert tp % 512 == 0 and tp // SEL_LEN <= LANES and past % 512 == 0
    l = 0
    wts = _layer_weights(w_in[l], nsa_cmp_pe[l], nsa_cmp_w[l], gdn_conv_w[l], gdn_a_log[l], gdn_dt_bias[l],
                         gdn_norm[l], w_out[l])
    g_mix, g_ffn, g_fin = norm_mix[l][None, :], norm_ffn[l][None, :], norm_final[None, :]
    wgu, wd = w_gate_up[l].astype(BF16), w_down[l].astype(BF16)
    hist = CONV_W - 1

    xp = x_prompt.reshape(bp * tp, D_MODEL)
    q, kv, conv_in, z, small, kaug, vaug, kwin, vwaug, kv4t_p, kvwt_p, act = _proj(
        xp, g_mix, wts["wm"], wts["ws"], wts["wkvt"], wts["cw"], jnp.zeros((bp, hist, GDN_CONV_CH), F32),
        *_rope_tables(np.arange(tp)), tm=256)
    ckv = _compress(kv, wts["wlo"], wts["whi"], wts["pelo"], wts["pehi"], batch=bp, seq=tp)
    nc = tp // CMP_STRIDE
    mt_p = _cmp_to_sel_t(nc - 1, tp // SEL_LEN, LANES, nc)
    o_nsa = _nsa_prompt(q, small, kaug, vaug, kwin, vwaug, ckv, mt_p, batch=bp, seq=tp, kc=512)
    smallt = small[:, A_COL:A_COL + SUBLANES].reshape(bp, tp, SUBLANES).transpose(0, 2, 1)
    prep = _gdn_prep(act, small, smallt, wts["pcol"], wts["prow"], batch=bp, seq=tp, tb=256)
    o_gdn, s_p = _gdn_scan(*prep, z, jnp.zeros((bp, GDN_HEADS, GDN_DK, GDN_DV), F32), wts["nw"],
                           batch=bp, seq=tp, tb=256)
    o_gdn = o_gdn.reshape(bp * tp, GDN_V_W)
    s_p = s_p.reshape(bp, GDN_HEADS, GDN_DK, GDN_DV)
    y_p = _mix_ffn(xp, o_nsa, o_gdn, wts["wo"], g_ffn, wgu, wd, g_fin, tm=512)
    tok_major = lambda a, comps: a.reshape(bp, comps, NSA_KV_HEADS, HEAD_DIM, -1).transpose(0, 4, 1, 2, 3)
    kv_p = tok_major(kv4t_p, 4)
    win_p = tok_major(kvwt_p[:, :, tp - min(WINDOW, tp):], 2)
    conv_p = conv_in.reshape(bp, tp, GDN_CONV_CH)[:, tp - hist:]

    xs = x_sample.reshape(bs, D_MODEL)
    q, kv, conv_in, z, small = _proj(xs, g_mix, wts["wm"], wts["ws"], wts["wkvt"], wts["cw"],
                                     jnp.zeros((1, hist, GDN_CONV_CH), F32),
                                     *_rope_tables(np.full((bs,), past)), tm=bs)[:5]
    n_sel = past // SEL_LEN + 1
    nsp = -(-n_sel // LANES) * LANES
    mt_s = _cmp_to_sel_t(past // CMP_STRIDE - 1, n_sel, nsp, past // CMP_STRIDE)
    cache = cache_nsa_kv[l].transpose(0, 2, 3, 4, 1).reshape(-1, 4 * NSA_KV_W, PAGE_SIZE)
    win = cache_nsa_win[l].transpose(0, 2, 3, 4, 1).reshape(bs, 2 * NSA_KV_W, -1)
    o_nsa = _nsa_sample(page_table, cache, q[:, None], kv[:, None], small[:, None], win, wts["wk_long"],
                        wts["wv_long"], wts["pelo"], wts["pehi"], mt_s, pps=min(16, n_pages), nb=2).reshape(bs, NSA_Q_W)
    o_gdn, s_s = _gdn_step(conv_in, z, small, state_gdn_conv[l], state_gdn_S[l], wts["cw"], wts["pcol"], wts["nw"])
    o_gdn = o_gdn.reshape(bs, GDN_V_W)
    y_s = _mix_ffn(xs, o_nsa, o_gdn, wts["wo"], g_ffn, wgu, wd, g_fin, tm=bs)
    kv_s = kv.reshape(bs, 1, 3, 2, NSA_KV_HEADS, HEAD_DIM)
    win_s = jnp.concatenate([cache_nsa_win[l], kv_s[:, :, 2]], axis=1)[:, -min(WINDOW, past + 1):]
    conv_s = jnp.concatenate([state_gdn_conv[l], conv_in[:, None]], axis=1)[:, -hist:]

    return (y_p.reshape(bp, tp, D_MODEL), y_s.reshape(bs, 1, D_MODEL),
            kv_p[None], win_p[None], s_p[None], conv_p[None],
            kv_s[:, :, 0:2].reshape(bs, 1, 4, NSA_KV_HEADS, HEAD_DIM)[None], win_s[None], s_s[None], conv_s[None])
```

```python
import functools

import numpy as np
import jax
import jax.numpy as jnp
from jax import lax
from jax.experimental import pallas as pl
from jax.experimental.pallas import tpu as pltpu

F32 = jnp.float32
BF16 = jnp.bfloat16

D_MODEL = 1024
PAGE_SIZE = 128
HEAD_DIM = 64
NSA_HEADS = 8
NSA_KV_HEADS = 2
NSA_REP = NSA_HEADS // NSA_KV_HEADS
CMP_LEN = 32
CMP_STRIDE = 16
SEL_LEN = 64
SEL_TOPK = 16
WINDOW = 512
Q_BLOCK = 128
ROPE_THETA = 10000.0
GDN_HEADS = 4
GDN_DK = 128
GDN_DV = 128
CONV_W = 4
NSA_Q_W = NSA_HEADS * HEAD_DIM
NSA_KV_W = NSA_KV_HEADS * HEAD_DIM
GDN_QK_W = GDN_HEADS * GDN_DK
GDN_V_W = GDN_HEADS * GDN_DV
GDN_CONV_CH = 2 * GDN_QK_W + GDN_V_W
NEG = -1e30
SEL_UNROLL = 4
LOG2E = 1.4426950408889634
SEL_BONUS = 1e4
EPS = 1e-6

LANES = 128
SUBLANES = 8
VMEM_LIMIT = 56 * 1024 * 1024

PROJ_ROWS = 256
FFN_ROWS = 512
GDN_ROWS = 256
SEL_KEYS = 512
SAMPLE_PAGES = 16
SAMPLE_ROWS = 2

GATE_COLS = 3 * NSA_HEADS
A_COL = GATE_COLS
B_COL = GATE_COLS + GDN_HEADS

NT_DIMS = (((1,), (1,)), ((), ()))
TN_DIMS = (((0,), (0,)), ((), ()))


def _dot(a, b):
    return jnp.dot(a, b, preferred_element_type=F32)


def _dot_nt(a, b):
    return lax.dot_general(a, b, NT_DIMS, preferred_element_type=F32)


def _sigmoid(x):
    return 1.0 / (1.0 + jnp.exp(-x))


def _silu(x):
    return x * _sigmoid(x)


def _split3(x):
    p1 = x.astype(BF16)
    r1 = x - p1.astype(F32)
    p2 = r1.astype(BF16)
    p3 = (r1 - p2.astype(F32)).astype(BF16)
    return p1, p2, p3


def _const_spec(shape):
    nd = len(shape)
    return pl.BlockSpec(shape, lambda *_: (0,) * nd, pipeline_mode=pl.Buffered(1))


def _proj_kernel(x_ref, g_ref, wm_ref, ws_ref, wkvt_ref, cos_ref, sin_ref, cost_ref, sint_ref, cw_ref, buf_ref,
                 q_ref, kv_ref, conv_ref, z_ref, small_ref, kaug_ref, vaug_ref, kwin_ref, vwaug_ref,
                 kv4t_ref, kvwt_ref, act_ref, xs_ref, *, tm, pos_rows):
    pad = SUBLANES
    hist = CONV_W - 1
    c0 = NSA_Q_W + 6 * NSA_KV_W

    @pl.when(pl.program_id(0) % (pos_rows // tm) == 0)
    def _():
        xs_ref[0:pad, :] = jnp.zeros((pad, GDN_CONV_CH), F32)
        xs_ref[pad - hist:pad, :] = buf_ref[...]

    x = x_ref[...]
    ms = jnp.mean(x * x, axis=-1, keepdims=True)
    xn = (x * lax.rsqrt(ms + EPS) * g_ref[...]).astype(BF16)
    cos = cos_ref[...]
    sin = sin_ref[...]
    lane = lax.broadcasted_iota(jnp.int32, (tm, LANES), 1)
    low_half = (lane % HEAD_DIM) < (HEAD_DIM // 2)

    def rope(v):
        rot = jnp.where(low_half, pltpu.roll(v, LANES - HEAD_DIM // 2, 1), pltpu.roll(v, HEAD_DIM // 2, 1))
        return v * cos + rot * sin

    wide = 2 * LANES

    def delta_piece(i):
        cols = slice(i * wide, (i + 1) * wide)
        part = i * wide // GDN_QK_W
        conv_in = _dot(xn, wm_ref[:, c0 + i * wide:c0 + (i + 1) * wide])
        conv_ref[:, cols] = conv_in
        xs_ref[pad:pad + tm, cols] = conv_in
        xs = xs_ref[:, cols]
        conv = xs[pad:] * cw_ref[hist:CONV_W, cols]
        for t in range(hist):
            conv = conv + pltpu.roll(xs, hist - t, 0)[pad:] * cw_ref[t:t + 1, cols]
        xs_ref[0:pad, cols] = xs[tm:tm + pad]
        act = _silu(conv)
        for h in range(wide // GDN_DK):
            hs = slice(h * GDN_DK, (h + 1) * GDN_DK)
            dst = act_ref.at[:, i * wide + h * GDN_DK:i * wide + (h + 1) * GDN_DK]
            if part == 0:
                dst[...] = _l2norm(act[:, hs]) * (GDN_DK ** -0.5)
            elif part == 1:
                dst[...] = _l2norm(act[:, hs])
            else:
                dst[...] = act[:, hs]

    def q_piece(i):
        q = _dot(xn, wm_ref[:, i * wide:(i + 1) * wide])
        for r in range(2):
            q_ref[:, i * wide + r * LANES:i * wide + (r + 1) * LANES] = (
                rope(q[:, r * LANES:(r + 1) * LANES]) * (HEAD_DIM ** -0.5 * LOG2E)).astype(BF16)

    group0 = lane < HEAD_DIM

    def kv_piece(br):
        kv = _dot(xn, wm_ref[:, NSA_Q_W + br * wide:NSA_Q_W + (br + 1) * wide])
        k = rope(kv[:, 0:LANES])
        v = kv[:, LANES:wide]
        kv_ref[:, br * wide:br * wide + LANES] = k
        kv_ref[:, br * wide + LANES:(br + 1) * wide] = v
        if br == 1:
            kaug_ref[:, LANES:2 * LANES] = k.astype(BF16)
        elif br == 2:
            kwin_ref[...] = k.astype(BF16)
        if br > 0:
            v_ref = vaug_ref if br == 1 else vwaug_ref
            v_ref[:, 0:LANES] = jnp.where(group0, v, 1.0).astype(BF16)
            v_ref[:, LANES:2 * LANES] = jnp.where(group0, 1.0, v).astype(BF16)

    half = HEAD_DIM // 2

    def kvt_piece(br):
        kvt = _dot_nt(wkvt_ref[br * wide:(br + 1) * wide, :], xn)
        cos_t = cost_ref[...]
        sin_t = sint_ref[...]
        parts = []
        for g in range(NSA_KV_HEADS):
            x1 = kvt[g * HEAD_DIM:g * HEAD_DIM + half]
            x2 = kvt[g * HEAD_DIM + half:(g + 1) * HEAD_DIM]
            parts += [x1 * cos_t - x2 * sin_t, x2 * cos_t + x1 * sin_t]
        dst = kv4t_ref if br < 2 else kvwt_ref
        dst[(br % 2) * wide:(br % 2) * wide + LANES, :] = jnp.concatenate(parts, axis=0)
        dst[(br % 2) * wide + LANES:(br % 2 + 1) * wide, :] = kvt[LANES:wide]

    def z_piece(i):
        zc = c0 + GDN_CONV_CH
        z_ref[:, i * wide:(i + 1) * wide] = _dot(xn, wm_ref[:, zc + i * wide:zc + (i + 1) * wide])

    others = ([functools.partial(q_piece, i) for i in range(NSA_Q_W // wide)]
              + [functools.partial(kv_piece, br) for br in range(3)]
              + [functools.partial(kvt_piece, br) for br in range(3)]
              + [functools.partial(z_piece, i) for i in range(GDN_V_W // wide)])
    n_delta = GDN_CONV_CH // wide
    for i in range(max(n_delta, len(others))):
        if i < n_delta:
            delta_piece(i)
        for other in others[i * len(others) // n_delta:(i + 1) * len(others) // n_delta]:
            other()
    small_ref[...] = _dot(xn, ws_ref[...])

    row0 = (pl.program_id(0) * tm) % pos_rows
    rows = row0 + lax.broadcasted_iota(jnp.int32, (tm, LANES), 0)
    kaug_ref[:, 0:LANES] = jnp.where(rows // SEL_LEN == lane, 1.0, 0.0).astype(BF16)


def _proj(x, g, wm, ws, wkvt, cw, buf, cos, sin, cos_t, sin_t, *, tm):
    rows = x.shape[0]
    pos_rows = cos.shape[0]
    n_pos_blk = pos_rows // tm
    n_seq = rows // pos_rows
    grid = (rows // tm,)
    row_spec = lambda w: pl.BlockSpec((tm, w), lambda i: (i, 0))
    pos_spec = pl.BlockSpec((tm, LANES), lambda i: (i % n_pos_blk, 0))
    pos_t_spec = pl.BlockSpec((HEAD_DIM // 2, tm), lambda i: (0, i % n_pos_blk))
    tok_minor = lambda r: pl.BlockSpec((None, r, tm), lambda i: (i // n_pos_blk, 0, i % n_pos_blk))
    out_shape = (
        jax.ShapeDtypeStruct((rows, NSA_Q_W), BF16),
        jax.ShapeDtypeStruct((rows, 6 * NSA_KV_W), F32),
        jax.ShapeDtypeStruct((rows, GDN_CONV_CH), F32),
        jax.ShapeDtypeStruct((rows, GDN_V_W), F32),
        jax.ShapeDtypeStruct((rows, LANES), F32),
        jax.ShapeDtypeStruct((rows, 2 * LANES), BF16),
        jax.ShapeDtypeStruct((rows, 2 * LANES), BF16),
        jax.ShapeDtypeStruct((rows, LANES), BF16),
        jax.ShapeDtypeStruct((rows, 2 * LANES), BF16),
    )
    out_shape_t = (jax.ShapeDtypeStruct((n_seq, 4 * NSA_KV_W, pos_rows), F32),
                   jax.ShapeDtypeStruct((n_seq, 2 * NSA_KV_W, pos_rows), F32),
                   jax.ShapeDtypeStruct((rows, GDN_CONV_CH), F32))
    return pl.pallas_call(
        functools.partial(_proj_kernel, tm=tm, pos_rows=pos_rows),
        grid=grid,
        in_specs=[row_spec(D_MODEL), _const_spec((1, D_MODEL)), _const_spec(wm.shape), _const_spec(ws.shape),
                  _const_spec(wkvt.shape), pos_spec, pos_spec, pos_t_spec, pos_t_spec, _const_spec(cw.shape),
                  pl.BlockSpec((None, CONV_W - 1, GDN_CONV_CH), lambda i: (i // n_pos_blk, 0, 0))],
        out_specs=(tuple(row_spec(s.shape[1]) for s in out_shape)
                   + (tok_minor(4 * NSA_KV_W), tok_minor(2 * NSA_KV_W), row_spec(GDN_CONV_CH))),
        out_shape=out_shape + out_shape_t,
        scratch_shapes=[pltpu.VMEM((SUBLANES + tm, GDN_CONV_CH), F32)],
        compiler_params=pltpu.CompilerParams(dimension_semantics=("arbitrary",), vmem_limit_bytes=VMEM_LIMIT),
        name="proj",
    )(x, g, wm, ws, wkvt, cos, sin, cos_t, sin_t, cw, buf)


def _mix_ffn_kernel(x_ref, on_ref, og_ref, wo_ref, gf_ref, wgu_ref, wd_ref, gl_ref, y_ref, *, d_ff):
    h = x_ref[...] + _dot(on_ref[...], wo_ref[0:NSA_Q_W, :]) + _dot(og_ref[...], wo_ref[NSA_Q_W:, :])
    ms = jnp.mean(h * h, axis=-1, keepdims=True)
    hn = (h * lax.rsqrt(ms + EPS) * gf_ref[...]).astype(BF16)
    gate = _dot(hn, wgu_ref[:, 0:d_ff])
    up = _dot(hn, wgu_ref[:, d_ff:])
    act = (_silu(gate) * up).astype(BF16)
    h = h + _dot(act, wd_ref[...])
    ms = jnp.mean(h * h, axis=-1, keepdims=True)
    y_ref[...] = h * lax.rsqrt(ms + EPS) * gl_ref[...]


def _mix_ffn(x, o_nsa, o_gdn, wo, gf, wgu, wd, gl, *, tm):
    rows = x.shape[0]
    d_ff = wd.shape[0]
    row_spec = lambda w: pl.BlockSpec((tm, w), lambda i: (i, 0))
    return pl.pallas_call(
        functools.partial(_mix_ffn_kernel, d_ff=d_ff),
        grid=(rows // tm,),
        in_specs=[row_spec(D_MODEL), row_spec(NSA_Q_W), row_spec(GDN_V_W), _const_spec(wo.shape),
                  _const_spec((1, D_MODEL)), _const_spec(wgu.shape), _const_spec(wd.shape),
                  _const_spec((1, D_MODEL))],
        out_specs=row_spec(D_MODEL),
        out_shape=jax.ShapeDtypeStruct((rows, D_MODEL), F32),
        compiler_params=pltpu.CompilerParams(dimension_semantics=("arbitrary",), vmem_limit_bytes=VMEM_LIMIT),
        name="mix_ffn",
    )(x, o_nsa, o_gdn, wo, gf, wgu, wd, gl)


def _compress_rows(load_rows, wlo_ref, whi_ref, pelo_ref, pehi_ref, nc):
    acc_lo = jnp.zeros((nc, 2 * LANES), F32)
    acc_hi = jnp.zeros((nc, 2 * LANES), F32)
    for l in range(CMP_STRIDE):
        x = load_rows(l)
        acc_lo = acc_lo + _dot((x + pelo_ref[l:l + 1, :]).astype(BF16), wlo_ref[l])
        acc_hi = acc_hi + _dot((x + pehi_ref[l:l + 1, :]).astype(BF16), whi_ref[l])
    nxt = pltpu.roll(acc_hi, nc - 1, 0)
    row = lax.broadcasted_iota(jnp.int32, (nc, 2 * LANES), 0)
    return jnp.where(row < nc - 1, acc_lo + nxt, 0.0)


def _compress_kernel(k_ref, v_ref, wlo_ref, whi_ref, pelo_ref, pehi_ref, ckv_ref, *, nc):
    load = lambda l: jnp.concatenate([k_ref[pl.ds(l, nc, stride=CMP_STRIDE), :],
                                      v_ref[pl.ds(l, nc, stride=CMP_STRIDE), :]], axis=1)
    ckv_ref[...] = _compress_rows(load, wlo_ref, whi_ref, pelo_ref, pehi_ref, nc).astype(BF16)


def _compress(kv, wlo, whi, pelo, pehi, *, batch, seq):
    nc = seq // CMP_STRIDE
    return pl.pallas_call(
        functools.partial(_compress_kernel, nc=nc),
        grid=(batch,),
        in_specs=[pl.BlockSpec((seq, LANES), lambda b: (b, 0)), pl.BlockSpec((seq, LANES), lambda b: (b, 1)),
                  _const_spec(wlo.shape), _const_spec(whi.shape), _const_spec(pelo.shape), _const_spec(pehi.shape)],
        out_specs=pl.BlockSpec((nc, 2 * LANES), lambda b: (b, 0)),
        out_shape=jax.ShapeDtypeStruct((batch * nc, 2 * LANES), BF16),
        compiler_params=pltpu.CompilerParams(dimension_semantics=("arbitrary",), vmem_limit_bytes=VMEM_LIMIT),
        name="compress",
    )(kv, kv, wlo, whi, pelo, pehi)


def _masked_softmax2(s):
    m = jnp.maximum(jnp.max(s, axis=-1, keepdims=True), NEG / 8)
    e = jnp.exp2(s - m)
    return e * (1.0 / jnp.maximum(jnp.sum(e, axis=-1, keepdims=True), 1e-30))


def _select_blocks(score, sel, blk, axis, rounds, filler=iter(())):
    n = score.shape[axis]
    for _ in range(rounds):
        mx = jnp.max(score, axis=axis, keepdims=True)
        idx = jnp.min(jnp.where(score == mx, blk, n), axis=axis, keepdims=True)
        hit = blk == idx
        sel = jnp.where(hit, jnp.where(mx > NEG / 2, 1.0, sel), sel)
        score = jnp.where(hit, -3e38, score)
        next(filler, None)
    for _ in filler:
        pass
    return sel


def _nsa_prompt_kernel(q_ref, small_ref, kaug_ref, vaug_ref, kwin_ref, vwaug_ref, ckv_ref, mt_ref, o_ref,
                       *, seq, kc):
    nc = seq // CMP_STRIDE
    qb = Q_BLOCK
    start = pl.program_id(1) * qb
    wlen = WINDOW + qb
    groups = range(NSA_KV_HEADS)
    stack = lambda x: jnp.concatenate([x] * NSA_REP, axis=0)

    lane = lax.broadcasted_iota(jnp.int32, (qb, LANES), 1)
    gates = _sigmoid(small_ref[...])

    qrow_c = lax.broadcasted_iota(jnp.int32, (qb, nc), 0)
    ncol = lax.broadcasted_iota(jnp.int32, (qb, nc), 1)
    cmp_bias = stack(jnp.where(ncol * CMP_STRIDE + (CMP_LEN - 1) - qrow_c <= start, 0.0, NEG))
    tail_bias = stack(jnp.where(lane <= lax.broadcasted_iota(jnp.int32, (qb, qb), 0), 0.0, NEG))
    wbase = pl.multiple_of(jnp.maximum(start - WINDOW, 0), qb)
    back = (lax.broadcasted_iota(jnp.int32, (qb, wlen), 0) - lax.broadcasted_iota(jnp.int32, (qb, wlen), 1)
            + (start - wbase))
    win_bias = stack(jnp.where(back >= 0, jnp.where(back <= WINDOW, 0.0, NEG), NEG))

    glanes = [slice(g * LANES, (g + 1) * LANES) for g in groups]
    qs = [jnp.concatenate([jnp.where((lane >= HEAD_DIM) == (g == 1), q_ref[:, r * LANES:(r + 1) * LANES], 0.0)
                           .astype(BF16) for r in range(NSA_REP)], axis=0) for g in groups]
    ps = [_masked_softmax2(_dot_nt(q, ckv_ref[:, 0:LANES]) + cmp_bias) for q in qs]
    psums =[p[0:qb] + p[qb:2 * qb] + p[2 * qb:3 * qb] + p[3 * qb:4 * qb] for p in ps]
    imp_t = jnp.concatenate([sum(_dot_nt(mt_ref[...], piece) for piece in _split3(psum)) for psum in psums], axis=1)
    o_cmps = [_dot(p.astype(BF16), ckv_ref[:, LANES:2 * LANES]) for p in ps]

    done = {}

    def independent_work():
        s = [_dot_nt(q, kwin_ref[pl.ds(wbase, wlen), :]) + win_bias for q in qs]
        yield
        mx = [jnp.max(x, axis=-1, keepdims=True) for x in s]
        yield
        e = []
        for g in groups:
            e.append(jnp.exp2(s[g] - mx[g]).astype(BF16))
            yield
        acc = [_dot(e[g], vwaug_ref[pl.ds(wbase, wlen), glanes[g]]) for g in groups]
        yield
        o_win = [a / pltpu.roll(a, HEAD_DIM, 1) for a in acc]
        yield
        rows_of = lambda x, r: x[r * qb:(r + 1) * qb]
        done["partial"] = []
        for g in groups:
            done["partial"].append(
                [gates[:, 3 * (g * NSA_REP + r):3 * (g * NSA_REP + r) + 1] * rows_of(o_cmps[g], r)
                 + gates[:, 3 * (g * NSA_REP + r) + 2:3 * (g * NSA_REP + r) + 3] * rows_of(o_win[g], r)
                 for r in range(NSA_REP)])
            yield
        s_tail = [_dot_nt(q, kaug_ref[pl.ds(start, qb), LANES:2 * LANES]) + tail_bias for q in qs]
        yield
        done["m0"] = [jnp.max(x, axis=-1, keepdims=True) for x in s_tail]
        yield
        done["acc0"] = [_dot(jnp.exp2(s_tail[g] - done["m0"][g]).astype(BF16), vaug_ref[pl.ds(start, qb), glanes[g]])
                        for g in groups]
        yield

    blk_t = lax.broadcasted_iota(jnp.int32, (LANES, 2 * qb), 0)
    qpos_t = start + lax.broadcasted_iota(jnp.int32, (LANES, 2 * qb), 1) % qb
    cur_t = qpos_t // SEL_LEN
    visible = blk_t * SEL_LEN <= qpos_t
    forced = (blk_t == 0) | (blk_t == cur_t) | (blk_t == cur_t - 1)
    sel_t = _select_blocks(jnp.where(visible, jnp.where(forced, NEG, imp_t), NEG),
                           jnp.where(visible, jnp.where(forced, 1.0, 0.0), 0.0), blk_t, 0, SEL_TOPK - 3,
                           independent_work())
    partial, m0s, acc0s = done["partial"], done["m0"], done["acc0"]
    bias_t = jnp.where(blk_t * SEL_LEN < start, jnp.where(sel_t > 0.5, 0.0, NEG), NEG).T.astype(BF16)
    n_chunks = (start + kc - 1) // kc
    q_augs = [jnp.concatenate([stack(bias_t[g * qb:(g + 1) * qb]), qs[g]], axis=1) for g in groups]

    def sel_step(c, carry):
        off = pl.multiple_of(c * kc, kc)
        k = kaug_ref[pl.ds(off, kc), :]
        out = []
        for g in groups:
            m, acc = carry[g]
            s = _dot_nt(q_augs[g], k)
            m_new = jnp.maximum(m, jnp.max(s, axis=-1, keepdims=True))
            e = jnp.exp2(s - m_new).astype(BF16)
            out.append((m_new, jnp.exp2(m - m_new) * acc + _dot(e, vaug_ref[pl.ds(off, kc), g * LANES:(g + 1) * LANES])))
        return tuple(out)

    def sel_steps(first, count, x):
        for u in range(count):
            x = sel_step(first + u, x)
        return x

    sel_out = lax.fori_loop(0, n_chunks // SEL_UNROLL, lambda c, x: sel_steps(SEL_UNROLL * c, SEL_UNROLL, x),
                            tuple(zip(m0s, acc0s)))
    done_chunks = n_chunks // SEL_UNROLL * SEL_UNROLL
    count = SEL_UNROLL // 2
    while count >= 1:
        take = (n_chunks - done_chunks) >= count
        sel_out = lax.cond(take, functools.partial(sel_steps, done_chunks, count), lambda x: x, sel_out)
        done_chunks = done_chunks + jnp.where(take, count, 0)
        count //= 2

    outs = [None] * NSA_REP
    for g in groups:
        acc = sel_out[g][1]
        o_sel = acc / pltpu.roll(acc, HEAD_DIM, 1)
        for r in range(NSA_REP):
            c0 = (g * NSA_REP + r) * 3
            o = partial[g][r] + gates[:, c0 + 1:c0 + 2] * o_sel[r * qb:(r + 1) * qb]
            outs[r] = o if g == 0 else jnp.where(lane < HEAD_DIM, outs[r], o)

    for r in range(NSA_REP):
        o_ref[:, r * LANES:(r + 1) * LANES] = outs[r].astype(BF16)


def _nsa_prompt(q, small, kaug, vaug, kwin, vwaug, ckv, mt, *, batch, seq, kc):
    nb = seq // Q_BLOCK
    nc = seq // CMP_STRIDE
    blk_spec = lambda w: pl.BlockSpec((Q_BLOCK, w), lambda b, i: (b * nb + i, 0))
    seq_spec = lambda w: pl.BlockSpec((seq, w), lambda b, i: (b, 0))
    return pl.pallas_call(
        functools.partial(_nsa_prompt_kernel, seq=seq, kc=kc),
        grid=(batch, nb),
        in_specs=[blk_spec(NSA_Q_W), blk_spec(LANES), seq_spec(2 * LANES), seq_spec(2 * LANES), seq_spec(LANES),
                  seq_spec(2 * LANES), pl.BlockSpec((nc, 2 * LANES), lambda b, i: (b, 0)), _const_spec(mt.shape)],
        out_specs=blk_spec(NSA_Q_W),
        out_shape=jax.ShapeDtypeStruct((batch * seq, NSA_Q_W), BF16),
        compiler_params=pltpu.CompilerParams(dimension_semantics=("arbitrary", "arbitrary"),
                                             vmem_limit_bytes=VMEM_LIMIT),
        name="nsa_prompt",
    )(q, small, kaug, vaug, kwin, vwaug, ckv, mt)


def _nsa_sample_kernel(pt_ref, *refs, past, pps, nb):
    del pt_ref
    pages = refs[:nb * pps]
    (q_ref, kvn_ref, small_ref, win_ref, perm_ref, wk_ref, wv_ref, pelo_ref, pehi_ref, mt_ref, exp_ref,
     o_ref, xs_ref, s_ref, vs_ref) = refs[nb * pps:]
    step = pl.program_id(1)
    n_pages = past // PAGE_SIZE
    cpp = PAGE_SIZE // CMP_STRIDE
    heads = NSA_HEADS
    seqs = range(nb)
    row = lax.broadcasted_iota(jnp.int32, (heads, LANES), 0)
    lane = lax.broadcasted_iota(jnp.int32, (heads, LANES), 1)
    mine = (lane >= HEAD_DIM) == (row >= NSA_REP)
    rr = row % NSA_REP
    qf = []
    for b in seqs:
        qsel = jnp.zeros((heads, LANES), F32)
        for r in range(NSA_REP):
            qsel = jnp.where(rr == r, q_ref[b, :, r * LANES:(r + 1) * LANES].astype(F32), qsel)
        qf.append(jnp.where(mine, qsel, 0.0))
    q = [x.astype(BF16) for x in qf]

    nc = past // CMP_STRIDE

    def take_pages(first_page):
        for k in range(pps):
            pg = first_page + k
            tok = slice(pg * PAGE_SIZE, (pg + 1) * PAGE_SIZE)
            for b in seqs:
                page = pages[b * pps + k]
                s_ref[b, :, tok] = _dot(q[b], page[2 * LANES:3 * LANES, :].astype(BF16))
                vs_ref[b, :, tok] = page[3 * LANES:4 * LANES, :].astype(BF16)
                x = _dot_nt(perm_ref[...], page[0:2 * LANES, :].astype(BF16))
                for l in range(CMP_STRIDE):
                    for c in range(2):
                        xs_ref[b, c, pg * cpp:(pg + 1) * cpp, l * LANES:(l + 1) * LANES] = (
                            x[l * cpp:(l + 1) * cpp, c * LANES:(c + 1) * LANES])

    for st in range(n_pages // pps):
        pl.when(step == st)(functools.partial(take_pages, st * pps))

    @pl.when(step == pl.num_programs(1) - 1)
    def _():
        n_cmp = nc - 1
        nsp = mt_ref.shape[0]
        cur = past // SEL_LEN

        crow = lax.broadcasted_iota(jnp.int32, (nc, LANES), 0)
        halves = []
        for c, w_ref in enumerate((wk_ref, wv_ref)):
            cl = slice(c * LANES, (c + 1) * LANES)
            for b in seqs:
                for l in range(CMP_STRIDE):
                    ll = slice(l * LANES, (l + 1) * LANES)
                    xs_ref[b, c, nc:nc + SUBLANES, ll] = jnp.broadcast_to(pelo_ref[l:l + 1, cl], (SUBLANES, LANES))
                    xs_ref[b, c, nc + SUBLANES:nc + 2 * SUBLANES, ll] = jnp.broadcast_to(pehi_ref[l:l + 1, cl],
                                                                                           (SUBLANES, LANES))
            accs = [_dot(xs_ref[b, c].astype(BF16), w_ref[...]) for b in seqs]
            halves.append([
                jnp.where(crow < nc - 1,
                          acc[0:nc, 0:LANES] + pltpu.roll(acc[0:nc, LANES:2 * LANES], nc - 1, 0)
                          + acc[nc:nc + 1, 0:LANES] + acc[nc + SUBLANES:nc + SUBLANES + 1, LANES:2 * LANES],
                          0.0).astype(BF16) for acc in accs])
        ck, cv = halves

        kvn = [kvn_ref[b] for b in seqs]
        nw = win_ref.shape[1] // 2
        s = [_dot(q[b], win_ref[b, 0:nw, :].astype(BF16)) for b in seqs]
        s_new = [jnp.sum(qf[b] * kvn[b][:, 4 * LANES:5 * LANES], axis=-1, keepdims=True) for b in seqs]
        m = [jnp.maximum(jnp.max(s[b], axis=-1, keepdims=True), s_new[b]) for b in seqs]
        e = [jnp.exp2(s[b] - m[b]) for b in seqs]
        e_new = [jnp.exp2(s_new[b] - m[b]) for b in seqs]
        o_win = [(_dot_nt(e[b].astype(BF16), win_ref[b, nw:, :].astype(BF16)) + e_new[b] * kvn[b][:, 5 * LANES:6 * LANES])
                 / (jnp.sum(e[b], axis=-1, keepdims=True) + e_new[b]) for b in seqs]

        ncol = lax.broadcasted_iota(jnp.int32, (1, nc), 1)
        cmp_valid = (ncol * CMP_STRIDE + CMP_LEN - 1 <= past) & (ncol < n_cmp)
        p = [_masked_softmax2(jnp.where(cmp_valid, _dot_nt(q[b], ck[b]), NEG)) for b in seqs]
        o_cmp = [_dot(p[b].astype(BF16), cv[b]) for b in seqs]

        prow = lax.broadcasted_iota(jnp.int32, (heads, nc), 0)
        psum = [jnp.where(prow < NSA_REP,
                          jnp.sum(jnp.where(prow < NSA_REP, p[b], 0.0), axis=0, keepdims=True),
                          jnp.sum(jnp.where(prow < NSA_REP, 0.0, p[b]), axis=0, keepdims=True)) for b in seqs]
        imp = [sum(_dot_nt(piece, mt_ref[...]) for piece in _split3(psum[b])) for b in seqs]
        blk = lax.broadcasted_iota(jnp.int32, (heads, nsp), 1)
        forced = (blk == 0) | (blk == cur) | (blk == cur - 1)
        score = [jnp.where(blk * SEL_LEN <= past, imp[b] + jnp.where(forced, SEL_BONUS, 0.0), NEG) for b in seqs]
        score_t = [x.T for x in score]
        bi = lax.broadcasted_iota(jnp.int32, (nsp, nsp), 0)
        bj = lax.broadcasted_iota(jnp.int32, (nsp, nsp), 1)
        tie = jnp.where(bi < bj, 1.0, 0.0)
        head_row = lax.broadcasted_iota(jnp.int32, (heads, nsp), 0)
        sel8 = []
        for b in seqs:
            picks = []
            for g in range(NSA_KV_HEADS):
                c = g * NSA_REP
                s_i, s_j = score_t[b][:, c:c + 1], score[b][c:c + 1, :]
                beats = jnp.where(s_i > s_j, 1.0, jnp.where(s_i == s_j, tie, 0.0))
                rank = jnp.sum(beats, axis=0, keepdims=True)
                picks.append(jnp.where(rank < SEL_TOPK, jnp.where(s_j > NEG / 2, 1.0, 0.0), 0.0))
            sel8.append(jnp.where(head_row < NSA_REP, picks[0], picks[1]))
        keep = [_dot(sel8[b].astype(BF16), exp_ref[...]) > 0.5 for b in seqs]

        new_ok = [sel8[b][:, cur:cur + 1] > 0.5 for b in seqs]
        s = [jnp.where(keep[b], s_ref[b], NEG) for b in seqs]
        s_new = [jnp.where(new_ok[b], jnp.sum(qf[b] * kvn[b][:, 2 * LANES:3 * LANES], axis=-1, keepdims=True), NEG)
                 for b in seqs]
        m = [jnp.maximum(jnp.max(s[b], axis=-1, keepdims=True), s_new[b]) for b in seqs]
        e = [jnp.exp2(s[b] - m[b]) for b in seqs]
        e_new = [jnp.where(new_ok[b], jnp.exp2(s_new[b] - m[b]), 0.0) for b in seqs]
        o_sel = [(_dot_nt(e[b].astype(BF16), vs_ref[b]) + e_new[b] * kvn[b][:, 3 * LANES:4 * LANES])
                 / (jnp.sum(e[b], axis=-1, keepdims=True) + e_new[b]) for b in seqs]

        for b in seqs:
            gates = _sigmoid(small_ref[b])
            gate = lambda br: jnp.sum(jnp.where(lane == row * 3 + br, gates, 0.0), axis=-1, keepdims=True)
            o = gate(0) * o_cmp[b] + gate(1) * o_sel[b] + gate(2) * o_win[b]
            for r in range(NSA_REP):
                o_ref[b, :, r * LANES:(r + 1) * LANES] = jnp.where(
                    lane[0:1] < HEAD_DIM, o[r:r + 1], o[NSA_REP + r:NSA_REP + r + 1]).astype(BF16)


def _nsa_sample(page_table, cache, q, kvn, small, win, wk, wv, pelo, pehi, mt, *, pps, nb):
    batch, n_pages = page_table.shape
    past = n_pages * PAGE_SIZE
    rows = cache.shape[1]
    cpp = PAGE_SIZE // CMP_STRIDE
    nsp = mt.shape[0]
    expand = jnp.asarray(np.arange(nsp)[:, None] == np.arange(past)[None, :] // SEL_LEN, BF16)
    tok = np.arange(PAGE_SIZE)
    perm = jnp.asarray((tok[None, :] % CMP_STRIDE) * cpp + tok[None, :] // CMP_STRIDE == tok[:, None], BF16)

    def page_spec(r, k):
        return pl.BlockSpec((None, rows, PAGE_SIZE),
                            lambda b, s, pt: (pt[(b * nb + r) * n_pages + s * pps + k], 0, 0))

    per_b = lambda shape: pl.BlockSpec((nb,) + shape, lambda b, s, pt: (b, 0, 0))
    const = lambda a: pl.BlockSpec(a.shape, lambda b, s, pt: (0,) * a.ndim, pipeline_mode=pl.Buffered(1))
    grid_spec = pltpu.PrefetchScalarGridSpec(
        num_scalar_prefetch=1,
        grid=(batch // nb, n_pages // pps),
        in_specs=[page_spec(r, k) for r in range(nb) for k in range(pps)] + [
            per_b((1, NSA_Q_W)), per_b((1, 6 * NSA_KV_W)), per_b((1, LANES)), per_b(win.shape[1:]),
            const(perm), const(wk), const(wv), const(pelo), const(pehi), const(mt), const(expand)],
        out_specs=per_b((1, NSA_Q_W)),
        scratch_shapes=[pltpu.VMEM((nb, 2, past // CMP_STRIDE + 2 * SUBLANES, CMP_STRIDE * LANES), F32),
                        pltpu.VMEM((nb, NSA_HEADS, past), F32),
                        pltpu.VMEM((nb, LANES, past), BF16)],
    )
    return pl.pallas_call(
        functools.partial(_nsa_sample_kernel, past=past, pps=pps, nb=nb),
        grid_spec=grid_spec,
        out_shape=jax.ShapeDtypeStruct((batch, 1, NSA_Q_W), BF16),
        compiler_params=pltpu.CompilerParams(dimension_semantics=("arbitrary", "arbitrary"),
                                             vmem_limit_bytes=VMEM_LIMIT),
        name="nsa_sample",
    )(page_table.reshape(-1), *([cache] * (nb * pps)), q, kvn, small, win, perm, wk, wv, pelo, pehi, mt, expand)


def _l2norm(x):
    return x * lax.rsqrt(jnp.sum(x * x, axis=-1, keepdims=True) + EPS)


def _softplus(x):
    return jnp.maximum(x, 0.0) + jnp.log(1.0 + jnp.exp(-jnp.abs(x)))


def _gdn_step_kernel(x_ref, z_ref, small_ref, buf_ref, s0_ref, cw_ref, pcol_ref, nw_ref, o_ref, sout_ref):
    hist = CONV_W - 1
    conv = jnp.sum(buf_ref[...] * cw_ref[0:hist, :], axis=0, keepdims=True) + x_ref[...] * cw_ref[hist:CONV_W, :]
    act = _silu(conv)
    small = small_ref[...]
    g_all = -jnp.exp(pcol_ref[0:1, :]) * _softplus(small + pcol_ref[1:2, :])
    beta_all = _sigmoid(small)
    row = lax.broadcasted_iota(jnp.int32, (SUBLANES, GDN_DK), 0)
    for h in range(GDN_HEADS):
        hs = slice(h * GDN_DK, (h + 1) * GDN_DK)
        qh = _l2norm(act[:, hs]) * (GDN_DK ** -0.5)
        kh = _l2norm(act[:, GDN_QK_W + h * GDN_DK:GDN_QK_W + (h + 1) * GDN_DK])
        vh = act[:, 2 * GDN_QK_W + h * GDN_DV:2 * GDN_QK_W + (h + 1) * GDN_DV]
        eg = jnp.exp(g_all[:, A_COL + h:A_COL + h + 1])
        bt = beta_all[:, B_COL + h:B_COL + h + 1]
        s = s0_ref[h]
        kq = jnp.where(row == 0, kh, jnp.where(row == 1, qh, 0.0)).astype(BF16)
        ks_qs = _dot(kq, s.astype(BF16))
        vn = bt * (vh - eg * ks_qs[0:1])
        o = eg * ks_qs[1:2] + jnp.sum(qh * kh, axis=-1, keepdims=True) * vn
        k8 = jnp.where(row == 0, kh, 0.0).astype(BF16)
        vn8 = jnp.where(row == 0, vn, 0.0).astype(BF16)
        sout_ref[h] = s * eg + lax.dot_general(k8, vn8, TN_DIMS, preferred_element_type=F32)
        on = o * lax.rsqrt(jnp.mean(o * o, axis=-1, keepdims=True) + EPS) * nw_ref[...]
        o_ref[:, hs] = (on * _silu(z_ref[:, hs])).astype(BF16)


def _gdn_step(x, z, small, buf, s0, cw, pcol, nw):
    batch = x.shape[0]
    per_b = lambda shape: pl.BlockSpec((None,) + shape, lambda b: (b,) + (0,) * len(shape))
    state = (GDN_HEADS, GDN_DK, GDN_DV)
    return pl.pallas_call(
        _gdn_step_kernel,
        grid=(batch,),
        in_specs=[per_b((1, GDN_CONV_CH)), per_b((1, GDN_V_W)), per_b((1, LANES)), per_b((CONV_W - 1, GDN_CONV_CH)),
                  per_b(state), _const_spec(cw.shape), _const_spec(pcol.shape), _const_spec(nw.shape)],
        out_specs=(per_b((1, GDN_V_W)), per_b(state)),
        out_shape=(jax.ShapeDtypeStruct((batch, 1, GDN_V_W), BF16),
                   jax.ShapeDtypeStruct((batch,) + state, F32)),
        compiler_params=pltpu.CompilerParams(dimension_semantics=("arbitrary",), vmem_limit_bytes=VMEM_LIMIT),
        name="gdn_step",
    )(x[:, None], z[:, None], small[:, None], buf, s0, cw, pcol, nw)


GDN_CHUNK = 64
GDN_STACK = GDN_HEADS * GDN_CHUNK


def _stack_heads(x, col0, width):
    return jnp.concatenate([x[:, col0 + h * width:col0 + (h + 1) * width] for h in range(GDN_HEADS)], axis=0)


def _gdn_prep_kernel(act_ref, small_ref, smallt_ref, pcol_ref, prow_ref, tril_ref, triu_ref,
                     u_ref, w_ref, qg_ref, kgt_ref, aqk_ref, gl_ref, *, tb):
    ck = GDN_CHUNK
    st = GDN_STACK
    act = act_ref[...]

    small = small_ref[...]
    g_col = -jnp.exp(pcol_ref[0:1, :]) * _softplus(small + pcol_ref[1:2, :])
    beta = _sigmoid(small)
    g_row = -jnp.exp(prow_ref[:, 0:1]) * _softplus(smallt_ref[...] + prow_ref[:, 1:2])
    gcum_col = sum(_dot(tril_ref[...], piece) for piece in _split3(g_col))
    gcum_row = sum(_dot(piece, triu_ref[...]) for piece in _split3(g_row))

    ii = lax.broadcasted_iota(jnp.int32, (st, st), 0)
    jj = lax.broadcasted_iota(jnp.int32, (st, st), 1)
    same_head = (ii // ck) == (jj // ck)
    incl = same_head & (ii >= jj)
    strict = same_head & (ii > jj)
    eye = jnp.where(ii == jj, 1.0, 0.0)
    hrow = lax.broadcasted_iota(jnp.int32, (SUBLANES, LANES), 0)

    lms, rhss = [], []
    for ci in range(tb // ck):
        r0 = ci * ck
        rs = slice(r0, r0 + ck)
        gc = _stack_heads(gcum_col[rs], A_COL, 1)
        bt = _stack_heads(beta[rs], B_COL, 1)
        gr = jnp.concatenate([gcum_row[h:h + 1, r0:r0 + ck] for h in range(GDN_HEADS)], axis=1)
        glast = [gcum_col[r0 + ck - 1:r0 + ck, A_COL + h:A_COL + h + 1] for h in range(GDN_HEADS)]
        gl_stack = jnp.concatenate([jnp.broadcast_to(x, (ck, 1)) for x in glast], axis=0)
        dec = jnp.where(incl, jnp.exp(jnp.where(incl, gc - gr, 0.0)), 0.0)
        qs = _stack_heads(act[rs], 0, GDN_DK)
        ks = _stack_heads(act[rs], GDN_QK_W, GDN_DK)
        vs = _stack_heads(act[rs], 2 * GDN_QK_W, GDN_DV)
        kb = ks * bt
        k16 = ks.astype(BF16)
        lms.append(jnp.where(strict, _dot_nt(kb.astype(BF16), k16) * dec, 0.0))
        eg = jnp.exp(gc)
        rhss.append(jnp.concatenate([vs * bt, kb * eg], axis=1).astype(BF16))
        orow = slice(ci * st, (ci + 1) * st)
        qg_ref[orow, :] = (qs * eg).astype(BF16)
        aqk_ref[orow, :] = (_dot_nt(qs.astype(BF16), k16) * dec).astype(BF16)
        kgt_ref[ci * GDN_DK:(ci + 1) * GDN_DK, :] = (ks * jnp.exp(gl_stack - gc)).T.astype(BF16)
        gl = jnp.zeros((SUBLANES, LANES), F32)
        for h in range(GDN_HEADS):
            gl = jnp.where(hrow == h, jnp.exp(glast[h]), gl)
        gl_ref[ci * SUBLANES:(ci + 1) * SUBLANES, :] = gl

    ainvs = [eye - lm for lm in lms]
    pws = lms
    n = 2
    while n < ck:
        pw16s = [pw.astype(BF16) for pw in pws]
        pws = [_dot(pw16, pw16) for pw16 in pw16s]
        ainvs = [ainv + _dot(ainv.astype(BF16), pw.astype(BF16)) for ainv, pw in zip(ainvs, pws)]
        n *= 2
    for ci, (ainv, rhs) in enumerate(zip(ainvs, rhss)):
        sol = _dot(ainv.astype(BF16), rhs)
        orow = slice(ci * st, (ci + 1) * st)
        u_ref[orow, :] = sol[:, 0:GDN_DV]
        w_ref[orow, :] = sol[:, GDN_DV:].astype(BF16)


def _gdn_prep(act, small, smallt, pcol, prow, *, batch, seq, tb):
    nblk = seq // tb
    ncb = tb // GDN_CHUNK
    blk = np.arange(tb)
    same = (blk[:, None] // GDN_CHUNK) == (blk[None, :] // GDN_CHUNK)
    tril = jnp.asarray(same & (blk[:, None] >= blk[None, :]), BF16)
    triu = jnp.asarray(same & (blk[:, None] <= blk[None, :]), BF16)
    row_spec = lambda r, w: pl.BlockSpec((r, w), lambda b, j: (b * nblk + j, 0))
    per_b = lambda shape: pl.BlockSpec((None,) + shape, lambda b, j: (b,) + (0,) * len(shape))
    n_chunks = batch * seq // GDN_CHUNK
    out_shape = (jax.ShapeDtypeStruct((n_chunks * GDN_STACK, GDN_DV), F32),
                 jax.ShapeDtypeStruct((n_chunks * GDN_STACK, GDN_DK), BF16),
                 jax.ShapeDtypeStruct((n_chunks * GDN_STACK, GDN_DK), BF16),
                 jax.ShapeDtypeStruct((n_chunks * GDN_DK, GDN_STACK), BF16),
                 jax.ShapeDtypeStruct((n_chunks * GDN_STACK, GDN_STACK), BF16),
                 jax.ShapeDtypeStruct((n_chunks * SUBLANES, LANES), F32))
    out_specs = (row_spec(ncb * GDN_STACK, GDN_DV), row_spec(ncb * GDN_STACK, GDN_DK),
                 row_spec(ncb * GDN_STACK, GDN_DK), row_spec(ncb * GDN_DK, GDN_STACK),
                 row_spec(ncb * GDN_STACK, GDN_STACK), row_spec(ncb * SUBLANES, LANES))
    return pl.pallas_call(
        functools.partial(_gdn_prep_kernel, tb=tb),
        grid=(batch, nblk),
        in_specs=[row_spec(tb, GDN_CONV_CH), row_spec(tb, LANES),
                  pl.BlockSpec((None, SUBLANES, tb), lambda b, j: (b, 0, j)),
                  _const_spec(pcol.shape), _const_spec(prow.shape),
                  _const_spec(tril.shape), _const_spec(triu.shape)],
        out_specs=out_specs,
        out_shape=out_shape,
        compiler_params=pltpu.CompilerParams(dimension_semantics=("arbitrary", "arbitrary"),
                                             vmem_limit_bytes=VMEM_LIMIT),
        name="gdn_prep",
    )(act, small, smallt, pcol, prow, tril, triu)


def _gdn_scan_kernel(u_ref, w_ref, qg_ref, kgt_ref, aqk_ref, gl_ref, z_ref, s0_ref, nw_ref, o_ref, sout_ref, s_ref,
                     *, batch, ncb):
    ck = GDN_CHUNK
    st = GDN_STACK
    sw = GDN_HEADS * GDN_DK

    @pl.when(pl.program_id(0) == 0)
    def _():
        s_ref[...] = s0_ref[...]

    tall_mask =(lax.broadcasted_iota(jnp.int32, (sw, st), 0) // GDN_DK
                 == lax.broadcasted_iota(jnp.int32, (sw, st), 1) // ck)

    seqs = range(batch)
    s = [s_ref[b] for b in seqs]
    for ci in range(ncb):
        rows = slice(ci * st, (ci + 1) * st)
        toks = slice(ci * ck, (ci + 1) * ck)
        lhs, kgt_bd, gl_rows = [], [], []
        for b in seqs:
            lhs.append([jnp.concatenate([w_ref[b, ci * st + h * ck:ci * st + (h + 1) * ck, :],
                                         qg_ref[b, ci * st + h * ck:ci * st + (h + 1) * ck, :]], axis=0)
                        for h in range(GDN_HEADS)])
            kgt = kgt_ref[b, ci * GDN_DK:(ci + 1) * GDN_DK, :]
            kgt_bd.append(jnp.where(tall_mask, jnp.concatenate([kgt] * GDN_HEADS, axis=0), 0.0))
            gl = gl_ref[b, ci * SUBLANES:(ci + 1) * SUBLANES, :]
            gl_rows.append(jnp.concatenate(
                [jnp.broadcast_to(gl[h:h + 1], (GDN_DK, GDN_DV)) for h in range(GDN_HEADS)], axis=0))
        s16 = [x.astype(BF16) for x in s]
        t1 = [[_dot(lhs[b][h], s16[b][h * GDN_DK:(h + 1) * GDN_DK]) for h in range(GDN_HEADS)] for b in seqs]
        w_s = [jnp.concatenate([t[0:ck] for t in t1[b]], axis=0) for b in seqs]
        q_s = [jnp.concatenate([t[ck:] for t in t1[b]], axis=0) for b in seqs]
        vn16 = [(u_ref[b, rows, :] - w_s[b]).astype(BF16) for b in seqs]
        o = [q_s[b] + _dot(aqk_ref[b, rows, :], vn16[b]) for b in seqs]
        s = [s[b] * gl_rows[b] + _dot(kgt_bd[b], vn16[b]) for b in seqs]
        for b in seqs:
            for h in range(GDN_HEADS):
                oh = o[b][h * ck:(h + 1) * ck]
                on = oh * lax.rsqrt(jnp.mean(oh * oh, axis=-1, keepdims=True) + EPS) * nw_ref[...]
                hs = slice(h * GDN_DV, (h + 1) * GDN_DV)
                o_ref[b, toks, hs] = (on * _silu(z_ref[b, toks, hs])).astype(BF16)

    for b in seqs:
        s_ref[b] = s[b]
    sout_ref[...] = s_ref[...]


def _gdn_scan(u, w, qg, kgt, aqk, gl, z, s0, nw, *, batch, seq, tb):
    nblk = seq // tb
    ncb = tb // GDN_CHUNK
    cps = seq // GDN_CHUNK
    sw = GDN_HEADS * GDN_DK
    r3 = lambda a, rows_per_chunk: a.reshape(batch, cps * rows_per_chunk, a.shape[-1])
    blk = lambda rows, width: pl.BlockSpec((batch, rows, width), lambda j: (0, j, 0))
    full = pl.BlockSpec((batch, sw, GDN_DV), lambda j: (0, 0, 0))
    return pl.pallas_call(
        functools.partial(_gdn_scan_kernel, batch=batch, ncb=ncb),
        grid=(nblk,),
        in_specs=[blk(ncb * GDN_STACK, GDN_DV), blk(ncb * GDN_STACK, GDN_DK), blk(ncb * GDN_STACK, GDN_DK),
                  blk(ncb * GDN_DK, GDN_STACK), blk(ncb * GDN_STACK, GDN_STACK), blk(ncb * SUBLANES, LANES),
                  blk(tb, GDN_V_W), full, _const_spec(nw.shape)],
        out_specs=(blk(tb, GDN_V_W), full),
        out_shape=(jax.ShapeDtypeStruct((batch, seq, GDN_V_W), BF16),
                   jax.ShapeDtypeStruct((batch, sw, GDN_DV), F32)),
        scratch_shapes=[pltpu.VMEM((batch, sw, GDN_DV), F32)],
        compiler_params=pltpu.CompilerParams(dimension_semantics=("arbitrary",), vmem_limit_bytes=VMEM_LIMIT),
        name="gdn_scan",
    )(r3(u, GDN_STACK), r3(w, GDN_STACK), r3(qg, GDN_STACK), r3(kgt, GDN_DK), r3(aqk, GDN_STACK),
      r3(gl, SUBLANES), z.reshape(batch, seq, GDN_V_W), s0.reshape(batch, sw, GDN_DV), nw)


def _rope_tables(pos):
    half = HEAD_DIM // 2
    inv = np.power(ROPE_THETA, -np.arange(half, dtype=np.float64) * 2.0 / HEAD_DIM)
    ang = np.asarray(pos, np.float64)[:, None] * inv[None, :]
    cos, sin = np.cos(ang).astype(np.float32), np.sin(ang).astype(np.float32)
    return tuple(jnp.asarray(t) for t in (np.tile(cos, (1, 4)), np.concatenate([-sin, sin, -sin, sin], axis=1),
                                          cos.T, sin.T))


def _cmp_to_sel_t(n_cmp, n_sel, rows, cols):
    cs = np.arange(cols)[None, :] * CMP_STRIDE
    ss = np.arange(rows)[:, None] * SEL_LEN
    hit = (cs < ss + SEL_LEN) & (cs + CMP_LEN > ss)
    hit &= (np.arange(cols)[None, :] < n_cmp) & (np.arange(rows)[:, None] < n_sel)
    return jnp.asarray(hit, BF16)


def _layer_weights(w_in, cmp_pe, cmp_w, conv_w, a_log, dt_bias, gdn_norm, w_out):
    cuts = np.cumsum([NSA_Q_W, 6 * NSA_KV_W, GATE_COLS, GDN_CONV_CH, GDN_V_W, GDN_HEADS]).tolist()
    wq, wkv, wgt, wconv, wz, wa, wb = jnp.split(w_in, cuts, axis=1)
    order = np.array([g * NSA_REP + r for r in range(NSA_REP) for g in range(NSA_KV_HEADS)])
    cols = (order[:, None] * HEAD_DIM + np.arange(HEAD_DIM)[None, :]).reshape(-1)
    wm = jnp.concatenate([wq[:, cols], wkv, wconv, wz], axis=1).astype(BF16)
    wkvt = wkv.T.astype(BF16)
    ws = jnp.concatenate([wgt, wa, wb], axis=1)
    ws = jnp.pad(ws, ((0, 0), (0, LANES - ws.shape[1]))).astype(BF16)
    wo = jnp.concatenate([w_out[:NSA_Q_W][cols], w_out[NSA_Q_W:]], axis=0).astype(BF16)

    def blockdiag(l0):
        wk, wv = cmp_w[0, l0:l0 + CMP_STRIDE], cmp_w[1, l0:l0 + CMP_STRIDE]
        z = jnp.zeros_like(wk)
        rows = [jnp.concatenate(r, axis=2) for r in ([wk, z, z, z], [z, wk, z, z], [z, z, wv, z], [z, z, z, wv])]
        return jnp.concatenate(rows, axis=1).astype(BF16)

    def long_contraction(c):
        halves = []
        for l0 in (0, CMP_STRIDE):
            w = cmp_w[c, l0:l0 + CMP_STRIDE]
            z = jnp.zeros_like(w)
            halves.append(jnp.concatenate([jnp.concatenate([w, z], axis=2), jnp.concatenate([z, w], axis=2)], axis=1))
        return jnp.concatenate(halves, axis=2).reshape(CMP_STRIDE * LANES, 2 * LANES).astype(BF16)

    def pe_rows(l0):
        pk, pv = cmp_pe[0, l0:l0 + CMP_STRIDE], cmp_pe[1, l0:l0 + CMP_STRIDE]
        return jnp.concatenate([pk, pk, pv, pv], axis=1)

    pcol = jnp.zeros((2, LANES), F32).at[0, A_COL:A_COL + GDN_HEADS].set(a_log)
    pcol = pcol.at[1, A_COL:A_COL + GDN_HEADS].set(dt_bias)
    prow = jnp.zeros((SUBLANES, 2), F32).at[0:GDN_HEADS, 0].set(a_log).at[0:GDN_HEADS, 1].set(dt_bias)
    return dict(wm=wm, ws=ws, wkvt=wkvt, wo=wo, wk_long=long_contraction(0), wv_long=long_contraction(1),
                wlo=blockdiag(0), whi=blockdiag(CMP_STRIDE), pelo=pe_rows(0),
                pehi=pe_rows(CMP_STRIDE), cw=conv_w, pcol=pcol, prow=prow, nw=gdn_norm[None, :])


def kernel(x_prompt, x_sample, cache_nsa_kv, cache_nsa_win, state_gdn_S, state_gdn_conv, page_table, norm_mix, w_in,
           nsa_cmp_pe, nsa_cmp_w, gdn_conv_w, gdn_a_log, gdn_dt_bias, gdn_norm, w_out, norm_ffn, w_gate_up, w_down,
           norm_final):
    bp, tp, _ = x_prompt.shape
    bs, ts, _ = x_sample.shape
    depth = w_in.shape[0]
    n_pages = page_table.shape[1]
    past = n_pages * PAGE_SIZE
    assert depth == 1 and ts == 1, "one layer and one new token per sample row"
    assert tp % 512 == 0 and tp // SEL_LEN <= LANES and past % 512 == 0
    l = 0
    wts = _layer_weights(w_in[l], nsa_cmp_pe[l], nsa_cmp_w[l], gdn_conv_w[l], gdn_a_log[l], gdn_dt_bias[l],
                         gdn_norm[l], w_out[l])
    g_mix, g_ffn, g_fin = norm_mix[l][None, :], norm_ffn[l][None, :], norm_final[None, :]
    wgu, wd = w_gate_up[l].astype(BF16), w_down[l].astype(BF16)
    hist = CONV_W - 1

    xp = x_prompt.reshape(bp * tp, D_MODEL)
    q, kv, conv_in, z, small, kaug, vaug, kwin, vwaug, kv4t_p, kvwt_p, act = _proj(
        xp, g_mix, wts["wm"], wts["ws"], wts["wkvt"], wts["cw"], jnp.zeros((bp, hist, GDN_CONV_CH), F32),
        *_rope_tables(np.arange(tp)), tm=PROJ_ROWS)
    ckv = _compress(kv, wts["wlo"], wts["whi"], wts["pelo"], wts["pehi"], batch=bp, seq=tp)
    nc = tp // CMP_STRIDE
    mt_p = _cmp_to_sel_t(nc - 1, tp // SEL_LEN, LANES, nc)
    o_nsa = _nsa_prompt(q, small, kaug, vaug, kwin, vwaug, ckv, mt_p, batch=bp, seq=tp, kc=SEL_KEYS)
    smallt = small[:, A_COL:A_COL + SUBLANES].reshape(bp, tp, SUBLANES).transpose(0, 2, 1)
    prep = _gdn_prep(act, small, smallt, wts["pcol"], wts["prow"], batch=bp, seq=tp, tb=GDN_ROWS)
    o_gdn, s_p = _gdn_scan(*prep, z, jnp.zeros((bp, GDN_HEADS, GDN_DK, GDN_DV), F32), wts["nw"],
                           batch=bp, seq=tp, tb=GDN_ROWS)
    o_gdn = o_gdn.reshape(bp * tp, GDN_V_W)
    s_p = s_p.reshape(bp, GDN_HEADS, GDN_DK, GDN_DV)
    y_p = _mix_ffn(xp, o_nsa, o_gdn, wts["wo"], g_ffn, wgu, wd, g_fin, tm=FFN_ROWS)
    tok_major = lambda a, comps: a.reshape(bp, comps, NSA_KV_HEADS, HEAD_DIM, -1).transpose(0, 4, 1, 2, 3)
    kv_p = tok_major(kv4t_p, 4)
    win_p = tok_major(kvwt_p[:, :, tp - min(WINDOW, tp):], 2)
    conv_p = conv_in.reshape(bp, tp, GDN_CONV_CH)[:, tp - hist:]

    xs = x_sample.reshape(bs, D_MODEL)
    q, kv, conv_in, z, small = _proj(xs, g_mix, wts["wm"], wts["ws"], wts["wkvt"], wts["cw"],
                                     jnp.zeros((1, hist, GDN_CONV_CH), F32),
                                     *_rope_tables(np.full((bs,), past)), tm=bs)[:5]
    n_sel = past // SEL_LEN + 1
    nsp = -(-n_sel // LANES) * LANES
    mt_s = _cmp_to_sel_t(past // CMP_STRIDE - 1, n_sel, nsp, past // CMP_STRIDE)
    cache = cache_nsa_kv[l].transpose(0, 2, 3, 4, 1).reshape(-1, 4 * NSA_KV_W, PAGE_SIZE)
    win = cache_nsa_win[l].transpose(0, 2, 3, 4, 1).reshape(bs, 2 * NSA_KV_W, -1)
    o_nsa = _nsa_sample(page_table, cache, q[:, None], kv[:, None], small[:, None], win, wts["wk_long"],
                        wts["wv_long"], wts["pelo"], wts["pehi"], mt_s, pps=min(SAMPLE_PAGES, n_pages), nb=SAMPLE_ROWS).reshape(bs, NSA_Q_W)
    o_gdn, s_s = _gdn_step(conv_in, z, small, state_gdn_conv[l], state_gdn_S[l], wts["cw"], wts["pcol"], wts["nw"])
    o_gdn = o_gdn.reshape(bs, GDN_V_W)
    y_s = _mix_ffn(xs, o_nsa, o_gdn, wts["wo"], g_ffn, wgu, wd, g_fin, tm=bs)
    kv_s = kv.reshape(bs, 1, 3, 2, NSA_KV_HEADS, HEAD_DIM)
    win_s = jnp.concatenate([cache_nsa_win[l], kv_s[:, :, 2]], axis=1)[:, -min(WINDOW, past + 1):]
    conv_s = jnp.concatenate([state_gdn_conv[l], conv_in[:, None]], axis=1)[:, -hist:]

    return (y_p.reshape(bp, tp, D_MODEL), y_s.reshape(bs, 1, D_MODEL),
            kv_p[None], win_p[None], s_p[None], conv_p[None],
            kv_s[:, :, 0:2].reshape(bs, 1, 4, NSA_KV_HEADS, HEAD_DIM)[None], win_s[None], s_s[None], conv_s[None])
```
